```python
import jax
import jax.numpy as jnp
from jax import lax
import numpy as np

D_MODEL = 1024
BATCH = 4
SEQ = 8192
DEPTH = 2

GRID_W = 64
CTX_LEN = 256
NORM_EPS = 1e-6
HEAD_DIM = 64
RET_W = 3 * D_MODEL // 8
RET_HEADS = RET_W // HEAD_DIM
RET_CHUNK = 128
RET_GN_EPS = 1e-5
ROPE_BASE = 10000.0
RWKV_W = 3 * D_MODEL // 8
RWKV_HEADS = RWKV_W // HEAD_DIM
DECAY_LORA = 64
ICLR_LORA = 64
GATE_LORA = 128
RWKV_ZW = 3 * RWKV_W + DECAY_LORA + ICLR_LORA
RWKV_GN_EPS = 64e-5
LRU_W = D_MODEL - RET_W - RWKV_W
LRU_BLOCKS = 4
LRU_BS = LRU_W // LRU_BLOCKS
LRU_CONV = 4
LRU_C = 8.0
MIX_W = RET_W + RWKV_W + LRU_W
PROJ_W = 4 * RET_W + RWKV_ZW + GATE_LORA + 2 * LRU_W
N_EXPERTS = 32
TOP_K = 4
D_EXPERT = D_MODEL
SWIGLU_LIMIT = 7.0
SWIGLU_ALPHA = 1.702
MOE_BLOCK = 256

kernel_name = 'hybrid_ret_rwkv7_rglru_moe_prefix_dit'


def rmsnorm(x, gain):
    x32 = x.astype(jnp.float32)
    y = x32 * lax.rsqrt(jnp.mean(x32 * x32, axis=-1, keepdims=True) + NORM_EPS)
    return (y * gain.astype(jnp.float32)).astype(x.dtype)


def modulate(h, shift, scale):
    return h * (1 + scale) + shift


def head_norm(y, gain, bias, eps):
    mu = jnp.mean(y, axis=-1, keepdims=True)
    yc = y - mu
    var = jnp.mean(yc * yc, axis=-1, keepdims=True)
    return yc * lax.rsqrt(var + eps) * gain + bias


def split_projection(p):
    sizes = (RET_W, RET_W, RET_W, RET_W, RWKV_ZW, GATE_LORA, LRU_W, LRU_W)
    bounds = [int(s) for s in np.cumsum(sizes)[:-1]]
    return jnp.split(p, bounds, axis=-1)


def time_flip(t, rev, axis):
    return jnp.flip(t, axis=axis) if rev else t


def axial_rope(t, rows, cols):
    half = HEAD_DIM // 2
    quarter = half // 2
    inv_freq = ROPE_BASE ** (-jnp.arange(quarter, dtype=jnp.float32) / quarter)

    def rotate(u, pos):
        ang = pos.astype(jnp.float32)[:, None] * inv_freq[None, :]
        cos = jnp.cos(ang)[None, :, None, :]
        sin = jnp.sin(ang)[None, :, None, :]
        u1, u2 = u[..., :quarter], u[..., quarter:]
        return jnp.concatenate([u1 * cos - u2 * sin, u1 * sin + u2 * cos], axis=-1)

    return jnp.concatenate([rotate(t[..., :half], rows), rotate(t[..., half:], cols)], axis=-1)


def retention_chunkwise(q, k, v, log_gamma, s0):
    bsz, nh, seq, dh = q.shape
    nc = seq // RET_CHUNK
    qc = q.reshape(bsz, nh, nc, RET_CHUNK, dh)
    kc = k.reshape(bsz, nh, nc, RET_CHUNK, dh)
    vc = v.reshape(bsz, nh, nc, RET_CHUNK, dh)
    pos = jnp.arange(RET_CHUNK, dtype=jnp.float32)
    lg = log_gamma[:, None]
    rel = pos[:, None] - pos[None, :]
    intra_decay = jnp.where(rel >= 0, jnp.exp(lg[:, :, None] * jnp.maximum(rel, 0.0)), 0.0)
    q_decay = jnp.exp(lg * (pos + 1.0))
    k_decay = jnp.exp(lg * (RET_CHUNK - 1.0 - pos))
    chunk_decay = jnp.exp(log_gamma * RET_CHUNK)[None, :, None, None]
    scores = jnp.einsum('bhcid,bhcjd->bhcij', qc, kc) * intra_decay[None, :, None]
    intra = jnp.einsum('bhcij,bhcjv->bhciv', scores, vc)
    chunk_kv = jnp.einsum('bhcjd,bhcjv->cbhdv', kc * k_decay[None, :, None, :, None], vc)

    def step(s, kv):
        return chunk_decay * s + kv, s

    s_final, s_before = lax.scan(step, s0, chunk_kv)
    inter = jnp.einsum('bhcid,cbhdv->bhciv', qc * q_decay[None, :, None, :, None], s_before)
    return (intra + inter).reshape(bsz, nh, seq, dh), s_final


def retention_mixer(pc, pl, rows, cols, decay_logit, gn_g, gn_b):
    def heads(t):
        return t.astype(jnp.float32).reshape(t.shape[0], t.shape[1], RET_HEADS, HEAD_DIM)

    def swap(t):
        return jnp.transpose(t, (0, 2, 1, 3))

    scale = HEAD_DIM ** -0.5
    qc = swap(heads(pc[0]) * scale)
    kc = swap(heads(pc[1]))
    vc = swap(heads(pc[2]))
    ql = swap(axial_rope(heads(pl[0]), rows, cols) * scale)
    kl = swap(axial_rope(heads(pl[1]), rows, cols))
    vl = swap(heads(pl[2]))
    log_gamma = jax.nn.log_sigmoid(decay_logit.astype(jnp.float32))
    s_zero = jnp.zeros((qc.shape[0], RET_HEADS, HEAD_DIM, HEAD_DIM), jnp.float32)
    outs_c, outs_l = [], []
    for d in range(2):
        rev = d == 1
        oc, s_ctx = retention_chunkwise(time_flip(qc, rev, 2), time_flip(kc, rev, 2), time_flip(vc, rev, 2), log_gamma[d], s_zero)
        ol, _ = retention_chunkwise(time_flip(ql, rev, 2), time_flip(kl, rev, 2), time_flip(vl, rev, 2), log_gamma[d], s_ctx)
        outs_c.append(time_flip(oc, rev, 2))
        outs_l.append(time_flip(ol, rev, 2))
    gain = gn_g.reshape(RET_HEADS, HEAD_DIM)
    bias = gn_b.reshape(RET_HEADS, HEAD_DIM)

    def finish(o, g):
        o = head_norm(swap(o), gain, bias, RET_GN_EPS)
        return o.reshape(g.shape) * jax.nn.silu(g.astype(jnp.float32))

    return finish(outs_c[0] + outs_c[1], pc[3]), finish(outs_l[0] + outs_l[1], pl[3])


def token_shift(z, rev):
    if rev:
        return jnp.pad(z, ((0, 0), (0, 1), (0, 0)))[:, 1:]
    return jnp.pad(z, ((0, 0), (1, 0), (0, 0)))[:, :-1]


def rwkv_heads(t):
    return t.reshape(t.shape[0], t.shape[1], RWKV_HEADS, HEAD_DIM)


def rwkv7_scan(inp, s0, rev):
    xs = tuple(jnp.moveaxis(t, 1, 0) for t in inp)

    def step(s, xt):
        r_t, w_t, k_t, v_t, kk_t, a_t = xt
        s_kk = jnp.einsum('bhvk,bhk->bhv', s, kk_t)
        s = s * w_t[:, :, None, :] - s_kk[..., None] * (kk_t * a_t)[:, :, None, :] + v_t[..., None] * k_t[:, :, None, :]
        return s, jnp.einsum('bhvk,bhk->bhv', s, r_t)

    s_final, y = lax.scan(step, s0, xs, reverse=rev)
    return jnp.moveaxis(y, 0, 1), s_final


def rwkv7_mixer(zc, gdc, zl, gdl, mu, w0, w_up, a0, a_up, k_k, k_a, g_up, r_k, gn_g, gn_b):
    r_k_h = r_k.reshape(RWKV_HEADS, HEAD_DIM)
    gain = gn_g.reshape(RWKV_HEADS, HEAD_DIM)
    bias = gn_b.reshape(RWKV_HEADS, HEAD_DIM)
    bounds = [RWKV_W, 2 * RWKV_W, 3 * RWKV_W, 3 * RWKV_W + DECAY_LORA]

    def direction_inputs(z, d):
        z = z.astype(jnp.float32)
        zd = z + (token_shift(z, d == 1) - z) * mu[d]
        r, k, v, wd, ad = jnp.split(zd, bounds, axis=-1)
        w_log = -jax.nn.softplus(-(w0[d] + jnp.tanh(wd) @ w_up[d])) - 0.5
        decay = jnp.exp(-jnp.exp(w_log))
        a = jax.nn.sigmoid(a0[d] + ad @ a_up[d])
        kk = rwkv_heads(k * k_k[d])
        kk = kk / jnp.maximum(jnp.sqrt(jnp.sum(kk * kk, axis=-1, keepdims=True)), 1e-12)
        k = k * (1.0 + (a - 1.0) * k_a[d])
        return (rwkv_heads(r), rwkv_heads(decay), rwkv_heads(k), rwkv_heads(v), kk, rwkv_heads(a))

    def bonus(inp):
        r, _, k, v, _, _ = inp
        return jnp.sum(r * k * r_k_h, axis=-1, keepdims=True) * v

    ys_c, ys_l, bon_c, bon_l = [], [], [], []
    for d in range(2):
        rev = d == 1
        ic = direction_inputs(zc, d)
        il = direction_inputs(zl, d)
        s0 = jnp.zeros((zc.shape[0], RWKV_HEADS, HEAD_DIM, HEAD_DIM), jnp.float32)
        yc, s_ctx = rwkv7_scan(ic, s0, rev)
        yl, _ = rwkv7_scan(il, s_ctx, rev)
        ys_c.append(yc)
        ys_l.append(yl)
        bon_c.append(bonus(ic))
        bon_l.append(bonus(il))

    def finish(y, bon, gd):
        g = jax.nn.sigmoid(gd.astype(jnp.float32)) @ g_up
        out = head_norm(y, gain, bias, RWKV_GN_EPS) + bon
        return out.reshape(g.shape) * g

    return (finish(ys_c[0] + ys_c[1], bon_c[0] + bon_c[1], gdc),
            finish(ys_l[0] + ys_l[1], bon_l[0] + bon_l[1], gdl))


def directional_conv(u, w, b, rev):
    seq = u.shape[1]
    pad = ((0, 0), (0, LRU_CONV - 1), (0, 0)) if rev else ((0, 0), (LRU_CONV - 1, 0), (0, 0))
    up = jnp.pad(u, pad)
    out = b
    for j in range(LRU_CONV):
        start = (LRU_CONV - 1 - j) if rev else j
        out = out + w[j] * up[:, start:start + seq]
    return out


def block_diag_linear(u, w, b):
    ub = u.reshape(u.shape[0], u.shape[1], LRU_BLOCKS, LRU_BS)
    return jnp.einsum('btni,nio->btno', ub, w).reshape(u.shape) + b


def linear_scan(a, b, h0, rev):
    def combine(lo, hi):
        a_lo, b_lo = lo
        a_hi, b_hi = hi
        return a_lo * a_hi, a_hi * b_lo + b_hi

    a_cum, b_cum = lax.associative_scan(combine, (a, b), reverse=rev, axis=1)
    return b_cum + a_cum * h0[:, None, :]


def rglru_mixer(xc, gc, xl, gl, conv_w, conv_b, wa, ba, wx, bx, lam):
    def direction_terms(u, d):
        u = directional_conv(u.astype(jnp.float32), conv_w[d], conv_b[d], d == 1)
        r = jax.nn.sigmoid(block_diag_linear(u, wa[d], ba[d]))
        i = jax.nn.sigmoid(block_diag_linear(u, wx[d], bx[d]))
        log_a = -LRU_C * r * jax.nn.softplus(-lam[d])
        a = jnp.exp(log_a)
        b = jnp.sqrt(-jnp.expm1(2.0 * log_a)) * (i * u)
        return a, b

    hs_c, hs_l = [], []
    for d in range(2):
        rev = d == 1
        a, b = direction_terms(xc, d)
        h_c = linear_scan(a, b, jnp.zeros((xc.shape[0], LRU_W), jnp.float32), rev)
        h_end = h_c[:, 0] if rev else h_c[:, -1]
        a, b = direction_terms(xl, d)
        h_l = linear_scan(a, b, h_end, rev)
        hs_c.append(h_c)
        hs_l.append(h_l)
    out_c = (hs_c[0] + hs_c[1]) * jax.nn.gelu(gc.astype(jnp.float32))
    out_l = (hs_l[0] + hs_l[1]) * jax.nn.gelu(gl.astype(jnp.float32))
    return out_c, out_l


def moe_ffn(h, w_router, b_router, w1, b1, w2, b2):
    n_tok, dm = h.shape
    logits = (h @ w_router).astype(jnp.float32) + b_router.astype(jnp.float32)
    top_val, top_idx = lax.top_k(logits, TOP_K)
    gates = jax.nn.softmax(top_val, axis=-1)
    n_assign = n_tok * TOP_K
    flat_e = top_idx.reshape(-1)
    flat_tok = jnp.arange(n_assign, dtype=jnp.int32) // TOP_K
    flat_g = gates.reshape(-1)
    order = jnp.argsort(flat_e)
    se, stok, sg = flat_e[order], flat_tok[order], flat_g[order]
    counts = jnp.bincount(flat_e, length=N_EXPERTS)
    padded = (counts + MOE_BLOCK - 1) // MOE_BLOCK * MOE_BLOCK
    start = jnp.cumsum(counts) - counts
    pend = jnp.cumsum(padded)
    pstart = pend - padded
    slot = pstart[se] + (jnp.arange(n_assign, dtype=jnp.int32) - start[se])
    n_blocks = (n_assign + N_EXPERTS * (MOE_BLOCK - 1) + MOE_BLOCK - 1) // MOE_BLOCK
    n_slots = n_blocks * MOE_BLOCK
    slot_tok = jnp.full((n_slots,), n_tok, jnp.int32).at[slot].set(stok)
    slot_gate = jnp.zeros((n_slots,), jnp.float32).at[slot].set(sg)
    block_e = jnp.minimum(jnp.searchsorted(pend, jnp.arange(n_blocks) * MOE_BLOCK, side='right'), N_EXPERTS - 1)
    h_pad = jnp.concatenate([h, jnp.zeros((1, dm), h.dtype)], axis=0)
    hb = h_pad[slot_tok].reshape(n_blocks, MOE_BLOCK, dm)

    def expert_block(args):
        xb, e = args
        gu = xb @ w1[e] + b1[e]
        glu = jnp.minimum(gu[:, :D_EXPERT], SWIGLU_LIMIT)
        lin = jnp.clip(gu[:, D_EXPERT:], -SWIGLU_LIMIT, SWIGLU_LIMIT)
        act = glu * jax.nn.sigmoid(SWIGLU_ALPHA * glu) * (lin + 1)
        return act @ w2[e] + b2[e]

    y = lax.map(expert_block, (hb, block_e)).reshape(n_slots, dm)
    y = y * slot_gate[:, None].astype(y.dtype)
    return jnp.zeros((n_tok + 1, dm), y.dtype).at[slot_tok].add(y)[:n_tok].astype(h.dtype)


def setup_inputs(seed: int = 0) -> dict:
    key = jax.random.key(seed)
    ks = iter(jax.random.split(key, 48))
    L, D = DEPTH, D_MODEL

    def nrm(shape, s):
        return jax.random.normal(next(ks), shape, jnp.float32) * s

    def uni(shape, lo, hi):
        return jax.random.uniform(next(ks), shape, jnp.float32, lo, hi)

    gamma = 1.0 - 2.0 ** (-5.0 - jnp.arange(RET_HEADS, dtype=jnp.float32))
    ret_base = jnp.log(gamma) - jnp.log1p(-gamma)
    lru_a = uni((L, 2, LRU_W), 0.9, 0.999) ** (1.0 / LRU_C)
    return {
        'x': nrm((BATCH, SEQ, D), 1.0),
        'c': nrm((BATCH, D), 1.0),
        'ctx': nrm((BATCH, CTX_LEN, D), 1.0),
        'c_ctx': nrm((D,), 1.0),
        'w_mod': nrm((L, D, 6 * D), 0.5 * D ** -0.5),
        'b_mod': nrm((L, 6 * D), 0.01),
        'norm1_g': 1.0 + nrm((L, D), 0.1),
        'norm2_g': 1.0 + nrm((L, D), 0.1),
        'w_in': nrm((L, D, PROJ_W), D ** -0.5),
        'w_out': nrm((L, MIX_W, D), MIX_W ** -0.5),
        'ret_decay_logit': ret_base + nrm((L, 2, RET_HEADS), 0.1),
        'ret_gn_g': 1.0 + nrm((L, RET_W), 0.05),
        'ret_gn_b': nrm((L, RET_W), 0.01),
        'rwkv_mu': uni((L, 2, RWKV_ZW), 0.0, 1.0),
        'rwkv_w0': uni((L, 2, RWKV_W), -5.5, -0.5),
        'rwkv_w_up': nrm((L, 2, DECAY_LORA, RWKV_W), 0.1),
        'rwkv_a0': nrm((L, 2, RWKV_W), 0.1),
        'rwkv_a_up': nrm((L, 2, ICLR_LORA, RWKV_W), 0.5 * ICLR_LORA ** -0.5),
        'rwkv_k_k': 0.85 + nrm((L, 2, RWKV_W), 0.05),
        'rwkv_k_a': 1.0 + nrm((L, 2, RWKV_W), 0.05),
        'rwkv_g_up': nrm((L, GATE_LORA, RWKV_W), GATE_LORA ** -0.5),
        'rwkv_r_k': nrm((L, RWKV_W), 0.1),
        'rwkv_gn_g': 1.0 + nrm((L, RWKV_W), 0.05),
        'rwkv_gn_b': nrm((L, RWKV_W), 0.01),
        'lru_conv_w': nrm((L, 2, LRU_CONV, LRU_W), LRU_CONV ** -0.5),
        'lru_conv_b': nrm((L, 2, LRU_W), 0.01),
        'lru_wa': nrm((L, 2, LRU_BLOCKS, LRU_BS, LRU_BS), LRU_BS ** -0.5),
        'lru_ba': nrm((L, 2, LRU_W), 0.01),
        'lru_wx': nrm((L, 2, LRU_BLOCKS, LRU_BS, LRU_BS), LRU_BS ** -0.5),
        'lru_bx': nrm((L, 2, LRU_W), 0.01),
        'lru_lambda': jnp.log(lru_a) - jnp.log1p(-lru_a),
        'moe_w_router': nrm((L, D, N_EXPERTS), D ** -0.5),
        'moe_b_router': nrm((L, N_EXPERTS), 0.01),
        'moe_w1': nrm((L, N_EXPERTS, D, 2 * D_EXPERT), D ** -0.5),
        'moe_b1': nrm((L, N_EXPERTS, 2 * D_EXPERT), 0.01),
        'moe_w2': nrm((L, N_EXPERTS, D_EXPERT, D), D_EXPERT ** -0.5),
        'moe_b2': nrm((L, N_EXPERTS, D), 0.01),
        'final_norm_g': 1.0 + nrm((D,), 0.1),
    }


def reference(x, c, ctx, c_ctx, w_mod, b_mod, norm1_g, norm2_g, w_in, w_out,
              ret_decay_logit, ret_gn_g, ret_gn_b,
              rwkv_mu, rwkv_w0, rwkv_w_up, rwkv_a0, rwkv_a_up, rwkv_k_k, rwkv_k_a,
              rwkv_g_up, rwkv_r_k, rwkv_gn_g, rwkv_gn_b,
              lru_conv_w, lru_conv_b, lru_wa, lru_ba, lru_wx, lru_bx, lru_lambda,
              moe_w_router, moe_b_router, moe_w1, moe_b1, moe_w2, moe_b2, final_norm_g):
    bsz, seq, dm = x.shape
    n_rows = seq // GRID_W
    rows = jnp.repeat(jnp.arange(n_rows), GRID_W)
    cols = jnp.arange(n_rows * GRID_W) % GRID_W
    silu_c = jax.nn.silu(c)
    silu_cc = jax.nn.silu(c_ctx)
    for l in range(DEPTH):
        last = l == DEPTH - 1
        mod_l = (silu_c @ w_mod[l] + b_mod[l])[:, None, :]
        mod_c = (silu_cc @ w_mod[l] + b_mod[l])[None, None, :]
        sh1, sc1, g1, sh2, sc2, g2 = jnp.split(mod_l, 6, axis=-1)
        sh1c, sc1c, g1c, sh2c, sc2c, g2c = jnp.split(mod_c, 6, axis=-1)

        hl = modulate(rmsnorm(x, norm1_g[l]), sh1, sc1)
        hc = modulate(rmsnorm(ctx, norm1_g[l]), sh1c, sc1c)
        pl = split_projection(hl @ w_in[l])
        pc = split_projection(hc @ w_in[l])
        ret_c, ret_l = retention_mixer(pc[0:4], pl[0:4], rows, cols,
                                       ret_decay_logit[l], ret_gn_g[l], ret_gn_b[l])
        rw_c, rw_l = rwkv7_mixer(pc[4], pc[5], pl[4], pl[5], rwkv_mu[l], rwkv_w0[l], rwkv_w_up[l],
                                 rwkv_a0[l], rwkv_a_up[l], rwkv_k_k[l], rwkv_k_a[l], rwkv_g_up[l],
                                 rwkv_r_k[l], rwkv_gn_g[l], rwkv_gn_b[l])
        lru_c, lru_l = rglru_mixer(pc[6], pc[7], pl[6], pl[7], lru_conv_w[l], lru_conv_b[l],
                                   lru_wa[l], lru_ba[l], lru_wx[l], lru_bx[l], lru_lambda[l])
        mix_l = jnp.concatenate([ret_l, rw_l, lru_l], axis=-1).astype(x.dtype) @ w_out[l]
        x = x + g1 * mix_l

        moe_args = (moe_w_router[l], moe_b_router[l], moe_w1[l], moe_b1[l], moe_w2[l], moe_b2[l])
        h2l = modulate(rmsnorm(x, norm2_g[l]), sh2, sc2)
        if last:
            y = moe_ffn(h2l.reshape(-1, dm), *moe_args)
            x = x + g2 * y.reshape(x.shape)
        else:
            mix_c = jnp.concatenate([ret_c, rw_c, lru_c], axis=-1).astype(ctx.dtype) @ w_out[l]
            ctx = ctx + g1c * mix_c
            h2c = modulate(rmsnorm(ctx, norm2_g[l]), sh2c, sc2c)
            tokens = jnp.concatenate([h2l.reshape(-1, dm), h2c.reshape(-1, dm)], axis=0)
            y = moe_ffn(tokens, *moe_args)
            n_lat = bsz * seq
            x = x + g2 * y[:n_lat].reshape(x.shape)
            ctx = ctx + g2c * y[n_lat:].reshape(ctx.shape)
    return rmsnorm(x, final_norm_g)
```

```python
import functools

import jax
import jax.numpy as jnp
from jax import lax
from jax.experimental import pallas as pl
from jax.experimental.pallas import tpu as pltpu

F32 = jnp.float32
BF16 = jnp.bfloat16

HEAD_DIM = 64
NORM_EPS = 1e-6
RET_GN_EPS = 1e-5
RWKV_GN_EPS = 64e-5
ROPE_BASE = 10000.0
GRID_W = 64
LRU_CONV = 4
LRU_C = 8.0
TOP_K = 4
SWIGLU_LIMIT = 7.0
SWIGLU_ALPHA = 1.702
DECAY_LORA = 64
ICLR_LORA = 64
GATE_LORA = 128

LANES = 128
TOKEN_TILE = 256
RET_CHUNK = 128
RWKV_CHUNK = 64
LRU_CHUNK = 128
MOE_TILE = 256
VMEM_LIMIT = 56 * 1024 * 1024


def _cparams(*sem):
    return pltpu.CompilerParams(dimension_semantics=sem, vmem_limit_bytes=VMEM_LIMIT)


def _scan_chunk(i, n_ctx, n_tot, rev):
    if not rev:
        return i
    return jnp.where(i < n_ctx, n_ctx - 1 - i, n_tot + n_ctx - 1 - i)


def _split3(a):
    hi = a.astype(BF16)
    r1 = a - hi.astype(F32)
    mid = r1.astype(BF16)
    lo = (r1 - mid.astype(F32)).astype(BF16)
    return hi, mid, lo


def _dot(a, b):
    return jnp.dot(a, b, preferred_element_type=F32)


def _dot_nt(a, b):
    return lax.dot_general(a, b, (((1,), (1,)), ((), ())), preferred_element_type=F32)


def _dot_tn(a, b):
    return lax.dot_general(a, b, (((0,), (0,)), ((), ())), preferred_element_type=F32)


def _dot_exact_rhs(a, b_bf):
    hi, mid, lo = _split3(a)
    return _dot(hi, b_bf) + _dot(mid, b_bf) + _dot(lo, b_bf)


def _dot_exact_lhs(a_bf, b):
    hi, mid, lo = _split3(b)
    return _dot(a_bf, hi) + _dot(a_bf, mid) + _dot(a_bf, lo)


def _dot_x3(a, b):
    a_hi = a.astype(BF16)
    a_lo = (a - a_hi.astype(F32)).astype(BF16)
    b_hi = b.astype(BF16)
    b_lo = (b - b_hi.astype(F32)).astype(BF16)
    return _dot(a_hi, b_hi) + _dot(a_lo, b_hi) + _dot(a_hi, b_lo)


def _sigmoid(x):
    return 1.0 / (1.0 + jnp.exp(-x))


def _softplus(x):
    return jnp.maximum(x, 0.0) + jnp.log(1.0 + jnp.exp(-jnp.abs(x)))


def _mod_kernel(c_ref, w_ref, b_ref, o_ref):
    c = c_ref[...]
    s = c * _sigmoid(c)
    o_ref[...] = _dot_x3(s, w_ref[...]) + b_ref[...]


def _modulation(cond, w_mod, b_mod):
    r, d = cond.shape
    n = w_mod.shape[1]
    tn = d
    return pl.pallas_call(
        _mod_kernel,
        grid=(n // tn,),
        in_specs=[pl.BlockSpec((r, d), lambda j: (0, 0)),
                  pl.BlockSpec((d, tn), lambda j: (0, j)),
                  pl.BlockSpec((1, tn), lambda j: (0, j))],
        out_specs=pl.BlockSpec((r, tn), lambda j: (0, j)),
        out_shape=jax.ShapeDtypeStruct((r, n), F32),
        compiler_params=_cparams("arbitrary"),
        name="modulation",
    )(cond, w_mod, b_mod.reshape(1, n))


def _in_proj_kernel(x_ref, mod_ref, g_ref, w_ref, *o_refs, bounds):
    x = x_ref[0]
    ms = jnp.mean(x * x, axis=-1, keepdims=True)
    h = x * lax.rsqrt(ms + NORM_EPS) * g_ref[...]
    h = h * (1.0 + mod_ref[0, 0, 1:2, :]) + mod_ref[0, 0, 0:1, :]
    hb = h.astype(BF16)
    for o_ref, (lo, hi) in zip(o_refs, bounds):
        o_ref[0] = _dot(hb, w_ref[:, lo:hi])


def _in_proj(xs, mod, norm_g, w_in_bf, bounds, n_ctx_tiles):
    b, t, d = xs.shape
    tm = TOKEN_TILE
    p = w_in_bf.shape[1]
    seg = lambda bi, i: (bi, jnp.where(i >= n_ctx_tiles, 1, 0), 0, 0)
    return pl.pallas_call(
        functools.partial(_in_proj_kernel, bounds=bounds),
        grid=(b, t // tm),
        in_specs=[pl.BlockSpec((1, tm, d), lambda bi, i: (bi, i, 0)),
                  pl.BlockSpec((1, 1, 2, d), seg),
                  pl.BlockSpec((1, d), lambda bi, i: (0, 0)),
                  pl.BlockSpec((d, p), lambda bi, i: (0, 0))],
        out_specs=[pl.BlockSpec((1, tm, hi - lo), lambda bi, i: (bi, i, 0)) for lo, hi in bounds],
        out_shape=[jax.ShapeDtypeStruct((b, t, hi - lo), F32) for lo, hi in bounds],
        compiler_params=_cparams("parallel", "parallel"),
        name="in_proj",
    )(xs, mod, norm_g.reshape(1, d), w_in_bf)


def _ret_kernel(q_ref, k_ref, v_ref, cos_ref, sin_ref, dq_ref, dk_ref, dmat_ref, gm_ref, bm_ref,
                o_ref, s_ref, *, n_heads):
    i = pl.program_id(1)

    @pl.when(i == 0)
    def _():
        s_ref[...] = jnp.zeros_like(s_ref)

    c, w = q_ref.shape[1], q_ref.shape[2]
    cos = cos_ref[...]
    sin = sin_ref[...]
    lane = lax.broadcasted_iota(jnp.int32, (c, LANES), 1)
    first = (lane % 32) < 16

    def rope(u):
        parts = []
        for j in range(w // LANES):
            uj = u[:, j * LANES:(j + 1) * LANES]
            nxt = pltpu.roll(uj, LANES - 16, axis=1)
            prv = pltpu.roll(uj, 16, axis=1)
            parts.append(jnp.where(first, nxt, prv))
        return u * cos + jnp.concatenate(parts, axis=1) * sin

    q = rope(q_ref[0])
    k = rope(k_ref[0])
    vb = v_ref[0].astype(BF16)
    kb = k.astype(BF16)
    s = s_ref[...]

    out = _dot((q * dq_ref[...]).astype(BF16), s.astype(BF16))

    head = lax.broadcasted_iota(jnp.int32, (c, w), 1) // HEAD_DIM
    q_stack = jnp.concatenate([jnp.where(head == h, q, 0.0) for h in range(n_heads)], axis=0).astype(BF16)
    sc = _dot_nt(q_stack, kb) * dmat_ref[...]
    for h in range(n_heads):
        oh = _dot(sc[h * c:(h + 1) * c].astype(BF16), vb)
        out = out + jnp.where(head == h, oh, 0.0)
    o_ref[0] = out

    ktv = _dot_tn((k * dk_ref[...]).astype(BF16), vb)
    s_ref[...] = gm_ref[...] * s + bm_ref[...] * ktv


def _retention(p_ret, cos_t, sin_t, tabs, n_ctx, rev):
    b, t, w4 = p_ret.shape
    w = w4 // 4
    c = RET_CHUNK
    n_tot = t // c
    n_heads = w // HEAD_DIM
    dq, dk, dmat, gm, bm = tabs
    tix = lambda i: _scan_chunk(i, n_ctx, n_tot, rev)
    col = lambda j: (lambda bi, i: (bi, tix(i), j))
    const = lambda bi, i: (0, 0)
    return pl.pallas_call(
        functools.partial(_ret_kernel, n_heads=n_heads),
        grid=(b, n_tot),
        in_specs=[pl.BlockSpec((1, c, w), col(0)), pl.BlockSpec((1, c, w), col(1)), pl.BlockSpec((1, c, w), col(2)),
                  pl.BlockSpec((c, w), lambda bi, i: (tix(i), 0)),
                  pl.BlockSpec((c, w), lambda bi, i: (tix(i), 0)),
                  pl.BlockSpec((c, w), const), pl.BlockSpec((c, w), const),
                  pl.BlockSpec((n_heads * c, c), const),
                  pl.BlockSpec((w, w), const), pl.BlockSpec((w, w), const)],
        out_specs=pl.BlockSpec((1, c, w), lambda bi, i: (bi, tix(i), 0)),
        out_shape=jax.ShapeDtypeStruct((b, t, w), F32),
        scratch_shapes=[pltpu.VMEM((w, w), F32)],
        compiler_params=_cparams("parallel", "arbitrary"),
        name="retention_rev" if rev else "retention_fwd",
    )(p_ret, p_ret, p_ret, cos_t, sin_t, dq, dk, dmat, gm, bm)


def _ret_tables(decay_logit, w, rev):
    n_heads = w // HEAD_DIM
    c = RET_CHUNK
    lg = jax.nn.log_sigmoid(decay_logit.astype(F32))
    t = jnp.arange(c, dtype=F32)
    p = (c - 1.0 - t) if rev else t
    rel = p[:, None] - p[None, :]
    scale = HEAD_DIM ** -0.5
    dmat = jnp.where(rel >= 0, jnp.exp(lg[:, None, None] * jnp.maximum(rel, 0.0)), 0.0) * scale
    dq = jnp.exp(lg[:, None] * (p + 1.0)) * scale
    dk = jnp.exp(lg[:, None] * (c - 1.0 - p))
    lanes = lambda a: jnp.repeat(a.T, HEAD_DIM, axis=1)
    hid = jnp.arange(w) // HEAD_DIM
    bm = (hid[:, None] == hid[None, :]).astype(F32)
    gm = bm * jnp.exp(lg * c)[hid][:, None]
    return lanes(dq), lanes(dk), dmat.reshape(n_heads * c, c), gm, bm


def _rope_tables(n_ctx_tok, seq, w):
    half = HEAD_DIM // 2
    quarter = half // 2
    inv_freq = ROPE_BASE ** (-jnp.arange(quarter, dtype=F32) / quarter)
    tok = jnp.arange(seq)
    rows = (tok // GRID_W).astype(F32)
    cols = (tok % GRID_W).astype(F32)
    o = jnp.arange(w) % HEAD_DIM
    pos = jnp.where(o[None, :] < half, rows[:, None], cols[:, None])
    ang = pos * inv_freq[o % quarter][None, :]
    sign = jnp.where((o % half) < quarter, -1.0, 1.0)[None, :]
    cos = jnp.concatenate([jnp.ones((n_ctx_tok, w), F32), jnp.cos(ang)], axis=0)
    sin = jnp.concatenate([jnp.zeros((n_ctx_tok, w), F32), jnp.sin(ang) * sign], axis=0)
    return cos, sin


def _rwkv_kernel(z_ref, mu_ref, vec_ref, wup_ref, aup_ref, e_ref, minc_ref, strict_ref, incl_ref,
                 y_ref, bon_ref, st_ref, zprev_ref, *, rev, n_ctx, w):
    i = pl.program_id(1)
    c = z_ref.shape[1]
    zw = z_ref.shape[2]

    @pl.when(i == 0)
    def _():
        st_ref[...] = jnp.zeros_like(st_ref)

    @pl.when((i == 0) | (i == n_ctx))
    def _():
        zprev_ref[...] = jnp.zeros_like(zprev_ref)

    z = z_ref[0]
    row = lax.broadcasted_iota(jnp.int32, (c, zw), 0)
    prev = jnp.broadcast_to(zprev_ref[0:1, :], (c, zw))
    if rev:
        zs = jnp.where(row == c - 1, prev, pltpu.roll(z, c - 1, axis=0))
        zprev_ref[0:1, :] = z[0:1, :]
    else:
        zs = jnp.where(row == 0, prev, pltpu.roll(z, 1, axis=0))
        zprev_ref[0:1, :] = z[c - 1:c, :]
    zd = z + (zs - z) * mu_ref[...]

    r = zd[:, 0:w]
    k = zd[:, w:2 * w]
    v = zd[:, 2 * w:3 * w]
    lora = zd[:, 3 * w:3 * w + LANES]
    lane = lax.broadcasted_iota(jnp.int32, (c, LANES), 1)
    lora = jnp.where(lane < DECAY_LORA, jnp.tanh(lora), lora)

    w0, a0, k_k, k_a, r_k = (vec_ref[j:j + 1, :] for j in range(5))
    e_bf = e_ref[...]
    w_log = -_softplus(-(w0 + _dot_x3(lora, wup_ref[...]))) - 0.5
    logw = -jnp.exp(w_log)
    a = _sigmoid(a0 + _dot_x3(lora, aup_ref[...]))
    kk0 = k * k_k
    kk = kk0 / jnp.maximum(jnp.sqrt(_dot_exact_rhs(kk0 * kk0, e_bf)), 1e-12)
    k2 = k * (1.0 + (a - 1.0) * k_a)
    bon_ref[0] = _dot((r * k2 * r_k).astype(BF16), e_bf) * v

    cinc = _dot_exact_lhs(minc_ref[...], logw)
    e_inc = jnp.exp(cinc)
    e_neg = jnp.exp(-cinc)
    rt = r * e_inc
    kt = k2 * e_neg
    bt = kk * a * e_neg
    kkt = kk * jnp.exp(cinc - logw)
    last = 0 if rev else c - 1
    w_end = e_inc[last:last + 1, :]

    strict = strict_ref[...] > 0.0
    incl = incl_ref[...] > 0.0
    lane_lo = lane < HEAD_DIM

    def stack(xw):
        return jnp.concatenate([jnp.where(lane_lo, xw, 0.0), jnp.where(lane_lo, 0.0, xw)], axis=0)

    def unstack(xs):
        return xs[0:c] + xs[c:2 * c]

    ys = []
    for j in range(w // LANES):
        sl = slice(j * LANES, (j + 1) * LANES)
        n_st = st_ref[j]
        kkt_w, rt_w, kt_w, bt_w, v_w = kkt[:, sl], rt[:, sl], kt[:, sl], bt[:, sl], v[:, sl]
        lhs = jnp.concatenate([stack(kkt_w), stack(rt_w)], axis=0).astype(BF16)
        rhs = jnp.concatenate([stack(bt_w), stack(kt_w)], axis=0).astype(BF16)
        g = _dot_nt(lhs, rhs)
        a_b = jnp.where(strict, g[0:2 * c, 0:2 * c], 0.0)
        a_k = jnp.where(strict, g[0:2 * c, 2 * c:4 * c], 0.0)
        r_b = jnp.where(incl, g[2 * c:4 * c, 0:2 * c], 0.0)
        r_k = jnp.where(incl, g[2 * c:4 * c, 2 * c:4 * c], 0.0)

        x0 = _dot_nt(jnp.concatenate([kkt_w, rt_w], axis=0).astype(BF16), n_st.astype(BF16))
        v_s = stack(v_w)
        v_sb = v_s.astype(BF16)
        u = stack(x0[0:c]) + _dot(a_k.astype(BF16), v_sb)

        pw = a_b
        u = u - _dot(pw.astype(BF16), u.astype(BF16))
        steps = 1
        while 2 * steps < c:
            steps *= 2
            pw_b = pw.astype(BF16)
            pw = _dot(pw_b, pw_b)
            u = u + _dot(pw.astype(BF16), u.astype(BF16))

        u_b = u.astype(BF16)
        y_s = _dot(jnp.concatenate([r_k, -r_b], axis=1).astype(BF16), jnp.concatenate([v_sb, u_b], axis=0))
        ys.append(x0[c:2 * c] + unstack(y_s))

        u_w = unstack(u)
        upd = _dot_tn(jnp.concatenate([v_w, u_w], axis=0).astype(BF16),
                      jnp.concatenate([kt_w, -bt_w], axis=0).astype(BF16))
        head_r = lax.broadcasted_iota(jnp.int32, (LANES, LANES), 0) // HEAD_DIM
        head_c = lax.broadcasted_iota(jnp.int32, (LANES, LANES), 1) // HEAD_DIM
        st_ref[j] = jnp.where(head_r == head_c, (n_st + upd) * w_end[:, sl], 0.0)
    y_ref[0] = jnp.concatenate(ys, axis=1)


def _rwkv(p_z, prm, n_ctx, rev):
    b, t, zw = p_z.shape
    w = (zw - DECAY_LORA - ICLR_LORA) // 3
    c = RWKV_CHUNK
    n_tot = t // c
    mu, vecs, wup, aup, e_bf, minc, strict, incl = prm
    tix = lambda i: _scan_chunk(i, n_ctx, n_tot, rev)
    const = lambda bi, i: (0, 0)
    full = lambda a: pl.BlockSpec(a.shape, const)
    return pl.pallas_call(
        functools.partial(_rwkv_kernel, rev=rev, n_ctx=n_ctx, w=w),
        grid=(b, n_tot),
        in_specs=[pl.BlockSpec((1, c, zw), lambda bi, i: (bi, tix(i), 0)),
                  full(mu), full(vecs), full(wup), full(aup), full(e_bf), full(minc), full(strict), full(incl)],
        out_specs=[pl.BlockSpec((1, c, w), lambda bi, i: (bi, tix(i), 0)),
                   pl.BlockSpec((1, c, w), lambda bi, i: (bi, tix(i), 0))],
        out_shape=[jax.ShapeDtypeStruct((b, t, w), F32), jax.ShapeDtypeStruct((b, t, w), F32)],
        scratch_shapes=[pltpu.VMEM((w // LANES, LANES, LANES), F32), pltpu.VMEM((8, zw), F32)],
        compiler_params=_cparams("parallel", "arbitrary"),
        name="rwkv7_rev" if rev else "rwkv7_fwd",
    )(p_z, mu, vecs, wup, aup, e_bf, minc, strict, incl)


def _rwkv_params(mu, w0, w_up, a0, a_up, k_k, k_a, r_k, rev):
    w = w0.shape[0]
    c = RWKV_CHUNK
    vecs = jnp.concatenate([jnp.stack([w0, a0, k_k, k_a, r_k]), jnp.zeros((3, w), F32)], axis=0)
    wup = jnp.concatenate([w_up, jnp.zeros((ICLR_LORA, w), F32)], axis=0)
    aup = jnp.concatenate([jnp.zeros((DECAY_LORA, w), F32), a_up], axis=0)
    hid = jnp.arange(w) // HEAD_DIM
    e_bf = (hid[:, None] == hid[None, :]).astype(BF16)
    t = jnp.arange(c)
    p = (c - 1 - t) if rev else t
    le = p[None, :] <= p[:, None]
    lt = p[None, :] < p[:, None]
    blk = jnp.arange(2 * c) // c
    same = blk[:, None] == blk[None, :]
    strict = (jnp.tile(lt, (2, 2)) & same).astype(F32)
    incl = (jnp.tile(le, (2, 2)) & same).astype(F32)
    return mu.reshape(1, -1), vecs, wup, aup, e_bf, le.astype(BF16), strict, incl


def _lru_kernel(x_ref, cw_ref, vec_ref, wa_ref, wx_ref, h_ref, hcar_ref, ucar_ref, *, rev, n_ctx):
    i = pl.program_id(1)
    c, w = x_ref.shape[1], x_ref.shape[2]

    @pl.when(i == 0)
    def _():
        hcar_ref[...] = jnp.zeros_like(hcar_ref)

    @pl.when((i == 0) | (i == n_ctx))
    def _():
        ucar_ref[...] = jnp.zeros_like(ucar_ref)

    u0 = x_ref[0]
    row = lax.broadcasted_iota(jnp.int32, (c, w), 0)

    def shifted(x, s, carry, fill):
        if rev:
            rolled = pltpu.roll(x, c - s, axis=0)
            edge = row >= c - s
        else:
            rolled = pltpu.roll(x, s, axis=0)
            edge = row < s
        if carry is None:
            return jnp.where(edge, fill, rolled)
        return jnp.where(edge, carry, rolled)

    conv = vec_ref[0:1, :] + cw_ref[LRU_CONV - 1:LRU_CONV, :] * u0
    for m in range(1, LRU_CONV):
        car = jnp.zeros((c, w), F32)
        for qpos in range(m):
            r_idx = (c - 1 - qpos) if rev else qpos
            car = jnp.where(row == r_idx, jnp.broadcast_to(ucar_ref[m - qpos - 1:m - qpos, :], (c, w)), car)
        conv = conv + cw_ref[LRU_CONV - 1 - m:LRU_CONV - m, :] * shifted(u0, m, car, None)
    for m in range(1, LRU_CONV):
        r_idx = (m - 1) if rev else (c - m)
        ucar_ref[m - 1:m, :] = u0[r_idx:r_idx + 1, :]

    cb = conv.astype(BF16)
    r = _sigmoid(_dot(cb, wa_ref[...]) + vec_ref[1:2, :])
    ig = _sigmoid(_dot(cb, wx_ref[...]) + vec_ref[2:3, :])
    log_a = -LRU_C * r * vec_ref[3:4, :]
    a = jnp.exp(log_a)
    bb = jnp.sqrt(1.0 - jnp.exp(2.0 * log_a)) * (ig * conv)

    s = 1
    while s < c:
        bb = bb + a * shifted(bb, s, None, 0.0)
        a = a * shifted(a, s, None, 1.0)
        s *= 2
    h = bb + a * hcar_ref[0:1, :]
    h_ref[0] = h
    last = 0 if rev else c - 1
    hcar_ref[0:1, :] = h[last:last + 1, :]


def _lru(p_lru, prm, n_ctx, rev):
    b, t, w2 = p_lru.shape
    w = w2 // 2
    c = LRU_CHUNK
    n_tot = t // c
    cw, vecs, wa, wx = prm
    tix = lambda i: _scan_chunk(i, n_ctx, n_tot, rev)
    const = lambda bi, i: (0, 0)
    return pl.pallas_call(
        functools.partial(_lru_kernel, rev=rev, n_ctx=n_ctx),
        grid=(b, n_tot),
        in_specs=[pl.BlockSpec((1, c, w), lambda bi, i: (bi, tix(i), 0)),
                  pl.BlockSpec(cw.shape, const), pl.BlockSpec(vecs.shape, const),
                  pl.BlockSpec(wa.shape, const), pl.BlockSpec(wx.shape, const)],
        out_specs=pl.BlockSpec((1, c, w), lambda bi, i: (bi, tix(i), 0)),
        out_shape=jax.ShapeDtypeStruct((b, t, w), F32),
        scratch_shapes=[pltpu.VMEM((8, w), F32), pltpu.VMEM((8, w), F32)],
        compiler_params=_cparams("parallel", "arbitrary"),
        name="rglru_rev" if rev else "rglru_fwd",
    )(p_lru, cw, vecs, wa, wx)


def _lru_params(conv_w, conv_b, wa, ba, wx, bx, lam):
    w = conv_b.shape[0]
    cw = jnp.concatenate([conv_w, jnp.zeros((8 - LRU_CONV, w), F32)], axis=0)
    vecs = jnp.concatenate([jnp.stack([conv_b, ba, bx, jax.nn.softplus(-lam)]), jnp.zeros((4, w), F32)], axis=0)
    return cw, vecs, jax.scipy.linalg.block_diag(*wa).astype(BF16), jax.scipy.linalg.block_diag(*wx).astype(BF16)


def _head_norm(y, e_bf, gain, bias, eps):
    inv = 1.0 / HEAD_DIM
    mu = _dot(y.astype(BF16), e_bf) * inv
    yc = y - mu
    var = _dot((yc * yc).astype(BF16), e_bf) * inv
    return yc * lax.rsqrt(var + eps) * gain + bias


def _mix_out_kernel(x_ref, mod_ref, of_ref, ob_ref, g_ref, yf_ref, yb_ref, bf_ref, bb_ref, gd_ref,
                    hf_ref, hb_ref, lg_ref, gn_ref, e_ref, gup_ref, wout_ref, n2g_ref, wr_ref, br_ref,
                    xo_ref, h2_ref, lo_ref, *, w_ret, w_rw):
    e_bf = e_ref[...]
    g = g_ref[0]
    ret = _head_norm(of_ref[0] + ob_ref[0], e_bf, gn_ref[0:1, :], gn_ref[1:2, :], RET_GN_EPS)
    ret = ret * (g * _sigmoid(g))
    gate = _dot(_sigmoid(gd_ref[0]).astype(BF16), gup_ref[...])
    rw = _head_norm(yf_ref[0] + yb_ref[0], e_bf, gn_ref[2:3, :], gn_ref[3:4, :], RWKV_GN_EPS)
    rw = (rw + bf_ref[0] + bb_ref[0]) * gate
    lg = lg_ref[0]
    gelu = 0.5 * lg * (1.0 + jnp.tanh(0.7978845608028654 * (lg + 0.044715 * (lg * lg * lg))))
    lru = (hf_ref[0] + hb_ref[0]) * gelu
    mix = (_dot(ret.astype(BF16), wout_ref[0:w_ret, :])
           + _dot(rw.astype(BF16), wout_ref[w_ret:w_ret + w_rw, :])
           + _dot(lru.astype(BF16), wout_ref[w_ret + w_rw:, :]))
    x = x_ref[0] + mod_ref[0, 0, 0:1, :] * mix
    xo_ref[0] = x
    ms = jnp.mean(x * x, axis=-1, keepdims=True)
    h2 = x * lax.rsqrt(ms + NORM_EPS) * n2g_ref[...]
    h2 = h2 * (1.0 + mod_ref[0, 0, 2:3, :]) + mod_ref[0, 0, 1:2, :]
    h2_ref[0] = h2.astype(BF16)
    lo_ref[0] = _dot_x3(h2, wr_ref[...]) + br_ref[...]


def _mix_out(xs, mod, o_f, o_b, p_ret, y_f, y_b, bon_f, bon_b, p_gd, h_f, h_b, p_lru,
             gn, e_bf, g_up_bf, w_out_bf, norm2_g, w_router, b_router, n_ctx_tiles):
    b, t, d = xs.shape
    tm = TOKEN_TILE
    w_ret, w_rw, w_lru = o_f.shape[2], y_f.shape[2], h_f.shape[2]
    ne = w_router.shape[1]
    tok = lambda wd, j=0: pl.BlockSpec((1, tm, wd), lambda bi, i: (bi, i, j))
    const = lambda a: pl.BlockSpec(a.shape, lambda bi, i: (0,) * a.ndim)
    seg = lambda bi, i: (bi, jnp.where(i >= n_ctx_tiles, 1, 0), 0, 0)
    n2g = norm2_g.reshape(1, d)
    br = b_router.reshape(1, ne)
    return pl.pallas_call(
        functools.partial(_mix_out_kernel, w_ret=w_ret, w_rw=w_rw),
        grid=(b, t // tm),
        in_specs=[tok(d), pl.BlockSpec((1, 1, 3, d), seg),
                  tok(w_ret), tok(w_ret), tok(w_ret, 3),
                  tok(w_rw), tok(w_rw), tok(w_rw), tok(w_rw), tok(GATE_LORA),
                  tok(w_lru), tok(w_lru), tok(w_lru, 1),
                  const(gn), const(e_bf), const(g_up_bf), const(w_out_bf), const(n2g), const(w_router), const(br)],
        out_specs=[tok(d), tok(d), tok(ne)],
        out_shape=[jax.ShapeDtypeStruct((b, t, d), F32), jax.ShapeDtypeStruct((b, t, d), BF16),
                   jax.ShapeDtypeStruct((b, t, ne), F32)],
        compiler_params=_cparams("parallel", "parallel"),
        name="mix_out",
    )(xs, mod, o_f, o_b, p_ret, y_f, y_b, bon_f, bon_b, p_gd, h_f, h_b, p_lru,
      gn, e_bf, g_up_bf, w_out_bf, n2g, w_router, br)


def _moe_kernel(be_ref, nu_ref, x_ref, w1_ref, b1_ref, w2_ref, b2_ref, gate_ref, y_ref):
    i = pl.program_id(0)
    de = w2_ref.shape[1]

    @pl.when(i < nu_ref[0])
    def _():
        gu = _dot(x_ref[...], w1_ref[0]) + b1_ref[0]
        glu = jnp.minimum(gu[:, :de], SWIGLU_LIMIT)
        lin = jnp.clip(gu[:, de:], -SWIGLU_LIMIT, SWIGLU_LIMIT)
        act = glu * _sigmoid(SWIGLU_ALPHA * glu) * (lin + 1.0)
        y = _dot(act.astype(BF16), w2_ref[0]) + b2_ref[0]
        y_ref[...] = y * gate_ref[...]

    @pl.when(i >= nu_ref[0])
    def _():
        y_ref[...] = jnp.zeros_like(y_ref)


def _moe_ffn(hb, block_e, n_used, slot_gate, w1_bf, b1, w2_bf, b2):
    n_slots, d = hb.shape
    tm = MOE_TILE
    ne, _, d2 = w1_bf.shape
    de = w2_bf.shape[1]
    return pl.pallas_call(
        _moe_kernel,
        grid_spec=pltpu.PrefetchScalarGridSpec(
            num_scalar_prefetch=2,
            grid=(n_slots // tm,),
            in_specs=[pl.BlockSpec((tm, d), lambda i, be, nu: (i, 0)),
                      pl.BlockSpec((1, d, d2), lambda i, be, nu: (be[i], 0, 0)),
                      pl.BlockSpec((1, 1, d2), lambda i, be, nu: (be[i], 0, 0)),
                      pl.BlockSpec((1, de, d), lambda i, be, nu: (be[i], 0, 0)),
                      pl.BlockSpec((1, 1, d), lambda i, be, nu: (be[i], 0, 0)),
                      pl.BlockSpec((tm, 1), lambda i, be, nu: (i, 0))],
            out_specs=pl.BlockSpec((tm, d), lambda i, be, nu: (i, 0)),
        ),
        out_shape=jax.ShapeDtypeStruct((n_slots, d), F32),
        compiler_params=_cparams("arbitrary"),
        name="moe_ffn",
    )(block_e, n_used, hb, w1_bf, b1.reshape(ne, 1, d2), w2_bf, b2.reshape(ne, 1, d), slot_gate.reshape(n_slots, 1))


def _route(logits, n_experts):
    n_tok = logits.shape[0]
    tm = MOE_TILE
    top_val, top_idx = lax.top_k(logits, TOP_K)
    gates = jax.nn.softmax(top_val, axis=-1)
    n_assign = n_tok * TOP_K
    flat_e = top_idx.reshape(-1)
    order = jnp.argsort(flat_e)
    se = flat_e[order]
    counts = jnp.bincount(flat_e, length=n_experts)
    padded = (counts + tm - 1) // tm * tm
    start = jnp.cumsum(counts) - counts
    pend = jnp.cumsum(padded)
    pstart = pend - padded
    slot = (pstart[se] + (jnp.arange(n_assign, dtype=jnp.int32) - start[se])).astype(jnp.int32)
    n_blocks = (n_assign + n_experts * (tm - 1) + tm - 1) // tm
    n_slots = n_blocks * tm
    slot_tok = jnp.full((n_slots,), n_tok, jnp.int32).at[slot].set((order // TOP_K).astype(jnp.int32))
    slot_gate = jnp.zeros((n_slots,), F32).at[slot].set(gates.reshape(-1)[order])
    block_e = jnp.minimum(jnp.searchsorted(pend, jnp.arange(n_blocks) * tm, side='right'), n_experts - 1)
    slot_flat = jnp.zeros((n_assign,), jnp.int32).at[order].set(slot)
    n_used = (pend[-1] // tm).astype(jnp.int32).reshape(1)
    return slot_tok, slot_gate, block_e.astype(jnp.int32), n_used, slot_flat


def _combine_kernel(x_ref, mod_ref, y_ref, g_ref, o_ref, *, final):
    y = y_ref[0, 0] + y_ref[1, 0] + y_ref[2, 0] + y_ref[3, 0]
    x = x_ref[0] + mod_ref[0, 0, 0:1, :] * y
    if final:
        ms = jnp.mean(x * x, axis=-1, keepdims=True)
        x = x * lax.rsqrt(ms + NORM_EPS) * g_ref[...]
    o_ref[0] = x


def _combine(xs, mod, yg, final_g, n_ctx_tiles, final):
    b, t, d = xs.shape
    tm = TOKEN_TILE
    seg = lambda bi, i: (bi, jnp.where(i >= n_ctx_tiles, 1, 0), 0, 0)
    return pl.pallas_call(
        functools.partial(_combine_kernel, final=final),
        grid=(b, t // tm),
        in_specs=[pl.BlockSpec((1, tm, d), lambda bi, i: (bi, i, 0)),
                  pl.BlockSpec((1, 1, 1, d), seg),
                  pl.BlockSpec((TOP_K, 1, tm, d), lambda bi, i: (0, bi, i, 0)),
                  pl.BlockSpec((1, d), lambda bi, i: (0, 0))],
        out_specs=pl.BlockSpec((1, tm, d), lambda bi, i: (bi, i, 0)),
        out_shape=jax.ShapeDtypeStruct((b, t, d), F32),
        compiler_params=_cparams("parallel", "parallel"),
        name="combine_final" if final else "combine",
    )(xs, mod, yg, final_g.reshape(1, d))


def kernel(x, c, ctx, c_ctx, w_mod, b_mod, norm1_g, norm2_g, w_in, w_out, ret_decay_logit, ret_gn_g, ret_gn_b, rwkv_mu, rwkv_w0, rwkv_w_up, rwkv_a0, rwkv_a_up, rwkv_k_k, rwkv_k_a, rwkv_g_up, rwkv_r_k, rwkv_gn_g, rwkv_gn_b, lru_conv_w, lru_conv_b, lru_wa, lru_ba, lru_wx, lru_bx, lru_lambda, moe_w_router, moe_b_router, moe_w1, moe_b1, moe_w2, moe_b2, final_norm_g):
    bsz, seq, dm = x.shape
    n_ctx_tok = ctx.shape[1]
    depth = w_in.shape[0]
    n_experts = moe_w_router.shape[2]
    w_ret = 3 * dm // 8
    w_rw = 3 * dm // 8
    w_lru = dm - w_ret - w_rw
    zw = 3 * w_rw + DECAY_LORA + ICLR_LORA
    sizes = (4 * w_ret, zw, GATE_LORA, 2 * w_lru)
    bounds, off = [], 0
    for s in sizes:
        bounds.append((off, off + s))
        off += s
    bounds = tuple(bounds)
    assert off == w_in.shape[2]
    assert n_ctx_tok % TOKEN_TILE == 0 and seq % TOKEN_TILE == 0 and seq % GRID_W == 0
    t_all = n_ctx_tok + seq
    n_ctx_tiles = n_ctx_tok // TOKEN_TILE

    xs = jnp.concatenate([ctx, x], axis=1)
    cos_t, sin_t = _rope_tables(n_ctx_tok, seq, w_ret)
    hid = jnp.arange(w_ret) // HEAD_DIM
    e_bf = (hid[:, None] == hid[None, :]).astype(BF16)
    cond = jnp.concatenate([c, c_ctx[None, :], jnp.zeros((8 - (bsz + 1) % 8, dm), F32)], axis=0)

    for l in range(depth):
        last = l == depth - 1
        mod = _modulation(cond, w_mod[l], b_mod[l])
        mod_l = mod[:bsz].reshape(bsz, 6, dm)
        mod_c = jnp.broadcast_to(mod[bsz].reshape(1, 6, dm), (bsz, 6, dm))
        modsel = jnp.stack([mod_c, mod_l], axis=1)

        p_ret, p_z, p_gd, p_lru = _in_proj(xs, modsel[:, :, 0:2], norm1_g[l], w_in[l].astype(BF16), bounds, n_ctx_tiles)

        ret_o, rw_y, rw_bon, lru_h = [], [], [], []
        for d in range(2):
            rev = d == 1
            ret_o.append(_retention(p_ret, cos_t, sin_t, _ret_tables(ret_decay_logit[l, d], w_ret, rev),
                                    n_ctx_tok // RET_CHUNK, rev))
            prm = _rwkv_params(rwkv_mu[l, d], rwkv_w0[l, d], rwkv_w_up[l, d], rwkv_a0[l, d], rwkv_a_up[l, d],
                               rwkv_k_k[l, d], rwkv_k_a[l, d], rwkv_r_k[l], rev)
            y, bon = _rwkv(p_z, prm, n_ctx_tok // RWKV_CHUNK, rev)
            rw_y.append(y)
            rw_bon.append(bon)
            lru_h.append(_lru(p_lru, _lru_params(lru_conv_w[l, d], lru_conv_b[l, d], lru_wa[l, d], lru_ba[l, d],
                                                 lru_wx[l, d], lru_bx[l, d], lru_lambda[l, d]),
                              n_ctx_tok // LRU_CHUNK, rev))

        gn = jnp.concatenate([jnp.stack([ret_gn_g[l], ret_gn_b[l], rwkv_gn_g[l], rwkv_gn_b[l]]),
                              jnp.zeros((4, w_ret), F32)], axis=0)
        xs, h2, logits = _mix_out(xs, modsel[:, :, 2:5], ret_o[0], ret_o[1], p_ret, rw_y[0], rw_y[1],
                                  rw_bon[0], rw_bon[1], p_gd, lru_h[0], lru_h[1], p_lru,
                                  gn, e_bf, rwkv_g_up[l].astype(BF16), w_out[l].astype(BF16), norm2_g[l],
                                  moe_w_router[l], moe_b_router[l], n_ctx_tiles)

        n_tok = bsz * t_all
        slot_tok, slot_gate, block_e, n_used, slot_flat = _route(logits.reshape(n_tok, n_experts), n_experts)
        h_pad = jnp.concatenate([h2.reshape(n_tok, dm), jnp.zeros((1, dm), BF16)], axis=0)
        y_sorted = _moe_ffn(h_pad[slot_tok], block_e, n_used, slot_gate,
                            moe_w1[l].astype(BF16), moe_b1[l], moe_w2[l].astype(BF16), moe_b2[l])
        yg = y_sorted[slot_flat.reshape(n_tok, TOP_K).T].reshape(TOP_K, bsz, t_all, dm)
        xs = _combine(xs, modsel[:, :, 5:6], yg, final_norm_g, n_ctx_tiles, last)
    return xs[:, n_ctx_tok:]
```

```python
import functools

import jax
import jax.numpy as jnp
from jax import lax
from jax.experimental import pallas as pl
from jax.experimental.pallas import tpu as pltpu

F32 = jnp.float32
BF16 = jnp.bfloat16

HEAD_DIM = 64
NORM_EPS = 1e-6
RET_GN_EPS = 1e-5
RWKV_GN_EPS = 64e-5
ROPE_BASE = 10000.0
GRID_W = 64
LRU_CONV = 4
LRU_C = 8.0
TOP_K = 4
SWIGLU_LIMIT = 7.0
SWIGLU_ALPHA = 1.702
DECAY_LORA = 64
ICLR_LORA = 64
GATE_LORA = 128

LANES = 128
TOKEN_TILE = 256
RET_CHUNK = 128
RWKV_CHUNK = 64
LRU_CHUNK = 128
MOE_TILE = 256
VMEM_LIMIT = 56 * 1024 * 1024


def _cparams(*sem):
    return pltpu.CompilerParams(dimension_semantics=sem, vmem_limit_bytes=VMEM_LIMIT)


def _scan_chunk(i, n_ctx, n_tot, rev):
    if not rev:
        return i
    return jnp.where(i < n_ctx, n_ctx - 1 - i, n_tot + n_ctx - 1 - i)


def _split3(a):
    hi = a.astype(BF16)
    r1 = a - hi.astype(F32)
    mid = r1.astype(BF16)
    lo = (r1 - mid.astype(F32)).astype(BF16)
    return hi, mid, lo


def _dot(a, b):
    return jnp.dot(a, b, preferred_element_type=F32)


def _dot_nt(a, b):
    return lax.dot_general(a, b, (((1,), (1,)), ((), ())), preferred_element_type=F32)


def _dot_tn(a, b):
    return lax.dot_general(a, b, (((0,), (0,)), ((), ())), preferred_element_type=F32)


def _dot_exact_rhs(a, b_bf):
    hi, mid, lo = _split3(a)
    return _dot(hi, b_bf) + _dot(mid, b_bf) + _dot(lo, b_bf)


def _dot_exact_lhs(a_bf, b):
    hi, mid, lo = _split3(b)
    return _dot(a_bf, hi) + _dot(a_bf, mid) + _dot(a_bf, lo)


def _dot_x3(a, b):
    a_hi = a.astype(BF16)
    a_lo = (a - a_hi.astype(F32)).astype(BF16)
    b_hi = b.astype(BF16)
    b_lo = (b - b_hi.astype(F32)).astype(BF16)
    return _dot(a_hi, b_hi) + _dot(a_lo, b_hi) + _dot(a_hi, b_lo)


def _sigmoid(x):
    return 1.0 / (1.0 + jnp.exp(-x))


def _softplus(x):
    return jnp.maximum(x, 0.0) + jnp.log(1.0 + jnp.exp(-jnp.abs(x)))


def _mod_kernel(c_ref, w_ref, b_ref, o_ref):
    c = c_ref[...]
    s = c * _sigmoid(c)
    o_ref[...] = _dot_x3(s, w_ref[...]) + b_ref[...]


def _modulation(cond, w_mod, b_mod):
    r, d = cond.shape
    n = w_mod.shape[1]
    tn = d
    return pl.pallas_call(
        _mod_kernel,
        grid=(n // tn,),
        in_specs=[pl.BlockSpec((r, d), lambda j: (0, 0)),
                  pl.BlockSpec((d, tn), lambda j: (0, j)),
                  pl.BlockSpec((1, tn), lambda j: (0, j))],
        out_specs=pl.BlockSpec((r, tn), lambda j: (0, j)),
        out_shape=jax.ShapeDtypeStruct((r, n), F32),
        compiler_params=_cparams("arbitrary"),
        name="modulation",
    )(cond, w_mod, b_mod.reshape(1, n))


def _in_proj_kernel(x_ref, mod_ref, g_ref, w_ref, *o_refs, bounds):
    x = x_ref[0]
    ms = jnp.mean(x * x, axis=-1, keepdims=True)
    h = x * lax.rsqrt(ms + NORM_EPS) * g_ref[...]
    h = h * (1.0 + mod_ref[0, 0, 1:2, :]) + mod_ref[0, 0, 0:1, :]
    hb = h.astype(BF16)
    for o_ref, (lo, hi) in zip(o_refs, bounds):
        o_ref[0] = _dot(hb, w_ref[:, lo:hi])


def _in_proj(xs, mod, norm_g, w_in_bf, bounds, n_ctx_tiles):
    b, t, d = xs.shape
    tm = TOKEN_TILE
    p = w_in_bf.shape[1]
    seg = lambda bi, i: (bi, jnp.where(i >= n_ctx_tiles, 1, 0), 0, 0)
    return pl.pallas_call(
        functools.partial(_in_proj_kernel, bounds=bounds),
        grid=(b, t // tm),
        in_specs=[pl.BlockSpec((1, tm, d), lambda bi, i: (bi, i, 0)),
                  pl.BlockSpec((1, 1, 2, d), seg),
                  pl.BlockSpec((1, d), lambda bi, i: (0, 0)),
                  pl.BlockSpec((d, p), lambda bi, i: (0, 0))],
        out_specs=[pl.BlockSpec((1, tm, hi - lo), lambda bi, i: (bi, i, 0)) for lo, hi in bounds],
        out_shape=[jax.ShapeDtypeStruct((b, t, hi - lo), F32) for lo, hi in bounds],
        compiler_params=_cparams("parallel", "parallel"),
        name="in_proj",
    )(xs, mod, norm_g.reshape(1, d), w_in_bf)


def _ret_kernel(q_ref, k_ref, v_ref, cos_ref, sin_ref, dq_ref, dk_ref, dmat_ref, gm_ref, bm_ref,
                o_ref, s_ref, *, n_heads):
    i = pl.program_id(1)

    @pl.when(i == 0)
    def _():
        s_ref[...] = jnp.zeros_like(s_ref)

    c, w = q_ref.shape[1], q_ref.shape[2]
    cos = cos_ref[...]
    sin = sin_ref[...]
    lane = lax.broadcasted_iota(jnp.int32, (c, LANES), 1)
    first = (lane % 32) < 16

    def rope(u):
        parts = []
        for j in range(w // LANES):
            uj = u[:, j * LANES:(j + 1) * LANES]
            nxt = pltpu.roll(uj, LANES - 16, axis=1)
            prv = pltpu.roll(uj, 16, axis=1)
            parts.append(jnp.where(first, nxt, prv))
        return u * cos + jnp.concatenate(parts, axis=1) * sin

    q = rope(q_ref[0])
    k = rope(k_ref[0])
    vb = v_ref[0].astype(BF16)
    kb = k.astype(BF16)
    s = s_ref[...]

    out = _dot((q * dq_ref[...]).astype(BF16), s.astype(BF16))

    head = lax.broadcasted_iota(jnp.int32, (c, w), 1) // HEAD_DIM
    q_stack = jnp.concatenate([jnp.where(head == h, q, 0.0) for h in range(n_heads)], axis=0).astype(BF16)
    sc = _dot_nt(q_stack, kb) * dmat_ref[...]
    for h in range(n_heads):
        oh = _dot(sc[h * c:(h + 1) * c].astype(BF16), vb)
        out = out + jnp.where(head == h, oh, 0.0)
    o_ref[0] = out

    ktv = _dot_tn((k * dk_ref[...]).astype(BF16), vb)
    s_ref[...] = gm_ref[...] * s + bm_ref[...] * ktv


def _retention(p_ret, cos_t, sin_t, tabs, n_ctx, rev):
    b, t, w4 = p_ret.shape
    w = w4 // 4
    c = RET_CHUNK
    n_tot = t // c
    n_heads = w // HEAD_DIM
    dq, dk, dmat, gm, bm = tabs
    tix = lambda i: _scan_chunk(i, n_ctx, n_tot, rev)
    col = lambda j: (lambda bi, i: (bi, tix(i), j))
    const = lambda bi, i: (0, 0)
    return pl.pallas_call(
        functools.partial(_ret_kernel, n_heads=n_heads),
        grid=(b, n_tot),
        in_specs=[pl.BlockSpec((1, c, w), col(0)), pl.BlockSpec((1, c, w), col(1)), pl.BlockSpec((1, c, w), col(2)),
                  pl.BlockSpec((c, w), lambda bi, i: (tix(i), 0)),
                  pl.BlockSpec((c, w), lambda bi, i: (tix(i), 0)),
                  pl.BlockSpec((c, w), const), pl.BlockSpec((c, w), const),
                  pl.BlockSpec((n_heads * c, c), const),
                  pl.BlockSpec((w, w), const), pl.BlockSpec((w, w), const)],
        out_specs=pl.BlockSpec((1, c, w), lambda bi, i: (bi, tix(i), 0)),
        out_shape=jax.ShapeDtypeStruct((b, t, w), F32),
        scratch_shapes=[pltpu.VMEM((w, w), F32)],
        compiler_params=_cparams("parallel", "arbitrary"),
        name="retention_rev" if rev else "retention_fwd",
    )(p_ret, p_ret, p_ret, cos_t, sin_t, dq, dk, dmat, gm, bm)


def _ret_tables(decay_logit, w, rev):
    n_heads = w // HEAD_DIM
    c = RET_CHUNK
    lg = jax.nn.log_sigmoid(decay_logit.astype(F32))
    t = jnp.arange(c, dtype=F32)
    p = (c - 1.0 - t) if rev else t
    rel = p[:, None] - p[None, :]
    scale = HEAD_DIM ** -0.5
    dmat = jnp.where(rel >= 0, jnp.exp(lg[:, None, None] * jnp.maximum(rel, 0.0)), 0.0) * scale
    dq = jnp.exp(lg[:, None] * (p + 1.0)) * scale
    dk = jnp.exp(lg[:, None] * (c - 1.0 - p))
    lanes = lambda a: jnp.repeat(a.T, HEAD_DIM, axis=1)
    hid = jnp.arange(w) // HEAD_DIM
    bm = (hid[:, None] == hid[None, :]).astype(F32)
    gm = bm * jnp.exp(lg * c)[hid][:, None]
    return lanes(dq), lanes(dk), dmat.reshape(n_heads * c, c), gm, bm


def _rope_tables(n_ctx_tok, seq, w):
    half = HEAD_DIM // 2
    quarter = half // 2
    inv_freq = ROPE_BASE ** (-jnp.arange(quarter, dtype=F32) / quarter)
    tok = jnp.arange(seq)
    rows = (tok // GRID_W).astype(F32)
    cols = (tok % GRID_W).astype(F32)
    o = jnp.arange(w) % HEAD_DIM
    pos = jnp.where(o[None, :] < half, rows[:, None], cols[:, None])
    ang = pos * inv_freq[o % quarter][None, :]
    sign = jnp.where((o % half) < quarter, -1.0, 1.0)[None, :]
    cos = jnp.concatenate([jnp.ones((n_ctx_tok, w), F32), jnp.cos(ang)], axis=0)
    sin = jnp.concatenate([jnp.zeros((n_ctx_tok, w), F32), jnp.sin(ang) * sign], axis=0)
    return cos, sin


def _rwkv_kernel(z_ref, mu_ref, vec_ref, wup_ref, aup_ref, e_ref, minc_ref, strict_ref, incl_ref,
                 y_ref, bon_ref, st_ref, zprev_ref, *, rev, n_ctx, w):
    i = pl.program_id(0)
    nb, c, zw = z_ref.shape
    n_pairs = w // LANES
    rows = nb * c

    @pl.when(i == 0)
    def _():
        st_ref[...] = jnp.zeros_like(st_ref)

    @pl.when((i == 0) | (i == n_ctx))
    def _():
        zprev_ref[...] = jnp.zeros_like(zprev_ref)

    z = z_ref[...].reshape(rows, zw)
    rin = lax.broadcasted_iota(jnp.int32, (rows, zw), 0) % c
    prev = jnp.concatenate([jnp.broadcast_to(zprev_ref[b, 0:1, :], (c, zw)) for b in range(nb)], axis=0)
    if rev:
        zs = jnp.where(rin == c - 1, prev, pltpu.roll(z, rows - 1, axis=0))
        for b in range(nb):
            zprev_ref[b, 0:1, :] = z[b * c:b * c + 1, :]
    else:
        zs = jnp.where(rin == 0, prev, pltpu.roll(z, 1, axis=0))
        for b in range(nb):
            zprev_ref[b, 0:1, :] = z[b * c + c - 1:b * c + c, :]
    zd = z + (zs - z) * mu_ref[...]

    r = zd[:, 0:w]
    k = zd[:, w:2 * w]
    v = zd[:, 2 * w:3 * w]
    lora = zd[:, 3 * w:3 * w + LANES]
    lane = lax.broadcasted_iota(jnp.int32, (rows, LANES), 1)
    lora = jnp.where(lane < DECAY_LORA, jnp.tanh(lora), lora)

    w0, a0, k_k, k_a, r_k = (vec_ref[j:j + 1, :] for j in range(5))
    e_bf = e_ref[...]
    w_log = -_softplus(-(w0 + _dot_x3(lora, wup_ref[...]))) - 0.5
    logw = -jnp.exp(w_log)
    a = _sigmoid(a0 + _dot_x3(lora, aup_ref[...]))
    kk0 = k * k_k
    kk = kk0 / jnp.maximum(jnp.sqrt(_dot_exact_rhs(kk0 * kk0, e_bf)), 1e-12)
    k2 = k * (1.0 + (a - 1.0) * k_a)
    bon_ref[...] = (_dot((r * k2 * r_k).astype(BF16), e_bf) * v).reshape(nb, c, w)

    cinc = _dot_exact_lhs(minc_ref[...], logw)
    e_inc = jnp.exp(cinc)
    e_neg = jnp.exp(-cinc)
    rt = r * e_inc
    kt = k2 * e_neg
    bt = kk * a * e_neg
    kkt = kk * jnp.exp(cinc - logw)
    last = 0 if rev else c - 1

    strict = strict_ref[...] > 0.0
    incl = incl_ref[...] > 0.0
    lane_lo = lax.broadcasted_iota(jnp.int32, (c, LANES), 1) < HEAD_DIM
    head_r = lax.broadcasted_iota(jnp.int32, (LANES, LANES), 0) // HEAD_DIM
    head_c = lax.broadcasted_iota(jnp.int32, (LANES, LANES), 1) // HEAD_DIM
    diag = head_r == head_c

    def stack(xw):
        return jnp.concatenate([jnp.where(lane_lo, xw, 0.0), jnp.where(lane_lo, 0.0, xw)], axis=0)

    def unstack(xs):
        return xs[0:c] + xs[c:2 * c]

    chains = [(b, j) for b in range(nb) for j in range(n_pairs)]

    def win(x, ch):
        b, j = ch
        return x[b * c:(b + 1) * c, j * LANES:(j + 1) * LANES]

    st = [st_ref[b * n_pairs + j] for b, j in chains]
    g = [_dot_nt(jnp.concatenate([stack(win(kkt, ch)), stack(win(rt, ch))], axis=0).astype(BF16),
                 jnp.concatenate([stack(win(bt, ch)), stack(win(kt, ch))], axis=0).astype(BF16)) for ch in chains]
    a_b = [jnp.where(strict, x[0:2 * c, 0:2 * c], 0.0).astype(BF16) for x in g]
    a_k = [jnp.where(strict, x[0:2 * c, 2 * c:4 * c], 0.0).astype(BF16) for x in g]
    r_kb = [jnp.concatenate([jnp.where(incl, x[2 * c:4 * c, 2 * c:4 * c], 0.0),
                             -jnp.where(incl, x[2 * c:4 * c, 0:2 * c], 0.0)], axis=1).astype(BF16) for x in g]
    x0 = [_dot_nt(jnp.concatenate([win(kkt, ch), win(rt, ch)], axis=0).astype(BF16), s.astype(BF16))
          for ch, s in zip(chains, st)]
    v_sb = [stack(win(v, ch)).astype(BF16) for ch in chains]
    u = [stack(x[0:c]) + _dot(ak, vs) for x, ak, vs in zip(x0, a_k, v_sb)]

    u = [x - _dot(p, x.astype(BF16)) for x, p in zip(u, a_b)]
    pw = a_b
    steps = 1
    while 2 * steps < c:
        steps *= 2
        pw = [_dot(p, p).astype(BF16) for p in pw]
        u = [x + _dot(p, x.astype(BF16)) for x, p in zip(u, pw)]

    y_s = [_dot(rk, jnp.concatenate([vs, x.astype(BF16)], axis=0)) for rk, vs, x in zip(r_kb, v_sb, u)]
    y = [x[c:2 * c] + unstack(ys) for x, ys in zip(x0, y_s)]
    upd = [_dot_tn(jnp.concatenate([win(v, ch), unstack(x)], axis=0).astype(BF16),
                   jnp.concatenate([win(kt, ch), -win(bt, ch)], axis=0).astype(BF16))
           for ch, x in zip(chains, u)]
    for n, (b, j) in enumerate(chains):
        w_end = e_inc[b * c + last:b * c + last + 1, j * LANES:(j + 1) * LANES]
        st_ref[b * n_pairs + j] = jnp.where(diag, (st[n] + upd[n]) * w_end, 0.0)
        y_ref[b, :, j * LANES:(j + 1) * LANES] = y[n]


def _rwkv(p_z, prm, n_ctx, rev):
    b, t, zw = p_z.shape
    w = (zw - DECAY_LORA - ICLR_LORA) // 3
    c = RWKV_CHUNK
    n_tot = t // c
    mu, vecs, wup, aup, e_bf, minc, strict, incl = prm
    tix = lambda i: _scan_chunk(i, n_ctx, n_tot, rev)
    const = lambda i: (0, 0)
    full = lambda a: pl.BlockSpec(a.shape, const)
    return pl.pallas_call(
        functools.partial(_rwkv_kernel, rev=rev, n_ctx=n_ctx, w=w),
        grid=(n_tot,),
        in_specs=[pl.BlockSpec((b, c, zw), lambda i: (0, tix(i), 0)),
                  full(mu), full(vecs), full(wup), full(aup), full(e_bf), full(minc), full(strict), full(incl)],
        out_specs=[pl.BlockSpec((b, c, w), lambda i: (0, tix(i), 0)),
                   pl.BlockSpec((b, c, w), lambda i: (0, tix(i), 0))],
        out_shape=[jax.ShapeDtypeStruct((b, t, w), F32), jax.ShapeDtypeStruct((b, t, w), F32)],
        scratch_shapes=[pltpu.VMEM((b * (w // LANES), LANES, LANES), F32), pltpu.VMEM((b, 8, zw), F32)],
        compiler_params=_cparams("arbitrary"),
        name="rwkv7_rev" if rev else "rwkv7_fwd",
    )(p_z, mu, vecs, wup, aup, e_bf, minc, strict, incl)


def _rwkv_params(mu, w0, w_up, a0, a_up, k_k, k_a, r_k, rev, n_batch):
    w = w0.shape[0]
    c = RWKV_CHUNK
    vecs = jnp.concatenate([jnp.stack([w0, a0, k_k, k_a, r_k]), jnp.zeros((3, w), F32)], axis=0)
    wup = jnp.concatenate([w_up, jnp.zeros((ICLR_LORA, w), F32)], axis=0)
    aup = jnp.concatenate([jnp.zeros((DECAY_LORA, w), F32), a_up], axis=0)
    hid = jnp.arange(w) // HEAD_DIM
    e_bf = (hid[:, None] == hid[None, :]).astype(BF16)
    t = jnp.arange(c)
    p = (c - 1 - t) if rev else t
    le = p[None, :] <= p[:, None]
    lt = p[None, :] < p[:, None]
    blk = jnp.arange(2 * c) // c
    same = blk[:, None] == blk[None, :]
    strict = (jnp.tile(lt, (2, 2)) & same).astype(F32)
    incl = (jnp.tile(le, (2, 2)) & same).astype(F32)
    minc = jnp.kron(jnp.eye(n_batch, dtype=F32), le.astype(F32)).astype(BF16)
    return mu.reshape(1, -1), vecs, wup, aup, e_bf, minc, strict, incl


def _lru_kernel(x_ref, cw_ref, vec_ref, wa_ref, wx_ref, h_ref, hcar_ref, ucar_ref, *, rev, n_ctx):
    i = pl.program_id(1)
    c, w = x_ref.shape[1], x_ref.shape[2]

    @pl.when(i == 0)
    def _():
        hcar_ref[...] = jnp.zeros_like(hcar_ref)

    @pl.when((i == 0) | (i == n_ctx))
    def _():
        ucar_ref[...] = jnp.zeros_like(ucar_ref)

    u0 = x_ref[0]
    row = lax.broadcasted_iota(jnp.int32, (c, w), 0)

    def shifted(x, s, carry, fill):
        if rev:
            rolled = pltpu.roll(x, c - s, axis=0)
            edge = row >= c - s
        else:
            rolled = pltpu.roll(x, s, axis=0)
            edge = row < s
        if carry is None:
            return jnp.where(edge, fill, rolled)
        return jnp.where(edge, carry, rolled)

    conv = vec_ref[0:1, :] + cw_ref[LRU_CONV - 1:LRU_CONV, :] * u0
    for m in range(1, LRU_CONV):
        car = jnp.zeros((c, w), F32)
        for qpos in range(m):
            r_idx = (c - 1 - qpos) if rev else qpos
            car = jnp.where(row == r_idx, jnp.broadcast_to(ucar_ref[m - qpos - 1:m - qpos, :], (c, w)), car)
        conv = conv + cw_ref[LRU_CONV - 1 - m:LRU_CONV - m, :] * shifted(u0, m, car, None)
    for m in range(1, LRU_CONV):
        r_idx = (m - 1) if rev else (c - m)
        ucar_ref[m - 1:m, :] = u0[r_idx:r_idx + 1, :]

    cb = conv.astype(BF16)
    r = _sigmoid(_dot(cb, wa_ref[...]) + vec_ref[1:2, :])
    ig = _sigmoid(_dot(cb, wx_ref[...]) + vec_ref[2:3, :])
    log_a = -LRU_C * r * vec_ref[3:4, :]
    a = jnp.exp(log_a)
    bb = jnp.sqrt(1.0 - jnp.exp(2.0 * log_a)) * (ig * conv)

    s = 1
    while s < c:
        bb = bb + a * shifted(bb, s, None, 0.0)
        a = a * shifted(a, s, None, 1.0)
        s *= 2
    h = bb + a * hcar_ref[0:1, :]
    h_ref[0] = h
    last = 0 if rev else c - 1
    hcar_ref[0:1, :] = h[last:last + 1, :]


def _lru(p_lru, prm, n_ctx, rev):
    b, t, w2 = p_lru.shape
    w = w2 // 2
    c = LRU_CHUNK
    n_tot = t // c
    cw, vecs, wa, wx = prm
    tix = lambda i: _scan_chunk(i, n_ctx, n_tot, rev)
    const = lambda bi, i: (0, 0)
    return pl.pallas_call(
        functools.partial(_lru_kernel, rev=rev, n_ctx=n_ctx),
        grid=(b, n_tot),
        in_specs=[pl.BlockSpec((1, c, w), lambda bi, i: (bi, tix(i), 0)),
                  pl.BlockSpec(cw.shape, const), pl.BlockSpec(vecs.shape, const),
                  pl.BlockSpec(wa.shape, const), pl.BlockSpec(wx.shape, const)],
        out_specs=pl.BlockSpec((1, c, w), lambda bi, i: (bi, tix(i), 0)),
        out_shape=jax.ShapeDtypeStruct((b, t, w), F32),
        scratch_shapes=[pltpu.VMEM((8, w), F32), pltpu.VMEM((8, w), F32)],
        compiler_params=_cparams("parallel", "arbitrary"),
        name="rglru_rev" if rev else "rglru_fwd",
    )(p_lru, cw, vecs, wa, wx)


def _lru_params(conv_w, conv_b, wa, ba, wx, bx, lam):
    w = conv_b.shape[0]
    cw = jnp.concatenate([conv_w, jnp.zeros((8 - LRU_CONV, w), F32)], axis=0)
    vecs = jnp.concatenate([jnp.stack([conv_b, ba, bx, jax.nn.softplus(-lam)]), jnp.zeros((4, w), F32)], axis=0)
    return cw, vecs, jax.scipy.linalg.block_diag(*wa).astype(BF16), jax.scipy.linalg.block_diag(*wx).astype(BF16)


def _head_norm(y, e_bf, gain, bias, eps):
    inv = 1.0 / HEAD_DIM
    mu = _dot(y.astype(BF16), e_bf) * inv
    yc = y - mu
    var = _dot((yc * yc).astype(BF16), e_bf) * inv
    return yc * lax.rsqrt(var + eps) * gain + bias


def _mix_out_kernel(x_ref, mod_ref, of_ref, ob_ref, g_ref, yf_ref, yb_ref, bf_ref, bb_ref, gd_ref,
                    hf_ref, hb_ref, lg_ref, gn_ref, e_ref, gup_ref, wout_ref, n2g_ref, wr_ref, br_ref,
                    xo_ref, h2_ref, lo_ref, *, w_ret, w_rw):
    e_bf = e_ref[...]
    g = g_ref[0]
    ret = _head_norm(of_ref[0] + ob_ref[0], e_bf, gn_ref[0:1, :], gn_ref[1:2, :], RET_GN_EPS)
    ret = ret * (g * _sigmoid(g))
    gate = _dot(_sigmoid(gd_ref[0]).astype(BF16), gup_ref[...])
    rw = _head_norm(yf_ref[0] + yb_ref[0], e_bf, gn_ref[2:3, :], gn_ref[3:4, :], RWKV_GN_EPS)
    rw = (rw + bf_ref[0] + bb_ref[0]) * gate
    lg = lg_ref[0]
    gelu = 0.5 * lg * (1.0 + jnp.tanh(0.7978845608028654 * (lg + 0.044715 * (lg * lg * lg))))
    lru = (hf_ref[0] + hb_ref[0]) * gelu
    mix = (_dot(ret.astype(BF16), wout_ref[0:w_ret, :])
           + _dot(rw.astype(BF16), wout_ref[w_ret:w_ret + w_rw, :])
           + _dot(lru.astype(BF16), wout_ref[w_ret + w_rw:, :]))
    x = x_ref[0] + mod_ref[0, 0, 0:1, :] * mix
    xo_ref[0] = x
    ms = jnp.mean(x * x, axis=-1, keepdims=True)
    h2 = x * lax.rsqrt(ms + NORM_EPS) * n2g_ref[...]
    h2 = h2 * (1.0 + mod_ref[0, 0, 2:3, :]) + mod_ref[0, 0, 1:2, :]
    h2_ref[0] = h2.astype(BF16)
    lo_ref[0] = _dot_x3(h2, wr_ref[...]) + br_ref[...]


def _mix_out(xs, mod, o_f, o_b, p_ret, y_f, y_b, bon_f, bon_b, p_gd, h_f, h_b, p_lru,
             gn, e_bf, g_up_bf, w_out_bf, norm2_g, w_router, b_router, n_ctx_tiles):
    b, t, d = xs.shape
    tm = TOKEN_TILE
    w_ret, w_rw, w_lru = o_f.shape[2], y_f.shape[2], h_f.shape[2]
    ne = w_router.shape[1]
    tok = lambda wd, j=0: pl.BlockSpec((1, tm, wd), lambda bi, i: (bi, i, j))
    const = lambda a: pl.BlockSpec(a.shape, lambda bi, i: (0,) * a.ndim)
    seg = lambda bi, i: (bi, jnp.where(i >= n_ctx_tiles, 1, 0), 0, 0)
    n2g = norm2_g.reshape(1, d)
    br = b_router.reshape(1, ne)
    return pl.pallas_call(
        functools.partial(_mix_out_kernel, w_ret=w_ret, w_rw=w_rw),
        grid=(b, t // tm),
        in_specs=[tok(d), pl.BlockSpec((1, 1, 3, d), seg),
                  tok(w_ret), tok(w_ret), tok(w_ret, 3),
                  tok(w_rw), tok(w_rw), tok(w_rw), tok(w_rw), tok(GATE_LORA),
                  tok(w_lru), tok(w_lru), tok(w_lru, 1),
                  const(gn), const(e_bf), const(g_up_bf), const(w_out_bf), const(n2g), const(w_router), const(br)],
        out_specs=[tok(d), tok(d), tok(ne)],
        out_shape=[jax.ShapeDtypeStruct((b, t, d), F32), jax.ShapeDtypeStruct((b, t, d), BF16),
                   jax.ShapeDtypeStruct((b, t, ne), F32)],
        compiler_params=_cparams("parallel", "parallel"),
        name="mix_out",
    )(xs, mod, o_f, o_b, p_ret, y_f, y_b, bon_f, bon_b, p_gd, h_f, h_b, p_lru,
      gn, e_bf, g_up_bf, w_out_bf, n2g, w_router, br)


def _moe_kernel(be_ref, first_ref, nu_ref, x_ref, w1_ref, b1_ref, w2_ref, b2_ref, y_ref, w1b_ref, w2b_ref):
    i = pl.program_id(0)
    de = w2_ref.shape[1]

    @pl.when(first_ref[i] == 1)
    def _():
        w1b_ref[...] = w1_ref[0].astype(BF16)
        w2b_ref[...] = w2_ref[0].astype(BF16)

    @pl.when(i < nu_ref[0])
    def _():
        gu = _dot(x_ref[...], w1b_ref[...]) + b1_ref[0]
        glu = jnp.minimum(gu[:, :de], SWIGLU_LIMIT)
        lin = jnp.clip(gu[:, de:], -SWIGLU_LIMIT, SWIGLU_LIMIT)
        act = glu * _sigmoid(SWIGLU_ALPHA * glu) * (lin + 1.0)
        y_ref[...] = (_dot(act.astype(BF16), w2b_ref[...]) + b2_ref[0]).astype(y_ref.dtype)

    @pl.when(i >= nu_ref[0])
    def _():
        y_ref[...] = jnp.zeros_like(y_ref)


def _moe_ffn(hb, block_e, first, n_used, w1, b1, w2, b2):
    n_slots, d = hb.shape
    tm = MOE_TILE
    ne, _, d2 = w1.shape
    de = w2.shape[1]
    return pl.pallas_call(
        _moe_kernel,
        grid_spec=pltpu.PrefetchScalarGridSpec(
            num_scalar_prefetch=3,
            grid=(n_slots // tm,),
            in_specs=[pl.BlockSpec((tm, d), lambda i, be, fi, nu: (i, 0)),
                      pl.BlockSpec((1, d, d2), lambda i, be, fi, nu: (be[i], 0, 0)),
                      pl.BlockSpec((1, 1, d2), lambda i, be, fi, nu: (be[i], 0, 0)),
                      pl.BlockSpec((1, de, d), lambda i, be, fi, nu: (be[i], 0, 0)),
                      pl.BlockSpec((1, 1, d), lambda i, be, fi, nu: (be[i], 0, 0))],
            out_specs=pl.BlockSpec((tm, d), lambda i, be, fi, nu: (i, 0)),
            scratch_shapes=[pltpu.VMEM((d, d2), BF16), pltpu.VMEM((de, d), BF16)],
        ),
        out_shape=jax.ShapeDtypeStruct((n_slots, d), BF16),
        compiler_params=_cparams("arbitrary"),
        name="moe_ffn",
    )(block_e, first, n_used, hb, w1, b1.reshape(ne, 1, d2), w2, b2.reshape(ne, 1, d))


def _route_kernel(lg_ref, tri_ref, idx_ref, gate_ref, rank_ref, cnt_ref, base_ref):
    i = pl.program_id(0)

    @pl.when(i == 0)
    def _():
        base_ref[...] = jnp.zeros_like(base_ref)

    lg = lg_ref[...]
    tr, ne = lg.shape
    lane = lax.broadcasted_iota(jnp.int32, (tr, ne), 1).astype(F32)
    out_lane = lax.broadcasted_iota(jnp.int32, (tr, LANES), 1)
    vals = lg
    sel = jnp.zeros((tr, ne), F32)
    picks, tops = [], []
    for _ in range(TOP_K):
        m = jnp.max(vals, axis=-1, keepdims=True)
        ix = jnp.min(jnp.where(vals == m, lane, float(ne)), axis=-1, keepdims=True)
        hit = lane == ix
        sel = jnp.where(hit, 1.0, sel)
        vals = jnp.where(hit, -jnp.inf, vals)
        picks.append(ix)
        tops.append(m)
    ex = [jnp.exp(t - tops[0]) for t in tops]
    den = ex[0] + ex[1] + ex[2] + ex[3]
    before = _dot(tri_ref[...], sel.astype(BF16)) + base_ref[0:1, :]
    idx_o = jnp.zeros((tr, LANES), F32)
    gate_o = jnp.zeros((tr, LANES), F32)
    rank_o = jnp.zeros((tr, LANES), F32)
    for k in range(TOP_K):
        rk = jnp.sum(jnp.where(lane == picks[k], before, 0.0), axis=-1, keepdims=True)
        idx_o = jnp.where(out_lane == k, picks[k], idx_o)
        gate_o = jnp.where(out_lane == k, ex[k] / den, gate_o)
        rank_o = jnp.where(out_lane == k, rk, rank_o)
    idx_ref[...] = idx_o.astype(jnp.int32)
    gate_ref[...] = gate_o
    rank_ref[...] = rank_o.astype(jnp.int32)
    total = base_ref[0:1, :] + jnp.sum(sel, axis=0, keepdims=True)
    base_ref[0:1, :] = total
    cnt_ref[...] = jnp.broadcast_to(total, cnt_ref.shape).astype(jnp.int32)


def _route(logits):
    n_tok, ne = logits.shape
    tr = TOKEN_TILE
    t = jnp.arange(tr)
    tri = (t[None, :] < t[:, None]).astype(BF16)
    tok = lambda: pl.BlockSpec((tr, LANES), lambda i: (i, 0))
    return pl.pallas_call(
        _route_kernel,
        grid=(n_tok // tr,),
        in_specs=[pl.BlockSpec((tr, ne), lambda i: (i, 0)), pl.BlockSpec((tr, tr), lambda i: (0, 0))],
        out_specs=[tok(), tok(), tok(), pl.BlockSpec((8, ne), lambda i: (0, 0))],
        out_shape=[jax.ShapeDtypeStruct((n_tok, LANES), jnp.int32), jax.ShapeDtypeStruct((n_tok, LANES), F32),
                   jax.ShapeDtypeStruct((n_tok, LANES), jnp.int32), jax.ShapeDtypeStruct((8, ne), jnp.int32)],
        scratch_shapes=[pltpu.VMEM((8, ne), F32)],
        compiler_params=_cparams("arbitrary"),
        name="route",
    )(logits, tri)


def _route_meta(idx, rank, counts):
    n_tok = idx.shape[0]
    ne = counts.shape[0]
    tm = MOE_TILE
    n_assign = n_tok * TOP_K
    padded = (counts + tm - 1) // tm * tm
    pend = jnp.cumsum(padded)
    pstart = pend - padded
    start = jnp.cumsum(counts) - counts
    eid = jnp.arange(ne, dtype=jnp.int32)
    slot = jnp.sum(jnp.where(idx[..., None] == eid, pstart, 0), axis=-1).astype(jnp.int32) + rank
    n_blocks = (n_assign + ne * (tm - 1) + tm - 1) // tm
    blk_start = jnp.arange(n_blocks, dtype=jnp.int32) * tm
    block_e = jnp.minimum(jnp.sum(pend[None, :] <= blk_start[:, None], axis=1), ne - 1).astype(jnp.int32)
    first = jnp.concatenate([jnp.ones((1,), jnp.int32), (block_e[1:] != block_e[:-1]).astype(jnp.int32)])
    n_used = (pend[-1] // tm).astype(jnp.int32).reshape(1)
    _, order = lax.sort_key_val(slot.reshape(-1), jnp.arange(n_assign, dtype=jnp.int32))
    off = jnp.arange(n_blocks * tm, dtype=jnp.int32) - jnp.repeat(pstart[block_e], tm)
    valid = off < jnp.repeat(counts[block_e], tm)
    pos = jnp.clip(jnp.repeat(start[block_e], tm) + off, 0, n_assign - 1)
    slot_tok = jnp.where(valid, order[pos] // TOP_K, n_tok).astype(jnp.int32)
    return slot, slot_tok, block_e, first, n_used


def _combine_kernel(x_ref, mod_ref, y_ref, gate_ref, g_ref, o_ref, *, final):
    gate = gate_ref[0]
    y = y_ref[0, 0].astype(F32) * gate[:, 0:1]
    for k in range(1, TOP_K):
        y = y + y_ref[k, 0].astype(F32) * gate[:, k:k + 1]
    x = x_ref[0] + mod_ref[0, 0, 0:1, :] * y
    if final:
        ms = jnp.mean(x * x, axis=-1, keepdims=True)
        x = x * lax.rsqrt(ms + NORM_EPS) * g_ref[...]
    o_ref[0] = x


def _combine(xs, mod, yg, gates, final_g, n_ctx_tiles, final):
    b, t, d = xs.shape
    tm = TOKEN_TILE
    skip = n_ctx_tiles if final else 0
    seg = lambda bi, i: (bi, jnp.where(i + skip >= n_ctx_tiles, 1, 0), 0, 0)
    return pl.pallas_call(
        functools.partial(_combine_kernel, final=final),
        grid=(b, t // tm - skip),
        in_specs=[pl.BlockSpec((1, tm, d), lambda bi, i: (bi, i + skip, 0)),
                  pl.BlockSpec((1, 1, 1, d), seg),
                  pl.BlockSpec((TOP_K, 1, tm, d), lambda bi, i: (0, bi, i + skip, 0)),
                  pl.BlockSpec((1, tm, LANES), lambda bi, i: (bi, i + skip, 0)),
                  pl.BlockSpec((1, d), lambda bi, i: (0, 0))],
        out_specs=pl.BlockSpec((1, tm, d), lambda bi, i: (bi, i, 0)),
        out_shape=jax.ShapeDtypeStruct((b, t - skip * tm, d), F32),
        compiler_params=_cparams("parallel", "parallel"),
        name="combine_final" if final else "combine",
    )(xs, mod, yg, gates, final_g.reshape(1, d))


def kernel(x, c, ctx, c_ctx, w_mod, b_mod, norm1_g, norm2_g, w_in, w_out, ret_decay_logit, ret_gn_g, ret_gn_b, rwkv_mu, rwkv_w0, rwkv_w_up, rwkv_a0, rwkv_a_up, rwkv_k_k, rwkv_k_a, rwkv_g_up, rwkv_r_k, rwkv_gn_g, rwkv_gn_b, lru_conv_w, lru_conv_b, lru_wa, lru_ba, lru_wx, lru_bx, lru_lambda, moe_w_router, moe_b_router, moe_w1, moe_b1, moe_w2, moe_b2, final_norm_g):
    bsz, seq, dm = x.shape
    n_ctx_tok = ctx.shape[1]
    depth = w_in.shape[0]
    n_experts = moe_w_router.shape[2]
    w_ret = 3 * dm // 8
    w_rw = 3 * dm // 8
    w_lru = dm - w_ret - w_rw
    zw = 3 * w_rw + DECAY_LORA + ICLR_LORA
    sizes = (4 * w_ret, zw, GATE_LORA, 2 * w_lru)
    bounds, off = [], 0
    for s in sizes:
        bounds.append((off, off + s))
        off += s
    bounds = tuple(bounds)
    assert off == w_in.shape[2]
    assert n_ctx_tok % TOKEN_TILE == 0 and seq % TOKEN_TILE == 0 and seq % GRID_W == 0
    t_all = n_ctx_tok + seq
    n_ctx_tiles = n_ctx_tok // TOKEN_TILE

    xs = jnp.concatenate([ctx, x], axis=1)
    cos_t, sin_t = _rope_tables(n_ctx_tok, seq, w_ret)
    hid = jnp.arange(w_ret) // HEAD_DIM
    e_bf = (hid[:, None] == hid[None, :]).astype(BF16)
    cond = jnp.concatenate([c, c_ctx[None, :], jnp.zeros((8 - (bsz + 1) % 8, dm), F32)], axis=0)

    for l in range(depth):
        last = l == depth - 1
        mod = _modulation(cond, w_mod[l], b_mod[l])
        mod_l = mod[:bsz].reshape(bsz, 6, dm)
        mod_c = jnp.broadcast_to(mod[bsz].reshape(1, 6, dm), (bsz, 6, dm))
        modsel = jnp.stack([mod_c, mod_l], axis=1)

        p_ret, p_z, p_gd, p_lru = _in_proj(xs, modsel[:, :, 0:2], norm1_g[l], w_in[l].astype(BF16), bounds, n_ctx_tiles)

        ret_o, rw_y, rw_bon, lru_h = [], [], [], []
        for d in range(2):
            rev = d == 1
            ret_o.append(_retention(p_ret, cos_t, sin_t, _ret_tables(ret_decay_logit[l, d], w_ret, rev),
                                    n_ctx_tok // RET_CHUNK, rev))
            prm = _rwkv_params(rwkv_mu[l, d], rwkv_w0[l, d], rwkv_w_up[l, d], rwkv_a0[l, d], rwkv_a_up[l, d],
                               rwkv_k_k[l, d], rwkv_k_a[l, d], rwkv_r_k[l], rev, bsz)
            y, bon = _rwkv(p_z, prm, n_ctx_tok // RWKV_CHUNK, rev)
            rw_y.append(y)
            rw_bon.append(bon)
            lru_h.append(_lru(p_lru, _lru_params(lru_conv_w[l, d], lru_conv_b[l, d], lru_wa[l, d], lru_ba[l, d],
                                                 lru_wx[l, d], lru_bx[l, d], lru_lambda[l, d]),
                              n_ctx_tok // LRU_CHUNK, rev))

        gn = jnp.concatenate([jnp.stack([ret_gn_g[l], ret_gn_b[l], rwkv_gn_g[l], rwkv_gn_b[l]]),
                              jnp.zeros((4, w_ret), F32)], axis=0)
        xs, h2, logits = _mix_out(xs, modsel[:, :, 2:5], ret_o[0], ret_o[1], p_ret, rw_y[0], rw_y[1],
                                  rw_bon[0], rw_bon[1], p_gd, lru_h[0], lru_h[1], p_lru,
                                  gn, e_bf, rwkv_g_up[l].astype(BF16), w_out[l].astype(BF16), norm2_g[l],
                                  moe_w_router[l], moe_b_router[l], n_ctx_tiles)

        n_tok = bsz * t_all
        idx, gates, rank, counts = _route(logits.reshape(n_tok, n_experts))
        slot, slot_tok, block_e, first, n_used = _route_meta(idx[:, :TOP_K], rank[:, :TOP_K], counts[0])
        h_pad = jnp.concatenate([h2.reshape(n_tok, dm), jnp.zeros((1, dm), BF16)], axis=0)
        y_sorted = _moe_ffn(h_pad[slot_tok], block_e, first, n_used, moe_w1[l], moe_b1[l], moe_w2[l], moe_b2[l])
        yg = y_sorted[slot.T].reshape(TOP_K, bsz, t_all, dm)
        xs = _combine(xs, modsel[:, :, 5:6], yg, gates.reshape(bsz, t_all, LANES), final_norm_g, n_ctx_tiles, last)
    return xs
```

```python
import functools

import jax
import jax.numpy as jnp
from jax import lax
from jax.experimental import pallas as pl
from jax.experimental.pallas import tpu as pltpu

F32 = jnp.float32
BF16 = jnp.bfloat16

HEAD_DIM = 64
NORM_EPS = 1e-6
RET_GN_EPS = 1e-5
RWKV_GN_EPS = 64e-5
ROPE_BASE = 10000.0
GRID_W = 64
LRU_CONV = 4
LRU_C = 8.0
TOP_K = 4
SWIGLU_LIMIT = 7.0
SWIGLU_ALPHA = 1.702
DECAY_LORA = 64
ICLR_LORA = 64
GATE_LORA = 128

LANES = 128
TOKEN_TILE = 256
RET_CHUNK = 128
RWKV_CHUNK = 64
LRU_CHUNK = 128
MOE_TILE = 512
VMEM_LIMIT = 56 * 1024 * 1024


def _cparams(*sem):
    return pltpu.CompilerParams(dimension_semantics=sem, vmem_limit_bytes=VMEM_LIMIT)


def _scan_chunk(i, n_ctx, n_tot, rev):
    if not rev:
        return i
    return jnp.where(i < n_ctx, n_ctx - 1 - i, n_tot + n_ctx - 1 - i)


def _split3(a):
    hi = a.astype(BF16)
    r1 = a - hi.astype(F32)
    mid = r1.astype(BF16)
    lo = (r1 - mid.astype(F32)).astype(BF16)
    return hi, mid, lo


def _dot(a, b):
    return jnp.dot(a, b, preferred_element_type=F32)


def _dot_nt(a, b):
    return lax.dot_general(a, b, (((1,), (1,)), ((), ())), preferred_element_type=F32)


def _dot_tn(a, b):
    return lax.dot_general(a, b, (((0,), (0,)), ((), ())), preferred_element_type=F32)


def _dot_exact_rhs(a, b_bf):
    hi, mid, lo = _split3(a)
    return _dot(hi, b_bf) + _dot(mid, b_bf) + _dot(lo, b_bf)


def _dot_exact_lhs(a_bf, b):
    hi, mid, lo = _split3(b)
    return _dot(a_bf, hi) + _dot(a_bf, mid) + _dot(a_bf, lo)


def _dot_x3(a, b):
    a_hi = a.astype(BF16)
    a_lo = (a - a_hi.astype(F32)).astype(BF16)
    b_hi = b.astype(BF16)
    b_lo = (b - b_hi.astype(F32)).astype(BF16)
    return _dot(a_hi, b_hi) + _dot(a_lo, b_hi) + _dot(a_hi, b_lo)


def _sigmoid(x):
    return 1.0 / (1.0 + jnp.exp(-x))


def _softplus(x):
    return jnp.maximum(x, 0.0) + jnp.log(1.0 + jnp.exp(-jnp.abs(x)))


def _mod_kernel(c_ref, w_ref, b_ref, o_ref):
    c = c_ref[...]
    s = c * _sigmoid(c)
    o_ref[...] = _dot_x3(s, w_ref[...]) + b_ref[...]


def _modulation(cond, w_mod, b_mod, layer):
    r, d = cond.shape
    nl, _, n = w_mod.shape
    tn = d
    return pl.pallas_call(
        _mod_kernel,
        grid=(n // tn,),
        in_specs=[pl.BlockSpec((r, d), lambda j: (0, 0)),
                  pl.BlockSpec((None, d, tn), lambda j: (layer, 0, j)),
                  pl.BlockSpec((None, 1, tn), lambda j: (layer, 0, j))],
        out_specs=pl.BlockSpec((r, tn), lambda j: (0, j)),
        out_shape=jax.ShapeDtypeStruct((r, n), F32),
        compiler_params=_cparams("arbitrary"),
        name="modulation",
    )(cond, w_mod, b_mod.reshape(nl, 1, n))


def _in_proj_kernel(x_ref, mod_ref, g_ref, w_ref, *o_refs, bounds):
    x = x_ref[0]
    ms = jnp.mean(x * x, axis=-1, keepdims=True)
    h = x * lax.rsqrt(ms + NORM_EPS) * g_ref[...]
    h = h * (1.0 + mod_ref[0, 0, 1:2, :]) + mod_ref[0, 0, 0:1, :]
    hb = h.astype(BF16)
    for o_ref, (lo, hi) in zip(o_refs, bounds):
        o_ref[0] = _dot(hb, w_ref[:, lo:hi])


def _in_proj(xs, mod, norm_g, w_in_bf, bounds, n_ctx_tiles):
    b, t, d = xs.shape
    tm = TOKEN_TILE
    p = w_in_bf.shape[1]
    seg = lambda bi, i: (bi, jnp.where(i >= n_ctx_tiles, 1, 0), 0, 0)
    return pl.pallas_call(
        functools.partial(_in_proj_kernel, bounds=bounds),
        grid=(b, t // tm),
        in_specs=[pl.BlockSpec((1, tm, d), lambda bi, i: (bi, i, 0)),
                  pl.BlockSpec((1, 1, 2, d), seg),
                  pl.BlockSpec((1, d), lambda bi, i: (0, 0)),
                  pl.BlockSpec((d, p), lambda bi, i: (0, 0))],
        out_specs=[pl.BlockSpec((1, tm, hi - lo), lambda bi, i: (bi, i, 0)) for lo, hi in bounds],
        out_shape=[jax.ShapeDtypeStruct((b, t, hi - lo), F32) for lo, hi in bounds],
        compiler_params=_cparams("parallel", "parallel"),
        name="in_proj",
    )(xs, mod, norm_g.reshape(1, d), w_in_bf)


def _ret_kernel(q_ref, k_ref, v_ref, cos_ref, sin_ref, dq_ref, dk_ref, dmat_ref, gm_ref, bm_ref,
                o_ref, s_ref, *, n_heads):
    i = pl.program_id(0)

    @pl.when(i == 0)
    def _():
        s_ref[...] = jnp.zeros_like(s_ref)

    nb, c, w = q_ref.shape
    cos = cos_ref[...]
    sin = sin_ref[...]
    lane = lax.broadcasted_iota(jnp.int32, (c, LANES), 1)
    first = (lane % 32) < 16

    def rope(u):
        parts = []
        for j in range(w // LANES):
            uj = u[:, j * LANES:(j + 1) * LANES]
            nxt = pltpu.roll(uj, LANES - 16, axis=1)
            prv = pltpu.roll(uj, 16, axis=1)
            parts.append(jnp.where(first, nxt, prv))
        return u * cos + jnp.concatenate(parts, axis=1) * sin

    bs = range(nb)
    head = lax.broadcasted_iota(jnp.int32, (c, w), 1) // HEAD_DIM
    q = [rope(q_ref[b]) for b in bs]
    k = [rope(k_ref[b]) for b in bs]
    vb = [v_ref[b].astype(BF16) for b in bs]
    s = [s_ref[b] for b in bs]
    out = [_dot((q[b] * dq_ref[...]).astype(BF16), s[b].astype(BF16)) for b in bs]
    sc = [_dot_nt(jnp.concatenate([jnp.where(head == h, q[b], 0.0) for h in range(n_heads)], axis=0).astype(BF16),
                  k[b].astype(BF16)) * dmat_ref[...] for b in bs]
    for h in range(n_heads):
        oh = [_dot(sc[b][h * c:(h + 1) * c].astype(BF16), vb[b]) for b in bs]
        out = [out[b] + jnp.where(head == h, oh[b], 0.0) for b in bs]
    ktv = [_dot_tn((k[b] * dk_ref[...]).astype(BF16), vb[b]) for b in bs]
    for b in bs:
        o_ref[b] = out[b]
        s_ref[b] = gm_ref[...] * s[b] + bm_ref[...] * ktv[b]


def _retention(p_ret, cos_t, sin_t, tabs, n_ctx, rev):
    b, t, w4 = p_ret.shape
    w = w4 // 4
    c = RET_CHUNK
    n_tot = t // c
    n_heads = w // HEAD_DIM
    dq, dk, dmat, gm, bm = tabs
    tix = lambda i: _scan_chunk(i, n_ctx, n_tot, rev)
    col = lambda j: (lambda i: (0, tix(i), j))
    const = lambda i: (0, 0)
    return pl.pallas_call(
        functools.partial(_ret_kernel, n_heads=n_heads),
        grid=(n_tot,),
        in_specs=[pl.BlockSpec((b, c, w), col(0)), pl.BlockSpec((b, c, w), col(1)), pl.BlockSpec((b, c, w), col(2)),
                  pl.BlockSpec((c, w), lambda i: (tix(i), 0)),
                  pl.BlockSpec((c, w), lambda i: (tix(i), 0)),
                  pl.BlockSpec((c, w), const), pl.BlockSpec((c, w), const),
                  pl.BlockSpec((n_heads * c, c), const),
                  pl.BlockSpec((w, w), const), pl.BlockSpec((w, w), const)],
        out_specs=pl.BlockSpec((b, c, w), lambda i: (0, tix(i), 0)),
        out_shape=jax.ShapeDtypeStruct((b, t, w), F32),
        scratch_shapes=[pltpu.VMEM((b, w, w), F32)],
        compiler_params=_cparams("arbitrary"),
        name="retention_rev" if rev else "retention_fwd",
    )(p_ret, p_ret, p_ret, cos_t, sin_t, dq, dk, dmat, gm, bm)


def _ret_tables(decay_logit, w, rev):
    n_heads = w // HEAD_DIM
    c = RET_CHUNK
    lg = jax.nn.log_sigmoid(decay_logit.astype(F32))
    t = jnp.arange(c, dtype=F32)
    p = (c - 1.0 - t) if rev else t
    rel = p[:, None] - p[None, :]
    scale = HEAD_DIM ** -0.5
    dmat = jnp.where(rel >= 0, jnp.exp(lg[:, None, None] * jnp.maximum(rel, 0.0)), 0.0) * scale
    dq = jnp.exp(lg[:, None] * (p + 1.0)) * scale
    dk = jnp.exp(lg[:, None] * (c - 1.0 - p))
    lanes = lambda a: jnp.repeat(a.T, HEAD_DIM, axis=1)
    hid = jnp.arange(w) // HEAD_DIM
    bm = (hid[:, None] == hid[None, :]).astype(F32)
    gm = bm * jnp.exp(lg * c)[hid][:, None]
    return lanes(dq), lanes(dk), dmat.reshape(n_heads * c, c), gm, bm


def _rope_tables(n_ctx_tok, seq, w):
    half = HEAD_DIM // 2
    quarter = half // 2
    inv_freq = ROPE_BASE ** (-jnp.arange(quarter, dtype=F32) / quarter)
    tok = jnp.arange(seq)
    rows = (tok // GRID_W).astype(F32)
    cols = (tok % GRID_W).astype(F32)
    o = jnp.arange(w) % HEAD_DIM
    pos = jnp.where(o[None, :] < half, rows[:, None], cols[:, None])
    ang = pos * inv_freq[o % quarter][None, :]
    sign = jnp.where((o % half) < quarter, -1.0, 1.0)[None, :]
    cos = jnp.concatenate([jnp.ones((n_ctx_tok, w), F32), jnp.cos(ang)], axis=0)
    sin = jnp.concatenate([jnp.zeros((n_ctx_tok, w), F32), jnp.sin(ang) * sign], axis=0)
    return cos, sin


def _rwkv_kernel(z_ref, mu_ref, vec_ref, wup_ref, aup_ref, e_ref, minc_ref, strict_ref, incl_ref,
                 y_ref, bon_ref, st_ref, zprev_ref, *, rev, n_ctx, w):
    i = pl.program_id(0)
    nb, c, zw = z_ref.shape
    n_pairs = w // LANES
    rows = nb * c

    @pl.when(i == 0)
    def _():
        st_ref[...] = jnp.zeros_like(st_ref)

    @pl.when((i == 0) | (i == n_ctx))
    def _():
        zprev_ref[...] = jnp.zeros_like(zprev_ref)

    z = z_ref[...].reshape(rows, zw)
    rin = lax.broadcasted_iota(jnp.int32, (rows, zw), 0) % c
    prev = jnp.concatenate([jnp.broadcast_to(zprev_ref[b, 0:1, :], (c, zw)) for b in range(nb)], axis=0)
    if rev:
        zs = jnp.where(rin == c - 1, prev, pltpu.roll(z, rows - 1, axis=0))
        for b in range(nb):
            zprev_ref[b, 0:1, :] = z[b * c:b * c + 1, :]
    else:
        zs = jnp.where(rin == 0, prev, pltpu.roll(z, 1, axis=0))
        for b in range(nb):
            zprev_ref[b, 0:1, :] = z[b * c + c - 1:b * c + c, :]
    zd = z + (zs - z) * mu_ref[...]

    r = zd[:, 0:w]
    k = zd[:, w:2 * w]
    v = zd[:, 2 * w:3 * w]
    lora = zd[:, 3 * w:3 * w + LANES]
    lane = lax.broadcasted_iota(jnp.int32, (rows, LANES), 1)
    lora = jnp.where(lane < DECAY_LORA, jnp.tanh(lora), lora)

    w0, a0, k_k, k_a, r_k = (vec_ref[j:j + 1, :] for j in range(5))
    e_bf = e_ref[...]
    w_log = -_softplus(-(w0 + _dot_x3(lora, wup_ref[...]))) - 0.5
    logw = -jnp.exp(w_log)
    a = _sigmoid(a0 + _dot_x3(lora, aup_ref[...]))
    kk0 = k * k_k
    kk = kk0 / jnp.maximum(jnp.sqrt(_dot_exact_rhs(kk0 * kk0, e_bf)), 1e-12)
    k2 = k * (1.0 + (a - 1.0) * k_a)
    bon_ref[...] = (_dot((r * k2 * r_k).astype(BF16), e_bf) * v).reshape(nb, c, w)

    cinc = _dot_exact_lhs(minc_ref[...], logw)
    e_inc = jnp.exp(cinc)
    e_neg = jnp.exp(-cinc)
    rt = r * e_inc
    kt = k2 * e_neg
    bt = kk * a * e_neg
    kkt = kk * jnp.exp(cinc - logw)
    last = 0 if rev else c - 1

    strict = strict_ref[...] > 0.0
    incl = incl_ref[...] > 0.0
    lane_lo = lax.broadcasted_iota(jnp.int32, (c, LANES), 1) < HEAD_DIM
    head_r = lax.broadcasted_iota(jnp.int32, (LANES, LANES), 0) // HEAD_DIM
    head_c = lax.broadcasted_iota(jnp.int32, (LANES, LANES), 1) // HEAD_DIM
    diag = head_r == head_c

    def stack(xw):
        return jnp.concatenate([jnp.where(lane_lo, xw, 0.0), jnp.where(lane_lo, 0.0, xw)], axis=0)

    def unstack(xs):
        return xs[0:c] + xs[c:2 * c]

    chains = [(b, j) for b in range(nb) for j in range(n_pairs)]

    def win(x, ch):
        b, j = ch
        return x[b * c:(b + 1) * c, j * LANES:(j + 1) * LANES]

    st = [st_ref[b * n_pairs + j] for b, j in chains]
    g = [_dot_nt(jnp.concatenate([stack(win(kkt, ch)), stack(win(rt, ch))], axis=0).astype(BF16),
                 jnp.concatenate([stack(win(bt, ch)), stack(win(kt, ch))], axis=0).astype(BF16)) for ch in chains]
    a_b = [jnp.where(strict, x[0:2 * c, 0:2 * c], 0.0).astype(BF16) for x in g]
    a_k = [jnp.where(strict, x[0:2 * c, 2 * c:4 * c], 0.0).astype(BF16) for x in g]
    r_kb = [jnp.concatenate([jnp.where(incl, x[2 * c:4 * c, 2 * c:4 * c], 0.0),
                             -jnp.where(incl, x[2 * c:4 * c, 0:2 * c], 0.0)], axis=1).astype(BF16) for x in g]
    x0 = [_dot_nt(jnp.concatenate([win(kkt, ch), win(rt, ch)], axis=0).astype(BF16), s.astype(BF16))
          for ch, s in zip(chains, st)]
    v_sb = [stack(win(v, ch)).astype(BF16) for ch in chains]
    u = [stack(x[0:c]) + _dot(ak, vs) for x, ak, vs in zip(x0, a_k, v_sb)]

    pw = a_b
    steps, sign = 1, -1.0
    while 2 * steps < c:
        both = [_dot(p, jnp.concatenate([p, x.astype(BF16)], axis=1)) for p, x in zip(pw, u)]
        u = [x + sign * y[:, 2 * c:4 * c] for x, y in zip(u, both)]
        pw = [y[:, 0:2 * c].astype(BF16) for y in both]
        steps, sign = 2 * steps, 1.0
    u = [x + sign * _dot(p, x.astype(BF16)) for x, p in zip(u, pw)]

    y_s = [_dot(rk, jnp.concatenate([vs, x.astype(BF16)], axis=0)) for rk, vs, x in zip(r_kb, v_sb, u)]
    y = [x[c:2 * c] + unstack(ys) for x, ys in zip(x0, y_s)]
    upd = [_dot_tn(jnp.concatenate([win(v, ch), unstack(x)], axis=0).astype(BF16),
                   jnp.concatenate([win(kt, ch), -win(bt, ch)], axis=0).astype(BF16))
           for ch, x in zip(chains, u)]
    for n, (b, j) in enumerate(chains):
        w_end = e_inc[b * c + last:b * c + last + 1, j * LANES:(j + 1) * LANES]
        st_ref[b * n_pairs + j] = jnp.where(diag, (st[n] + upd[n]) * w_end, 0.0)
        y_ref[b, :, j * LANES:(j + 1) * LANES] = y[n]


def _rwkv(p_z, prm, n_ctx, rev):
    b, t, zw = p_z.shape
    w = (zw - DECAY_LORA - ICLR_LORA) // 3
    c = RWKV_CHUNK
    n_tot = t // c
    mu, vecs, wup, aup, e_bf, minc, strict, incl = prm
    tix = lambda i: _scan_chunk(i, n_ctx, n_tot, rev)
    const = lambda i: (0, 0)
    full = lambda a: pl.BlockSpec(a.shape, const)
    return pl.pallas_call(
        functools.partial(_rwkv_kernel, rev=rev, n_ctx=n_ctx, w=w),
        grid=(n_tot,),
        in_specs=[pl.BlockSpec((b, c, zw), lambda i: (0, tix(i), 0)),
                  full(mu), full(vecs), full(wup), full(aup), full(e_bf), full(minc), full(strict), full(incl)],
        out_specs=[pl.BlockSpec((b, c, w), lambda i: (0, tix(i), 0)),
                   pl.BlockSpec((b, c, w), lambda i: (0, tix(i), 0))],
        out_shape=[jax.ShapeDtypeStruct((b, t, w), F32), jax.ShapeDtypeStruct((b, t, w), F32)],
        scratch_shapes=[pltpu.VMEM((b * (w // LANES), LANES, LANES), F32), pltpu.VMEM((b, 8, zw), F32)],
        compiler_params=_cparams("arbitrary"),
        name="rwkv7_rev" if rev else "rwkv7_fwd",
    )(p_z, mu, vecs, wup, aup, e_bf, minc, strict, incl)


def _rwkv_params(mu, w0, w_up, a0, a_up, k_k, k_a, r_k, rev, n_batch):
    w = w0.shape[0]
    c = RWKV_CHUNK
    vecs = jnp.concatenate([jnp.stack([w0, a0, k_k, k_a, r_k]), jnp.zeros((3, w), F32)], axis=0)
    wup = jnp.concatenate([w_up, jnp.zeros((ICLR_LORA, w), F32)], axis=0)
    aup = jnp.concatenate([jnp.zeros((DECAY_LORA, w), F32), a_up], axis=0)
    hid = jnp.arange(w) // HEAD_DIM
    e_bf = (hid[:, None] == hid[None, :]).astype(BF16)
    t = jnp.arange(c)
    p = (c - 1 - t) if rev else t
    le = p[None, :] <= p[:, None]
    lt = p[None, :] < p[:, None]
    blk = jnp.arange(2 * c) // c
    same = blk[:, None] == blk[None, :]
    strict = (jnp.tile(lt, (2, 2)) & same).astype(F32)
    incl = (jnp.tile(le, (2, 2)) & same).astype(F32)
    minc = jnp.kron(jnp.eye(n_batch, dtype=F32), le.astype(F32)).astype(BF16)
    return mu.reshape(1, -1), vecs, wup, aup, e_bf, minc, strict, incl


def _lru_kernel(x_ref, cw_ref, vec_ref, wa_ref, wx_ref, h_ref, hcar_ref, ucar_ref, *, rev, n_ctx):
    i = pl.program_id(0)
    nb, c, w = x_ref.shape
    rows = nb * c

    @pl.when(i == 0)
    def _():
        hcar_ref[...] = jnp.zeros_like(hcar_ref)

    @pl.when((i == 0) | (i == n_ctx))
    def _():
        ucar_ref[...] = jnp.zeros_like(ucar_ref)

    u0 = x_ref[...].reshape(rows, w)
    row = lax.broadcasted_iota(jnp.int32, (rows, w), 0) % c

    def per_batch(ref, j):
        return jnp.concatenate([jnp.broadcast_to(ref[b, j:j + 1, :], (c, w)) for b in range(nb)], axis=0)

    def shifted(x, s, carry, fill):
        if rev:
            rolled = pltpu.roll(x, rows - s, axis=0)
            edge = row >= c - s
        else:
            rolled = pltpu.roll(x, s, axis=0)
            edge = row < s
        if carry is None:
            return jnp.where(edge, fill, rolled)
        return jnp.where(edge, carry, rolled)

    conv = vec_ref[0:1, :] + cw_ref[LRU_CONV - 1:LRU_CONV, :] * u0
    for m in range(1, LRU_CONV):
        car = jnp.zeros((rows, w), F32)
        for qpos in range(m):
            r_idx = (c - 1 - qpos) if rev else qpos
            car = jnp.where(row == r_idx, per_batch(ucar_ref, m - qpos - 1), car)
        conv = conv + cw_ref[LRU_CONV - 1 - m:LRU_CONV - m, :] * shifted(u0, m, car, None)
    for m in range(1, LRU_CONV):
        r_idx = (m - 1) if rev else (c - m)
        for b in range(nb):
            ucar_ref[b, m - 1:m, :] = u0[b * c + r_idx:b * c + r_idx + 1, :]

    cb = conv.astype(BF16)
    r = _sigmoid(_dot(cb, wa_ref[...]) + vec_ref[1:2, :])
    ig = _sigmoid(_dot(cb, wx_ref[...]) + vec_ref[2:3, :])
    log_a = -LRU_C * r * vec_ref[3:4, :]
    a = jnp.exp(log_a)
    bb = jnp.sqrt(1.0 - jnp.exp(2.0 * log_a)) * (ig * conv)

    s = 1
    while s < c:
        bb = bb + a * shifted(bb, s, None, 0.0)
        a = a * shifted(a, s, None, 1.0)
        s *= 2
    h = bb + a * per_batch(hcar_ref, 0)
    h_ref[...] = h.reshape(nb, c, w)
    last = 0 if rev else c - 1
    for b in range(nb):
        hcar_ref[b, 0:1, :] = h[b * c + last:b * c + last + 1, :]


def _lru(p_lru, prm, n_ctx, rev):
    b, t, w2 = p_lru.shape
    w = w2 // 2
    c = LRU_CHUNK
    n_tot = t // c
    cw, vecs, wa, wx = prm
    tix = lambda i: _scan_chunk(i, n_ctx, n_tot, rev)
    const = lambda i: (0, 0)
    return pl.pallas_call(
        functools.partial(_lru_kernel, rev=rev, n_ctx=n_ctx),
        grid=(n_tot,),
        in_specs=[pl.BlockSpec((b, c, w), lambda i: (0, tix(i), 0)),
                  pl.BlockSpec(cw.shape, const), pl.BlockSpec(vecs.shape, const),
                  pl.BlockSpec(wa.shape, const), pl.BlockSpec(wx.shape, const)],
        out_specs=pl.BlockSpec((b, c, w), lambda i: (0, tix(i), 0)),
        out_shape=jax.ShapeDtypeStruct((b, t, w), F32),
        scratch_shapes=[pltpu.VMEM((b, 8, w), F32), pltpu.VMEM((b, 8, w), F32)],
        compiler_params=_cparams("arbitrary"),
        name="rglru_rev" if rev else "rglru_fwd",
    )(p_lru, cw, vecs, wa, wx)


def _lru_params(conv_w, conv_b, wa, ba, wx, bx, lam):
    w = conv_b.shape[0]
    cw = jnp.concatenate([conv_w, jnp.zeros((8 - LRU_CONV, w), F32)], axis=0)
    vecs = jnp.concatenate([jnp.stack([conv_b, ba, bx, jax.nn.softplus(-lam)]), jnp.zeros((4, w), F32)], axis=0)
    return cw, vecs, jax.scipy.linalg.block_diag(*wa).astype(BF16), jax.scipy.linalg.block_diag(*wx).astype(BF16)


def _head_norm(y, e_bf, gain, bias, eps):
    inv = 1.0 / HEAD_DIM
    mu = _dot(y.astype(BF16), e_bf) * inv
    yc = y - mu
    var = _dot((yc * yc).astype(BF16), e_bf) * inv
    return yc * lax.rsqrt(var + eps) * gain + bias


def _mix_out_kernel(x_ref, mod_ref, of_ref, ob_ref, g_ref, yf_ref, yb_ref, bf_ref, bb_ref, gd_ref,
                    hf_ref, hb_ref, lg_ref, gn_ref, e_ref, gup_ref, wout_ref, n2g_ref, wr_ref, br_ref,
                    xo_ref, h2_ref, lo_ref, *, w_ret, w_rw):
    e_bf = e_ref[...]
    g = g_ref[0]
    ret = _head_norm(of_ref[0] + ob_ref[0], e_bf, gn_ref[0:1, :], gn_ref[1:2, :], RET_GN_EPS)
    ret = ret * (g * _sigmoid(g))
    gate = _dot(_sigmoid(gd_ref[0]).astype(BF16), gup_ref[...])
    rw = _head_norm(yf_ref[0] + yb_ref[0], e_bf, gn_ref[2:3, :], gn_ref[3:4, :], RWKV_GN_EPS)
    rw = (rw + bf_ref[0] + bb_ref[0]) * gate
    lg = lg_ref[0]
    gelu = 0.5 * lg * (1.0 + jnp.tanh(0.7978845608028654 * (lg + 0.044715 * (lg * lg * lg))))
    lru = (hf_ref[0] + hb_ref[0]) * gelu
    mix = (_dot(ret.astype(BF16), wout_ref[0:w_ret, :])
           + _dot(rw.astype(BF16), wout_ref[w_ret:w_ret + w_rw, :])
           + _dot(lru.astype(BF16), wout_ref[w_ret + w_rw:, :]))
    x = x_ref[0] + mod_ref[0, 0, 0:1, :] * mix
    xo_ref[0] = x
    ms = jnp.mean(x * x, axis=-1, keepdims=True)
    h2 = x * lax.rsqrt(ms + NORM_EPS) * n2g_ref[...]
    h2 = h2 * (1.0 + mod_ref[0, 0, 2:3, :]) + mod_ref[0, 0, 1:2, :]
    h2_ref[0] = h2.astype(BF16)
    lo_ref[0] = _dot_x3(h2, wr_ref[...]) + br_ref[...]


def _mix_out(xs, mod, o_f, o_b, p_ret, y_f, y_b, bon_f, bon_b, p_gd, h_f, h_b, p_lru,
             gn, e_bf, g_up_bf, w_out_bf, norm2_g, w_router, b_router, n_ctx_tiles):
    b, t, d = xs.shape
    tm = TOKEN_TILE
    w_ret, w_rw, w_lru = o_f.shape[2], y_f.shape[2], h_f.shape[2]
    ne = w_router.shape[1]
    tok = lambda wd, j=0: pl.BlockSpec((1, tm, wd), lambda bi, i: (bi, i, j))
    const = lambda a: pl.BlockSpec(a.shape, lambda bi, i: (0,) * a.ndim)
    seg = lambda bi, i: (bi, jnp.where(i >= n_ctx_tiles, 1, 0), 0, 0)
    n2g = norm2_g.reshape(1, d)
    br = b_router.reshape(1, ne)
    return pl.pallas_call(
        functools.partial(_mix_out_kernel, w_ret=w_ret, w_rw=w_rw),
        grid=(b, t // tm),
        in_specs=[tok(d), pl.BlockSpec((1, 1, 3, d), seg),
                  tok(w_ret), tok(w_ret), tok(w_ret, 3),
                  tok(w_rw), tok(w_rw), tok(w_rw), tok(w_rw), tok(GATE_LORA),
                  tok(w_lru), tok(w_lru), tok(w_lru, 1),
                  const(gn), const(e_bf), const(g_up_bf), const(w_out_bf), const(n2g), const(w_router), const(br)],
        out_specs=[tok(d), tok(d), tok(ne)],
        out_shape=[jax.ShapeDtypeStruct((b, t, d), F32), jax.ShapeDtypeStruct((b, t, d), BF16),
                   jax.ShapeDtypeStruct((b, t, ne), F32)],
        compiler_params=_cparams("parallel", "parallel"),
        name="mix_out",
    )(xs, mod, o_f, o_b, p_ret, y_f, y_b, bon_f, bon_b, p_gd, h_f, h_b, p_lru,
      gn, e_bf, g_up_bf, w_out_bf, n2g, w_router, br)


def _moe_kernel(be_ref, first_ref, nu_ref, x_ref, w1_ref, b1_ref, w2_ref, b2_ref, y_ref, w1b_ref, w2b_ref):
    i = pl.program_id(0)
    de = w2_ref.shape[1]

    @pl.when(first_ref[i] == 1)
    def _():
        w1b_ref[...] = w1_ref[0].astype(BF16)
        w2b_ref[...] = w2_ref[0].astype(BF16)

    @pl.when(i < nu_ref[0])
    def _():
        gu = _dot(x_ref[...], w1b_ref[...]) + b1_ref[0]
        glu = jnp.minimum(gu[:, :de], SWIGLU_LIMIT)
        lin = jnp.clip(gu[:, de:], -SWIGLU_LIMIT, SWIGLU_LIMIT)
        act = glu * _sigmoid(SWIGLU_ALPHA * glu) * (lin + 1.0)
        y_ref[...] = (_dot(act.astype(BF16), w2b_ref[...]) + b2_ref[0]).astype(y_ref.dtype)

    @pl.when(i >= nu_ref[0])
    def _():
        y_ref[...] = jnp.zeros_like(y_ref)


def _moe_ffn(hb, block_e, first, n_used, w1, b1, w2, b2, layer):
    n_slots, d = hb.shape
    tm = MOE_TILE
    nl, ne, _, d2 = w1.shape
    de = w2.shape[2]
    wsel = lambda i, be, fi, nu: (layer, be[i], 0, 0)
    return pl.pallas_call(
        _moe_kernel,
        grid_spec=pltpu.PrefetchScalarGridSpec(
            num_scalar_prefetch=3,
            grid=(n_slots // tm,),
            in_specs=[pl.BlockSpec((tm, d), lambda i, be, fi, nu: (i, 0)),
                      pl.BlockSpec((None, 1, d, d2), wsel),
                      pl.BlockSpec((None, 1, 1, d2), wsel),
                      pl.BlockSpec((None, 1, de, d), wsel),
                      pl.BlockSpec((None, 1, 1, d), wsel)],
            out_specs=pl.BlockSpec((tm, d), lambda i, be, fi, nu: (i, 0)),
            scratch_shapes=[pltpu.VMEM((d, d2), BF16), pltpu.VMEM((de, d), BF16)],
        ),
        out_shape=jax.ShapeDtypeStruct((n_slots, d), BF16),
        compiler_params=_cparams("arbitrary"),
        name="moe_ffn",
    )(block_e, first, n_used, hb, w1, b1.reshape(nl, ne, 1, d2), w2, b2.reshape(nl, ne, 1, d))


def _route_kernel(lg_ref, tri_ref, idx_ref, gate_ref, rank_ref, cnt_ref, base_ref):
    i = pl.program_id(0)

    @pl.when(i == 0)
    def _():
        base_ref[...] = jnp.zeros_like(base_ref)

    lg = lg_ref[...]
    tr, ne = lg.shape
    lane = lax.broadcasted_iota(jnp.int32, (tr, ne), 1).astype(F32)
    out_lane = lax.broadcasted_iota(jnp.int32, (tr, LANES), 1)
    vals = lg
    sel = jnp.zeros((tr, ne), F32)
    picks, tops = [], []
    for _ in range(TOP_K):
        m = jnp.max(vals, axis=-1, keepdims=True)
        ix = jnp.min(jnp.where(vals == m, lane, float(ne)), axis=-1, keepdims=True)
        hit = lane == ix
        sel = jnp.where(hit, 1.0, sel)
        vals = jnp.where(hit, -jnp.inf, vals)
        picks.append(ix)
        tops.append(m)
    ex = [jnp.exp(t - tops[0]) for t in tops]
    den = ex[0] + ex[1] + ex[2] + ex[3]
    before = _dot(tri_ref[...], sel.astype(BF16)) + base_ref[0:1, :]
    idx_o = jnp.zeros((tr, LANES), F32)
    gate_o = jnp.zeros((tr, LANES), F32)
    rank_o = jnp.zeros((tr, LANES), F32)
    for k in range(TOP_K):
        rk = jnp.sum(jnp.where(lane == picks[k], before, 0.0), axis=-1, keepdims=True)
        idx_o = jnp.where(out_lane == k, picks[k], idx_o)
        gate_o = jnp.where(out_lane == k, ex[k] / den, gate_o)
        rank_o = jnp.where(out_lane == k, rk, rank_o)
    idx_ref[...] = idx_o.astype(jnp.int32)
    gate_ref[...] = gate_o
    rank_ref[...] = rank_o.astype(jnp.int32)
    total = base_ref[0:1, :] + jnp.sum(sel, axis=0, keepdims=True)
    base_ref[0:1, :] = total
    cnt_ref[...] = jnp.broadcast_to(total, cnt_ref.shape).astype(jnp.int32)


def _route(logits):
    n_tok, ne = logits.shape
    tr = TOKEN_TILE
    t = jnp.arange(tr)
    tri = (t[None, :] < t[:, None]).astype(BF16)
    tok = lambda: pl.BlockSpec((tr, LANES), lambda i: (i, 0))
    return pl.pallas_call(
        _route_kernel,
        grid=(n_tok // tr,),
        in_specs=[pl.BlockSpec((tr, ne), lambda i: (i, 0)), pl.BlockSpec((tr, tr), lambda i: (0, 0))],
        out_specs=[tok(), tok(), tok(), pl.BlockSpec((8, ne), lambda i: (0, 0))],
        out_shape=[jax.ShapeDtypeStruct((n_tok, LANES), jnp.int32), jax.ShapeDtypeStruct((n_tok, LANES), F32),
                   jax.ShapeDtypeStruct((n_tok, LANES), jnp.int32), jax.ShapeDtypeStruct((8, ne), jnp.int32)],
        scratch_shapes=[pltpu.VMEM((8, ne), F32)],
        compiler_params=_cparams("arbitrary"),
        name="route",
    )(logits, tri)


def _route_meta(idx, rank, counts):
    n_tok = idx.shape[0]
    ne = counts.shape[0]
    tm = MOE_TILE
    n_assign = n_tok * TOP_K
    padded = (counts + tm - 1) // tm * tm
    pend = jnp.cumsum(padded)
    pstart = pend - padded
    start = jnp.cumsum(counts) - counts
    eid = jnp.arange(ne, dtype=jnp.int32)
    slot = jnp.sum(jnp.where(idx[..., None] == eid, pstart, 0), axis=-1).astype(jnp.int32) + rank
    n_blocks = (n_assign + ne * (tm - 1) + tm - 1) // tm
    blk_start = jnp.arange(n_blocks, dtype=jnp.int32) * tm
    block_e = jnp.minimum(jnp.sum(pend[None, :] <= blk_start[:, None], axis=1), ne - 1).astype(jnp.int32)
    first = jnp.concatenate([jnp.ones((1,), jnp.int32), (block_e[1:] != block_e[:-1]).astype(jnp.int32)])
    n_used = (pend[-1] // tm).astype(jnp.int32).reshape(1)
    _, order = lax.sort_key_val(slot.reshape(-1), jnp.arange(n_assign, dtype=jnp.int32))
    off = jnp.arange(n_blocks * tm, dtype=jnp.int32) - jnp.repeat(pstart[block_e], tm)
    valid = off < jnp.repeat(counts[block_e], tm)
    pos = jnp.clip(jnp.repeat(start[block_e], tm) + off, 0, n_assign - 1)
    slot_tok = jnp.where(valid, order[pos] // TOP_K, 0).astype(jnp.int32)
    return slot, slot_tok, block_e, first, n_used


def _combine_kernel(x_ref, mod_ref, y_ref, gate_ref, g_ref, o_ref, *, final):
    gate = gate_ref[0]
    y = y_ref[0, 0].astype(F32) * gate[:, 0:1]
    for k in range(1, TOP_K):
        y = y + y_ref[k, 0].astype(F32) * gate[:, k:k + 1]
    x = x_ref[0] + mod_ref[0, 0, 0:1, :] * y
    if final:
        ms = jnp.mean(x * x, axis=-1, keepdims=True)
        x = x * lax.rsqrt(ms + NORM_EPS) * g_ref[...]
    o_ref[0] = x


def _combine(xs, mod, yg, gates, final_g, n_ctx_tiles, final):
    b, t, d = xs.shape
    tm = TOKEN_TILE
    skip = n_ctx_tiles if final else 0
    seg = lambda bi, i: (bi, jnp.where(i + skip >= n_ctx_tiles, 1, 0), 0, 0)
    return pl.pallas_call(
        functools.partial(_combine_kernel, final=final),
        grid=(b, t // tm - skip),
        in_specs=[pl.BlockSpec((1, tm, d), lambda bi, i: (bi, i + skip, 0)),
                  pl.BlockSpec((1, 1, 1, d), seg),
                  pl.BlockSpec((TOP_K, 1, tm, d), lambda bi, i: (0, bi, i + skip, 0)),
                  pl.BlockSpec((1, tm, LANES), lambda bi, i: (bi, i + skip, 0)),
                  pl.BlockSpec((1, d), lambda bi, i: (0, 0))],
        out_specs=pl.BlockSpec((1, tm, d), lambda bi, i: (bi, i, 0)),
        out_shape=jax.ShapeDtypeStruct((b, t - skip * tm, d), F32),
        compiler_params=_cparams("parallel", "parallel"),
        name="combine_final" if final else "combine",
    )(xs, mod, yg, gates, final_g.reshape(1, d))


def kernel(x, c, ctx, c_ctx, w_mod, b_mod, norm1_g, norm2_g, w_in, w_out, ret_decay_logit, ret_gn_g, ret_gn_b, rwkv_mu, rwkv_w0, rwkv_w_up, rwkv_a0, rwkv_a_up, rwkv_k_k, rwkv_k_a, rwkv_g_up, rwkv_r_k, rwkv_gn_g, rwkv_gn_b, lru_conv_w, lru_conv_b, lru_wa, lru_ba, lru_wx, lru_bx, lru_lambda, moe_w_router, moe_b_router, moe_w1, moe_b1, moe_w2, moe_b2, final_norm_g):
    bsz, seq, dm = x.shape
    n_ctx_tok = ctx.shape[1]
    depth = w_in.shape[0]
    n_experts = moe_w_router.shape[2]
    w_ret = 3 * dm // 8
    w_rw = 3 * dm // 8
    w_lru = dm - w_ret - w_rw
    zw = 3 * w_rw + DECAY_LORA + ICLR_LORA
    sizes = (4 * w_ret, zw, GATE_LORA, 2 * w_lru)
    bounds, off = [], 0
    for s in sizes:
        bounds.append((off, off + s))
        off += s
    bounds = tuple(bounds)
    assert off == w_in.shape[2]
    assert n_ctx_tok % TOKEN_TILE == 0 and seq % TOKEN_TILE == 0 and seq % GRID_W == 0
    t_all = n_ctx_tok + seq
    n_ctx_tiles = n_ctx_tok // TOKEN_TILE

    xs = jnp.concatenate([ctx, x], axis=1)
    cos_t, sin_t = _rope_tables(n_ctx_tok, seq, w_ret)
    hid = jnp.arange(w_ret) // HEAD_DIM
    e_bf = (hid[:, None] == hid[None, :]).astype(BF16)
    cond = jnp.concatenate([c, c_ctx[None, :], jnp.zeros((8 - (bsz + 1) % 8, dm), F32)], axis=0)

    for l in range(depth):
        last = l == depth - 1
        mod = _modulation(cond, w_mod, b_mod, l)
        mod_l = mod[:bsz].reshape(bsz, 6, dm)
        mod_c = jnp.broadcast_to(mod[bsz].reshape(1, 6, dm), (bsz, 6, dm))
        modsel = jnp.stack([mod_c, mod_l], axis=1)

        p_ret, p_z, p_gd, p_lru = _in_proj(xs, modsel[:, :, 0:2], norm1_g[l], w_in[l].astype(BF16), bounds, n_ctx_tiles)

        ret_o, rw_y, rw_bon, lru_h = [], [], [], []
        for d in range(2):
            rev = d == 1
            ret_o.append(_retention(p_ret, cos_t, sin_t, _ret_tables(ret_decay_logit[l, d], w_ret, rev),
                                    n_ctx_tok // RET_CHUNK, rev))
            prm = _rwkv_params(rwkv_mu[l, d], rwkv_w0[l, d], rwkv_w_up[l, d], rwkv_a0[l, d], rwkv_a_up[l, d],
                               rwkv_k_k[l, d], rwkv_k_a[l, d], rwkv_r_k[l], rev, bsz)
            y, bon = _rwkv(p_z, prm, n_ctx_tok // RWKV_CHUNK, rev)
            rw_y.append(y)
            rw_bon.append(bon)
            lru_h.append(_lru(p_lru, _lru_params(lru_conv_w[l, d], lru_conv_b[l, d], lru_wa[l, d], lru_ba[l, d],
                                                 lru_wx[l, d], lru_bx[l, d], lru_lambda[l, d]),
                              n_ctx_tok // LRU_CHUNK, rev))

        gn = jnp.concatenate([jnp.stack([ret_gn_g[l], ret_gn_b[l], rwkv_gn_g[l], rwkv_gn_b[l]]),
                              jnp.zeros((4, w_ret), F32)], axis=0)
        xs, h2, logits = _mix_out(xs, modsel[:, :, 2:5], ret_o[0], ret_o[1], p_ret, rw_y[0], rw_y[1],
                                  rw_bon[0], rw_bon[1], p_gd, lru_h[0], lru_h[1], p_lru,
                                  gn, e_bf, rwkv_g_up[l].astype(BF16), w_out[l].astype(BF16), norm2_g[l],
                                  moe_w_router[l], moe_b_router[l], n_ctx_tiles)

        n_tok = bsz * t_all
        idx, gates, rank, counts = _route(logits.reshape(n_tok, n_experts))
        slot, slot_tok, block_e, first, n_used = _route_meta(idx[:, :TOP_K], rank[:, :TOP_K], counts[0])
        y_sorted = _moe_ffn(h2.reshape(n_tok, dm)[slot_tok], block_e, first, n_used,
                            moe_w1, moe_b1, moe_w2, moe_b2, l)
        yg = y_sorted[slot.T].reshape(TOP_K, bsz, t_all, dm)
        xs = _combine(xs, modsel[:, :, 5:6], yg, gates.reshape(bsz, t_all, LANES), final_norm_g, n_ctx_tiles, last)
    return xs
```

```python
import functools

import jax
import jax.numpy as jnp
from jax import lax
from jax.experimental import pallas as pl
from jax.experimental.pallas import tpu as pltpu

F32 = jnp.float32
BF16 = jnp.bfloat16

HEAD_DIM = 64
NORM_EPS = 1e-6
RET_GN_EPS = 1e-5
RWKV_GN_EPS = 64e-5
ROPE_BASE = 10000.0
GRID_W = 64
LRU_CONV = 4
LRU_C = 8.0
TOP_K = 4
SWIGLU_LIMIT = 7.0
SWIGLU_ALPHA = 1.702
DECAY_LORA = 64
ICLR_LORA = 64
GATE_LORA = 128

LANES = 128
TOKEN_TILE = 256
RET_CHUNK = 128
RWKV_CHUNK = 64
LRU_CHUNK = 128
MOE_TILE = 512
VMEM_LIMIT = 56 * 1024 * 1024


def _cparams(*sem):
    return pltpu.CompilerParams(dimension_semantics=sem, vmem_limit_bytes=VMEM_LIMIT)


def _scan_chunk(i, n_ctx, n_tot, rev):
    if not rev:
        return i
    return jnp.where(i < n_ctx, n_ctx - 1 - i, n_tot + n_ctx - 1 - i)


def _split3(a):
    hi = a.astype(BF16)
    r1 = a - hi.astype(F32)
    mid = r1.astype(BF16)
    lo = (r1 - mid.astype(F32)).astype(BF16)
    return hi, mid, lo


def _dot(a, b):
    return jnp.dot(a, b, preferred_element_type=F32)


def _dot_nt(a, b):
    return lax.dot_general(a, b, (((1,), (1,)), ((), ())), preferred_element_type=F32)


def _dot_tn(a, b):
    return lax.dot_general(a, b, (((0,), (0,)), ((), ())), preferred_element_type=F32)


def _dot_exact_rhs(a, b_bf):
    hi, mid, lo = _split3(a)
    return _dot(hi, b_bf) + _dot(mid, b_bf) + _dot(lo, b_bf)


def _dot_exact_lhs(a_bf, b):
    hi, mid, lo = _split3(b)
    return _dot(a_bf, hi) + _dot(a_bf, mid) + _dot(a_bf, lo)


def _dot_x3(a, b):
    a_hi = a.astype(BF16)
    a_lo = (a - a_hi.astype(F32)).astype(BF16)
    b_hi = b.astype(BF16)
    b_lo = (b - b_hi.astype(F32)).astype(BF16)
    return _dot(a_hi, b_hi) + _dot(a_lo, b_hi) + _dot(a_hi, b_lo)


def _sigmoid(x):
    return 1.0 / (1.0 + jnp.exp(-x))


def _softplus(x):
    return jnp.maximum(x, 0.0) + jnp.log(1.0 + jnp.exp(-jnp.abs(x)))


def _mod_kernel(c_ref, w_ref, b_ref, o_ref):
    c = c_ref[...]
    s = c * _sigmoid(c)
    o_ref[...] = _dot_x3(s, w_ref[...]) + b_ref[...]


def _modulation(cond, w_mod, b_mod, layer):
    r, d = cond.shape
    nl, _, n = w_mod.shape
    tn = d
    return pl.pallas_call(
        _mod_kernel,
        grid=(n // tn,),
        in_specs=[pl.BlockSpec((r, d), lambda j: (0, 0)),
                  pl.BlockSpec((None, d, tn), lambda j: (layer, 0, j)),
                  pl.BlockSpec((None, 1, tn), lambda j: (layer, 0, j))],
        out_specs=pl.BlockSpec((r, tn), lambda j: (0, j)),
        out_shape=jax.ShapeDtypeStruct((r, n), F32),
        compiler_params=_cparams("arbitrary"),
        name="modulation",
    )(cond, w_mod, b_mod.reshape(nl, 1, n))


def _in_proj_kernel(x_ref, mod_ref, g_ref, w_ref, *o_refs, bounds):
    x = x_ref[0]
    ms = jnp.mean(x * x, axis=-1, keepdims=True)
    h = x * lax.rsqrt(ms + NORM_EPS) * g_ref[...]
    h = h * (1.0 + mod_ref[0, 0, 1:2, :]) + mod_ref[0, 0, 0:1, :]
    hb = h.astype(BF16)
    for o_ref, (lo, hi) in zip(o_refs, bounds):
        o_ref[0] = _dot(hb, w_ref[:, lo:hi])


def _in_proj(xs, mod, norm_g, w_in_bf, bounds, n_ctx_tiles):
    b, t, d = xs.shape
    tm = TOKEN_TILE
    p = w_in_bf.shape[1]
    seg = lambda bi, i: (bi, jnp.where(i >= n_ctx_tiles, 1, 0), 0, 0)
    return pl.pallas_call(
        functools.partial(_in_proj_kernel, bounds=bounds),
        grid=(b, t // tm),
        in_specs=[pl.BlockSpec((1, tm, d), lambda bi, i: (bi, i, 0)),
                  pl.BlockSpec((1, 1, 2, d), seg),
                  pl.BlockSpec((1, d), lambda bi, i: (0, 0)),
                  pl.BlockSpec((d, p), lambda bi, i: (0, 0))],
        out_specs=[pl.BlockSpec((1, tm, hi - lo), lambda bi, i: (bi, i, 0)) for lo, hi in bounds],
        out_shape=[jax.ShapeDtypeStruct((b, t, hi - lo), F32) for lo, hi in bounds],
        compiler_params=_cparams("parallel", "parallel"),
        name="in_proj",
    )(xs, mod, norm_g.reshape(1, d), w_in_bf)


def _ret_kernel(q_ref, k_ref, v_ref, cos_ref, sin_ref, dq_ref, dk_ref, dmat_ref, gm_ref, bm_ref,
                o_ref, s_ref, *, n_heads):
    i = pl.program_id(0)

    @pl.when(i == 0)
    def _():
        s_ref[...] = jnp.zeros_like(s_ref)

    nb, c, w = q_ref.shape
    cos = cos_ref[...]
    sin = sin_ref[...]
    lane = lax.broadcasted_iota(jnp.int32, (c, LANES), 1)
    first = (lane % 32) < 16

    def rope(u):
        parts = []
        for j in range(w // LANES):
            uj = u[:, j * LANES:(j + 1) * LANES]
            nxt = pltpu.roll(uj, LANES - 16, axis=1)
            prv = pltpu.roll(uj, 16, axis=1)
            parts.append(jnp.where(first, nxt, prv))
        return u * cos + jnp.concatenate(parts, axis=1) * sin

    bs = range(nb)
    head = lax.broadcasted_iota(jnp.int32, (c, w), 1) // HEAD_DIM
    q = [rope(q_ref[b]) for b in bs]
    k = [rope(k_ref[b]) for b in bs]
    vb = [v_ref[b].astype(BF16) for b in bs]
    s = [s_ref[b] for b in bs]
    out = [_dot((q[b] * dq_ref[...]).astype(BF16), s[b].astype(BF16)) for b in bs]
    sc = [_dot_nt(jnp.concatenate([jnp.where(head == h, q[b], 0.0) for h in range(n_heads)], axis=0).astype(BF16),
                  k[b].astype(BF16)) * dmat_ref[...] for b in bs]
    for h in range(n_heads):
        oh = [_dot(sc[b][h * c:(h + 1) * c].astype(BF16), vb[b]) for b in bs]
        out = [out[b] + jnp.where(head == h, oh[b], 0.0) for b in bs]
    ktv = [_dot_tn((k[b] * dk_ref[...]).astype(BF16), vb[b]) for b in bs]
    for b in bs:
        o_ref[b] = out[b]
        s_ref[b] = gm_ref[...] * s[b] + bm_ref[...] * ktv[b]


def _retention(p_ret, cos_t, sin_t, tabs, n_ctx, rev):
    b, t, w4 = p_ret.shape
    w = w4 // 4
    c = RET_CHUNK
    n_tot = t // c
    n_heads = w // HEAD_DIM
    dq, dk, dmat, gm, bm = tabs
    tix = lambda i: _scan_chunk(i, n_ctx, n_tot, rev)
    col = lambda j: (lambda i: (0, tix(i), j))
    const = lambda i: (0, 0)
    return pl.pallas_call(
        functools.partial(_ret_kernel, n_heads=n_heads),
        grid=(n_tot,),
        in_specs=[pl.BlockSpec((b, c, w), col(0)), pl.BlockSpec((b, c, w), col(1)), pl.BlockSpec((b, c, w), col(2)),
                  pl.BlockSpec((c, w), lambda i: (tix(i), 0)),
                  pl.BlockSpec((c, w), lambda i: (tix(i), 0)),
                  pl.BlockSpec((c, w), const), pl.BlockSpec((c, w), const),
                  pl.BlockSpec((n_heads * c, c), const),
                  pl.BlockSpec((w, w), const), pl.BlockSpec((w, w), const)],
        out_specs=pl.BlockSpec((b, c, w), lambda i: (0, tix(i), 0)),
        out_shape=jax.ShapeDtypeStruct((b, t, w), F32),
        scratch_shapes=[pltpu.VMEM((b, w, w), F32)],
        compiler_params=_cparams("arbitrary"),
        name="retention_rev" if rev else "retention_fwd",
    )(p_ret, p_ret, p_ret, cos_t, sin_t, dq, dk, dmat, gm, bm)


def _ret_tables(decay_logit, w, rev):
    n_heads = w // HEAD_DIM
    c = RET_CHUNK
    lg = jax.nn.log_sigmoid(decay_logit.astype(F32))
    t = jnp.arange(c, dtype=F32)
    p = (c - 1.0 - t) if rev else t
    rel = p[:, None] - p[None, :]
    scale = HEAD_DIM ** -0.5
    dmat = jnp.where(rel >= 0, jnp.exp(lg[:, None, None] * jnp.maximum(rel, 0.0)), 0.0) * scale
    dq = jnp.exp(lg[:, None] * (p + 1.0)) * scale
    dk = jnp.exp(lg[:, None] * (c - 1.0 - p))
    lanes = lambda a: jnp.repeat(a.T, HEAD_DIM, axis=1)
    hid = jnp.arange(w) // HEAD_DIM
    bm = (hid[:, None] == hid[None, :]).astype(F32)
    gm = bm * jnp.exp(lg * c)[hid][:, None]
    return lanes(dq), lanes(dk), dmat.reshape(n_heads * c, c), gm, bm


def _rope_tables(n_ctx_tok, seq, w):
    half = HEAD_DIM // 2
    quarter = half // 2
    inv_freq = ROPE_BASE ** (-jnp.arange(quarter, dtype=F32) / quarter)
    tok = jnp.arange(seq)
    rows = (tok // GRID_W).astype(F32)
    cols = (tok % GRID_W).astype(F32)
    o = jnp.arange(w) % HEAD_DIM
    pos = jnp.where(o[None, :] < half, rows[:, None], cols[:, None])
    ang = pos * inv_freq[o % quarter][None, :]
    sign = jnp.where((o % half) < quarter, -1.0, 1.0)[None, :]
    cos = jnp.concatenate([jnp.ones((n_ctx_tok, w), F32), jnp.cos(ang)], axis=0)
    sin = jnp.concatenate([jnp.zeros((n_ctx_tok, w), F32), jnp.sin(ang) * sign], axis=0)
    return cos, sin


def _rwkv_kernel(z_ref, mu_ref, vec_ref, wup_ref, aup_ref, e_ref, minc_ref, strict_ref, incl_ref,
                 y_ref, bon_ref, st_ref, zprev_ref, *, rev, n_ctx, w):
    i = pl.program_id(0)
    nb, c, zw = z_ref.shape
    n_pairs = w // LANES
    rows = nb * c

    @pl.when(i == 0)
    def _():
        st_ref[...] = jnp.zeros_like(st_ref)

    @pl.when((i == 0) | (i == n_ctx))
    def _():
        zprev_ref[...] = jnp.zeros_like(zprev_ref)

    z = z_ref[...].reshape(rows, zw)
    rin = lax.broadcasted_iota(jnp.int32, (rows, zw), 0) % c
    prev = jnp.concatenate([jnp.broadcast_to(zprev_ref[b, 0:1, :], (c, zw)) for b in range(nb)], axis=0)
    if rev:
        zs = jnp.where(rin == c - 1, prev, pltpu.roll(z, rows - 1, axis=0))
        for b in range(nb):
            zprev_ref[b, 0:1, :] = z[b * c:b * c + 1, :]
    else:
        zs = jnp.where(rin == 0, prev, pltpu.roll(z, 1, axis=0))
        for b in range(nb):
            zprev_ref[b, 0:1, :] = z[b * c + c - 1:b * c + c, :]
    zd = z + (zs - z) * mu_ref[...]

    r = zd[:, 0:w]
    k = zd[:, w:2 * w]
    v = zd[:, 2 * w:3 * w]
    lora = zd[:, 3 * w:3 * w + LANES]
    lane = lax.broadcasted_iota(jnp.int32, (rows, LANES), 1)
    lora = jnp.where(lane < DECAY_LORA, jnp.tanh(lora), lora)

    w0, a0, k_k, k_a, r_k = (vec_ref[j:j + 1, :] for j in range(5))
    e_bf = e_ref[...]
    w_log = -_softplus(-(w0 + _dot_x3(lora, wup_ref[...]))) - 0.5
    logw = -jnp.exp(w_log)
    a = _sigmoid(a0 + _dot_x3(lora, aup_ref[...]))
    kk0 = k * k_k
    kk = kk0 / jnp.maximum(jnp.sqrt(_dot_exact_rhs(kk0 * kk0, e_bf)), 1e-12)
    k2 = k * (1.0 + (a - 1.0) * k_a)
    bon_ref[...] = (_dot((r * k2 * r_k).astype(BF16), e_bf) * v).reshape(nb, c, w)

    cinc = _dot_exact_lhs(minc_ref[...], logw)
    e_inc = jnp.exp(cinc)
    e_neg = jnp.exp(-cinc)
    rt = r * e_inc
    kt = k2 * e_neg
    bt = kk * a * e_neg
    kkt = kk * jnp.exp(cinc - logw)
    last = 0 if rev else c - 1

    strict = strict_ref[...] > 0.0
    incl = incl_ref[...] > 0.0
    lane_lo = lax.broadcasted_iota(jnp.int32, (c, LANES), 1) < HEAD_DIM
    head_r = lax.broadcasted_iota(jnp.int32, (LANES, LANES), 0) // HEAD_DIM
    head_c = lax.broadcasted_iota(jnp.int32, (LANES, LANES), 1) // HEAD_DIM
    diag = head_r == head_c

    def stack(xw):
        return jnp.concatenate([jnp.where(lane_lo, xw, 0.0), jnp.where(lane_lo, 0.0, xw)], axis=0)

    def unstack(xs):
        return xs[0:c] + xs[c:2 * c]

    chains = [(b, j) for b in range(nb) for j in range(n_pairs)]

    def win(x, ch):
        b, j = ch
        return x[b * c:(b + 1) * c, j * LANES:(j + 1) * LANES]

    st = [st_ref[b * n_pairs + j] for b, j in chains]
    g = [_dot_nt(jnp.concatenate([stack(win(kkt, ch)), stack(win(rt, ch))], axis=0).astype(BF16),
                 jnp.concatenate([stack(win(bt, ch)), stack(win(kt, ch))], axis=0).astype(BF16)) for ch in chains]
    a_b = [jnp.where(strict, x[0:2 * c, 0:2 * c], 0.0).astype(BF16) for x in g]
    a_k = [jnp.where(strict, x[0:2 * c, 2 * c:4 * c], 0.0).astype(BF16) for x in g]
    r_kb = [jnp.concatenate([jnp.where(incl, x[2 * c:4 * c, 2 * c:4 * c], 0.0),
                             -jnp.where(incl, x[2 * c:4 * c, 0:2 * c], 0.0)], axis=1).astype(BF16) for x in g]
    x0 = [_dot_nt(jnp.concatenate([win(kkt, ch), win(rt, ch)], axis=0).astype(BF16), s.astype(BF16))
          for ch, s in zip(chains, st)]
    v_sb = [stack(win(v, ch)).astype(BF16) for ch in chains]
    u = [stack(x[0:c]) + _dot(ak, vs) for x, ak, vs in zip(x0, a_k, v_sb)]

    pw = a_b
    steps, sign = 1, -1.0
    while 2 * steps < c:
        both = [_dot(p, jnp.concatenate([p, x.astype(BF16)], axis=1)) for p, x in zip(pw, u)]
        u = [x + sign * y[:, 2 * c:4 * c] for x, y in zip(u, both)]
        pw = [y[:, 0:2 * c].astype(BF16) for y in both]
        steps, sign = 2 * steps, 1.0
    u = [x + sign * _dot(p, x.astype(BF16)) for x, p in zip(u, pw)]

    y_s = [_dot(rk, jnp.concatenate([vs, x.astype(BF16)], axis=0)) for rk, vs, x in zip(r_kb, v_sb, u)]
    y = [x[c:2 * c] + unstack(ys) for x, ys in zip(x0, y_s)]
    upd = [_dot_tn(jnp.concatenate([win(v, ch), unstack(x)], axis=0).astype(BF16),
                   jnp.concatenate([win(kt, ch), -win(bt, ch)], axis=0).astype(BF16))
           for ch, x in zip(chains, u)]
    for n, (b, j) in enumerate(chains):
        w_end = e_inc[b * c + last:b * c + last + 1, j * LANES:(j + 1) * LANES]
        st_ref[b * n_pairs + j] = jnp.where(diag, (st[n] + upd[n]) * w_end, 0.0)
        y_ref[b, :, j * LANES:(j + 1) * LANES] = y[n]


def _rwkv(p_z, prm, n_ctx, rev):
    b, t, zw = p_z.shape
    w = (zw - DECAY_LORA - ICLR_LORA) // 3
    c = RWKV_CHUNK
    n_tot = t // c
    mu, vecs, wup, aup, e_bf, minc, strict, incl = prm
    tix = lambda i: _scan_chunk(i, n_ctx, n_tot, rev)
    const = lambda i: (0, 0)
    full = lambda a: pl.BlockSpec(a.shape, const)
    return pl.pallas_call(
        functools.partial(_rwkv_kernel, rev=rev, n_ctx=n_ctx, w=w),
        grid=(n_tot,),
        in_specs=[pl.BlockSpec((b, c, zw), lambda i: (0, tix(i), 0)),
                  full(mu), full(vecs), full(wup), full(aup), full(e_bf), full(minc), full(strict), full(incl)],
        out_specs=[pl.BlockSpec((b, c, w), lambda i: (0, tix(i), 0)),
                   pl.BlockSpec((b, c, w), lambda i: (0, tix(i), 0))],
        out_shape=[jax.ShapeDtypeStruct((b, t, w), F32), jax.ShapeDtypeStruct((b, t, w), F32)],
        scratch_shapes=[pltpu.VMEM((b * (w // LANES), LANES, LANES), F32), pltpu.VMEM((b, 8, zw), F32)],
        compiler_params=_cparams("arbitrary"),
        name="rwkv7_rev" if rev else "rwkv7_fwd",
    )(p_z, mu, vecs, wup, aup, e_bf, minc, strict, incl)


def _rwkv_params(mu, w0, w_up, a0, a_up, k_k, k_a, r_k, rev, n_batch):
    w = w0.shape[0]
    c = RWKV_CHUNK
    vecs = jnp.concatenate([jnp.stack([w0, a0, k_k, k_a, r_k]), jnp.zeros((3, w), F32)], axis=0)
    wup = jnp.concatenate([w_up, jnp.zeros((ICLR_LORA, w), F32)], axis=0)
    aup = jnp.concatenate([jnp.zeros((DECAY_LORA, w), F32), a_up], axis=0)
    hid = jnp.arange(w) // HEAD_DIM
    e_bf = (hid[:, None] == hid[None, :]).astype(BF16)
    t = jnp.arange(c)
    p = (c - 1 - t) if rev else t
    le = p[None, :] <= p[:, None]
    lt = p[None, :] < p[:, None]
    blk = jnp.arange(2 * c) // c
    same = blk[:, None] == blk[None, :]
    strict = (jnp.tile(lt, (2, 2)) & same).astype(F32)
    incl = (jnp.tile(le, (2, 2)) & same).astype(F32)
    minc = jnp.kron(jnp.eye(n_batch, dtype=F32), le.astype(F32)).astype(BF16)
    return mu.reshape(1, -1), vecs, wup, aup, e_bf, minc, strict, incl


def _lru_kernel(x_ref, cw_ref, vec_ref, wa_ref, wx_ref, h_ref, hcar_ref, ucar_ref, *, rev, n_ctx):
    i = pl.program_id(0)
    nb, c, w = x_ref.shape
    rows = nb * c

    @pl.when(i == 0)
    def _():
        hcar_ref[...] = jnp.zeros_like(hcar_ref)

    @pl.when((i == 0) | (i == n_ctx))
    def _():
        ucar_ref[...] = jnp.zeros_like(ucar_ref)

    u0 = x_ref[...].reshape(rows, w)
    row = lax.broadcasted_iota(jnp.int32, (rows, w), 0) % c

    def per_batch(ref, j):
        return jnp.concatenate([jnp.broadcast_to(ref[b, j:j + 1, :], (c, w)) for b in range(nb)], axis=0)

    def shifted(x, s, carry, fill):
        if rev:
            rolled = pltpu.roll(x, rows - s, axis=0)
            edge = row >= c - s
        else:
            rolled = pltpu.roll(x, s, axis=0)
            edge = row < s
        if carry is None:
            return jnp.where(edge, fill, rolled)
        return jnp.where(edge, carry, rolled)

    conv = vec_ref[0:1, :] + cw_ref[LRU_CONV - 1:LRU_CONV, :] * u0
    for m in range(1, LRU_CONV):
        car = jnp.zeros((rows, w), F32)
        for qpos in range(m):
            r_idx = (c - 1 - qpos) if rev else qpos
            car = jnp.where(row == r_idx, per_batch(ucar_ref, m - qpos - 1), car)
        conv = conv + cw_ref[LRU_CONV - 1 - m:LRU_CONV - m, :] * shifted(u0, m, car, None)
    for m in range(1, LRU_CONV):
        r_idx = (m - 1) if rev else (c - m)
        for b in range(nb):
            ucar_ref[b, m - 1:m, :] = u0[b * c + r_idx:b * c + r_idx + 1, :]

    cb = conv.astype(BF16)
    r = _sigmoid(_dot(cb, wa_ref[...]) + vec_ref[1:2, :])
    ig = _sigmoid(_dot(cb, wx_ref[...]) + vec_ref[2:3, :])
    log_a = -LRU_C * r * vec_ref[3:4, :]
    a = jnp.exp(log_a)
    bb = jnp.sqrt(1.0 - jnp.exp(2.0 * log_a)) * (ig * conv)

    s = 1
    while s < c:
        bb = bb + a * shifted(bb, s, None, 0.0)
        a = a * shifted(a, s, None, 1.0)
        s *= 2
    h = bb + a * per_batch(hcar_ref, 0)
    h_ref[...] = h.reshape(nb, c, w)
    last = 0 if rev else c - 1
    for b in range(nb):
        hcar_ref[b, 0:1, :] = h[b * c + last:b * c + last + 1, :]


def _lru(p_lru, prm, n_ctx, rev):
    b, t, w2 = p_lru.shape
    w = w2 // 2
    c = LRU_CHUNK
    n_tot = t // c
    cw, vecs, wa, wx = prm
    tix = lambda i: _scan_chunk(i, n_ctx, n_tot, rev)
    const = lambda i: (0, 0)
    return pl.pallas_call(
        functools.partial(_lru_kernel, rev=rev, n_ctx=n_ctx),
        grid=(n_tot,),
        in_specs=[pl.BlockSpec((b, c, w), lambda i: (0, tix(i), 0)),
                  pl.BlockSpec(cw.shape, const), pl.BlockSpec(vecs.shape, const),
                  pl.BlockSpec(wa.shape, const), pl.BlockSpec(wx.shape, const)],
        out_specs=pl.BlockSpec((b, c, w), lambda i: (0, tix(i), 0)),
        out_shape=jax.ShapeDtypeStruct((b, t, w), F32),
        scratch_shapes=[pltpu.VMEM((b, 8, w), F32), pltpu.VMEM((b, 8, w), F32)],
        compiler_params=_cparams("arbitrary"),
        name="rglru_rev" if rev else "rglru_fwd",
    )(p_lru, cw, vecs, wa, wx)


def _lru_params(conv_w, conv_b, wa, ba, wx, bx, lam):
    w = conv_b.shape[0]
    cw = jnp.concatenate([conv_w, jnp.zeros((8 - LRU_CONV, w), F32)], axis=0)
    vecs = jnp.concatenate([jnp.stack([conv_b, ba, bx, jax.nn.softplus(-lam)]), jnp.zeros((4, w), F32)], axis=0)
    return cw, vecs, jax.scipy.linalg.block_diag(*wa).astype(BF16), jax.scipy.linalg.block_diag(*wx).astype(BF16)


def _head_norm(y, e_bf, gain, bias, eps):
    inv = 1.0 / HEAD_DIM
    mu = _dot(y.astype(BF16), e_bf) * inv
    yc = y - mu
    var = _dot((yc * yc).astype(BF16), e_bf) * inv
    return yc * lax.rsqrt(var + eps) * gain + bias


def _mix_out_kernel(x_ref, mod_ref, of_ref, ob_ref, g_ref, yf_ref, yb_ref, bf_ref, bb_ref, gd_ref,
                    hf_ref, hb_ref, lg_ref, gn_ref, e_ref, gup_ref, wout_ref, n2g_ref, wr_ref, br_ref, tri_ref,
                    xo_ref, h2_ref, idx_ref, gate_ref, rank_ref, cnt_ref, base_ref, *, w_ret, w_rw):
    @pl.when((pl.program_id(0) == 0) & (pl.program_id(1) == 0))
    def _():
        base_ref[...] = jnp.zeros_like(base_ref)

    e_bf = e_ref[...]
    g = g_ref[0]
    ret = _head_norm(of_ref[0] + ob_ref[0], e_bf, gn_ref[0:1, :], gn_ref[1:2, :], RET_GN_EPS)
    ret = ret * (g * _sigmoid(g))
    gate = _dot(_sigmoid(gd_ref[0]).astype(BF16), gup_ref[...])
    rw = _head_norm(yf_ref[0] + yb_ref[0], e_bf, gn_ref[2:3, :], gn_ref[3:4, :], RWKV_GN_EPS)
    rw = (rw + bf_ref[0] + bb_ref[0]) * gate
    lg = lg_ref[0]
    gelu = 0.5 * lg * (1.0 + jnp.tanh(0.7978845608028654 * (lg + 0.044715 * (lg * lg * lg))))
    lru = (hf_ref[0] + hb_ref[0]) * gelu
    mix = (_dot(ret.astype(BF16), wout_ref[0:w_ret, :])
           + _dot(rw.astype(BF16), wout_ref[w_ret:w_ret + w_rw, :])
           + _dot(lru.astype(BF16), wout_ref[w_ret + w_rw:, :]))
    x = x_ref[0] + mod_ref[0, 0, 0:1, :] * mix
    xo_ref[0] = x
    ms = jnp.mean(x * x, axis=-1, keepdims=True)
    h2 = x * lax.rsqrt(ms + NORM_EPS) * n2g_ref[...]
    h2 = h2 * (1.0 + mod_ref[0, 0, 2:3, :]) + mod_ref[0, 0, 1:2, :]
    h2_ref[0] = h2.astype(BF16)
    logits = _dot_x3(h2, wr_ref[...]) + br_ref[...]
    idx_o, gate_o, rank_o = _route_tile(logits, tri_ref[...], base_ref)
    idx_ref[0] = idx_o
    gate_ref[0] = gate_o
    rank_ref[0] = rank_o
    cnt_ref[...] = base_ref[...].astype(jnp.int32)


def _mix_out(xs, mod, o_f, o_b, p_ret, y_f, y_b, bon_f, bon_b, p_gd, h_f, h_b, p_lru,
             gn, e_bf, g_up_bf, w_out_bf, norm2_g, w_router, b_router, n_ctx_tiles):
    b, t, d = xs.shape
    tm = TOKEN_TILE
    w_ret, w_rw, w_lru = o_f.shape[2], y_f.shape[2], h_f.shape[2]
    ne = w_router.shape[1]
    tok = lambda wd, j=0: pl.BlockSpec((1, tm, wd), lambda bi, i: (bi, i, j))
    const = lambda a: pl.BlockSpec(a.shape, lambda bi, i: (0,) * a.ndim)
    seg = lambda bi, i: (bi, jnp.where(i >= n_ctx_tiles, 1, 0), 0, 0)
    n2g = norm2_g.reshape(1, d)
    br = b_router.reshape(1, ne)
    tt = jnp.arange(tm)
    tri = (tt[None, :] < tt[:, None]).astype(BF16)
    return pl.pallas_call(
        functools.partial(_mix_out_kernel, w_ret=w_ret, w_rw=w_rw),
        grid=(b, t // tm),
        in_specs=[tok(d), pl.BlockSpec((1, 1, 3, d), seg),
                  tok(w_ret), tok(w_ret), tok(w_ret, 3),
                  tok(w_rw), tok(w_rw), tok(w_rw), tok(w_rw), tok(GATE_LORA),
                  tok(w_lru), tok(w_lru), tok(w_lru, 1),
                  const(gn), const(e_bf), const(g_up_bf), const(w_out_bf), const(n2g), const(w_router), const(br),
                  const(tri)],
        out_specs=[tok(d), tok(d), tok(LANES), tok(LANES), tok(LANES),
                   pl.BlockSpec((8, ne), lambda bi, i: (0, 0))],
        out_shape=[jax.ShapeDtypeStruct((b, t, d), F32), jax.ShapeDtypeStruct((b, t, d), BF16),
                   jax.ShapeDtypeStruct((b, t, LANES), jnp.int32), jax.ShapeDtypeStruct((b, t, LANES), F32),
                   jax.ShapeDtypeStruct((b, t, LANES), jnp.int32), jax.ShapeDtypeStruct((8, ne), jnp.int32)],
        scratch_shapes=[pltpu.VMEM((8, ne), F32)],
        compiler_params=_cparams("arbitrary", "arbitrary"),
        name="mix_out",
    )(xs, mod, o_f, o_b, p_ret, y_f, y_b, bon_f, bon_b, p_gd, h_f, h_b, p_lru,
      gn, e_bf, g_up_bf, w_out_bf, n2g, w_router, br, tri)


def _moe_kernel(be_ref, first_ref, nu_ref, x_ref, w1_ref, b1_ref, w2_ref, b2_ref, y_ref, w1b_ref, w2b_ref):
    i = pl.program_id(0)
    de = w2_ref.shape[1]

    @pl.when(first_ref[i] == 1)
    def _():
        w1b_ref[...] = w1_ref[0].astype(BF16)
        w2b_ref[...] = w2_ref[0].astype(BF16)

    @pl.when(i < nu_ref[0])
    def _():
        gu = _dot(x_ref[...], w1b_ref[...]) + b1_ref[0]
        glu = jnp.minimum(gu[:, :de], SWIGLU_LIMIT)
        lin = jnp.clip(gu[:, de:], -SWIGLU_LIMIT, SWIGLU_LIMIT)
        act = glu * _sigmoid(SWIGLU_ALPHA * glu) * (lin + 1.0)
        y_ref[...] = (_dot(act.astype(BF16), w2b_ref[...]) + b2_ref[0]).astype(y_ref.dtype)

    @pl.when(i >= nu_ref[0])
    def _():
        y_ref[...] = jnp.zeros_like(y_ref)


def _moe_ffn(hb, block_e, first, n_used, w1, b1, w2, b2, layer):
    n_slots, d = hb.shape
    tm = MOE_TILE
    nl, ne, _, d2 = w1.shape
    de = w2.shape[2]
    wsel = lambda i, be, fi, nu: (layer, be[i], 0, 0)
    return pl.pallas_call(
        _moe_kernel,
        grid_spec=pltpu.PrefetchScalarGridSpec(
            num_scalar_prefetch=3,
            grid=(n_slots // tm,),
            in_specs=[pl.BlockSpec((tm, d), lambda i, be, fi, nu: (i, 0)),
                      pl.BlockSpec((None, 1, d, d2), wsel),
                      pl.BlockSpec((None, 1, 1, d2), wsel),
                      pl.BlockSpec((None, 1, de, d), wsel),
                      pl.BlockSpec((None, 1, 1, d), wsel)],
            out_specs=pl.BlockSpec((tm, d), lambda i, be, fi, nu: (i, 0)),
            scratch_shapes=[pltpu.VMEM((d, d2), BF16), pltpu.VMEM((de, d), BF16)],
        ),
        out_shape=jax.ShapeDtypeStruct((n_slots, d), BF16),
        compiler_params=_cparams("arbitrary"),
        name="moe_ffn",
    )(block_e, first, n_used, hb, w1, b1.reshape(nl, ne, 1, d2), w2, b2.reshape(nl, ne, 1, d))


def _route_tile(lg, tri, base_ref):
    tr, ne = lg.shape
    lane = lax.broadcasted_iota(jnp.int32, (tr, ne), 1).astype(F32)
    out_lane = lax.broadcasted_iota(jnp.int32, (tr, LANES), 1)
    vals = lg
    sel = jnp.zeros((tr, ne), F32)
    picks, tops = [], []
    for _ in range(TOP_K):
        m = jnp.max(vals, axis=-1, keepdims=True)
        ix = jnp.min(jnp.where(vals == m, lane, float(ne)), axis=-1, keepdims=True)
        hit = lane == ix
        sel = jnp.where(hit, 1.0, sel)
        vals = jnp.where(hit, -jnp.inf, vals)
        picks.append(ix)
        tops.append(m)
    ex = [jnp.exp(t - tops[0]) for t in tops]
    den = ex[0] + ex[1] + ex[2] + ex[3]
    before = _dot(tri, sel.astype(BF16)) + base_ref[0:1, :]
    idx_o = jnp.zeros((tr, LANES), F32)
    gate_o = jnp.zeros((tr, LANES), F32)
    rank_o = jnp.zeros((tr, LANES), F32)
    for k in range(TOP_K):
        rk = jnp.sum(jnp.where(lane == picks[k], before, 0.0), axis=-1, keepdims=True)
        idx_o = jnp.where(out_lane == k, picks[k], idx_o)
        gate_o = jnp.where(out_lane == k, ex[k] / den, gate_o)
        rank_o = jnp.where(out_lane == k, rk, rank_o)
    total = base_ref[0:1, :] + jnp.sum(sel, axis=0, keepdims=True)
    base_ref[...] = jnp.broadcast_to(total, base_ref.shape)
    return idx_o.astype(jnp.int32), gate_o, rank_o.astype(jnp.int32)


def _route_meta(idx, rank, counts):
    n_tok = idx.shape[0]
    ne = counts.shape[0]
    tm = MOE_TILE
    n_assign = n_tok * TOP_K
    padded = (counts + tm - 1) // tm * tm
    pend = jnp.cumsum(padded)
    pstart = pend - padded
    start = jnp.cumsum(counts) - counts
    eid = jnp.arange(ne, dtype=jnp.int32)
    slot = jnp.sum(jnp.where(idx[..., None] == eid, pstart, 0), axis=-1).astype(jnp.int32) + rank
    n_blocks = (n_assign + ne * (tm - 1) + tm - 1) // tm
    blk_start = jnp.arange(n_blocks, dtype=jnp.int32) * tm
    block_e = jnp.minimum(jnp.sum(pend[None, :] <= blk_start[:, None], axis=1), ne - 1).astype(jnp.int32)
    first = jnp.concatenate([jnp.ones((1,), jnp.int32), (block_e[1:] != block_e[:-1]).astype(jnp.int32)])
    n_used = (pend[-1] // tm).astype(jnp.int32).reshape(1)
    _, order = lax.sort_key_val(slot.reshape(-1), jnp.arange(n_assign, dtype=jnp.int32))
    off = jnp.arange(n_blocks * tm, dtype=jnp.int32) - jnp.repeat(pstart[block_e], tm)
    valid = off < jnp.repeat(counts[block_e], tm)
    pos = jnp.clip(jnp.repeat(start[block_e], tm) + off, 0, n_assign - 1)
    spread = jnp.arange(n_blocks * tm, dtype=jnp.int32) % n_tok
    slot_tok = jnp.where(valid, order[pos] // TOP_K, spread).astype(jnp.int32)
    return slot, slot_tok, block_e, first, n_used


def _combine_kernel(x_ref, mod_ref, y_ref, gate_ref, g_ref, o_ref, *, final):
    gate = gate_ref[0]
    y = y_ref[0, 0].astype(F32) * gate[:, 0:1]
    for k in range(1, TOP_K):
        y = y + y_ref[k, 0].astype(F32) * gate[:, k:k + 1]
    x = x_ref[0] + mod_ref[0, 0, 0:1, :] * y
    if final:
        ms = jnp.mean(x * x, axis=-1, keepdims=True)
        x = x * lax.rsqrt(ms + NORM_EPS) * g_ref[...]
    o_ref[0] = x


def _combine(xs, mod, yg, gates, final_g, n_ctx_tiles, final):
    b, t, d = xs.shape
    tm = TOKEN_TILE
    skip = n_ctx_tiles if final else 0
    seg = lambda bi, i: (bi, jnp.where(i + skip >= n_ctx_tiles, 1, 0), 0, 0)
    return pl.pallas_call(
        functools.partial(_combine_kernel, final=final),
        grid=(b, t // tm - skip),
        in_specs=[pl.BlockSpec((1, tm, d), lambda bi, i: (bi, i + skip, 0)),
                  pl.BlockSpec((1, 1, 1, d), seg),
                  pl.BlockSpec((TOP_K, 1, tm, d), lambda bi, i: (0, bi, i + skip, 0)),
                  pl.BlockSpec((1, tm, LANES), lambda bi, i: (bi, i + skip, 0)),
                  pl.BlockSpec((1, d), lambda bi, i: (0, 0))],
        out_specs=pl.BlockSpec((1, tm, d), lambda bi, i: (bi, i, 0)),
        out_shape=jax.ShapeDtypeStruct((b, t - skip * tm, d), F32),
        compiler_params=_cparams("parallel", "parallel"),
        name="combine_final" if final else "combine",
    )(xs, mod, yg, gates, final_g.reshape(1, d))


def kernel(x, c, ctx, c_ctx, w_mod, b_mod, norm1_g, norm2_g, w_in, w_out, ret_decay_logit, ret_gn_g, ret_gn_b, rwkv_mu, rwkv_w0, rwkv_w_up, rwkv_a0, rwkv_a_up, rwkv_k_k, rwkv_k_a, rwkv_g_up, rwkv_r_k, rwkv_gn_g, rwkv_gn_b, lru_conv_w, lru_conv_b, lru_wa, lru_ba, lru_wx, lru_bx, lru_lambda, moe_w_router, moe_b_router, moe_w1, moe_b1, moe_w2, moe_b2, final_norm_g):
    bsz, seq, dm = x.shape
    n_ctx_tok = ctx.shape[1]
    depth = w_in.shape[0]
    n_experts = moe_w_router.shape[2]
    w_ret = 3 * dm // 8
    w_rw = 3 * dm // 8
    w_lru = dm - w_ret - w_rw
    zw = 3 * w_rw + DECAY_LORA + ICLR_LORA
    sizes = (4 * w_ret, zw, GATE_LORA, 2 * w_lru)
    bounds, off = [], 0
    for s in sizes:
        bounds.append((off, off + s))
        off += s
    bounds = tuple(bounds)
    assert off == w_in.shape[2]
    assert n_ctx_tok % TOKEN_TILE == 0 and seq % TOKEN_TILE == 0 and seq % GRID_W == 0
    t_all = n_ctx_tok + seq
    n_ctx_tiles = n_ctx_tok // TOKEN_TILE

    xs = jnp.concatenate([ctx, x], axis=1)
    cos_t, sin_t = _rope_tables(n_ctx_tok, seq, w_ret)
    hid = jnp.arange(w_ret) // HEAD_DIM
    e_bf = (hid[:, None] == hid[None, :]).astype(BF16)
    cond = jnp.concatenate([c, c_ctx[None, :], jnp.zeros((8 - (bsz + 1) % 8, dm), F32)], axis=0)

    for l in range(depth):
        last = l == depth - 1
        mod = _modulation(cond, w_mod, b_mod, l)
        mod_l = mod[:bsz].reshape(bsz, 6, dm)
        mod_c = jnp.broadcast_to(mod[bsz].reshape(1, 6, dm), (bsz, 6, dm))
        modsel = jnp.stack([mod_c, mod_l], axis=1)

        p_ret, p_z, p_gd, p_lru = _in_proj(xs, modsel[:, :, 0:2], norm1_g[l], w_in[l].astype(BF16), bounds, n_ctx_tiles)

        ret_o, rw_y, rw_bon, lru_h = [], [], [], []
        for d in range(2):
            rev = d == 1
            ret_o.append(_retention(p_ret, cos_t, sin_t, _ret_tables(ret_decay_logit[l, d], w_ret, rev),
                                    n_ctx_tok // RET_CHUNK, rev))
            prm = _rwkv_params(rwkv_mu[l, d], rwkv_w0[l, d], rwkv_w_up[l, d], rwkv_a0[l, d], rwkv_a_up[l, d],
                               rwkv_k_k[l, d], rwkv_k_a[l, d], rwkv_r_k[l], rev, bsz)
            y, bon = _rwkv(p_z, prm, n_ctx_tok // RWKV_CHUNK, rev)
            rw_y.append(y)
            rw_bon.append(bon)
            lru_h.append(_lru(p_lru, _lru_params(lru_conv_w[l, d], lru_conv_b[l, d], lru_wa[l, d], lru_ba[l, d],
                                                 lru_wx[l, d], lru_bx[l, d], lru_lambda[l, d]),
                              n_ctx_tok // LRU_CHUNK, rev))

        gn = jnp.concatenate([jnp.stack([ret_gn_g[l], ret_gn_b[l], rwkv_gn_g[l], rwkv_gn_b[l]]),
                              jnp.zeros((4, w_ret), F32)], axis=0)
        xs, h2, idx, gates, rank, counts = _mix_out(
            xs, modsel[:, :, 2:5], ret_o[0], ret_o[1], p_ret, rw_y[0], rw_y[1],
            rw_bon[0], rw_bon[1], p_gd, lru_h[0], lru_h[1], p_lru,
            gn, e_bf, rwkv_g_up[l].astype(BF16), w_out[l].astype(BF16), norm2_g[l],
            moe_w_router[l], moe_b_router[l], n_ctx_tiles)

        n_tok = bsz * t_all
        slot, slot_tok, block_e, first, n_used = _route_meta(
            idx.reshape(n_tok, LANES)[:, :TOP_K], rank.reshape(n_tok, LANES)[:, :TOP_K], counts[0])
        y_sorted = _moe_ffn(h2.reshape(n_tok, dm)[slot_tok], block_e, first, n_used,
                            moe_w1, moe_b1, moe_w2, moe_b2, l)
        yg = y_sorted[slot.T].reshape(TOP_K, bsz, t_all, dm)
        xs = _combine(xs, modsel[:, :, 5:6], yg, gates, final_norm_g, n_ctx_tiles, last)
    return xs
```

```python
import functools

import jax
import jax.numpy as jnp
from jax import lax
from jax.experimental import pallas as pl
from jax.experimental.pallas import tpu as pltpu

F32 = jnp.float32
BF16 = jnp.bfloat16

HEAD_DIM = 64
NORM_EPS = 1e-6
RET_GN_EPS = 1e-5
RWKV_GN_EPS = 64e-5
ROPE_BASE = 10000.0
GRID_W = 64
LRU_CONV = 4
LRU_C = 8.0
TOP_K = 4
SWIGLU_LIMIT = 7.0
SWIGLU_ALPHA = 1.702
DECAY_LORA = 64
ICLR_LORA = 64
GATE_LORA = 128

LANES = 128
TOKEN_TILE = 256
RET_CHUNK = 128
RWKV_CHUNK = 64
LRU_CHUNK = 128
MOE_TILE = 512
VMEM_LIMIT = 56 * 1024 * 1024


def _cparams(*sem):
    return pltpu.CompilerParams(dimension_semantics=sem, vmem_limit_bytes=VMEM_LIMIT)


def _scan_chunk(i, n_ctx, n_tot, rev):
    if not rev:
        return i
    return jnp.where(i < n_ctx, n_ctx - 1 - i, n_tot + n_ctx - 1 - i)


def _split3(a):
    hi = a.astype(BF16)
    r1 = a - hi.astype(F32)
    mid = r1.astype(BF16)
    lo = (r1 - mid.astype(F32)).astype(BF16)
    return hi, mid, lo


def _dot(a, b):
    return jnp.dot(a, b, preferred_element_type=F32)


def _dot_nt(a, b):
    return lax.dot_general(a, b, (((1,), (1,)), ((), ())), preferred_element_type=F32)


def _dot_tn(a, b):
    return lax.dot_general(a, b, (((0,), (0,)), ((), ())), preferred_element_type=F32)


def _dot_exact_rhs(a, b_bf):
    hi, mid, lo = _split3(a)
    return _dot(hi, b_bf) + _dot(mid, b_bf) + _dot(lo, b_bf)


def _dot_exact_lhs(a_bf, b):
    hi, mid, lo = _split3(b)
    return _dot(a_bf, hi) + _dot(a_bf, mid) + _dot(a_bf, lo)


def _dot_x3(a, b):
    a_hi = a.astype(BF16)
    a_lo = (a - a_hi.astype(F32)).astype(BF16)
    b_hi = b.astype(BF16)
    b_lo = (b - b_hi.astype(F32)).astype(BF16)
    return _dot(a_hi, b_hi) + _dot(a_lo, b_hi) + _dot(a_hi, b_lo)


def _dot_x3k(a, b):
    a_hi = a.astype(BF16)
    a_lo = (a - a_hi.astype(F32)).astype(BF16)
    b_hi = b.astype(BF16)
    b_lo = (b - b_hi.astype(F32)).astype(BF16)
    return (_dot(jnp.concatenate([a_hi, a_lo], axis=1), jnp.concatenate([b_hi, b_hi], axis=0))
            + _dot(a_hi, b_lo))


def _sigmoid(x):
    return 1.0 / (1.0 + jnp.exp(-x))


def _softplus(x):
    return jnp.maximum(x, 0.0) + jnp.log(1.0 + jnp.exp(-jnp.abs(x)))


def _mod_kernel(c_ref, w_ref, b_ref, o_ref):
    c = c_ref[...]
    s = c * _sigmoid(c)
    o_ref[...] = _dot_x3(s, w_ref[...]) + b_ref[...]


def _modulation(cond, w_mod, b_mod, layer):
    r, d = cond.shape
    nl, _, n = w_mod.shape
    tn = d
    return pl.pallas_call(
        _mod_kernel,
        grid=(n // tn,),
        in_specs=[pl.BlockSpec((r, d), lambda j: (0, 0)),
                  pl.BlockSpec((None, d, tn), lambda j: (layer, 0, j)),
                  pl.BlockSpec((None, 1, tn), lambda j: (layer, 0, j))],
        out_specs=pl.BlockSpec((r, tn), lambda j: (0, j)),
        out_shape=jax.ShapeDtypeStruct((r, n), F32),
        compiler_params=_cparams("arbitrary"),
        name="modulation",
    )(cond, w_mod, b_mod.reshape(nl, 1, n))


def _in_proj_kernel(x_ref, mod_ref, g_ref, w_ref, *o_refs, bounds):
    x = x_ref[0]
    ms = jnp.mean(x * x, axis=-1, keepdims=True)
    h = x * lax.rsqrt(ms + NORM_EPS) * g_ref[...]
    h = h * (1.0 + mod_ref[0, 0, 1:2, :]) + mod_ref[0, 0, 0:1, :]
    hb = h.astype(BF16)
    for o_ref, (lo, hi) in zip(o_refs, bounds):
        o_ref[0] = _dot(hb, w_ref[:, lo:hi])


def _in_proj(xs, mod, norm_g, w_in_bf, bounds, n_ctx_tiles):
    b, t, d = xs.shape
    tm = TOKEN_TILE
    p = w_in_bf.shape[1]
    seg = lambda bi, i: (bi, jnp.where(i >= n_ctx_tiles, 1, 0), 0, 0)
    return pl.pallas_call(
        functools.partial(_in_proj_kernel, bounds=bounds),
        grid=(b, t // tm),
        in_specs=[pl.BlockSpec((1, tm, d), lambda bi, i: (bi, i, 0)),
                  pl.BlockSpec((1, 1, 2, d), seg),
                  pl.BlockSpec((1, d), lambda bi, i: (0, 0)),
                  pl.BlockSpec((d, p), lambda bi, i: (0, 0))],
        out_specs=[pl.BlockSpec((1, tm, hi - lo), lambda bi, i: (bi, i, 0)) for lo, hi in bounds],
        out_shape=[jax.ShapeDtypeStruct((b, t, hi - lo), F32) for lo, hi in bounds],
        compiler_params=_cparams("parallel", "parallel"),
        name="in_proj",
    )(xs, mod, norm_g.reshape(1, d), w_in_bf)


def _ret_kernel(q_ref, k_ref, v_ref, cos_ref, sin_ref, dq_ref, dk_ref, dmat_ref, gm_ref, bm_ref,
                o_ref, s_ref, *, n_heads):
    i = pl.program_id(0)

    @pl.when(i == 0)
    def _():
        s_ref[...] = jnp.zeros_like(s_ref)

    nb, c, w = q_ref.shape
    cos = cos_ref[...]
    sin = sin_ref[...]
    lane = lax.broadcasted_iota(jnp.int32, (c, LANES), 1)
    first = (lane % 32) < 16

    def rope(u):
        parts = []
        for j in range(w // LANES):
            uj = u[:, j * LANES:(j + 1) * LANES]
            nxt = pltpu.roll(uj, LANES - 16, axis=1)
            prv = pltpu.roll(uj, 16, axis=1)
            parts.append(jnp.where(first, nxt, prv))
        return u * cos + jnp.concatenate(parts, axis=1) * sin

    bs = range(nb)
    head = lax.broadcasted_iota(jnp.int32, (c, w), 1) // HEAD_DIM
    q = [rope(q_ref[b]) for b in bs]
    k = [rope(k_ref[b]) for b in bs]
    vb = [v_ref[b].astype(BF16) for b in bs]
    s = [s_ref[b] for b in bs]
    out = [_dot((q[b] * dq_ref[...]).astype(BF16), s[b].astype(BF16)) for b in bs]
    sc = [_dot_nt(jnp.concatenate([jnp.where(head == h, q[b], 0.0) for h in range(n_heads)], axis=0).astype(BF16),
                  k[b].astype(BF16)) * dmat_ref[...] for b in bs]
    for h in range(n_heads):
        oh = [_dot(sc[b][h * c:(h + 1) * c].astype(BF16), vb[b]) for b in bs]
        out = [out[b] + jnp.where(head == h, oh[b], 0.0) for b in bs]
    ktv = [_dot_tn((k[b] * dk_ref[...]).astype(BF16), vb[b]) for b in bs]
    for b in bs:
        o_ref[b] = out[b]
        s_ref[b] = gm_ref[...] * s[b] + bm_ref[...] * ktv[b]


def _retention(p_ret, cos_t, sin_t, tabs, n_ctx, rev):
    b, t, w4 = p_ret.shape
    w = w4 // 4
    c = RET_CHUNK
    n_tot = t // c
    n_heads = w // HEAD_DIM
    dq, dk, dmat, gm, bm = tabs
    tix = lambda i: _scan_chunk(i, n_ctx, n_tot, rev)
    col = lambda j: (lambda i: (0, tix(i), j))
    const = lambda i: (0, 0)
    return pl.pallas_call(
        functools.partial(_ret_kernel, n_heads=n_heads),
        grid=(n_tot,),
        in_specs=[pl.BlockSpec((b, c, w), col(0)), pl.BlockSpec((b, c, w), col(1)), pl.BlockSpec((b, c, w), col(2)),
                  pl.BlockSpec((c, w), lambda i: (tix(i), 0)),
                  pl.BlockSpec((c, w), lambda i: (tix(i), 0)),
                  pl.BlockSpec((c, w), const), pl.BlockSpec((c, w), const),
                  pl.BlockSpec((n_heads * c, c), const),
                  pl.BlockSpec((w, w), const), pl.BlockSpec((w, w), const)],
        out_specs=pl.BlockSpec((b, c, w), lambda i: (0, tix(i), 0)),
        out_shape=jax.ShapeDtypeStruct((b, t, w), F32),
        scratch_shapes=[pltpu.VMEM((b, w, w), F32)],
        compiler_params=_cparams("arbitrary"),
        name="retention_rev" if rev else "retention_fwd",
    )(p_ret, p_ret, p_ret, cos_t, sin_t, dq, dk, dmat, gm, bm)


def _ret_tables(decay_logit, w, rev):
    n_heads = w // HEAD_DIM
    c = RET_CHUNK
    lg = jax.nn.log_sigmoid(decay_logit.astype(F32))
    t = jnp.arange(c, dtype=F32)
    p = (c - 1.0 - t) if rev else t
    rel = p[:, None] - p[None, :]
    scale = HEAD_DIM ** -0.5
    dmat = jnp.where(rel >= 0, jnp.exp(lg[:, None, None] * jnp.maximum(rel, 0.0)), 0.0) * scale
    dq = jnp.exp(lg[:, None] * (p + 1.0)) * scale
    dk = jnp.exp(lg[:, None] * (c - 1.0 - p))
    lanes = lambda a: jnp.repeat(a.T, HEAD_DIM, axis=1)
    hid = jnp.arange(w) // HEAD_DIM
    bm = (hid[:, None] == hid[None, :]).astype(F32)
    gm = bm * jnp.exp(lg * c)[hid][:, None]
    return lanes(dq), lanes(dk), dmat.reshape(n_heads * c, c), gm, bm


def _rope_tables(n_ctx_tok, seq, w):
    half = HEAD_DIM // 2
    quarter = half // 2
    inv_freq = ROPE_BASE ** (-jnp.arange(quarter, dtype=F32) / quarter)
    tok = jnp.arange(seq)
    rows = (tok // GRID_W).astype(F32)
    cols = (tok % GRID_W).astype(F32)
    o = jnp.arange(w) % HEAD_DIM
    pos = jnp.where(o[None, :] < half, rows[:, None], cols[:, None])
    ang = pos * inv_freq[o % quarter][None, :]
    sign = jnp.where((o % half) < quarter, -1.0, 1.0)[None, :]
    cos = jnp.concatenate([jnp.ones((n_ctx_tok, w), F32), jnp.cos(ang)], axis=0)
    sin = jnp.concatenate([jnp.zeros((n_ctx_tok, w), F32), jnp.sin(ang) * sign], axis=0)
    return cos, sin


def _rwkv_kernel(z_ref, mu_ref, vec_ref, wup_ref, aup_ref, e_ref, minc_ref, strict_ref, incl_ref,
                 y_ref, bon_ref, st_ref, zprev_ref, *, rev, n_ctx, w):
    i = pl.program_id(0)
    nb, c, zw = z_ref.shape
    n_pairs = w // LANES
    rows = nb * c

    @pl.when(i == 0)
    def _():
        st_ref[...] = jnp.zeros_like(st_ref)

    @pl.when((i == 0) | (i == n_ctx))
    def _():
        zprev_ref[...] = jnp.zeros_like(zprev_ref)

    z = z_ref[...].reshape(rows, zw)
    rin = lax.broadcasted_iota(jnp.int32, (rows, zw), 0) % c
    prev = jnp.concatenate([jnp.broadcast_to(zprev_ref[b, 0:1, :], (c, zw)) for b in range(nb)], axis=0)
    if rev:
        zs = jnp.where(rin == c - 1, prev, pltpu.roll(z, rows - 1, axis=0))
        for b in range(nb):
            zprev_ref[b, 0:1, :] = z[b * c:b * c + 1, :]
    else:
        zs = jnp.where(rin == 0, prev, pltpu.roll(z, 1, axis=0))
        for b in range(nb):
            zprev_ref[b, 0:1, :] = z[b * c + c - 1:b * c + c, :]
    zd = z + (zs - z) * mu_ref[...]

    r = zd[:, 0:w]
    k = zd[:, w:2 * w]
    v = zd[:, 2 * w:3 * w]
    lora = zd[:, 3 * w:3 * w + LANES]
    lane = lax.broadcasted_iota(jnp.int32, (rows, LANES), 1)
    lora = jnp.where(lane < DECAY_LORA, jnp.tanh(lora), lora)

    w0, a0, k_k, k_a, r_k = (vec_ref[j:j + 1, :] for j in range(5))
    e2 = e_ref[...]
    e22 = jnp.concatenate([e2, e2], axis=0)

    def head_sums(x, pieces):
        outs = []
        for j in range(n_pairs):
            xj = x[:, j * LANES:(j + 1) * LANES]
            if pieces == 1:
                outs.append(_dot(xj.astype(BF16), e2))
            else:
                hi = xj.astype(BF16)
                mid = (xj - hi.astype(F32)).astype(BF16)
                outs.append(_dot(jnp.concatenate([hi, mid], axis=1), e22))
        return jnp.concatenate(outs, axis=1)

    w_log = -_softplus(-(w0 + _dot_x3k(lora, wup_ref[...]))) - 0.5
    logw = -jnp.exp(w_log)
    a = _sigmoid(a0 + _dot_x3k(lora, aup_ref[...]))
    kk0 = k * k_k
    kk = kk0 / jnp.maximum(jnp.sqrt(head_sums(kk0 * kk0, 2)), 1e-12)
    k2 = k * (1.0 + (a - 1.0) * k_a)
    bon_ref[...] = (head_sums(r * k2 * r_k, 1) * v).reshape(nb, c, w)

    cinc = jnp.concatenate(
        [_dot(minc_ref[...], jnp.concatenate(_split3(logw[b * c:(b + 1) * c]), axis=0)) for b in range(nb)], axis=0)
    e_inc = jnp.exp(cinc)
    e_neg = jnp.exp(-cinc)
    rt = r * e_inc
    kt = k2 * e_neg
    bt = kk * a * e_neg
    kkt = kk * jnp.exp(cinc - logw)
    last = 0 if rev else c - 1

    strict = strict_ref[...] > 0.0
    incl = incl_ref[...] > 0.0
    lane_lo = lax.broadcasted_iota(jnp.int32, (c, LANES), 1) < HEAD_DIM
    head_r = lax.broadcasted_iota(jnp.int32, (LANES, LANES), 0) // HEAD_DIM
    head_c = lax.broadcasted_iota(jnp.int32, (LANES, LANES), 1) // HEAD_DIM
    diag = head_r == head_c

    def stack(xw):
        return jnp.concatenate([jnp.where(lane_lo, xw, 0.0), jnp.where(lane_lo, 0.0, xw)], axis=0)

    chains = [(b, j) for b in range(nb) for j in range(n_pairs)]

    def win(x, ch):
        b, j = ch
        return x[b * c:(b + 1) * c, j * LANES:(j + 1) * LANES]

    st = [st_ref[b * n_pairs + j] for b, j in chains]
    lhs = [jnp.concatenate([win(kkt, ch), win(rt, ch)], axis=0).astype(BF16) for ch in chains]
    g = [_dot_nt(l, jnp.concatenate([stack(win(bt, ch)), stack(win(kt, ch))], axis=0).astype(BF16))
         for l, ch in zip(lhs, chains)]
    a_b = [jnp.where(strict, x[0:c, 0:2 * c], 0.0) for x in g]
    a_k = [jnp.where(strict, x[0:c, 2 * c:4 * c], 0.0).astype(BF16) for x in g]
    r_kb = [jnp.concatenate([jnp.where(incl, x[c:2 * c, 2 * c:4 * c], 0.0),
                             -jnp.where(incl, x[c:2 * c, 0:2 * c], 0.0)], axis=1).astype(BF16) for x in g]
    x0 = [_dot_nt(l, s.astype(BF16)) for l, s in zip(lhs, st)]
    v_sb = [stack(win(v, ch)).astype(BF16) for ch in chains]
    u = [x[0:c] + _dot(ak, vs) for x, ak, vs in zip(x0, a_k, v_sb)]

    pw = a_b
    steps, sign = 1, -1.0
    while 2 * steps < c:
        both = [_dot(p.astype(BF16), jnp.concatenate([stack(p), stack(x)], axis=1).astype(BF16))
                for p, x in zip(pw, u)]
        u = [x + sign * y[:, 2 * c:4 * c] for x, y in zip(u, both)]
        pw = [y[:, 0:2 * c] for y in both]
        steps, sign = 2 * steps, 1.0
    u = [x + sign * _dot(p.astype(BF16), stack(x).astype(BF16)) for x, p in zip(u, pw)]

    y = [x[c:2 * c] + _dot(rk, jnp.concatenate([vs, stack(uu).astype(BF16)], axis=0))
         for x, rk, vs, uu in zip(x0, r_kb, v_sb, u)]
    upd = [_dot_tn(jnp.concatenate([win(v, ch), x], axis=0).astype(BF16),
                   jnp.concatenate([win(kt, ch), -win(bt, ch)], axis=0).astype(BF16))
           for ch, x in zip(chains, u)]
    for n, (b, j) in enumerate(chains):
        w_end = e_inc[b * c + last:b * c + last + 1, j * LANES:(j + 1) * LANES]
        st_ref[b * n_pairs + j] = jnp.where(diag, (st[n] + upd[n]) * w_end, 0.0)
        y_ref[b, :, j * LANES:(j + 1) * LANES] = y[n]


def _rwkv(p_z, prm, n_ctx, rev):
    b, t, zw = p_z.shape
    w = (zw - DECAY_LORA - ICLR_LORA) // 3
    c = RWKV_CHUNK
    n_tot = t // c
    mu, vecs, wup, aup, e_bf, minc, strict, incl = prm
    tix = lambda i: _scan_chunk(i, n_ctx, n_tot, rev)
    const = lambda i: (0, 0)
    full = lambda a: pl.BlockSpec(a.shape, const)
    return pl.pallas_call(
        functools.partial(_rwkv_kernel, rev=rev, n_ctx=n_ctx, w=w),
        grid=(n_tot,),
        in_specs=[pl.BlockSpec((b, c, zw), lambda i: (0, tix(i), 0)),
                  full(mu), full(vecs), full(wup), full(aup), full(e_bf), full(minc), full(strict), full(incl)],
        out_specs=[pl.BlockSpec((b, c, w), lambda i: (0, tix(i), 0)),
                   pl.BlockSpec((b, c, w), lambda i: (0, tix(i), 0))],
        out_shape=[jax.ShapeDtypeStruct((b, t, w), F32), jax.ShapeDtypeStruct((b, t, w), F32)],
        scratch_shapes=[pltpu.VMEM((b * (w // LANES), LANES, LANES), F32), pltpu.VMEM((b, 8, zw), F32)],
        compiler_params=_cparams("arbitrary"),
        name="rwkv7_rev" if rev else "rwkv7_fwd",
    )(p_z, mu, vecs, wup, aup, e_bf, minc, strict, incl)


def _rwkv_params(mu, w0, w_up, a0, a_up, k_k, k_a, r_k, rev, n_batch):
    w = w0.shape[0]
    c = RWKV_CHUNK
    vecs = jnp.concatenate([jnp.stack([w0, a0, k_k, k_a, r_k]), jnp.zeros((3, w), F32)], axis=0)
    wup = jnp.concatenate([w_up, jnp.zeros((ICLR_LORA, w), F32)], axis=0)
    aup = jnp.concatenate([jnp.zeros((DECAY_LORA, w), F32), a_up], axis=0)
    hid = jnp.arange(LANES) // HEAD_DIM
    e_bf = (hid[:, None] == hid[None, :]).astype(BF16)
    t = jnp.arange(c)
    p = (c - 1 - t) if rev else t
    le = p[None, :] <= p[:, None]
    lt = p[None, :] < p[:, None]
    strict = jnp.tile(lt, (1, 2)).astype(F32)
    incl = jnp.tile(le, (1, 2)).astype(F32)
    minc = jnp.tile(le, (1, 3)).astype(BF16)
    return mu.reshape(1, -1), vecs, wup, aup, e_bf, minc, strict, incl


def _lru_kernel(x_ref, cw_ref, vec_ref, wa_ref, wx_ref, h_ref, hcar_ref, ucar_ref, *, rev, n_ctx):
    i = pl.program_id(0)
    nb, c, w = x_ref.shape
    rows = nb * c

    @pl.when(i == 0)
    def _():
        hcar_ref[...] = jnp.zeros_like(hcar_ref)

    @pl.when((i == 0) | (i == n_ctx))
    def _():
        ucar_ref[...] = jnp.zeros_like(ucar_ref)

    u0 = x_ref[...].reshape(rows, w)
    row = lax.broadcasted_iota(jnp.int32, (rows, w), 0) % c

    def per_batch(ref, j):
        return jnp.concatenate([jnp.broadcast_to(ref[b, j:j + 1, :], (c, w)) for b in range(nb)], axis=0)

    def shifted(x, s, carry, fill):
        if rev:
            rolled = pltpu.roll(x, rows - s, axis=0)
            edge = row >= c - s
        else:
            rolled = pltpu.roll(x, s, axis=0)
            edge = row < s
        if carry is None:
            return jnp.where(edge, fill, rolled)
        return jnp.where(edge, carry, rolled)

    conv = vec_ref[0:1, :] + cw_ref[LRU_CONV - 1:LRU_CONV, :] * u0
    for m in range(1, LRU_CONV):
        car = jnp.zeros((rows, w), F32)
        for qpos in range(m):
            r_idx = (c - 1 - qpos) if rev else qpos
            car = jnp.where(row == r_idx, per_batch(ucar_ref, m - qpos - 1), car)
        conv = conv + cw_ref[LRU_CONV - 1 - m:LRU_CONV - m, :] * shifted(u0, m, car, None)
    for m in range(1, LRU_CONV):
        r_idx = (m - 1) if rev else (c - m)
        for b in range(nb):
            ucar_ref[b, m - 1:m, :] = u0[b * c + r_idx:b * c + r_idx + 1, :]

    cb = conv.astype(BF16)
    r = _sigmoid(_dot(cb, wa_ref[...]) + vec_ref[1:2, :])
    ig = _sigmoid(_dot(cb, wx_ref[...]) + vec_ref[2:3, :])
    log_a = -LRU_C * r * vec_ref[3:4, :]
    a = jnp.exp(log_a)
    bb = jnp.sqrt(1.0 - jnp.exp(2.0 * log_a)) * (ig * conv)

    s = 1
    while s < c:
        bb = bb + a * shifted(bb, s, None, 0.0)
        a = a * shifted(a, s, None, 1.0)
        s *= 2
    h = bb + a * per_batch(hcar_ref, 0)
    h_ref[...] = h.reshape(nb, c, w)
    last = 0 if rev else c - 1
    for b in range(nb):
        hcar_ref[b, 0:1, :] = h[b * c + last:b * c + last + 1, :]


def _lru(p_lru, prm, n_ctx, rev):
    b, t, w2 = p_lru.shape
    w = w2 // 2
    c = LRU_CHUNK
    n_tot = t // c
    cw, vecs, wa, wx = prm
    tix = lambda i: _scan_chunk(i, n_ctx, n_tot, rev)
    const = lambda i: (0, 0)
    return pl.pallas_call(
        functools.partial(_lru_kernel, rev=rev, n_ctx=n_ctx),
        grid=(n_tot,),
        in_specs=[pl.BlockSpec((b, c, w), lambda i: (0, tix(i), 0)),
                  pl.BlockSpec(cw.shape, const), pl.BlockSpec(vecs.shape, const),
                  pl.BlockSpec(wa.shape, const), pl.BlockSpec(wx.shape, const)],
        out_specs=pl.BlockSpec((b, c, w), lambda i: (0, tix(i), 0)),
        out_shape=jax.ShapeDtypeStruct((b, t, w), F32),
        scratch_shapes=[pltpu.VMEM((b, 8, w), F32), pltpu.VMEM((b, 8, w), F32)],
        compiler_params=_cparams("arbitrary"),
        name="rglru_rev" if rev else "rglru_fwd",
    )(p_lru, cw, vecs, wa, wx)


def _lru_params(conv_w, conv_b, wa, ba, wx, bx, lam):
    w = conv_b.shape[0]
    cw = jnp.concatenate([conv_w, jnp.zeros((8 - LRU_CONV, w), F32)], axis=0)
    vecs = jnp.concatenate([jnp.stack([conv_b, ba, bx, jax.nn.softplus(-lam)]), jnp.zeros((4, w), F32)], axis=0)
    return cw, vecs, jax.scipy.linalg.block_diag(*wa).astype(BF16), jax.scipy.linalg.block_diag(*wx).astype(BF16)


def _head_norm(y, e2, gain, bias, eps):
    inv = 1.0 / HEAD_DIM

    def head_sums(x):
        xb = x.astype(BF16)
        return jnp.concatenate([_dot(xb[:, j:j + LANES], e2) for j in range(0, x.shape[1], LANES)], axis=1)

    yc = y - head_sums(y) * inv
    var = head_sums(yc * yc) * inv
    return yc * lax.rsqrt(var + eps) * gain + bias


def _mix_out_kernel(x_ref, mod_ref, of_ref, ob_ref, g_ref, yf_ref, yb_ref, bf_ref, bb_ref, gd_ref,
                    hf_ref, hb_ref, lg_ref, gn_ref, e_ref, gup_ref, wout_ref, n2g_ref, wr_ref, br_ref, tri_ref,
                    xo_ref, h2_ref, idx_ref, gate_ref, rank_ref, cnt_ref, base_ref, *, w_ret, w_rw):
    @pl.when((pl.program_id(0) == 0) & (pl.program_id(1) == 0))
    def _():
        base_ref[...] = jnp.zeros_like(base_ref)

    e_bf = e_ref[...]
    g = g_ref[0]
    ret = _head_norm(of_ref[0] + ob_ref[0], e_bf, gn_ref[0:1, :], gn_ref[1:2, :], RET_GN_EPS)
    ret = ret * (g * _sigmoid(g))
    gate = _dot(_sigmoid(gd_ref[0]).astype(BF16), gup_ref[...])
    rw = _head_norm(yf_ref[0] + yb_ref[0], e_bf, gn_ref[2:3, :], gn_ref[3:4, :], RWKV_GN_EPS)
    rw = (rw + bf_ref[0] + bb_ref[0]) * gate
    lg = lg_ref[0]
    gelu = 0.5 * lg * (1.0 + jnp.tanh(0.7978845608028654 * (lg + 0.044715 * (lg * lg * lg))))
    lru = (hf_ref[0] + hb_ref[0]) * gelu
    mix = (_dot(ret.astype(BF16), wout_ref[0:w_ret, :])
           + _dot(rw.astype(BF16), wout_ref[w_ret:w_ret + w_rw, :])
           + _dot(lru.astype(BF16), wout_ref[w_ret + w_rw:, :]))
    x = x_ref[0] + mod_ref[0, 0, 0:1, :] * mix
    xo_ref[0] = x
    ms = jnp.mean(x * x, axis=-1, keepdims=True)
    h2 = x * lax.rsqrt(ms + NORM_EPS) * n2g_ref[...]
    h2 = h2 * (1.0 + mod_ref[0, 0, 2:3, :]) + mod_ref[0, 0, 1:2, :]
    h2_ref[0] = h2.astype(BF16)
    logits = _dot_x3(h2, wr_ref[...]) + br_ref[...]
    idx_o, gate_o, rank_o = _route_tile(logits, tri_ref[...], base_ref)
    idx_ref[0] = idx_o
    gate_ref[0] = gate_o
    rank_ref[0] = rank_o
    cnt_ref[...] = base_ref[...].astype(jnp.int32)


def _mix_out(xs, mod, o_f, o_b, p_ret, y_f, y_b, bon_f, bon_b, p_gd, h_f, h_b, p_lru,
             gn, e_bf, g_up_bf, w_out_bf, norm2_g, w_router, b_router, n_ctx_tiles):
    b, t, d = xs.shape
    tm = TOKEN_TILE
    w_ret, w_rw, w_lru = o_f.shape[2], y_f.shape[2], h_f.shape[2]
    ne = w_router.shape[1]
    tok = lambda wd, j=0: pl.BlockSpec((1, tm, wd), lambda bi, i: (bi, i, j))
    const = lambda a: pl.BlockSpec(a.shape, lambda bi, i: (0,) * a.ndim)
    seg = lambda bi, i: (bi, jnp.where(i >= n_ctx_tiles, 1, 0), 0, 0)
    n2g = norm2_g.reshape(1, d)
    br = b_router.reshape(1, ne)
    tt = jnp.arange(tm)
    tri = (tt[None, :] < tt[:, None]).astype(BF16)
    return pl.pallas_call(
        functools.partial(_mix_out_kernel, w_ret=w_ret, w_rw=w_rw),
        grid=(b, t // tm),
        in_specs=[tok(d), pl.BlockSpec((1, 1, 3, d), seg),
                  tok(w_ret), tok(w_ret), tok(w_ret, 3),
                  tok(w_rw), tok(w_rw), tok(w_rw), tok(w_rw), tok(GATE_LORA),
                  tok(w_lru), tok(w_lru), tok(w_lru, 1),
                  const(gn), const(e_bf), const(g_up_bf), const(w_out_bf), const(n2g), const(w_router), const(br),
                  const(tri)],
        out_specs=[tok(d), tok(d), tok(LANES), tok(LANES), tok(LANES),
                   pl.BlockSpec((8, ne), lambda bi, i: (0, 0))],
        out_shape=[jax.ShapeDtypeStruct((b, t, d), F32), jax.ShapeDtypeStruct((b, t, d), BF16),
                   jax.ShapeDtypeStruct((b, t, LANES), jnp.int32), jax.ShapeDtypeStruct((b, t, LANES), F32),
                   jax.ShapeDtypeStruct((b, t, LANES), jnp.int32), jax.ShapeDtypeStruct((8, ne), jnp.int32)],
        scratch_shapes=[pltpu.VMEM((8, ne), F32)],
        compiler_params=_cparams("arbitrary", "arbitrary"),
        name="mix_out",
    )(xs, mod, o_f, o_b, p_ret, y_f, y_b, bon_f, bon_b, p_gd, h_f, h_b, p_lru,
      gn, e_bf, g_up_bf, w_out_bf, n2g, w_router, br, tri)


def _moe_kernel(be_ref, first_ref, nu_ref, x_ref, w1_ref, b1_ref, w2_ref, b2_ref, y_ref, w1b_ref, w2b_ref):
    i = pl.program_id(0)
    de = w2_ref.shape[1]

    @pl.when(first_ref[i] == 1)
    def _():
        w1b_ref[...] = w1_ref[0].astype(BF16)
        w2b_ref[...] = w2_ref[0].astype(BF16)

    @pl.when(i < nu_ref[0])
    def _():
        gu = _dot(x_ref[...], w1b_ref[...]) + b1_ref[0]
        glu = jnp.minimum(gu[:, :de], SWIGLU_LIMIT)
        lin = jnp.clip(gu[:, de:], -SWIGLU_LIMIT, SWIGLU_LIMIT)
        act = glu * _sigmoid(SWIGLU_ALPHA * glu) * (lin + 1.0)
        y_ref[...] = (_dot(act.astype(BF16), w2b_ref[...]) + b2_ref[0]).astype(y_ref.dtype)

    @pl.when(i >= nu_ref[0])
    def _():
        y_ref[...] = jnp.zeros_like(y_ref)


def _moe_ffn(hb, block_e, first, n_used, w1, b1, w2, b2, layer):
    n_slots, d = hb.shape
    tm = MOE_TILE
    nl, ne, _, d2 = w1.shape
    de = w2.shape[2]
    wsel = lambda i, be, fi, nu: (layer, be[i], 0, 0)
    return pl.pallas_call(
        _moe_kernel,
        grid_spec=pltpu.PrefetchScalarGridSpec(
            num_scalar_prefetch=3,
            grid=(n_slots // tm,),
            in_specs=[pl.BlockSpec((tm, d), lambda i, be, fi, nu: (i, 0)),
                      pl.BlockSpec((None, 1, d, d2), wsel),
                      pl.BlockSpec((None, 1, 1, d2), wsel),
                      pl.BlockSpec((None, 1, de, d), wsel),
                      pl.BlockSpec((None, 1, 1, d), wsel)],
            out_specs=pl.BlockSpec((tm, d), lambda i, be, fi, nu: (i, 0)),
            scratch_shapes=[pltpu.VMEM((d, d2), BF16), pltpu.VMEM((de, d), BF16)],
        ),
        out_shape=jax.ShapeDtypeStruct((n_slots, d), BF16),
        compiler_params=_cparams("arbitrary"),
        name="moe_ffn",
    )(block_e, first, n_used, hb, w1, b1.reshape(nl, ne, 1, d2), w2, b2.reshape(nl, ne, 1, d))


def _route_tile(lg, tri, base_ref):
    tr, ne = lg.shape
    lane = lax.broadcasted_iota(jnp.int32, (tr, ne), 1).astype(F32)
    out_lane = lax.broadcasted_iota(jnp.int32, (tr, LANES), 1)
    vals = lg
    sel = jnp.zeros((tr, ne), F32)
    picks, tops = [], []
    for _ in range(TOP_K):
        m = jnp.max(vals, axis=-1, keepdims=True)
        ix = jnp.min(jnp.where(vals == m, lane, float(ne)), axis=-1, keepdims=True)
        hit = lane == ix
        sel = jnp.where(hit, 1.0, sel)
        vals = jnp.where(hit, -jnp.inf, vals)
        picks.append(ix)
        tops.append(m)
    ex = [jnp.exp(t - tops[0]) for t in tops]
    den = ex[0] + ex[1] + ex[2] + ex[3]
    before = _dot(tri, sel.astype(BF16)) + base_ref[0:1, :]
    idx_o = jnp.zeros((tr, LANES), F32)
    gate_o = jnp.zeros((tr, LANES), F32)
    rank_o = jnp.zeros((tr, LANES), F32)
    for k in range(TOP_K):
        rk = jnp.sum(jnp.where(lane == picks[k], before, 0.0), axis=-1, keepdims=True)
        idx_o = jnp.where(out_lane == k, picks[k], idx_o)
        gate_o = jnp.where(out_lane == k, ex[k] / den, gate_o)
        rank_o = jnp.where(out_lane == k, rk, rank_o)
    total = base_ref[0:1, :] + jnp.sum(sel, axis=0, keepdims=True)
    base_ref[...] = jnp.broadcast_to(total, base_ref.shape)
    return idx_o.astype(jnp.int32), gate_o, rank_o.astype(jnp.int32)


def _route_meta(idx, rank, counts):
    n_tok = idx.shape[0]
    ne = counts.shape[0]
    tm = MOE_TILE
    n_assign = n_tok * TOP_K
    padded = (counts + tm - 1) // tm * tm
    pend = jnp.cumsum(padded)
    pstart = pend - padded
    start = jnp.cumsum(counts) - counts
    eid = jnp.arange(ne, dtype=jnp.int32)
    slot = jnp.sum(jnp.where(idx[..., None] == eid, pstart, 0), axis=-1).astype(jnp.int32) + rank
    n_blocks = (n_assign + ne * (tm - 1) + tm - 1) // tm
    blk_start = jnp.arange(n_blocks, dtype=jnp.int32) * tm
    block_e = jnp.minimum(jnp.sum(pend[None, :] <= blk_start[:, None], axis=1), ne - 1).astype(jnp.int32)
    first = jnp.concatenate([jnp.ones((1,), jnp.int32), (block_e[1:] != block_e[:-1]).astype(jnp.int32)])
    n_used = (pend[-1] // tm).astype(jnp.int32).reshape(1)
    _, order = lax.sort_key_val(slot.reshape(-1), jnp.arange(n_assign, dtype=jnp.int32))
    off = jnp.arange(n_blocks * tm, dtype=jnp.int32) - jnp.repeat(pstart[block_e], tm)
    valid = off < jnp.repeat(counts[block_e], tm)
    pos = jnp.clip(jnp.repeat(start[block_e], tm) + off, 0, n_assign - 1)
    spread = jnp.arange(n_blocks * tm, dtype=jnp.int32) % n_tok
    slot_tok = jnp.where(valid, order[pos] // TOP_K, spread).astype(jnp.int32)
    return slot, slot_tok, block_e, first, n_used


def _combine_kernel(x_ref, mod_ref, y_ref, gate_ref, g_ref, o_ref, *, final):
    gate = gate_ref[0]
    y = y_ref[0, 0].astype(F32) * gate[:, 0:1]
    for k in range(1, TOP_K):
        y = y + y_ref[k, 0].astype(F32) * gate[:, k:k + 1]
    x = x_ref[0] + mod_ref[0, 0, 0:1, :] * y
    if final:
        ms = jnp.mean(x * x, axis=-1, keepdims=True)
        x = x * lax.rsqrt(ms + NORM_EPS) * g_ref[...]
    o_ref[0] = x


def _combine(xs, mod, yg, gates, final_g, n_ctx_tiles, final):
    b, t, d = xs.shape
    tm = TOKEN_TILE
    skip = n_ctx_tiles if final else 0
    seg = lambda bi, i: (bi, jnp.where(i + skip >= n_ctx_tiles, 1, 0), 0, 0)
    return pl.pallas_call(
        functools.partial(_combine_kernel, final=final),
        grid=(b, t // tm - skip),
        in_specs=[pl.BlockSpec((1, tm, d), lambda bi, i: (bi, i + skip, 0)),
                  pl.BlockSpec((1, 1, 1, d), seg),
                  pl.BlockSpec((TOP_K, 1, tm, d), lambda bi, i: (0, bi, i + skip, 0)),
                  pl.BlockSpec((1, tm, LANES), lambda bi, i: (bi, i + skip, 0)),
                  pl.BlockSpec((1, d), lambda bi, i: (0, 0))],
        out_specs=pl.BlockSpec((1, tm, d), lambda bi, i: (bi, i, 0)),
        out_shape=jax.ShapeDtypeStruct((b, t - skip * tm, d), F32),
        compiler_params=_cparams("parallel", "parallel"),
        name="combine_final" if final else "combine",
    )(xs, mod, yg, gates, final_g.reshape(1, d))


def kernel(x, c, ctx, c_ctx, w_mod, b_mod, norm1_g, norm2_g, w_in, w_out, ret_decay_logit, ret_gn_g, ret_gn_b, rwkv_mu, rwkv_w0, rwkv_w_up, rwkv_a0, rwkv_a_up, rwkv_k_k, rwkv_k_a, rwkv_g_up, rwkv_r_k, rwkv_gn_g, rwkv_gn_b, lru_conv_w, lru_conv_b, lru_wa, lru_ba, lru_wx, lru_bx, lru_lambda, moe_w_router, moe_b_router, moe_w1, moe_b1, moe_w2, moe_b2, final_norm_g):
    bsz, seq, dm = x.shape
    n_ctx_tok = ctx.shape[1]
    depth = w_in.shape[0]
    n_experts = moe_w_router.shape[2]
    w_ret = 3 * dm // 8
    w_rw = 3 * dm // 8
    w_lru = dm - w_ret - w_rw
    zw = 3 * w_rw + DECAY_LORA + ICLR_LORA
    sizes = (4 * w_ret, zw, GATE_LORA, 2 * w_lru)
    bounds, off = [], 0
    for s in sizes:
        bounds.append((off, off + s))
        off += s
    bounds = tuple(bounds)
    assert off == w_in.shape[2]
    assert n_ctx_tok % TOKEN_TILE == 0 and seq % TOKEN_TILE == 0 and seq % GRID_W == 0
    t_all = n_ctx_tok + seq
    n_ctx_tiles = n_ctx_tok // TOKEN_TILE

    xs = jnp.concatenate([ctx, x], axis=1)
    cos_t, sin_t = _rope_tables(n_ctx_tok, seq, w_ret)
    hid = jnp.arange(LANES) // HEAD_DIM
    e_bf = (hid[:, None] == hid[None, :]).astype(BF16)
    cond =jnp.concatenate([c, c_ctx[None, :], jnp.zeros((8 - (bsz + 1) % 8, dm), F32)], axis=0)

    for l in range(depth):
        last = l == depth - 1
        mod = _modulation(cond, w_mod, b_mod, l)
        mod_l = mod[:bsz].reshape(bsz, 6, dm)
        mod_c = jnp.broadcast_to(mod[bsz].reshape(1, 6, dm), (bsz, 6, dm))
        modsel = jnp.stack([mod_c, mod_l], axis=1)

        p_ret, p_z, p_gd, p_lru = _in_proj(xs, modsel[:, :, 0:2], norm1_g[l], w_in[l].astype(BF16), bounds, n_ctx_tiles)

        ret_o, rw_y, rw_bon, lru_h = [], [], [], []
        for d in range(2):
            rev = d == 1
            ret_o.append(_retention(p_ret, cos_t, sin_t, _ret_tables(ret_decay_logit[l, d], w_ret, rev),
                                    n_ctx_tok // RET_CHUNK, rev))
            prm = _rwkv_params(rwkv_mu[l, d], rwkv_w0[l, d], rwkv_w_up[l, d], rwkv_a0[l, d], rwkv_a_up[l, d],
                               rwkv_k_k[l, d], rwkv_k_a[l, d], rwkv_r_k[l], rev, bsz)
            y, bon = _rwkv(p_z, prm, n_ctx_tok // RWKV_CHUNK, rev)
            rw_y.append(y)
            rw_bon.append(bon)
            lru_h.append(_lru(p_lru, _lru_params(lru_conv_w[l, d], lru_conv_b[l, d], lru_wa[l, d], lru_ba[l, d],
                                                 lru_wx[l, d], lru_bx[l, d], lru_lambda[l, d]),
                              n_ctx_tok // LRU_CHUNK, rev))

        gn = jnp.concatenate([jnp.stack([ret_gn_g[l], ret_gn_b[l], rwkv_gn_g[l], rwkv_gn_b[l]]),
                              jnp.zeros((4, w_ret), F32)], axis=0)
        xs, h2, idx, gates, rank, counts = _mix_out(
            xs, modsel[:, :, 2:5], ret_o[0], ret_o[1], p_ret, rw_y[0], rw_y[1],
            rw_bon[0], rw_bon[1], p_gd, lru_h[0], lru_h[1], p_lru,
            gn, e_bf, rwkv_g_up[l].astype(BF16), w_out[l].astype(BF16), norm2_g[l],
            moe_w_router[l], moe_b_router[l], n_ctx_tiles)

        n_tok = bsz * t_all
        slot, slot_tok, block_e, first, n_used = _route_meta(
            idx.reshape(n_tok, LANES)[:, :TOP_K], rank.reshape(n_tok, LANES)[:, :TOP_K], counts[0])
        y_sorted = _moe_ffn(h2.reshape(n_tok, dm)[slot_tok], block_e, first, n_used,
                            moe_w1, moe_b1, moe_w2, moe_b2, l)
        yg = y_sorted[slot.T].reshape(TOP_K, bsz, t_all, dm)
        xs = _combine(xs, modsel[:, :, 5:6], yg, gates, final_norm_g, n_ctx_tiles, last)
    return xs
```

```python
import functools

import jax
import jax.numpy as jnp
from jax import lax
from jax.experimental import pallas as pl
from jax.experimental.pallas import tpu as pltpu

F32 = jnp.float32
BF16 = jnp.bfloat16

HEAD_DIM = 64
NORM_EPS = 1e-6
RET_GN_EPS = 1e-5
RWKV_GN_EPS = 64e-5
ROPE_BASE = 10000.0
GRID_W = 64
LRU_CONV = 4
LRU_C = 8.0
TOP_K = 4
SWIGLU_LIMIT = 7.0
SWIGLU_ALPHA = 1.702
DECAY_LORA = 64
ICLR_LORA = 64
GATE_LORA = 128

LANES = 128
TOKEN_TILE = 256
RET_CHUNK = 128
RWKV_CHUNK = 64
LRU_CHUNK = 128
MOE_TILE = 512
VMEM_LIMIT = 56 * 1024 * 1024


def _cparams(*sem):
    return pltpu.CompilerParams(dimension_semantics=sem, vmem_limit_bytes=VMEM_LIMIT)


def _scan_chunk(i, n_ctx, n_tot, rev):
    if not rev:
        return i
    return jnp.where(i < n_ctx, n_ctx - 1 - i, n_tot + n_ctx - 1 - i)


def _split3(a):
    hi = a.astype(BF16)
    r1 = a - hi.astype(F32)
    mid = r1.astype(BF16)
    lo = (r1 - mid.astype(F32)).astype(BF16)
    return hi, mid, lo


def _dot(a, b):
    return jnp.dot(a, b, preferred_element_type=F32)


def _dot_nt(a, b):
    return lax.dot_general(a, b, (((1,), (1,)), ((), ())), preferred_element_type=F32)


def _dot_tn(a, b):
    return lax.dot_general(a, b, (((0,), (0,)), ((), ())), preferred_element_type=F32)


def _dot_exact_rhs(a, b_bf):
    hi, mid, lo = _split3(a)
    return _dot(hi, b_bf) + _dot(mid, b_bf) + _dot(lo, b_bf)


def _dot_exact_lhs(a_bf, b):
    hi, mid, lo = _split3(b)
    return _dot(a_bf, hi) + _dot(a_bf, mid) + _dot(a_bf, lo)


def _dot_x3(a, b):
    a_hi = a.astype(BF16)
    a_lo = (a - a_hi.astype(F32)).astype(BF16)
    b_hi = b.astype(BF16)
    b_lo = (b - b_hi.astype(F32)).astype(BF16)
    return _dot(a_hi, b_hi) + _dot(a_lo, b_hi) + _dot(a_hi, b_lo)


def _dot_x3k(a, b):
    a_hi = a.astype(BF16)
    a_lo = (a - a_hi.astype(F32)).astype(BF16)
    b_hi = b.astype(BF16)
    b_lo = (b - b_hi.astype(F32)).astype(BF16)
    return (_dot(jnp.concatenate([a_hi, a_lo], axis=1), jnp.concatenate([b_hi, b_hi], axis=0))
            + _dot(a_hi, b_lo))


def _sigmoid(x):
    return 1.0 / (1.0 + jnp.exp(-x))


def _softplus(x):
    return jnp.maximum(x, 0.0) + jnp.log(1.0 + jnp.exp(-jnp.abs(x)))


def _mod_kernel(c_ref, w_ref, b_ref, o_ref):
    c = c_ref[...]
    s = c * _sigmoid(c)
    o_ref[...] = _dot_x3(s, w_ref[...]) + b_ref[...]


def _modulation(cond, w_mod, b_mod, layer):
    r, d = cond.shape
    nl, _, n = w_mod.shape
    tn = d
    return pl.pallas_call(
        _mod_kernel,
        grid=(n // tn,),
        in_specs=[pl.BlockSpec((r, d), lambda j: (0, 0)),
                  pl.BlockSpec((None, d, tn), lambda j: (layer, 0, j)),
                  pl.BlockSpec((None, 1, tn), lambda j: (layer, 0, j))],
        out_specs=pl.BlockSpec((r, tn), lambda j: (0, j)),
        out_shape=jax.ShapeDtypeStruct((r, n), F32),
        compiler_params=_cparams("arbitrary"),
        name="modulation",
    )(cond, w_mod, b_mod.reshape(nl, 1, n))


def _in_proj_kernel(x_ref, mod_ref, g_ref, w_ref, *o_refs, bounds):
    x = x_ref[0]
    ms = jnp.mean(x * x, axis=-1, keepdims=True)
    h = x * lax.rsqrt(ms + NORM_EPS) * g_ref[...]
    h = h * (1.0 + mod_ref[0, 0, 1:2, :]) + mod_ref[0, 0, 0:1, :]
    hb = h.astype(BF16)
    for o_ref, (lo, hi) in zip(o_refs, bounds):
        o_ref[0] = _dot(hb, w_ref[:, lo:hi]).astype(o_ref.dtype)


def _in_proj(xs, mod, norm_g, w_in_bf, bounds, dtypes, n_ctx_tiles):
    b, t, d = xs.shape
    tm = TOKEN_TILE
    p = w_in_bf.shape[1]
    seg = lambda bi, i: (bi, jnp.where(i >= n_ctx_tiles, 1, 0), 0, 0)
    return pl.pallas_call(
        functools.partial(_in_proj_kernel, bounds=bounds),
        grid=(b, t // tm),
        in_specs=[pl.BlockSpec((1, tm, d), lambda bi, i: (bi, i, 0)),
                  pl.BlockSpec((1, 1, 2, d), seg),
                  pl.BlockSpec((1, d), lambda bi, i: (0, 0)),
                  pl.BlockSpec((d, p), lambda bi, i: (0, 0))],
        out_specs=[pl.BlockSpec((1, tm, hi - lo), lambda bi, i: (bi, i, 0)) for lo, hi in bounds],
        out_shape=[jax.ShapeDtypeStruct((b, t, hi - lo), dt) for (lo, hi), dt in zip(bounds, dtypes)],
        compiler_params=_cparams("parallel", "parallel"),
        name="in_proj",
    )(xs, mod, norm_g.reshape(1, d), w_in_bf)


def _ret_kernel(q_ref, k_ref, v_ref, cos_ref, sin_ref, dq_ref, dk_ref, dmat_ref, gm_ref, bm_ref,
                o_ref, s_ref):
    i = pl.program_id(0)

    @pl.when(i == 0)
    def _():
        s_ref[...] = jnp.zeros_like(s_ref)

    nb, c, w = q_ref.shape
    cos = cos_ref[...]
    sin = sin_ref[...]
    lane = lax.broadcasted_iota(jnp.int32, (c, LANES), 1)
    first = (lane % 32) < 16

    def rope(u):
        parts = []
        for j in range(w // LANES):
            uj = u[:, j * LANES:(j + 1) * LANES]
            nxt = pltpu.roll(uj, LANES - 16, axis=1)
            prv = pltpu.roll(uj, 16, axis=1)
            parts.append(jnp.where(first, nxt, prv))
        return u * cos + jnp.concatenate(parts, axis=1) * sin

    lane_lo = lane < HEAD_DIM

    def stack(xw):
        return jnp.concatenate([jnp.where(lane_lo, xw, 0.0), jnp.where(lane_lo, 0.0, xw)], axis=0)

    n_pairs = w // LANES
    q = [rope(q_ref[b].astype(F32)) for b in range(nb)]
    k = [rope(k_ref[b].astype(F32)) for b in range(nb)]
    chains = [(b, j) for b in range(nb) for j in range(n_pairs)]
    pair = lambda x, j: x[:, j * LANES:(j + 1) * LANES]
    qw = [pair(q[b], j) for b, j in chains]
    kw = [pair(k[b], j) for b, j in chains]
    vw = [pair(v_ref[b], j) for b, j in chains]
    s = [s_ref[b * n_pairs + j] for b, j in chains]
    inter = [_dot((x * pair(dq_ref[...], j)).astype(BF16), st.astype(BF16)) for x, st, (b, j) in zip(qw, s, chains)]
    sc = [_dot_nt(x.astype(BF16), stack(y).astype(BF16)) * dmat_ref[j]
          for x, y, (b, j) in zip(qw, kw, chains)]
    intra = [_dot(x.astype(BF16), stack(y).astype(BF16)) for x, y in zip(sc, vw)]
    ktv = [_dot_tn((y * pair(dk_ref[...], j)).astype(BF16), z.astype(BF16))
           for y, z, (b, j) in zip(kw, vw, chains)]
    for n, (b, j) in enumerate(chains):
        o_ref[b, :, j * LANES:(j + 1) * LANES] = (inter[n] + intra[n]).astype(o_ref.dtype)
        s_ref[b * n_pairs + j] = gm_ref[j] * s[n] + bm_ref[...] * ktv[n]


def _retention(p_ret, cos_t, sin_t, tabs, n_ctx, rev):
    b, t, w4 = p_ret.shape
    w = w4 // 4
    c = RET_CHUNK
    n_tot = t // c
    n_pairs = w // LANES
    dq, dk, dmat, gm, bm = tabs
    tix = lambda i: _scan_chunk(i, n_ctx, n_tot, rev)
    col = lambda j: (lambda i: (0, tix(i), j))
    const = lambda a: pl.BlockSpec(a.shape, lambda i: (0,) * a.ndim)
    return pl.pallas_call(
        _ret_kernel,
        grid=(n_tot,),
        in_specs=[pl.BlockSpec((b, c, w), col(0)), pl.BlockSpec((b, c, w), col(1)), pl.BlockSpec((b, c, w), col(2)),
                  pl.BlockSpec((c, w), lambda i: (tix(i), 0)),
                  pl.BlockSpec((c, w), lambda i: (tix(i), 0)),
                  const(dq), const(dk), const(dmat), const(gm), const(bm)],
        out_specs=pl.BlockSpec((b, c, w), lambda i: (0, tix(i), 0)),
        out_shape=jax.ShapeDtypeStruct((b, t, w), BF16),
        scratch_shapes=[pltpu.VMEM((b * n_pairs, LANES, LANES), F32)],
        compiler_params=_cparams("arbitrary"),
        name="retention_rev" if rev else "retention_fwd",
    )(p_ret, p_ret, p_ret, cos_t, sin_t, dq, dk, dmat, gm, bm)


def _ret_tables(decay_logit, w, rev):
    n_heads = w // HEAD_DIM
    c = RET_CHUNK
    lg = jax.nn.log_sigmoid(decay_logit.astype(F32))
    t = jnp.arange(c, dtype=F32)
    p = (c - 1.0 - t) if rev else t
    rel = p[:, None] - p[None, :]
    scale = HEAD_DIM ** -0.5
    dmat = jnp.where(rel >= 0, jnp.exp(lg[:, None, None] * jnp.maximum(rel, 0.0)), 0.0) * scale
    dq = jnp.exp(lg[:, None] * (p + 1.0)) * scale
    dk = jnp.exp(lg[:, None] * (c - 1.0 - p))
    lanes = lambda a: jnp.repeat(a.T, HEAD_DIM, axis=1)
    n_pairs = n_heads // 2
    dmat_w = dmat.reshape(n_pairs, 2, c, c).transpose(0, 2, 1, 3).reshape(n_pairs, c, 2 * c)
    hid = jnp.arange(LANES) // HEAD_DIM
    bm = (hid[:, None] == hid[None, :]).astype(F32)
    gm = bm[None] * jnp.exp(lg * c).reshape(n_pairs, 2)[:, hid][:, :, None]
    return lanes(dq), lanes(dk), dmat_w, gm, bm


def _rope_tables(n_ctx_tok, seq, w):
    half = HEAD_DIM // 2
    quarter = half // 2
    inv_freq = ROPE_BASE ** (-jnp.arange(quarter, dtype=F32) / quarter)
    tok = jnp.arange(seq)
    rows = (tok // GRID_W).astype(F32)
    cols = (tok % GRID_W).astype(F32)
    o = jnp.arange(w) % HEAD_DIM
    pos = jnp.where(o[None, :] < half, rows[:, None], cols[:, None])
    ang = pos * inv_freq[o % quarter][None, :]
    sign = jnp.where((o % half) < quarter, -1.0, 1.0)[None, :]
    cos = jnp.concatenate([jnp.ones((n_ctx_tok, w), F32), jnp.cos(ang)], axis=0)
    sin = jnp.concatenate([jnp.zeros((n_ctx_tok, w), F32), jnp.sin(ang) * sign], axis=0)
    return cos, sin


def _rwkv_kernel(z_ref, mu_ref, vec_ref, wup_ref, aup_ref, e_ref, minc_ref, strict_ref, incl_ref,
                 y_ref, bon_ref, st_ref, zprev_ref, *, rev, n_ctx, w):
    i = pl.program_id(0)
    nb, c, zw = z_ref.shape
    n_pairs = w // LANES
    rows = nb * c

    @pl.when(i == 0)
    def _():
        st_ref[...] = jnp.zeros_like(st_ref)

    @pl.when((i == 0) | (i == n_ctx))
    def _():
        zprev_ref[...] = jnp.zeros_like(zprev_ref)

    z = z_ref[...].reshape(rows, zw)
    rin = lax.broadcasted_iota(jnp.int32, (rows, zw), 0) % c
    prev = jnp.concatenate([jnp.broadcast_to(zprev_ref[b, 0:1, :], (c, zw)) for b in range(nb)], axis=0)
    if rev:
        zs = jnp.where(rin == c - 1, prev, pltpu.roll(z, rows - 1, axis=0))
        for b in range(nb):
            zprev_ref[b, 0:1, :] = z[b * c:b * c + 1, :]
    else:
        zs = jnp.where(rin == 0, prev, pltpu.roll(z, 1, axis=0))
        for b in range(nb):
            zprev_ref[b, 0:1, :] = z[b * c + c - 1:b * c + c, :]
    zd = z + (zs - z) * mu_ref[...]

    r = zd[:, 0:w]
    k = zd[:, w:2 * w]
    v = zd[:, 2 * w:3 * w]
    lora = zd[:, 3 * w:3 * w + LANES]
    lane = lax.broadcasted_iota(jnp.int32, (rows, LANES), 1)
    lora = jnp.where(lane < DECAY_LORA, jnp.tanh(lora), lora)

    w0, a0, k_k, k_a, r_k = (vec_ref[j:j + 1, :] for j in range(5))
    e2 = e_ref[...]
    e22 = jnp.concatenate([e2, e2], axis=0)

    def head_sums(x, pieces):
        outs = []
        for j in range(n_pairs):
            xj = x[:, j * LANES:(j + 1) * LANES]
            if pieces == 1:
                outs.append(_dot(xj.astype(BF16), e2))
            else:
                hi = xj.astype(BF16)
                mid = (xj - hi.astype(F32)).astype(BF16)
                outs.append(_dot(jnp.concatenate([hi, mid], axis=1), e22))
        return jnp.concatenate(outs, axis=1)

    w_log = -_softplus(-(w0 + _dot_x3k(lora, wup_ref[...]))) - 0.5
    logw = -jnp.exp(w_log)
    a = _sigmoid(a0 + _dot_x3k(lora, aup_ref[...]))
    kk0 = k * k_k
    kk = kk0 / jnp.maximum(jnp.sqrt(head_sums(kk0 * kk0, 2)), 1e-12)
    k2 = k * (1.0 + (a - 1.0) * k_a)
    bon_ref[...] = (head_sums(r * k2 * r_k, 1) * v).reshape(nb, c, w).astype(bon_ref.dtype)

    cinc = jnp.concatenate(
        [_dot(minc_ref[...], jnp.concatenate(_split3(logw[b * c:(b + 1) * c]), axis=0)) for b in range(nb)], axis=0)
    e_inc = jnp.exp(cinc)
    e_neg = jnp.exp(-cinc)
    rt = r * e_inc
    kt = k2 * e_neg
    bt = kk * a * e_neg
    kkt = kk * jnp.exp(cinc - logw)
    last = 0 if rev else c - 1

    strict = strict_ref[...] > 0.0
    incl = incl_ref[...] > 0.0
    lane_lo = lax.broadcasted_iota(jnp.int32, (c, LANES), 1) < HEAD_DIM
    head_r = lax.broadcasted_iota(jnp.int32, (LANES, LANES), 0) // HEAD_DIM
    head_c = lax.broadcasted_iota(jnp.int32, (LANES, LANES), 1) // HEAD_DIM
    diag = head_r == head_c

    def stack(xw):
        return jnp.concatenate([jnp.where(lane_lo, xw, 0.0), jnp.where(lane_lo, 0.0, xw)], axis=0)

    chains = [(b, j) for b in range(nb) for j in range(n_pairs)]

    def win(x, ch):
        b, j = ch
        return x[b * c:(b + 1) * c, j * LANES:(j + 1) * LANES]

    st = [st_ref[b * n_pairs + j] for b, j in chains]
    lhs = [jnp.concatenate([win(kkt, ch), win(rt, ch)], axis=0).astype(BF16) for ch in chains]
    g = [_dot_nt(l, jnp.concatenate([stack(win(bt, ch)), stack(win(kt, ch))], axis=0).astype(BF16))
         for l, ch in zip(lhs, chains)]
    a_b = [jnp.where(strict, x[0:c, 0:2 * c], 0.0) for x in g]
    a_k = [jnp.where(strict, x[0:c, 2 * c:4 * c], 0.0).astype(BF16) for x in g]
    r_kb = [jnp.concatenate([jnp.where(incl, x[c:2 * c, 2 * c:4 * c], 0.0),
                             -jnp.where(incl, x[c:2 * c, 0:2 * c], 0.0)], axis=1).astype(BF16) for x in g]
    x0 = [_dot_nt(l, s.astype(BF16)) for l, s in zip(lhs, st)]
    v_sb = [stack(win(v, ch)).astype(BF16) for ch in chains]
    u = [x[0:c] + _dot(ak, vs) for x, ak, vs in zip(x0, a_k, v_sb)]

    pw = a_b
    steps, sign = 1, -1.0
    while 2 * steps < c:
        both = [_dot(p.astype(BF16), jnp.concatenate([stack(p), stack(x)], axis=1).astype(BF16))
                for p, x in zip(pw, u)]
        u = [x + sign * y[:, 2 * c:4 * c] for x, y in zip(u, both)]
        pw = [y[:, 0:2 * c] for y in both]
        steps, sign = 2 * steps, 1.0
    u = [x + sign * _dot(p.astype(BF16), stack(x).astype(BF16)) for x, p in zip(u, pw)]

    y = [x[c:2 * c] + _dot(rk, jnp.concatenate([vs, stack(uu).astype(BF16)], axis=0))
         for x, rk, vs, uu in zip(x0, r_kb, v_sb, u)]
    upd = [_dot_tn(jnp.concatenate([win(v, ch), x], axis=0).astype(BF16),
                   jnp.concatenate([win(kt, ch), -win(bt, ch)], axis=0).astype(BF16))
           for ch, x in zip(chains, u)]
    for n, (b, j) in enumerate(chains):
        w_end = e_inc[b * c + last:b * c + last + 1, j * LANES:(j + 1) * LANES]
        st_ref[b * n_pairs + j] = jnp.where(diag, (st[n] + upd[n]) * w_end, 0.0)
        y_ref[b, :, j * LANES:(j + 1) * LANES] = y[n].astype(y_ref.dtype)


def _rwkv(p_z, prm, n_ctx, rev):
    b, t, zw = p_z.shape
    w = (zw - DECAY_LORA - ICLR_LORA) // 3
    c = RWKV_CHUNK
    n_tot = t // c
    mu, vecs, wup, aup, e_bf, minc, strict, incl = prm
    tix = lambda i: _scan_chunk(i, n_ctx, n_tot, rev)
    const = lambda i: (0, 0)
    full = lambda a: pl.BlockSpec(a.shape, const)
    return pl.pallas_call(
        functools.partial(_rwkv_kernel, rev=rev, n_ctx=n_ctx, w=w),
        grid=(n_tot,),
        in_specs=[pl.BlockSpec((b, c, zw), lambda i: (0, tix(i), 0)),
                  full(mu), full(vecs), full(wup), full(aup), full(e_bf), full(minc), full(strict), full(incl)],
        out_specs=[pl.BlockSpec((b, c, w), lambda i: (0, tix(i), 0)),
                   pl.BlockSpec((b, c, w), lambda i: (0, tix(i), 0))],
        out_shape=[jax.ShapeDtypeStruct((b, t, w), BF16), jax.ShapeDtypeStruct((b, t, w), BF16)],
        scratch_shapes=[pltpu.VMEM((b * (w // LANES), LANES, LANES), F32), pltpu.VMEM((b, 8, zw), F32)],
        compiler_params=_cparams("arbitrary"),
        name="rwkv7_rev" if rev else "rwkv7_fwd",
    )(p_z, mu, vecs, wup, aup, e_bf, minc, strict, incl)


def _rwkv_params(mu, w0, w_up, a0, a_up, k_k, k_a, r_k, rev, n_batch):
    w = w0.shape[0]
    c = RWKV_CHUNK
    vecs = jnp.concatenate([jnp.stack([w0, a0, k_k, k_a, r_k]), jnp.zeros((3, w), F32)], axis=0)
    wup = jnp.concatenate([w_up, jnp.zeros((ICLR_LORA, w), F32)], axis=0)
    aup = jnp.concatenate([jnp.zeros((DECAY_LORA, w), F32), a_up], axis=0)
    hid = jnp.arange(LANES) // HEAD_DIM
    e_bf = (hid[:, None] == hid[None, :]).astype(BF16)
    t = jnp.arange(c)
    p = (c - 1 - t) if rev else t
    le = p[None, :] <= p[:, None]
    lt = p[None, :] < p[:, None]
    strict = jnp.tile(lt, (1, 2)).astype(F32)
    incl = jnp.tile(le, (1, 2)).astype(F32)
    minc = jnp.tile(le, (1, 3)).astype(BF16)
    return mu.reshape(1, -1), vecs, wup, aup, e_bf, minc, strict, incl


def _lru_kernel(x_ref, cw_ref, vec_ref, wa_ref, wx_ref, h_ref, hcar_ref, ucar_ref, *, rev, n_ctx):
    i = pl.program_id(0)
    nb, c, w = x_ref.shape
    rows = nb * c

    @pl.when(i == 0)
    def _():
        hcar_ref[...] = jnp.zeros_like(hcar_ref)

    @pl.when((i == 0) | (i == n_ctx))
    def _():
        ucar_ref[...] = jnp.zeros_like(ucar_ref)

    u0 = x_ref[...].astype(F32).reshape(rows, w)
    row = lax.broadcasted_iota(jnp.int32, (rows, w), 0) % c

    def per_batch(ref, j):
        return jnp.concatenate([jnp.broadcast_to(ref[b, j:j + 1, :], (c, w)) for b in range(nb)], axis=0)

    def shifted(x, s, carry, fill):
        if rev:
            rolled = pltpu.roll(x, rows - s, axis=0)
            edge = row >= c - s
        else:
            rolled = pltpu.roll(x, s, axis=0)
            edge = row < s
        if carry is None:
            return jnp.where(edge, fill, rolled)
        return jnp.where(edge, carry, rolled)

    conv = vec_ref[0:1, :] + cw_ref[LRU_CONV - 1:LRU_CONV, :] * u0
    for m in range(1, LRU_CONV):
        car = jnp.zeros((rows, w), F32)
        for qpos in range(m):
            r_idx = (c - 1 - qpos) if rev else qpos
            car = jnp.where(row == r_idx, per_batch(ucar_ref, m - qpos - 1), car)
        conv = conv + cw_ref[LRU_CONV - 1 - m:LRU_CONV - m, :] * shifted(u0, m, car, None)
    for m in range(1, LRU_CONV):
        r_idx = (m - 1) if rev else (c - m)
        for b in range(nb):
            ucar_ref[b, m - 1:m, :] = u0[b * c + r_idx:b * c + r_idx + 1, :]

    cb = conv.astype(BF16)
    r = _sigmoid(_dot(cb, wa_ref[...]) + vec_ref[1:2, :])
    ig = _sigmoid(_dot(cb, wx_ref[...]) + vec_ref[2:3, :])
    log_a = -LRU_C * r * vec_ref[3:4, :]
    a = jnp.exp(log_a)
    bb = jnp.sqrt(1.0 - jnp.exp(2.0 * log_a)) * (ig * conv)

    s = 1
    while s < c:
        bb = bb + a * shifted(bb, s, None, 0.0)
        a = a * shifted(a, s, None, 1.0)
        s *= 2
    h = bb + a * per_batch(hcar_ref, 0)
    h_ref[...] = h.reshape(nb, c, w).astype(h_ref.dtype)
    last = 0 if rev else c - 1
    for b in range(nb):
        hcar_ref[b, 0:1, :] = h[b * c + last:b * c + last + 1, :]


def _lru(p_lru, prm, n_ctx, rev):
    b, t, w2 = p_lru.shape
    w = w2 // 2
    c = LRU_CHUNK
    n_tot = t // c
    cw, vecs, wa, wx = prm
    tix = lambda i: _scan_chunk(i, n_ctx, n_tot, rev)
    const = lambda i: (0, 0)
    return pl.pallas_call(
        functools.partial(_lru_kernel, rev=rev, n_ctx=n_ctx),
        grid=(n_tot,),
        in_specs=[pl.BlockSpec((b, c, w), lambda i: (0, tix(i), 0)),
                  pl.BlockSpec(cw.shape, const), pl.BlockSpec(vecs.shape, const),
                  pl.BlockSpec(wa.shape, const), pl.BlockSpec(wx.shape, const)],
        out_specs=pl.BlockSpec((b, c, w), lambda i: (0, tix(i), 0)),
        out_shape=jax.ShapeDtypeStruct((b, t, w), BF16),
        scratch_shapes=[pltpu.VMEM((b, 8, w), F32), pltpu.VMEM((b, 8, w), F32)],
        compiler_params=_cparams("arbitrary"),
        name="rglru_rev" if rev else "rglru_fwd",
    )(p_lru, cw, vecs, wa, wx)


def _lru_params(conv_w, conv_b, wa, ba, wx, bx, lam):
    w = conv_b.shape[0]
    cw = jnp.concatenate([conv_w, jnp.zeros((8 - LRU_CONV, w), F32)], axis=0)
    vecs = jnp.concatenate([jnp.stack([conv_b, ba, bx, jax.nn.softplus(-lam)]), jnp.zeros((4, w), F32)], axis=0)
    return cw, vecs, jax.scipy.linalg.block_diag(*wa).astype(BF16), jax.scipy.linalg.block_diag(*wx).astype(BF16)


def _head_norm(y, e2, gain, bias, eps):
    inv = 1.0 / HEAD_DIM

    def head_sums(x):
        xb = x.astype(BF16)
        return jnp.concatenate([_dot(xb[:, j:j + LANES], e2) for j in range(0, x.shape[1], LANES)], axis=1)

    yc = y - head_sums(y) * inv
    var = head_sums(yc * yc) * inv
    return yc * lax.rsqrt(var + eps) * gain + bias


def _mix_out_kernel(x_ref, mod_ref, of_ref, ob_ref, g_ref, yf_ref, yb_ref, bf_ref, bb_ref, gd_ref,
                    hf_ref, hb_ref, lg_ref, gn_ref, e_ref, gup_ref, wout_ref, n2g_ref, wr_ref, br_ref, tri_ref,
                    xo_ref, h2_ref, idx_ref, gate_ref, rank_ref, cnt_ref, base_ref, *, w_ret, w_rw):
    @pl.when((pl.program_id(0) == 0) & (pl.program_id(1) == 0))
    def _():
        base_ref[...] = jnp.zeros_like(base_ref)

    e_bf = e_ref[...]
    f32 = lambda ref: ref[0].astype(F32)
    g = f32(g_ref)
    ret = _head_norm(f32(of_ref) + f32(ob_ref), e_bf, gn_ref[0:1, :], gn_ref[1:2, :], RET_GN_EPS)
    ret = ret * (g * _sigmoid(g))
    gate = _dot(_sigmoid(f32(gd_ref)).astype(BF16), gup_ref[...])
    rw = _head_norm(f32(yf_ref) + f32(yb_ref), e_bf, gn_ref[2:3, :], gn_ref[3:4, :], RWKV_GN_EPS)
    rw = (rw + f32(bf_ref) + f32(bb_ref)) * gate
    lg = f32(lg_ref)
    gelu = 0.5 * lg * (1.0 + jnp.tanh(0.7978845608028654 * (lg + 0.044715 * (lg * lg * lg))))
    lru = (f32(hf_ref) + f32(hb_ref)) * gelu
    mix = (_dot(ret.astype(BF16), wout_ref[0:w_ret, :])
           + _dot(rw.astype(BF16), wout_ref[w_ret:w_ret + w_rw, :])
           + _dot(lru.astype(BF16), wout_ref[w_ret + w_rw:, :]))
    x = x_ref[0] + mod_ref[0, 0, 0:1, :] * mix
    xo_ref[0] = x
    ms = jnp.mean(x * x, axis=-1, keepdims=True)
    h2 = x * lax.rsqrt(ms + NORM_EPS) * n2g_ref[...]
    h2 = h2 * (1.0 + mod_ref[0, 0, 2:3, :]) + mod_ref[0, 0, 1:2, :]
    h2_ref[0] = h2.astype(BF16)
    logits = _dot_x3(h2, wr_ref[...]) + br_ref[...]
    idx_o, gate_o, rank_o = _route_tile(logits, tri_ref[...], base_ref)
    idx_ref[0] = idx_o
    gate_ref[0] = gate_o
    rank_ref[0] = rank_o
    cnt_ref[...] = base_ref[...].astype(jnp.int32)


def _mix_out(xs, mod, o_f, o_b, p_ret, y_f, y_b, bon_f, bon_b, p_gd, h_f, h_b, p_lru,
             gn, e_bf, g_up_bf, w_out_bf, norm2_g, w_router, b_router, n_ctx_tiles):
    b, t, d = xs.shape
    tm = TOKEN_TILE
    w_ret, w_rw, w_lru = o_f.shape[2], y_f.shape[2], h_f.shape[2]
    ne = w_router.shape[1]
    tok = lambda wd, j=0: pl.BlockSpec((1, tm, wd), lambda bi, i: (bi, i, j))
    const = lambda a: pl.BlockSpec(a.shape, lambda bi, i: (0,) * a.ndim)
    seg = lambda bi, i: (bi, jnp.where(i >= n_ctx_tiles, 1, 0), 0, 0)
    n2g = norm2_g.reshape(1, d)
    br = b_router.reshape(1, ne)
    tt = jnp.arange(tm)
    tri = (tt[None, :] < tt[:, None]).astype(BF16)
    return pl.pallas_call(
        functools.partial(_mix_out_kernel, w_ret=w_ret, w_rw=w_rw),
        grid=(b, t // tm),
        in_specs=[tok(d), pl.BlockSpec((1, 1, 3, d), seg),
                  tok(w_ret), tok(w_ret), tok(w_ret, 3),
                  tok(w_rw), tok(w_rw), tok(w_rw), tok(w_rw), tok(GATE_LORA),
                  tok(w_lru), tok(w_lru), tok(w_lru, 1),
                  const(gn), const(e_bf), const(g_up_bf), const(w_out_bf), const(n2g), const(w_router), const(br),
                  const(tri)],
        out_specs=[tok(d), tok(d), tok(LANES), tok(LANES), tok(LANES),
                   pl.BlockSpec((8, ne), lambda bi, i: (0, 0))],
        out_shape=[jax.ShapeDtypeStruct((b, t, d), F32), jax.ShapeDtypeStruct((b, t, d), BF16),
                   jax.ShapeDtypeStruct((b, t, LANES), jnp.int32), jax.ShapeDtypeStruct((b, t, LANES), F32),
                   jax.ShapeDtypeStruct((b, t, LANES), jnp.int32), jax.ShapeDtypeStruct((8, ne), jnp.int32)],
        scratch_shapes=[pltpu.VMEM((8, ne), F32)],
        compiler_params=_cparams("arbitrary", "arbitrary"),
        name="mix_out",
    )(xs, mod, o_f, o_b, p_ret, y_f, y_b, bon_f, bon_b, p_gd, h_f, h_b, p_lru,
      gn, e_bf, g_up_bf, w_out_bf, n2g, w_router, br, tri)


def _moe_kernel(be_ref, first_ref, nu_ref, x_ref, w1_ref, b1_ref, w2_ref, b2_ref, y_ref, w1b_ref, w2b_ref):
    i = pl.program_id(0)
    de = w2_ref.shape[1]

    @pl.when(first_ref[i] == 1)
    def _():
        w1b_ref[...] = w1_ref[0].astype(BF16)
        w2b_ref[...] = w2_ref[0].astype(BF16)

    @pl.when(i < nu_ref[0])
    def _():
        gu = _dot(x_ref[...], w1b_ref[...]) + b1_ref[0]
        glu = jnp.minimum(gu[:, :de], SWIGLU_LIMIT)
        lin = jnp.clip(gu[:, de:], -SWIGLU_LIMIT, SWIGLU_LIMIT)
        act = glu * _sigmoid(SWIGLU_ALPHA * glu) * (lin + 1.0)
        y_ref[...] = (_dot(act.astype(BF16), w2b_ref[...]) + b2_ref[0]).astype(y_ref.dtype)

    @pl.when(i >= nu_ref[0])
    def _():
        y_ref[...] = jnp.zeros_like(y_ref)


def _moe_ffn(hb, block_e, first, n_used, w1, b1, w2, b2, layer):
    n_slots, d = hb.shape
    tm = MOE_TILE
    nl, ne, _, d2 = w1.shape
    de = w2.shape[2]
    wsel = lambda i, be, fi, nu: (layer, be[i], 0, 0)
    return pl.pallas_call(
        _moe_kernel,
        grid_spec=pltpu.PrefetchScalarGridSpec(
            num_scalar_prefetch=3,
            grid=(n_slots // tm,),
            in_specs=[pl.BlockSpec((tm, d), lambda i, be, fi, nu: (i, 0)),
                      pl.BlockSpec((None, 1, d, d2), wsel),
                      pl.BlockSpec((None, 1, 1, d2), wsel),
                      pl.BlockSpec((None, 1, de, d), wsel),
                      pl.BlockSpec((None, 1, 1, d), wsel)],
            out_specs=pl.BlockSpec((tm, d), lambda i, be, fi, nu: (i, 0)),
            scratch_shapes=[pltpu.VMEM((d, d2), BF16), pltpu.VMEM((de, d), BF16)],
        ),
        out_shape=jax.ShapeDtypeStruct((n_slots, d), BF16),
        compiler_params=_cparams("arbitrary"),
        name="moe_ffn",
    )(block_e, first, n_used, hb, w1, b1.reshape(nl, ne, 1, d2), w2, b2.reshape(nl, ne, 1, d))


def _route_tile(lg, tri, base_ref):
    tr, ne = lg.shape
    lane = lax.broadcasted_iota(jnp.int32, (tr, ne), 1).astype(F32)
    out_lane = lax.broadcasted_iota(jnp.int32, (tr, LANES), 1)
    vals = lg
    sel = jnp.zeros((tr, ne), F32)
    picks, tops = [], []
    for _ in range(TOP_K):
        m = jnp.max(vals, axis=-1, keepdims=True)
        ix = jnp.min(jnp.where(vals == m, lane, float(ne)), axis=-1, keepdims=True)
        hit = lane == ix
        sel = jnp.where(hit, 1.0, sel)
        vals = jnp.where(hit, -jnp.inf, vals)
        picks.append(ix)
        tops.append(m)
    ex = [jnp.exp(t - tops[0]) for t in tops]
    den = ex[0] + ex[1] + ex[2] + ex[3]
    before = _dot(tri, sel.astype(BF16)) + base_ref[0:1, :]
    idx_o = jnp.zeros((tr, LANES), F32)
    gate_o = jnp.zeros((tr, LANES), F32)
    rank_o = jnp.zeros((tr, LANES), F32)
    for k in range(TOP_K):
        rk = jnp.sum(jnp.where(lane == picks[k], before, 0.0), axis=-1, keepdims=True)
        idx_o = jnp.where(out_lane == k, picks[k], idx_o)
        gate_o = jnp.where(out_lane == k, ex[k] / den, gate_o)
        rank_o = jnp.where(out_lane == k, rk, rank_o)
    total = base_ref[0:1, :] + jnp.sum(sel, axis=0, keepdims=True)
    base_ref[...] = jnp.broadcast_to(total, base_ref.shape)
    return idx_o.astype(jnp.int32), gate_o, rank_o.astype(jnp.int32)


def _route_meta(idx, rank, counts):
    n_tok = idx.shape[0]
    ne = counts.shape[0]
    tm = MOE_TILE
    n_assign = n_tok * TOP_K
    padded = (counts + tm - 1) // tm * tm
    pend = jnp.cumsum(padded)
    pstart = pend - padded
    start = jnp.cumsum(counts) - counts
    eid = jnp.arange(ne, dtype=jnp.int32)
    slot = jnp.sum(jnp.where(idx[..., None] == eid, pstart, 0), axis=-1).astype(jnp.int32) + rank
    n_blocks = (n_assign + ne * (tm - 1) + tm - 1) // tm
    blk_start = jnp.arange(n_blocks, dtype=jnp.int32) * tm
    block_e = jnp.minimum(jnp.sum(pend[None, :] <= blk_start[:, None], axis=1), ne - 1).astype(jnp.int32)
    first = jnp.concatenate([jnp.ones((1,), jnp.int32), (block_e[1:] != block_e[:-1]).astype(jnp.int32)])
    n_used = (pend[-1] // tm).astype(jnp.int32).reshape(1)
    _, order = lax.sort_key_val(slot.reshape(-1), jnp.arange(n_assign, dtype=jnp.int32))
    off = jnp.arange(n_blocks * tm, dtype=jnp.int32) - jnp.repeat(pstart[block_e], tm)
    valid = off < jnp.repeat(counts[block_e], tm)
    pos = jnp.clip(jnp.repeat(start[block_e], tm) + off, 0, n_assign - 1)
    spread = jnp.arange(n_blocks * tm, dtype=jnp.int32) % n_tok
    slot_tok = jnp.where(valid, order[pos] // TOP_K, spread).astype(jnp.int32)
    return slot, slot_tok, block_e, first, n_used


def _combine_kernel(x_ref, mod_ref, y_ref, gate_ref, g_ref, o_ref, *, final):
    gate = gate_ref[0]
    y = y_ref[0, 0].astype(F32) * gate[:, 0:1]
    for k in range(1, TOP_K):
        y = y + y_ref[k, 0].astype(F32) * gate[:, k:k + 1]
    x = x_ref[0] + mod_ref[0, 0, 0:1, :] * y
    if final:
        ms = jnp.mean(x * x, axis=-1, keepdims=True)
        x = x * lax.rsqrt(ms + NORM_EPS) * g_ref[...]
    o_ref[0] = x


def _combine(xs, mod, yg, gates, final_g, n_ctx_tiles, final):
    b, t, d = xs.shape
    tm = TOKEN_TILE
    skip = n_ctx_tiles if final else 0
    seg = lambda bi, i: (bi, jnp.where(i + skip >= n_ctx_tiles, 1, 0), 0, 0)
    return pl.pallas_call(
        functools.partial(_combine_kernel, final=final),
        grid=(b, t // tm - skip),
        in_specs=[pl.BlockSpec((1, tm, d), lambda bi, i: (bi, i + skip, 0)),
                  pl.BlockSpec((1, 1, 1, d), seg),
                  pl.BlockSpec((TOP_K, 1, tm, d), lambda bi, i: (0, bi, i + skip, 0)),
                  pl.BlockSpec((1, tm, LANES), lambda bi, i: (bi, i + skip, 0)),
                  pl.BlockSpec((1, d), lambda bi, i: (0, 0))],
        out_specs=pl.BlockSpec((1, tm, d), lambda bi, i: (bi, i, 0)),
        out_shape=jax.ShapeDtypeStruct((b, t - skip * tm, d), F32),
        compiler_params=_cparams("parallel", "parallel"),
        name="combine_final" if final else "combine",
    )(xs, mod, yg, gates, final_g.reshape(1, d))


def kernel(x, c, ctx, c_ctx, w_mod, b_mod, norm1_g, norm2_g, w_in, w_out, ret_decay_logit, ret_gn_g, ret_gn_b, rwkv_mu, rwkv_w0, rwkv_w_up, rwkv_a0, rwkv_a_up, rwkv_k_k, rwkv_k_a, rwkv_g_up, rwkv_r_k, rwkv_gn_g, rwkv_gn_b, lru_conv_w, lru_conv_b, lru_wa, lru_ba, lru_wx, lru_bx, lru_lambda, moe_w_router, moe_b_router, moe_w1, moe_b1, moe_w2, moe_b2, final_norm_g):
    bsz, seq, dm = x.shape
    n_ctx_tok = ctx.shape[1]
    depth = w_in.shape[0]
    n_experts = moe_w_router.shape[2]
    w_ret = 3 * dm // 8
    w_rw = 3 * dm // 8
    w_lru = dm - w_ret - w_rw
    zw = 3 * w_rw + DECAY_LORA + ICLR_LORA
    sizes = (4 * w_ret, zw, GATE_LORA, 2 * w_lru)
    bounds, off = [], 0
    for s in sizes:
        bounds.append((off, off + s))
        off += s
    bounds = tuple(bounds)
    assert off == w_in.shape[2]
    assert n_ctx_tok % TOKEN_TILE == 0 and seq % TOKEN_TILE == 0 and seq % GRID_W == 0
    t_all = n_ctx_tok + seq
    n_ctx_tiles = n_ctx_tok // TOKEN_TILE

    xs = jnp.concatenate([ctx, x], axis=1)
    cos_t, sin_t = _rope_tables(n_ctx_tok, seq, w_ret)
    hid = jnp.arange(LANES) // HEAD_DIM
    e_bf = (hid[:, None] == hid[None, :]).astype(BF16)
    cond =jnp.concatenate([c, c_ctx[None, :], jnp.zeros((8 - (bsz + 1) % 8, dm), F32)], axis=0)

    for l in range(depth):
        last = l == depth - 1
        mod = _modulation(cond, w_mod, b_mod, l)
        mod_l = mod[:bsz].reshape(bsz, 6, dm)
        mod_c = jnp.broadcast_to(mod[bsz].reshape(1, 6, dm), (bsz, 6, dm))
        modsel = jnp.stack([mod_c, mod_l], axis=1)

        p_ret, p_z, p_gd, p_lru = _in_proj(xs, modsel[:, :, 0:2], norm1_g[l], w_in[l].astype(BF16), bounds,
                                           (BF16, F32, BF16, BF16), n_ctx_tiles)

        ret_o, rw_y, rw_bon, lru_h = [], [], [], []
        for d in range(2):
            rev = d == 1
            ret_o.append(_retention(p_ret, cos_t, sin_t, _ret_tables(ret_decay_logit[l, d], w_ret, rev),
                                    n_ctx_tok // RET_CHUNK, rev))
            prm = _rwkv_params(rwkv_mu[l, d], rwkv_w0[l, d], rwkv_w_up[l, d], rwkv_a0[l, d], rwkv_a_up[l, d],
                               rwkv_k_k[l, d], rwkv_k_a[l, d], rwkv_r_k[l], rev, bsz)
            y, bon = _rwkv(p_z, prm, n_ctx_tok // RWKV_CHUNK, rev)
            rw_y.append(y)
            rw_bon.append(bon)
            lru_h.append(_lru(p_lru, _lru_params(lru_conv_w[l, d], lru_conv_b[l, d], lru_wa[l, d], lru_ba[l, d],
                                                 lru_wx[l, d], lru_bx[l, d], lru_lambda[l, d]),
                              n_ctx_tok // LRU_CHUNK, rev))

        gn = jnp.concatenate([jnp.stack([ret_gn_g[l], ret_gn_b[l], rwkv_gn_g[l], rwkv_gn_b[l]]),
                              jnp.zeros((4, w_ret), F32)], axis=0)
        xs, h2, idx, gates, rank, counts = _mix_out(
            xs, modsel[:, :, 2:5], ret_o[0], ret_o[1], p_ret, rw_y[0], rw_y[1],
            rw_bon[0], rw_bon[1], p_gd, lru_h[0], lru_h[1], p_lru,
            gn, e_bf, rwkv_g_up[l].astype(BF16), w_out[l].astype(BF16), norm2_g[l],
            moe_w_router[l], moe_b_router[l], n_ctx_tiles)

        n_tok = bsz * t_all
        slot, slot_tok, block_e, first, n_used = _route_meta(
            idx.reshape(n_tok, LANES)[:, :TOP_K], rank.reshape(n_tok, LANES)[:, :TOP_K], counts[0])
        y_sorted = _moe_ffn(h2.reshape(n_tok, dm)[slot_tok], block_e, first, n_used,
                            moe_w1, moe_b1, moe_w2, moe_b2, l)
        yg = y_sorted[slot.T].reshape(TOP_K, bsz, t_all, dm)
        xs = _combine(xs, modsel[:, :, 5:6], yg, gates, final_norm_g, n_ctx_tiles, last)
    return xs
```

```python
import functools

import jax
import jax.numpy as jnp
from jax import lax
from jax.experimental import pallas as pl
from jax.experimental.pallas import tpu as pltpu

F32 = jnp.float32
BF16 = jnp.bfloat16

HEAD_DIM = 64
NORM_EPS = 1e-6
RET_GN_EPS = 1e-5
RWKV_GN_EPS = 64e-5
ROPE_BASE = 10000.0
GRID_W = 64
LRU_CONV = 4
LRU_C = 8.0
TOP_K = 4
SWIGLU_LIMIT = 7.0
SWIGLU_ALPHA = 1.702
DECAY_LORA = 64
ICLR_LORA = 64
GATE_LORA = 128

LANES = 128
TOKEN_TILE = 256
RET_CHUNK = 128
RWKV_CHUNK = 64
LRU_CHUNK = 128
MOE_TILE = 512
VMEM_LIMIT = 56 * 1024 * 1024


def _cparams(*sem):
    return pltpu.CompilerParams(dimension_semantics=sem, vmem_limit_bytes=VMEM_LIMIT)


def _scan_chunk(i, n_ctx, n_tot, rev):
    if not rev:
        return i
    return jnp.where(i < n_ctx, n_ctx - 1 - i, n_tot + n_ctx - 1 - i)


def _split3(a):
    hi = a.astype(BF16)
    r1 = a - hi.astype(F32)
    mid = r1.astype(BF16)
    lo = (r1 - mid.astype(F32)).astype(BF16)
    return hi, mid, lo


def _dot(a, b):
    return jnp.dot(a, b, preferred_element_type=F32)


def _dot_nt(a, b):
    return lax.dot_general(a, b, (((1,), (1,)), ((), ())), preferred_element_type=F32)


def _dot_tn(a, b):
    return lax.dot_general(a, b, (((0,), (0,)), ((), ())), preferred_element_type=F32)


def _dot_exact_rhs(a, b_bf):
    hi, mid, lo = _split3(a)
    return _dot(hi, b_bf) + _dot(mid, b_bf) + _dot(lo, b_bf)


def _dot_exact_lhs(a_bf, b):
    hi, mid, lo = _split3(b)
    return _dot(a_bf, hi) + _dot(a_bf, mid) + _dot(a_bf, lo)


def _dot_x3(a, b):
    a_hi = a.astype(BF16)
    a_lo = (a - a_hi.astype(F32)).astype(BF16)
    b_hi = b.astype(BF16)
    b_lo = (b - b_hi.astype(F32)).astype(BF16)
    return _dot(a_hi, b_hi) + _dot(a_lo, b_hi) + _dot(a_hi, b_lo)


def _dot_x3k(a, b):
    a_hi = a.astype(BF16)
    a_lo = (a - a_hi.astype(F32)).astype(BF16)
    b_hi = b.astype(BF16)
    b_lo = (b - b_hi.astype(F32)).astype(BF16)
    return (_dot(jnp.concatenate([a_hi, a_lo], axis=1), jnp.concatenate([b_hi, b_hi], axis=0))
            + _dot(a_hi, b_lo))


def _sigmoid(x):
    return 1.0 / (1.0 + jnp.exp(-x))


def _softplus(x):
    return jnp.maximum(x, 0.0) + jnp.log(1.0 + jnp.exp(-jnp.abs(x)))


def _mod_kernel(c_ref, w_ref, b_ref, o_ref):
    c = c_ref[...]
    s = c * _sigmoid(c)
    o_ref[...] = _dot_x3(s, w_ref[...]) + b_ref[...]


def _modulation(cond, w_mod, b_mod, layer):
    r, d = cond.shape
    nl, _, n = w_mod.shape
    tn = d
    return pl.pallas_call(
        _mod_kernel,
        grid=(n // tn,),
        in_specs=[pl.BlockSpec((r, d), lambda j: (0, 0)),
                  pl.BlockSpec((None, d, tn), lambda j: (layer, 0, j)),
                  pl.BlockSpec((None, 1, tn), lambda j: (layer, 0, j))],
        out_specs=pl.BlockSpec((r, tn), lambda j: (0, j)),
        out_shape=jax.ShapeDtypeStruct((r, n), F32),
        compiler_params=_cparams("arbitrary"),
        name="modulation",
    )(cond, w_mod, b_mod.reshape(nl, 1, n))


def _in_proj_kernel(x_ref, mod_ref, g_ref, w_ref, *o_refs, bounds):
    x = x_ref[0]
    ms = jnp.mean(x * x, axis=-1, keepdims=True)
    h = x * lax.rsqrt(ms + NORM_EPS) * g_ref[...]
    h = h * (1.0 + mod_ref[0, 0, 1:2, :]) + mod_ref[0, 0, 0:1, :]
    hb = h.astype(BF16)
    for o_ref, (lo, hi) in zip(o_refs, bounds):
        o_ref[0] = _dot(hb, w_ref[:, lo:hi]).astype(o_ref.dtype)


def _in_proj(xs, mod, norm_g, w_in_bf, bounds, dtypes, n_ctx_tiles):
    b, t, d = xs.shape
    tm = TOKEN_TILE
    p = w_in_bf.shape[1]
    seg = lambda bi, i: (bi, jnp.where(i >= n_ctx_tiles, 1, 0), 0, 0)
    return pl.pallas_call(
        functools.partial(_in_proj_kernel, bounds=bounds),
        grid=(b, t // tm),
        in_specs=[pl.BlockSpec((1, tm, d), lambda bi, i: (bi, i, 0)),
                  pl.BlockSpec((1, 1, 2, d), seg),
                  pl.BlockSpec((1, d), lambda bi, i: (0, 0)),
                  pl.BlockSpec((d, p), lambda bi, i: (0, 0))],
        out_specs=[pl.BlockSpec((1, tm, hi - lo), lambda bi, i: (bi, i, 0)) for lo, hi in bounds],
        out_shape=[jax.ShapeDtypeStruct((b, t, hi - lo), dt) for (lo, hi), dt in zip(bounds, dtypes)],
        compiler_params=_cparams("parallel", "parallel"),
        name="in_proj",
    )(xs, mod, norm_g.reshape(1, d), w_in_bf)


def _ret_kernel(q_ref, k_ref, v_ref, cos_ref, sin_ref, dq_ref, dk_ref, dmat_ref, gm_ref, bm_ref,
                o_ref, s_ref):
    i = pl.program_id(0)

    @pl.when(i == 0)
    def _():
        s_ref[...] = jnp.zeros_like(s_ref)

    nb, c, w = q_ref.shape
    cos = cos_ref[...]
    sin = sin_ref[...]
    lane = lax.broadcasted_iota(jnp.int32, (c, LANES), 1)
    first = (lane % 32) < 16

    def rope(u):
        parts = []
        for j in range(w // LANES):
            uj = u[:, j * LANES:(j + 1) * LANES]
            nxt = pltpu.roll(uj, LANES - 16, axis=1)
            prv = pltpu.roll(uj, 16, axis=1)
            parts.append(jnp.where(first, nxt, prv))
        return u * cos + jnp.concatenate(parts, axis=1) * sin

    lane_lo = lane < HEAD_DIM

    def stack(xw):
        return jnp.concatenate([jnp.where(lane_lo, xw, 0.0), jnp.where(lane_lo, 0.0, xw)], axis=0)

    n_pairs = w // LANES
    q = [rope(q_ref[b].astype(F32)) for b in range(nb)]
    k = [rope(k_ref[b].astype(F32)) for b in range(nb)]
    chains = [(b, j) for b in range(nb) for j in range(n_pairs)]
    pair = lambda x, j: x[:, j * LANES:(j + 1) * LANES]
    qw = [pair(q[b], j) for b, j in chains]
    kw = [pair(k[b], j) for b, j in chains]
    vw = [pair(v_ref[b], j) for b, j in chains]
    s = [s_ref[b * n_pairs + j] for b, j in chains]
    inter = [_dot((x * pair(dq_ref[...], j)).astype(BF16), st.astype(BF16)) for x, st, (b, j) in zip(qw, s, chains)]
    sc = [_dot_nt(x.astype(BF16), stack(y).astype(BF16)) * dmat_ref[j]
          for x, y, (b, j) in zip(qw, kw, chains)]
    intra = [_dot(x.astype(BF16), stack(y).astype(BF16)) for x, y in zip(sc, vw)]
    ktv = [_dot_tn((y * pair(dk_ref[...], j)).astype(BF16), z.astype(BF16))
           for y, z, (b, j) in zip(kw, vw, chains)]
    for n, (b, j) in enumerate(chains):
        o_ref[b, :, j * LANES:(j + 1) * LANES] = (inter[n] + intra[n]).astype(o_ref.dtype)
        s_ref[b * n_pairs + j] = gm_ref[j] * s[n] + bm_ref[...] * ktv[n]


def _retention(p_ret, cos_t, sin_t, tabs, n_ctx, rev):
    b, t, w4 = p_ret.shape
    w = w4 // 4
    c = RET_CHUNK
    n_tot = t // c
    n_pairs = w // LANES
    dq, dk, dmat, gm, bm = tabs
    tix = lambda i: _scan_chunk(i, n_ctx, n_tot, rev)
    col = lambda j: (lambda i: (0, tix(i), j))
    const = lambda a: pl.BlockSpec(a.shape, lambda i: (0,) * a.ndim)
    return pl.pallas_call(
        _ret_kernel,
        grid=(n_tot,),
        in_specs=[pl.BlockSpec((b, c, w), col(0)), pl.BlockSpec((b, c, w), col(1)), pl.BlockSpec((b, c, w), col(2)),
                  pl.BlockSpec((c, w), lambda i: (tix(i), 0)),
                  pl.BlockSpec((c, w), lambda i: (tix(i), 0)),
                  const(dq), const(dk), const(dmat), const(gm), const(bm)],
        out_specs=pl.BlockSpec((b, c, w), lambda i: (0, tix(i), 0)),
        out_shape=jax.ShapeDtypeStruct((b, t, w), BF16),
        scratch_shapes=[pltpu.VMEM((b * n_pairs, LANES, LANES), F32)],
        compiler_params=_cparams("arbitrary"),
        name="retention_rev" if rev else "retention_fwd",
    )(p_ret, p_ret, p_ret, cos_t, sin_t, dq, dk, dmat, gm, bm)


def _ret_tables(decay_logit, w, rev):
    n_heads = w // HEAD_DIM
    c = RET_CHUNK
    lg = jax.nn.log_sigmoid(decay_logit.astype(F32))
    t = jnp.arange(c, dtype=F32)
    p = (c - 1.0 - t) if rev else t
    rel = p[:, None] - p[None, :]
    scale = HEAD_DIM ** -0.5
    dmat = jnp.where(rel >= 0, jnp.exp(lg[:, None, None] * jnp.maximum(rel, 0.0)), 0.0) * scale
    dq = jnp.exp(lg[:, None] * (p + 1.0)) * scale
    dk = jnp.exp(lg[:, None] * (c - 1.0 - p))
    lanes = lambda a: jnp.repeat(a.T, HEAD_DIM, axis=1)
    n_pairs = n_heads // 2
    dmat_w = dmat.reshape(n_pairs, 2, c, c).transpose(0, 2, 1, 3).reshape(n_pairs, c, 2 * c)
    hid = jnp.arange(LANES) // HEAD_DIM
    bm = (hid[:, None] == hid[None, :]).astype(F32)
    gm = bm[None] * jnp.exp(lg * c).reshape(n_pairs, 2)[:, hid][:, :, None]
    return lanes(dq), lanes(dk), dmat_w, gm, bm


def _rope_tables(n_ctx_tok, seq, w):
    half = HEAD_DIM // 2
    quarter = half // 2
    inv_freq = ROPE_BASE ** (-jnp.arange(quarter, dtype=F32) / quarter)
    tok = jnp.arange(seq)
    rows = (tok // GRID_W).astype(F32)
    cols = (tok % GRID_W).astype(F32)
    o = jnp.arange(w) % HEAD_DIM
    pos = jnp.where(o[None, :] < half, rows[:, None], cols[:, None])
    ang = pos * inv_freq[o % quarter][None, :]
    sign = jnp.where((o % half) < quarter, -1.0, 1.0)[None, :]
    cos = jnp.concatenate([jnp.ones((n_ctx_tok, w), F32), jnp.cos(ang)], axis=0)
    sin = jnp.concatenate([jnp.zeros((n_ctx_tok, w), F32), jnp.sin(ang) * sign], axis=0)
    return cos, sin


def _rwkv_kernel(z_ref, mu_ref, vec_ref, wup_ref, aup_ref, e_ref, minc_ref, strict_ref, incl_ref,
                 y_ref, bon_ref, st_ref, zprev_ref, *, rev, n_ctx, w):
    i = pl.program_id(0)
    nb, c, zw = z_ref.shape
    n_pairs = w // LANES
    rows = nb * c

    @pl.when(i == 0)
    def _():
        st_ref[...] = jnp.zeros_like(st_ref)

    @pl.when((i == 0) | (i == n_ctx))
    def _():
        zprev_ref[...] = jnp.zeros_like(zprev_ref)

    w0, a0, k_k, k_a, r_k = (vec_ref[j:j + 1, :] for j in range(5))
    e2 = e_ref[...]
    e22 = jnp.concatenate([e2, e2], axis=0)
    strict = strict_ref[...] > 0.0
    incl = incl_ref[...] > 0.0
    lane_lo = lax.broadcasted_iota(jnp.int32, (c, LANES), 1) < HEAD_DIM
    head_r = lax.broadcasted_iota(jnp.int32, (LANES, LANES), 0) // HEAD_DIM
    head_c = lax.broadcasted_iota(jnp.int32, (LANES, LANES), 1) // HEAD_DIM
    diag = head_r == head_c
    last = 0 if rev else c - 1

    def head_sums(x, pieces):
        outs = []
        for j in range(n_pairs):
            xj = x[:, j * LANES:(j + 1) * LANES]
            if pieces == 1:
                outs.append(_dot(xj.astype(BF16), e2))
            else:
                hi = xj.astype(BF16)
                mid = (xj - hi.astype(F32)).astype(BF16)
                outs.append(_dot(jnp.concatenate([hi, mid], axis=1), e22))
        return jnp.concatenate(outs, axis=1)

    def stack(xw):
        return jnp.concatenate([jnp.where(lane_lo, xw, 0.0), jnp.where(lane_lo, 0.0, xw)], axis=0)

    chains = [(b, j) for b in range(nb) for j in range(n_pairs)]

    def win(x, ch):
        b, j = ch
        return x[b * c:(b + 1) * c, j * LANES:(j + 1) * LANES]

    z = z_ref[...].reshape(rows, zw)
    rin = lax.broadcasted_iota(jnp.int32, (rows, zw), 0) % c
    prev = jnp.concatenate([jnp.broadcast_to(zprev_ref[b, 0:1, :], (c, zw)) for b in range(nb)], axis=0)
    if rev:
        zs = jnp.where(rin == c - 1, prev, pltpu.roll(z, rows - 1, axis=0))
        for b in range(nb):
            zprev_ref[b, 0:1, :] = z[b * c:b * c + 1, :]
    else:
        zs = jnp.where(rin == 0, prev, pltpu.roll(z, 1, axis=0))
        for b in range(nb):
            zprev_ref[b, 0:1, :] = z[b * c + c - 1:b * c + c, :]
    zd = z + (zs - z) * mu_ref[...]
    r = zd[:, 0:w]
    k = zd[:, w:2 * w]
    v = zd[:, 2 * w:3 * w]
    lora = zd[:, 3 * w:3 * w + LANES]
    lane = lax.broadcasted_iota(jnp.int32, (rows, LANES), 1)
    lora = jnp.where(lane < DECAY_LORA, jnp.tanh(lora), lora)
    w_log = -_softplus(-(w0 + _dot_x3k(lora, wup_ref[...]))) - 0.5
    logw = -jnp.exp(w_log)
    a = _sigmoid(a0 + _dot_x3k(lora, aup_ref[...]))
    kk0 = k * k_k
    kk = kk0 / jnp.maximum(jnp.sqrt(head_sums(kk0 * kk0, 2)), 1e-12)
    k2 = k * (1.0 + (a - 1.0) * k_a)
    bon_ref[...] = (head_sums(r * k2 * r_k, 1) * v).reshape(nb, c, w).astype(bon_ref.dtype)

    cinc = jnp.concatenate(
        [_dot(minc_ref[...], jnp.concatenate(_split3(logw[b * c:(b + 1) * c]), axis=0)) for b in range(nb)], axis=0)
    e_inc = jnp.exp(cinc)
    e_neg = jnp.exp(-cinc)
    rt = r * e_inc
    kt = k2 * e_neg
    bt = kk * a * e_neg
    kkt = kk * jnp.exp(cinc - logw)

    st = [st_ref[b * n_pairs + j] for b, j in chains]
    lhs = [jnp.concatenate([win(kkt, ch), win(rt, ch)], axis=0).astype(BF16) for ch in chains]
    g = [_dot_nt(l, jnp.concatenate([stack(win(bt, ch)), stack(win(kt, ch))], axis=0).astype(BF16))
         for l, ch in zip(lhs, chains)]
    a_b = [jnp.where(strict, x[0:c, 0:2 * c], 0.0) for x in g]
    a_k = [jnp.where(strict, x[0:c, 2 * c:4 * c], 0.0).astype(BF16) for x in g]
    r_kb = [jnp.concatenate([jnp.where(incl, x[c:2 * c, 2 * c:4 * c], 0.0),
                             -jnp.where(incl, x[c:2 * c, 0:2 * c], 0.0)], axis=1).astype(BF16) for x in g]
    x0 = [_dot_nt(l, s.astype(BF16)) for l, s in zip(lhs, st)]
    v_sb = [stack(win(v, ch)).astype(BF16) for ch in chains]
    u = [x[0:c] + _dot(ak, vs) for x, ak, vs in zip(x0, a_k, v_sb)]

    pw = a_b
    steps, sign = 1, -1.0
    while 2 * steps < c:
        both = [_dot(p.astype(BF16), jnp.concatenate([stack(p), stack(x)], axis=1).astype(BF16))
                for p, x in zip(pw, u)]
        u = [x + sign * y[:, 2 * c:4 * c] for x, y in zip(u, both)]
        pw = [y[:, 0:2 * c] for y in both]
        steps, sign = 2 * steps, 1.0
    u = [x + sign * _dot(p.astype(BF16), stack(x).astype(BF16)) for x, p in zip(u, pw)]

    y = [x[c:2 * c] + _dot(rk, jnp.concatenate([vs, stack(uu).astype(BF16)], axis=0))
         for x, rk, vs, uu in zip(x0, r_kb, v_sb, u)]
    upd = [_dot_tn(jnp.concatenate([win(v, ch), x], axis=0).astype(BF16),
                   jnp.concatenate([win(kt, ch), -win(bt, ch)], axis=0).astype(BF16))
           for ch, x in zip(chains, u)]
    for n, (b, j) in enumerate(chains):
        w_end = e_inc[b * c + last:b * c + last + 1, j * LANES:(j + 1) * LANES]
        st_ref[b * n_pairs + j] = jnp.where(diag, (st[n] + upd[n]) * w_end, 0.0)
        y_ref[b, :, j * LANES:(j + 1) * LANES] = y[n].astype(y_ref.dtype)


def _rwkv(p_z, prm, n_ctx, rev):
    b, t, zw = p_z.shape
    w = (zw - DECAY_LORA - ICLR_LORA) // 3
    c = RWKV_CHUNK
    n_tot = t // c
    mu, vecs, wup, aup, e_bf, minc, strict, incl = prm
    tix = lambda i: _scan_chunk(i, n_ctx, n_tot, rev)
    const = lambda i: (0, 0)
    full = lambda a: pl.BlockSpec(a.shape, const)
    return pl.pallas_call(
        functools.partial(_rwkv_kernel, rev=rev, n_ctx=n_ctx, w=w),
        grid=(n_tot,),
        in_specs=[pl.BlockSpec((b, c, zw), lambda i: (0, tix(i), 0)),
                  full(mu), full(vecs), full(wup), full(aup), full(e_bf), full(minc), full(strict), full(incl)],
        out_specs=[pl.BlockSpec((b, c, w), lambda i: (0, tix(i), 0)),
                   pl.BlockSpec((b, c, w), lambda i: (0, tix(i), 0))],
        out_shape=[jax.ShapeDtypeStruct((b, t, w), BF16), jax.ShapeDtypeStruct((b, t, w), BF16)],
        scratch_shapes=[pltpu.VMEM((b * (w // LANES), LANES, LANES), F32), pltpu.VMEM((b, 8, zw), F32)],
        compiler_params=_cparams("arbitrary"),
        name="rwkv7_rev" if rev else "rwkv7_fwd",
    )(p_z, mu, vecs, wup, aup, e_bf, minc, strict, incl)


def _rwkv_params(mu, w0, w_up, a0, a_up, k_k, k_a, r_k, rev, n_batch):
    w = w0.shape[0]
    c = RWKV_CHUNK
    vecs = jnp.concatenate([jnp.stack([w0, a0, k_k, k_a, r_k]), jnp.zeros((3, w), F32)], axis=0)
    wup = jnp.concatenate([w_up, jnp.zeros((ICLR_LORA, w), F32)], axis=0)
    aup = jnp.concatenate([jnp.zeros((DECAY_LORA, w), F32), a_up], axis=0)
    hid = jnp.arange(LANES) // HEAD_DIM
    e_bf = (hid[:, None] == hid[None, :]).astype(BF16)
    t = jnp.arange(c)
    p = (c - 1 - t) if rev else t
    le = p[None, :] <= p[:, None]
    lt = p[None, :] < p[:, None]
    strict = jnp.tile(lt, (1, 2)).astype(F32)
    incl = jnp.tile(le, (1, 2)).astype(F32)
    minc = jnp.tile(le, (1, 3)).astype(BF16)
    return mu.reshape(1, -1), vecs, wup, aup, e_bf, minc, strict, incl


def _lru_kernel(x_ref, cw_ref, vec_ref, wa_ref, wx_ref, h_ref, hcar_ref, ucar_ref, *, rev, n_ctx):
    i = pl.program_id(0)
    nb, c, w = x_ref.shape
    rows = nb * c

    @pl.when(i == 0)
    def _():
        hcar_ref[...] = jnp.zeros_like(hcar_ref)

    @pl.when((i == 0) | (i == n_ctx))
    def _():
        ucar_ref[...] = jnp.zeros_like(ucar_ref)

    u0 = x_ref[...].astype(F32).reshape(rows, w)
    row = lax.broadcasted_iota(jnp.int32, (rows, w), 0) % c

    def per_batch(ref, j):
        return jnp.concatenate([jnp.broadcast_to(ref[b, j:j + 1, :], (c, w)) for b in range(nb)], axis=0)

    def shifted(x, s, carry, fill):
        if rev:
            rolled = pltpu.roll(x, rows - s, axis=0)
            edge = row >= c - s
        else:
            rolled = pltpu.roll(x, s, axis=0)
            edge = row < s
        if carry is None:
            return jnp.where(edge, fill, rolled)
        return jnp.where(edge, carry, rolled)

    conv = vec_ref[0:1, :] + cw_ref[LRU_CONV - 1:LRU_CONV, :] * u0
    for m in range(1, LRU_CONV):
        car = jnp.zeros((rows, w), F32)
        for qpos in range(m):
            r_idx = (c - 1 - qpos) if rev else qpos
            car = jnp.where(row == r_idx, per_batch(ucar_ref, m - qpos - 1), car)
        conv = conv + cw_ref[LRU_CONV - 1 - m:LRU_CONV - m, :] * shifted(u0, m, car, None)
    for m in range(1, LRU_CONV):
        r_idx = (m - 1) if rev else (c - m)
        for b in range(nb):
            ucar_ref[b, m - 1:m, :] = u0[b * c + r_idx:b * c + r_idx + 1, :]

    cb = conv.astype(BF16)
    r = _sigmoid(_dot(cb, wa_ref[...]) + vec_ref[1:2, :])
    ig = _sigmoid(_dot(cb, wx_ref[...]) + vec_ref[2:3, :])
    log_a = -LRU_C * r * vec_ref[3:4, :]
    a = jnp.exp(log_a)
    bb = jnp.sqrt(1.0 - jnp.exp(2.0 * log_a)) * (ig * conv)

    s = 1
    while s < c:
        bb = bb + a * shifted(bb, s, None, 0.0)
        a = a * shifted(a, s, None, 1.0)
        s *= 2
    h = bb + a * per_batch(hcar_ref, 0)
    h_ref[...] = h.reshape(nb, c, w).astype(h_ref.dtype)
    last = 0 if rev else c - 1
    for b in range(nb):
        hcar_ref[b, 0:1, :] = h[b * c + last:b * c + last + 1, :]


def _lru(p_lru, prm, n_ctx, rev):
    b, t, w2 = p_lru.shape
    w = w2 // 2
    c = LRU_CHUNK
    n_tot = t // c
    cw, vecs, wa, wx = prm
    tix = lambda i: _scan_chunk(i, n_ctx, n_tot, rev)
    const = lambda i: (0, 0)
    return pl.pallas_call(
        functools.partial(_lru_kernel, rev=rev, n_ctx=n_ctx),
        grid=(n_tot,),
        in_specs=[pl.BlockSpec((b, c, w), lambda i: (0, tix(i), 0)),
                  pl.BlockSpec(cw.shape, const), pl.BlockSpec(vecs.shape, const),
                  pl.BlockSpec(wa.shape, const), pl.BlockSpec(wx.shape, const)],
        out_specs=pl.BlockSpec((b, c, w), lambda i: (0, tix(i), 0)),
        out_shape=jax.ShapeDtypeStruct((b, t, w), BF16),
        scratch_shapes=[pltpu.VMEM((b, 8, w), F32), pltpu.VMEM((b, 8, w), F32)],
        compiler_params=_cparams("arbitrary"),
        name="rglru_rev" if rev else "rglru_fwd",
    )(p_lru, cw, vecs, wa, wx)


def _lru_params(conv_w, conv_b, wa, ba, wx, bx, lam):
    w = conv_b.shape[0]
    cw = jnp.concatenate([conv_w, jnp.zeros((8 - LRU_CONV, w), F32)], axis=0)
    vecs = jnp.concatenate([jnp.stack([conv_b, ba, bx, jax.nn.softplus(-lam)]), jnp.zeros((4, w), F32)], axis=0)
    return cw, vecs, jax.scipy.linalg.block_diag(*wa).astype(BF16), jax.scipy.linalg.block_diag(*wx).astype(BF16)


def _head_norm(y, e2, gain, bias, eps):
    inv = 1.0 / HEAD_DIM

    def head_sums(x):
        xb = x.astype(BF16)
        return jnp.concatenate([_dot(xb[:, j:j + LANES], e2) for j in range(0, x.shape[1], LANES)], axis=1)

    yc = y - head_sums(y) * inv
    var = head_sums(yc * yc) * inv
    return yc * lax.rsqrt(var + eps) * gain + bias


def _mix_out_kernel(x_ref, mod_ref, of_ref, ob_ref, g_ref, yf_ref, yb_ref, bf_ref, bb_ref, gd_ref,
                    hf_ref, hb_ref, lg_ref, gn_ref, e_ref, gup_ref, wout_ref, n2g_ref, wr_ref, br_ref, tri_ref,
                    xo_ref, h2_ref, idx_ref, gate_ref, rank_ref, cnt_ref, base_ref, *, w_ret, w_rw):
    @pl.when((pl.program_id(0) == 0) & (pl.program_id(1) == 0))
    def _():
        base_ref[...] = jnp.zeros_like(base_ref)

    e_bf = e_ref[...]
    f32 = lambda ref: ref[0].astype(F32)
    g = f32(g_ref)
    ret = _head_norm(f32(of_ref) + f32(ob_ref), e_bf, gn_ref[0:1, :], gn_ref[1:2, :], RET_GN_EPS)
    ret = ret * (g * _sigmoid(g))
    gate = _dot(_sigmoid(f32(gd_ref)).astype(BF16), gup_ref[...])
    rw = _head_norm(f32(yf_ref) + f32(yb_ref), e_bf, gn_ref[2:3, :], gn_ref[3:4, :], RWKV_GN_EPS)
    rw = (rw + f32(bf_ref) + f32(bb_ref)) * gate
    lg = f32(lg_ref)
    gelu = 0.5 * lg * (1.0 + jnp.tanh(0.7978845608028654 * (lg + 0.044715 * (lg * lg * lg))))
    lru = (f32(hf_ref) + f32(hb_ref)) * gelu
    mix = (_dot(ret.astype(BF16), wout_ref[0:w_ret, :])
           + _dot(rw.astype(BF16), wout_ref[w_ret:w_ret + w_rw, :])
           + _dot(lru.astype(BF16), wout_ref[w_ret + w_rw:, :]))
    x = x_ref[0] + mod_ref[0, 0, 0:1, :] * mix
    xo_ref[0] = x
    ms = jnp.mean(x * x, axis=-1, keepdims=True)
    h2 = x * lax.rsqrt(ms + NORM_EPS) * n2g_ref[...]
    h2 = h2 * (1.0 + mod_ref[0, 0, 2:3, :]) + mod_ref[0, 0, 1:2, :]
    h2_ref[0] = h2.astype(BF16)
    logits = _dot_x3(h2, wr_ref[...]) + br_ref[...]
    idx_o, gate_o, rank_o = _route_tile(logits, tri_ref[...], base_ref)
    idx_ref[0] = idx_o
    gate_ref[0] = gate_o
    rank_ref[0] = rank_o
    cnt_ref[...] = base_ref[...].astype(jnp.int32)


def _mix_out(xs, mod, o_f, o_b, p_ret, y_f, y_b, bon_f, bon_b, p_gd, h_f, h_b, p_lru,
             gn, e_bf, g_up_bf, w_out_bf, norm2_g, w_router, b_router, n_ctx_tiles):
    b, t, d = xs.shape
    tm = TOKEN_TILE
    w_ret, w_rw, w_lru = o_f.shape[2], y_f.shape[2], h_f.shape[2]
    ne = w_router.shape[1]
    tok = lambda wd, j=0: pl.BlockSpec((1, tm, wd), lambda bi, i: (bi, i, j))
    const = lambda a: pl.BlockSpec(a.shape, lambda bi, i: (0,) * a.ndim)
    seg = lambda bi, i: (bi, jnp.where(i >= n_ctx_tiles, 1, 0), 0, 0)
    n2g = norm2_g.reshape(1, d)
    br = b_router.reshape(1, ne)
    tt = jnp.arange(tm)
    tri = (tt[None, :] < tt[:, None]).astype(BF16)
    return pl.pallas_call(
        functools.partial(_mix_out_kernel, w_ret=w_ret, w_rw=w_rw),
        grid=(b, t // tm),
        in_specs=[tok(d), pl.BlockSpec((1, 1, 3, d), seg),
                  tok(w_ret), tok(w_ret), tok(w_ret, 3),
                  tok(w_rw), tok(w_rw), tok(w_rw), tok(w_rw), tok(GATE_LORA),
                  tok(w_lru), tok(w_lru), tok(w_lru, 1),
                  const(gn), const(e_bf), const(g_up_bf), const(w_out_bf), const(n2g), const(w_router), const(br),
                  const(tri)],
        out_specs=[tok(d), tok(d), tok(LANES), tok(LANES), tok(LANES),
                   pl.BlockSpec((8, ne), lambda bi, i: (0, 0))],
        out_shape=[jax.ShapeDtypeStruct((b, t, d), F32), jax.ShapeDtypeStruct((b, t, d), BF16),
                   jax.ShapeDtypeStruct((b, t, LANES), jnp.int32), jax.ShapeDtypeStruct((b, t, LANES), F32),
                   jax.ShapeDtypeStruct((b, t, LANES), jnp.int32), jax.ShapeDtypeStruct((8, ne), jnp.int32)],
        scratch_shapes=[pltpu.VMEM((8, ne), F32)],
        compiler_params=_cparams("arbitrary", "arbitrary"),
        name="mix_out",
    )(xs, mod, o_f, o_b, p_ret, y_f, y_b, bon_f, bon_b, p_gd, h_f, h_b, p_lru,
      gn, e_bf, g_up_bf, w_out_bf, n2g, w_router, br, tri)


def _moe_kernel(be_ref, first_ref, nu_ref, *refs, blk0, chained):
    x_ref, w1_ref, b1_ref, w2_ref, b2_ref = refs[1:6] if chained else refs[0:5]
    y_ref, w1b_ref, w2b_ref = refs[-3:]
    i = pl.program_id(0)
    blk = i + blk0
    de = w2_ref.shape[1]

    @pl.when((first_ref[blk] == 1) | (i == 0))
    def _():
        w1b_ref[...] = w1_ref[0].astype(BF16)
        w2b_ref[...] = w2_ref[0].astype(BF16)

    @pl.when(blk < nu_ref[0])
    def _():
        gu = _dot(x_ref[...], w1b_ref[...]) + b1_ref[0]
        glu = jnp.minimum(gu[:, :de], SWIGLU_LIMIT)
        lin = jnp.clip(gu[:, de:], -SWIGLU_LIMIT, SWIGLU_LIMIT)
        act = glu * _sigmoid(SWIGLU_ALPHA * glu) * (lin + 1.0)
        y_ref[...] = (_dot(act.astype(BF16), w2b_ref[...]) + b2_ref[0]).astype(y_ref.dtype)

    @pl.when(blk >= nu_ref[0])
    def _():
        y_ref[...] = jnp.zeros_like(y_ref)


def _moe_ffn(hb, block_e, first, n_used, w1, b1, w2, b2, layer, blk0, n_slots, y_prev=None):
    rows, d = hb.shape
    tm = MOE_TILE
    nl, ne, _, d2 = w1.shape
    de = w2.shape[2]
    chained = y_prev is not None
    wsel = lambda i, be, fi, nu: (layer, be[i + blk0], 0, 0)
    in_specs = [pl.BlockSpec((tm, d), lambda i, be, fi, nu: (i, 0)),
                pl.BlockSpec((None, 1, d, d2), wsel),
                pl.BlockSpec((None, 1, 1, d2), wsel),
                pl.BlockSpec((None, 1, de, d), wsel),
                pl.BlockSpec((None, 1, 1, d), wsel)]
    args = [hb, w1, b1.reshape(nl, ne, 1, d2), w2, b2.reshape(nl, ne, 1, d)]
    if chained:
        in_specs = [pl.BlockSpec(memory_space=pl.ANY)] + in_specs
        args = [y_prev] + args
    return pl.pallas_call(
        functools.partial(_moe_kernel, blk0=blk0, chained=chained),
        grid_spec=pltpu.PrefetchScalarGridSpec(
            num_scalar_prefetch=3,
            grid=(rows // tm,),
            in_specs=in_specs,
            out_specs=pl.BlockSpec((tm, d), lambda i, be, fi, nu: (i + blk0, 0)),
            scratch_shapes=[pltpu.VMEM((d, d2), BF16), pltpu.VMEM((de, d), BF16)],
        ),
        out_shape=jax.ShapeDtypeStruct((n_slots, d), BF16),
        input_output_aliases={3: 0} if chained else {},
        compiler_params=_cparams("arbitrary"),
        name="moe_ffn",
    )(block_e, first, n_used, *args)


def _route_tile(lg, tri, base_ref):
    tr, ne = lg.shape
    lane = lax.broadcasted_iota(jnp.int32, (tr, ne), 1).astype(F32)
    out_lane = lax.broadcasted_iota(jnp.int32, (tr, LANES), 1)
    vals = lg
    sel = jnp.zeros((tr, ne), F32)
    picks, tops = [], []
    for _ in range(TOP_K):
        m = jnp.max(vals, axis=-1, keepdims=True)
        ix = jnp.min(jnp.where(vals == m, lane, float(ne)), axis=-1, keepdims=True)
        hit = lane == ix
        sel = jnp.where(hit, 1.0, sel)
        vals = jnp.where(hit, -jnp.inf, vals)
        picks.append(ix)
        tops.append(m)
    ex = [jnp.exp(t - tops[0]) for t in tops]
    den = ex[0] + ex[1] + ex[2] + ex[3]
    before = _dot(tri, sel.astype(BF16)) + base_ref[0:1, :]
    idx_o = jnp.zeros((tr, LANES), F32)
    gate_o = jnp.zeros((tr, LANES), F32)
    rank_o = jnp.zeros((tr, LANES), F32)
    for k in range(TOP_K):
        rk = jnp.sum(jnp.where(lane == picks[k], before, 0.0), axis=-1, keepdims=True)
        idx_o = jnp.where(out_lane == k, picks[k], idx_o)
        gate_o = jnp.where(out_lane == k, ex[k] / den, gate_o)
        rank_o = jnp.where(out_lane == k, rk, rank_o)
    total = base_ref[0:1, :] + jnp.sum(sel, axis=0, keepdims=True)
    base_ref[...] = jnp.broadcast_to(total, base_ref.shape)
    return idx_o.astype(jnp.int32), gate_o, rank_o.astype(jnp.int32)


def _route_meta(idx, rank, counts):
    n_tok = idx.shape[0]
    ne = counts.shape[0]
    tm = MOE_TILE
    n_assign = n_tok * TOP_K
    padded = (counts + tm - 1) // tm * tm
    pend = jnp.cumsum(padded)
    pstart = pend - padded
    start = jnp.cumsum(counts) - counts
    eid = jnp.arange(ne, dtype=jnp.int32)
    slot = jnp.sum(jnp.where(idx[..., None] == eid, pstart, 0), axis=-1).astype(jnp.int32) + rank
    n_blocks = (n_assign + ne * (tm - 1) + tm - 1) // tm
    blk_start = jnp.arange(n_blocks, dtype=jnp.int32) * tm
    block_e = jnp.minimum(jnp.sum(pend[None, :] <= blk_start[:, None], axis=1), ne - 1).astype(jnp.int32)
    first = jnp.concatenate([jnp.ones((1,), jnp.int32), (block_e[1:] != block_e[:-1]).astype(jnp.int32)])
    n_used = (pend[-1] // tm).astype(jnp.int32).reshape(1)
    _, order = lax.sort_key_val(slot.reshape(-1), jnp.arange(n_assign, dtype=jnp.int32))
    off = jnp.arange(n_blocks * tm, dtype=jnp.int32) - jnp.repeat(pstart[block_e], tm)
    valid = off < jnp.repeat(counts[block_e], tm)
    pos = jnp.clip(jnp.repeat(start[block_e], tm) + off, 0, n_assign - 1)
    spread = jnp.arange(n_blocks * tm, dtype=jnp.int32) % n_tok
    slot_tok = jnp.where(valid, order[pos] // TOP_K, spread).astype(jnp.int32)
    return slot, slot_tok, block_e, first, n_used


def _combine_kernel(x_ref, mod_ref, y_ref, gate_ref, g_ref, *rest, final):
    o_ref = rest[-1]
    gate = gate_ref[0]
    y = y_ref[0, 0].astype(F32) * gate[:, 0:1]
    for k in range(1, TOP_K):
        y = y + y_ref[k, 0].astype(F32) * gate[:, k:k + 1]
    x = x_ref[0] + mod_ref[0, 0, 0:1, :] * y
    if final:
        ms = jnp.mean(x * x, axis=-1, keepdims=True)
        x = x * lax.rsqrt(ms + NORM_EPS) * g_ref[...]
    o_ref[0] = x


def _combine(xs, mod, yg, gates, final_g, n_ctx_tiles, final, b0, prev=None):
    b, t, d = xs.shape
    nbh = yg.shape[1]
    tm = TOKEN_TILE
    skip = n_ctx_tiles if final else 0
    seg = lambda bi, i: (bi + b0, jnp.where(i + skip >= n_ctx_tiles, 1, 0), 0, 0)
    in_specs = [pl.BlockSpec((1, tm, d), lambda bi, i: (bi + b0, i + skip, 0)),
                pl.BlockSpec((1, 1, 1, d), seg),
                pl.BlockSpec((TOP_K, 1, tm, d), lambda bi, i: (0, bi, i + skip, 0)),
                pl.BlockSpec((1, tm, LANES), lambda bi, i: (bi + b0, i + skip, 0)),
                pl.BlockSpec((1, d), lambda bi, i: (0, 0))]
    args = [xs, mod, yg, gates, final_g.reshape(1, d)]
    aliases = {} if final else {0: 0}
    if prev is not None:
        in_specs.append(pl.BlockSpec(memory_space=pl.ANY))
        args.append(prev)
        aliases = {5: 0}
    return pl.pallas_call(
        functools.partial(_combine_kernel, final=final),
        grid=(nbh, t // tm - skip),
        in_specs=in_specs,
        out_specs=pl.BlockSpec((1, tm, d), lambda bi, i: (bi + b0, i, 0)),
        out_shape=jax.ShapeDtypeStruct((b, t - skip * tm, d), F32),
        input_output_aliases=aliases,
        compiler_params=_cparams("parallel", "parallel"),
        name="combine_final" if final else "combine",
    )(*args)


def kernel(x, c, ctx, c_ctx, w_mod, b_mod, norm1_g, norm2_g, w_in, w_out, ret_decay_logit, ret_gn_g, ret_gn_b, rwkv_mu, rwkv_w0, rwkv_w_up, rwkv_a0, rwkv_a_up, rwkv_k_k, rwkv_k_a, rwkv_g_up, rwkv_r_k, rwkv_gn_g, rwkv_gn_b, lru_conv_w, lru_conv_b, lru_wa, lru_ba, lru_wx, lru_bx, lru_lambda, moe_w_router, moe_b_router, moe_w1, moe_b1, moe_w2, moe_b2, final_norm_g):
    bsz, seq, dm = x.shape
    n_ctx_tok = ctx.shape[1]
    depth = w_in.shape[0]
    n_experts = moe_w_router.shape[2]
    w_ret = 3 * dm // 8
    w_rw = 3 * dm // 8
    w_lru = dm - w_ret - w_rw
    zw = 3 * w_rw + DECAY_LORA + ICLR_LORA
    sizes = (4 * w_ret, zw, GATE_LORA, 2 * w_lru)
    bounds, off = [], 0
    for s in sizes:
        bounds.append((off, off + s))
        off += s
    bounds = tuple(bounds)
    assert off == w_in.shape[2]
    assert n_ctx_tok % TOKEN_TILE == 0 and seq % TOKEN_TILE == 0 and seq % GRID_W == 0
    t_all = n_ctx_tok + seq
    n_ctx_tiles = n_ctx_tok // TOKEN_TILE

    xs = jnp.concatenate([ctx, x], axis=1)
    cos_t, sin_t = _rope_tables(n_ctx_tok, seq, w_ret)
    hid = jnp.arange(LANES) // HEAD_DIM
    e_bf = (hid[:, None] == hid[None, :]).astype(BF16)
    cond =jnp.concatenate([c, c_ctx[None, :], jnp.zeros((8 - (bsz + 1) % 8, dm), F32)], axis=0)

    for l in range(depth):
        last = l == depth - 1
        mod = _modulation(cond, w_mod, b_mod, l)
        mod_l = mod[:bsz].reshape(bsz, 6, dm)
        mod_c = jnp.broadcast_to(mod[bsz].reshape(1, 6, dm), (bsz, 6, dm))
        modsel = jnp.stack([mod_c, mod_l], axis=1)

        p_ret, p_z, p_gd, p_lru = _in_proj(xs, modsel[:, :, 0:2], norm1_g[l], w_in[l].astype(BF16), bounds,
                                           (BF16, F32, BF16, BF16), n_ctx_tiles)

        ret_o, rw_y, rw_bon, lru_h = [], [], [], []
        for d in range(2):
            rev = d == 1
            ret_o.append(_retention(p_ret, cos_t, sin_t, _ret_tables(ret_decay_logit[l, d], w_ret, rev),
                                    n_ctx_tok // RET_CHUNK, rev))
            prm = _rwkv_params(rwkv_mu[l, d], rwkv_w0[l, d], rwkv_w_up[l, d], rwkv_a0[l, d], rwkv_a_up[l, d],
                               rwkv_k_k[l, d], rwkv_k_a[l, d], rwkv_r_k[l], rev, bsz)
            y, bon = _rwkv(p_z, prm, n_ctx_tok // RWKV_CHUNK, rev)
            rw_y.append(y)
            rw_bon.append(bon)
            lru_h.append(_lru(p_lru, _lru_params(lru_conv_w[l, d], lru_conv_b[l, d], lru_wa[l, d], lru_ba[l, d],
                                                 lru_wx[l, d], lru_bx[l, d], lru_lambda[l, d]),
                              n_ctx_tok // LRU_CHUNK, rev))

        gn = jnp.concatenate([jnp.stack([ret_gn_g[l], ret_gn_b[l], rwkv_gn_g[l], rwkv_gn_b[l]]),
                              jnp.zeros((4, w_ret), F32)], axis=0)
        xs, h2, idx, gates, rank, counts = _mix_out(
            xs, modsel[:, :, 2:5], ret_o[0], ret_o[1], p_ret, rw_y[0], rw_y[1],
            rw_bon[0], rw_bon[1], p_gd, lru_h[0], lru_h[1], p_lru,
            gn, e_bf, rwkv_g_up[l].astype(BF16), w_out[l].astype(BF16), norm2_g[l],
            moe_w_router[l], moe_b_router[l], n_ctx_tiles)

        n_tok = bsz * t_all
        slot, slot_tok, block_e, first, n_used = _route_meta(
            idx.reshape(n_tok, LANES)[:, :TOP_K], rank.reshape(n_tok, LANES)[:, :TOP_K], counts[0])
        n_slots = slot_tok.shape[0]
        row_a = (n_slots // MOE_TILE + 1) // 2 * MOE_TILE
        h2f = h2.reshape(n_tok, dm)
        y_sorted = None
        for lo, hi in ((0, row_a), (row_a, n_slots)):
            y_sorted = _moe_ffn(h2f[slot_tok[lo:hi]], block_e, first, n_used, moe_w1, moe_b1, moe_w2, moe_b2, l,
                                lo // MOE_TILE, n_slots, y_prev=y_sorted)
        bh = (bsz + 1) // 2
        slot_b = slot.reshape(bsz, t_all, TOP_K)
        out = None
        for lo, hi in ((0, bh), (bh, bsz))[:2 if bsz > bh else 1]:
            yg = y_sorted[jnp.moveaxis(slot_b[lo:hi], 2, 0)]
            out = _combine(xs, modsel[:, :, 5:6], yg, gates, final_norm_g, n_ctx_tiles, last, lo,
                           prev=out if last else None)
            xs = xs if last else out
        xs = out
    return xs
```

```python
import functools

import jax
import jax.numpy as jnp
from jax import lax
from jax.experimental import pallas as pl
from jax.experimental.pallas import tpu as pltpu

F32 = jnp.float32
BF16 = jnp.bfloat16

HEAD_DIM = 64
NORM_EPS = 1e-6
RET_GN_EPS = 1e-5
RWKV_GN_EPS = 64e-5
ROPE_BASE = 10000.0
GRID_W = 64
LRU_CONV = 4
LRU_C = 8.0
TOP_K = 4
SWIGLU_LIMIT = 7.0
SWIGLU_ALPHA = 1.702
DECAY_LORA = 64
ICLR_LORA = 64
GATE_LORA = 128

LANES = 128
TOKEN_TILE = 256
RET_CHUNK = 128
RWKV_CHUNK = 64
LRU_CHUNK = 128
MOE_TILE = 512
VMEM_LIMIT = 56 * 1024 * 1024


def _cparams(*sem):
    return pltpu.CompilerParams(dimension_semantics=sem, vmem_limit_bytes=VMEM_LIMIT)


def _scan_chunk(i, n_ctx, n_tot, rev):
    if not rev:
        return i
    return jnp.where(i < n_ctx, n_ctx - 1 - i, n_tot + n_ctx - 1 - i)


def _split3(a):
    hi = a.astype(BF16)
    r1 = a - hi.astype(F32)
    mid = r1.astype(BF16)
    lo = (r1 - mid.astype(F32)).astype(BF16)
    return hi, mid, lo


def _dot(a, b):
    return jnp.dot(a, b, preferred_element_type=F32)


def _dot_nt(a, b):
    return lax.dot_general(a, b, (((1,), (1,)), ((), ())), preferred_element_type=F32)


def _dot_tn(a, b):
    return lax.dot_general(a, b, (((0,), (0,)), ((), ())), preferred_element_type=F32)


def _dot_exact_rhs(a, b_bf):
    hi, mid, lo = _split3(a)
    return _dot(hi, b_bf) + _dot(mid, b_bf) + _dot(lo, b_bf)


def _dot_exact_lhs(a_bf, b):
    hi, mid, lo = _split3(b)
    return _dot(a_bf, hi) + _dot(a_bf, mid) + _dot(a_bf, lo)


def _dot_x3(a, b):
    a_hi = a.astype(BF16)
    a_lo = (a - a_hi.astype(F32)).astype(BF16)
    b_hi = b.astype(BF16)
    b_lo = (b - b_hi.astype(F32)).astype(BF16)
    return _dot(a_hi, b_hi) + _dot(a_lo, b_hi) + _dot(a_hi, b_lo)


def _dot_x3k(a, b):
    a_hi = a.astype(BF16)
    a_lo = (a - a_hi.astype(F32)).astype(BF16)
    b_hi = b.astype(BF16)
    b_lo = (b - b_hi.astype(F32)).astype(BF16)
    return (_dot(jnp.concatenate([a_hi, a_lo], axis=1), jnp.concatenate([b_hi, b_hi], axis=0))
            + _dot(a_hi, b_lo))


def _sigmoid(x):
    return 1.0 / (1.0 + jnp.exp(-x))


def _softplus(x):
    return jnp.maximum(x, 0.0) + jnp.log(1.0 + jnp.exp(-jnp.abs(x)))


def _mod_kernel(c_ref, w_ref, b_ref, o_ref):
    c = c_ref[...]
    s = c * _sigmoid(c)
    o_ref[...] = _dot_x3(s, w_ref[...]) + b_ref[...]


def _modulation(cond, w_mod, b_mod, layer):
    r, d = cond.shape
    nl, _, n = w_mod.shape
    tn = d
    return pl.pallas_call(
        _mod_kernel,
        grid=(n // tn,),
        in_specs=[pl.BlockSpec((r, d), lambda j: (0, 0)),
                  pl.BlockSpec((None, d, tn), lambda j: (layer, 0, j)),
                  pl.BlockSpec((None, 1, tn), lambda j: (layer, 0, j))],
        out_specs=pl.BlockSpec((r, tn), lambda j: (0, j)),
        out_shape=jax.ShapeDtypeStruct((r, n), F32),
        compiler_params=_cparams("arbitrary"),
        name="modulation",
    )(cond, w_mod, b_mod.reshape(nl, 1, n))


def _in_proj_kernel(x_ref, mod_ref, g_ref, w_ref, *o_refs, bounds):
    x = x_ref[0]
    ms = jnp.mean(x * x, axis=-1, keepdims=True)
    h = x * lax.rsqrt(ms + NORM_EPS) * g_ref[...]
    h = h * (1.0 + mod_ref[0, 0, 1:2, :]) + mod_ref[0, 0, 0:1, :]
    hb = h.astype(BF16)
    for o_ref, (lo, hi) in zip(o_refs, bounds):
        o_ref[0] = _dot(hb, w_ref[:, lo:hi]).astype(o_ref.dtype)


def _in_proj(xs, mod, norm_g, w_in_bf, bounds, dtypes, n_ctx_tiles):
    b, t, d = xs.shape
    tm = TOKEN_TILE
    p = w_in_bf.shape[1]
    seg = lambda bi, i: (bi, jnp.where(i >= n_ctx_tiles, 1, 0), 0, 0)
    return pl.pallas_call(
        functools.partial(_in_proj_kernel, bounds=bounds),
        grid=(b, t // tm),
        in_specs=[pl.BlockSpec((1, tm, d), lambda bi, i: (bi, i, 0)),
                  pl.BlockSpec((1, 1, 2, d), seg),
                  pl.BlockSpec((1, d), lambda bi, i: (0, 0)),
                  pl.BlockSpec((d, p), lambda bi, i: (0, 0))],
        out_specs=[pl.BlockSpec((1, tm, hi - lo), lambda bi, i: (bi, i, 0)) for lo, hi in bounds],
        out_shape=[jax.ShapeDtypeStruct((b, t, hi - lo), dt) for (lo, hi), dt in zip(bounds, dtypes)],
        compiler_params=_cparams("parallel", "parallel"),
        name="in_proj",
    )(xs, mod, norm_g.reshape(1, d), w_in_bf)


def _ret_kernel(q_ref, k_ref, v_ref, cos_ref, sin_ref, dq_ref, dk_ref, dmat_ref, gm_ref, bm_ref,
                o_ref, s_ref):
    i = pl.program_id(0)

    @pl.when(i == 0)
    def _():
        s_ref[...] = jnp.zeros_like(s_ref)

    nb, c, w = q_ref.shape
    cos = cos_ref[...]
    sin = sin_ref[...]
    lane = lax.broadcasted_iota(jnp.int32, (c, LANES), 1)
    first = (lane % 32) < 16

    def rope(u):
        parts = []
        for j in range(w // LANES):
            uj = u[:, j * LANES:(j + 1) * LANES]
            nxt = pltpu.roll(uj, LANES - 16, axis=1)
            prv = pltpu.roll(uj, 16, axis=1)
            parts.append(jnp.where(first, nxt, prv))
        return u * cos + jnp.concatenate(parts, axis=1) * sin

    lane_lo = lane < HEAD_DIM

    def stack(xw):
        return jnp.concatenate([jnp.where(lane_lo, xw, 0.0), jnp.where(lane_lo, 0.0, xw)], axis=0)

    n_pairs = w // LANES
    q = [rope(q_ref[b].astype(F32)) for b in range(nb)]
    k = [rope(k_ref[b].astype(F32)) for b in range(nb)]
    chains = [(b, j) for b in range(nb) for j in range(n_pairs)]
    pair = lambda x, j: x[:, j * LANES:(j + 1) * LANES]
    qw = [pair(q[b], j) for b, j in chains]
    kw = [pair(k[b], j) for b, j in chains]
    vw = [pair(v_ref[b], j) for b, j in chains]
    s = [s_ref[b * n_pairs + j] for b, j in chains]
    inter = [_dot((x * pair(dq_ref[...], j)).astype(BF16), st.astype(BF16)) for x, st, (b, j) in zip(qw, s, chains)]
    sc = [_dot_nt(x.astype(BF16), stack(y).astype(BF16)) * dmat_ref[j]
          for x, y, (b, j) in zip(qw, kw, chains)]
    intra = [_dot(x.astype(BF16), stack(y).astype(BF16)) for x, y in zip(sc, vw)]
    ktv = [_dot_tn((y * pair(dk_ref[...], j)).astype(BF16), z.astype(BF16))
           for y, z, (b, j) in zip(kw, vw, chains)]
    for n, (b, j) in enumerate(chains):
        o_ref[b, :, j * LANES:(j + 1) * LANES] = (inter[n] + intra[n]).astype(o_ref.dtype)
        s_ref[b * n_pairs + j] = gm_ref[j] * s[n] + bm_ref[...] * ktv[n]


def _retention(p_ret, cos_t, sin_t, tabs, n_ctx, rev):
    b, t, w4 = p_ret.shape
    w = w4 // 4
    c = RET_CHUNK
    n_tot = t // c
    n_pairs = w // LANES
    dq, dk, dmat, gm, bm = tabs
    tix = lambda i: _scan_chunk(i, n_ctx, n_tot, rev)
    col = lambda j: (lambda i: (0, tix(i), j))
    const = lambda a: pl.BlockSpec(a.shape, lambda i: (0,) * a.ndim)
    return pl.pallas_call(
        _ret_kernel,
        grid=(n_tot,),
        in_specs=[pl.BlockSpec((b, c, w), col(0)), pl.BlockSpec((b, c, w), col(1)), pl.BlockSpec((b, c, w), col(2)),
                  pl.BlockSpec((c, w), lambda i: (tix(i), 0)),
                  pl.BlockSpec((c, w), lambda i: (tix(i), 0)),
                  const(dq), const(dk), const(dmat), const(gm), const(bm)],
        out_specs=pl.BlockSpec((b, c, w), lambda i: (0, tix(i), 0)),
        out_shape=jax.ShapeDtypeStruct((b, t, w), BF16),
        scratch_shapes=[pltpu.VMEM((b * n_pairs, LANES, LANES), F32)],
        compiler_params=_cparams("arbitrary"),
        name="retention_rev" if rev else "retention_fwd",
    )(p_ret, p_ret, p_ret, cos_t, sin_t, dq, dk, dmat, gm, bm)


def _ret_tables(decay_logit, w, rev):
    n_heads = w // HEAD_DIM
    c = RET_CHUNK
    lg = jax.nn.log_sigmoid(decay_logit.astype(F32))
    t = jnp.arange(c, dtype=F32)
    p = (c - 1.0 - t) if rev else t
    rel = p[:, None] - p[None, :]
    scale = HEAD_DIM ** -0.5
    dmat = jnp.where(rel >= 0, jnp.exp(lg[:, None, None] * jnp.maximum(rel, 0.0)), 0.0) * scale
    dq = jnp.exp(lg[:, None] * (p + 1.0)) * scale
    dk = jnp.exp(lg[:, None] * (c - 1.0 - p))
    lanes = lambda a: jnp.repeat(a.T, HEAD_DIM, axis=1)
    n_pairs = n_heads // 2
    dmat_w = dmat.reshape(n_pairs, 2, c, c).transpose(0, 2, 1, 3).reshape(n_pairs, c, 2 * c)
    hid = jnp.arange(LANES) // HEAD_DIM
    bm = (hid[:, None] == hid[None, :]).astype(F32)
    gm = bm[None] * jnp.exp(lg * c).reshape(n_pairs, 2)[:, hid][:, :, None]
    return lanes(dq), lanes(dk), dmat_w, gm, bm


def _rope_tables(n_ctx_tok, seq, w):
    half = HEAD_DIM // 2
    quarter = half // 2
    inv_freq = ROPE_BASE ** (-jnp.arange(quarter, dtype=F32) / quarter)
    tok = jnp.arange(seq)
    rows = (tok // GRID_W).astype(F32)
    cols = (tok % GRID_W).astype(F32)
    o = jnp.arange(w) % HEAD_DIM
    pos = jnp.where(o[None, :] < half, rows[:, None], cols[:, None])
    ang = pos * inv_freq[o % quarter][None, :]
    sign = jnp.where((o % half) < quarter, -1.0, 1.0)[None, :]
    cos = jnp.concatenate([jnp.ones((n_ctx_tok, w), F32), jnp.cos(ang)], axis=0)
    sin = jnp.concatenate([jnp.zeros((n_ctx_tok, w), F32), jnp.sin(ang) * sign], axis=0)
    return cos, sin


def _rwkv_kernel(z_ref, mu_ref, vec_ref, wup_ref, aup_ref, e_ref, minc_ref, strict_ref, incl_ref,
                 y_ref, bon_ref, st_ref, zprev_ref, *, rev, n_ctx, w):
    i = pl.program_id(0)
    nb, c, zw = z_ref.shape
    n_pairs = w // LANES
    rows = nb * c

    @pl.when(i == 0)
    def _():
        st_ref[...] = jnp.zeros_like(st_ref)

    @pl.when((i == 0) | (i == n_ctx))
    def _():
        zprev_ref[...] = jnp.zeros_like(zprev_ref)

    w0, a0, k_k, k_a, r_k = (vec_ref[j:j + 1, :] for j in range(5))
    e2 = e_ref[...]
    e22 = jnp.concatenate([e2, e2], axis=0)
    strict = strict_ref[...] > 0.0
    incl = incl_ref[...] > 0.0
    lane_lo = lax.broadcasted_iota(jnp.int32, (c, LANES), 1) < HEAD_DIM
    head_r = lax.broadcasted_iota(jnp.int32, (LANES, LANES), 0) // HEAD_DIM
    head_c = lax.broadcasted_iota(jnp.int32, (LANES, LANES), 1) // HEAD_DIM
    diag = head_r == head_c
    last = 0 if rev else c - 1

    def head_sums(x, pieces):
        outs = []
        for j in range(n_pairs):
            xj = x[:, j * LANES:(j + 1) * LANES]
            if pieces == 1:
                outs.append(_dot(xj.astype(BF16), e2))
            else:
                hi = xj.astype(BF16)
                mid = (xj - hi.astype(F32)).astype(BF16)
                outs.append(_dot(jnp.concatenate([hi, mid], axis=1), e22))
        return jnp.concatenate(outs, axis=1)

    def stack(xw):
        return jnp.concatenate([jnp.where(lane_lo, xw, 0.0), jnp.where(lane_lo, 0.0, xw)], axis=0)

    chains = [(b, j) for b in range(nb) for j in range(n_pairs)]

    def win(x, ch):
        b, j = ch
        return x[b * c:(b + 1) * c, j * LANES:(j + 1) * LANES]

    z = z_ref[...].reshape(rows, zw)
    rin = lax.broadcasted_iota(jnp.int32, (rows, zw), 0) % c
    prev = jnp.concatenate([jnp.broadcast_to(zprev_ref[b, 0:1, :], (c, zw)) for b in range(nb)], axis=0)
    if rev:
        zs = jnp.where(rin == c - 1, prev, pltpu.roll(z, rows - 1, axis=0))
        for b in range(nb):
            zprev_ref[b, 0:1, :] = z[b * c:b * c + 1, :]
    else:
        zs = jnp.where(rin == 0, prev, pltpu.roll(z, 1, axis=0))
        for b in range(nb):
            zprev_ref[b, 0:1, :] = z[b * c + c - 1:b * c + c, :]
    zd = z + (zs - z) * mu_ref[...]
    r = zd[:, 0:w]
    k = zd[:, w:2 * w]
    v = zd[:, 2 * w:3 * w]
    lora = zd[:, 3 * w:3 * w + LANES]
    lane = lax.broadcasted_iota(jnp.int32, (rows, LANES), 1)
    lora = jnp.where(lane < DECAY_LORA, jnp.tanh(lora), lora)
    w_log = -_softplus(-(w0 + _dot_x3k(lora, wup_ref[...]))) - 0.5
    logw = -jnp.exp(w_log)
    a = _sigmoid(a0 + _dot_x3k(lora, aup_ref[...]))
    kk0 = k * k_k
    kk = kk0 / jnp.maximum(jnp.sqrt(head_sums(kk0 * kk0, 2)), 1e-12)
    k2 = k * (1.0 + (a - 1.0) * k_a)
    bon_ref[...] = (head_sums(r * k2 * r_k, 1) * v).reshape(nb, c, w).astype(bon_ref.dtype)

    cinc = jnp.concatenate(
        [_dot(minc_ref[...], jnp.concatenate(_split3(logw[b * c:(b + 1) * c]), axis=0)) for b in range(nb)], axis=0)
    e_inc = jnp.exp(cinc)
    e_neg = jnp.exp(-cinc)
    rt = r * e_inc
    kt = k2 * e_neg
    bt = kk * a * e_neg
    kkt = kk * jnp.exp(cinc - logw)

    st = [st_ref[b * n_pairs + j] for b, j in chains]
    lhs = [jnp.concatenate([win(kkt, ch), win(rt, ch)], axis=0).astype(BF16) for ch in chains]
    g = [_dot_nt(l, jnp.concatenate([stack(win(bt, ch)), stack(win(kt, ch))], axis=0).astype(BF16))
         for l, ch in zip(lhs, chains)]
    a_b = [jnp.where(strict, x[0:c, 0:2 * c], 0.0) for x in g]
    a_k = [jnp.where(strict, x[0:c, 2 * c:4 * c], 0.0).astype(BF16) for x in g]
    r_kb = [jnp.concatenate([jnp.where(incl, x[c:2 * c, 2 * c:4 * c], 0.0),
                             -jnp.where(incl, x[c:2 * c, 0:2 * c], 0.0)], axis=1).astype(BF16) for x in g]
    x0 = [_dot_nt(l, s.astype(BF16)) for l, s in zip(lhs, st)]
    v_sb = [stack(win(v, ch)).astype(BF16) for ch in chains]
    u = [x[0:c] + _dot(ak, vs) for x, ak, vs in zip(x0, a_k, v_sb)]

    pw = a_b
    steps, sign = 1, -1.0
    while 2 * steps < c:
        both = [_dot(p.astype(BF16), jnp.concatenate([stack(p), stack(x)], axis=1).astype(BF16))
                for p, x in zip(pw, u)]
        u = [x + sign * y[:, 2 * c:4 * c] for x, y in zip(u, both)]
        pw = [y[:, 0:2 * c] for y in both]
        steps, sign = 2 * steps, 1.0
    u = [x + sign * _dot(p.astype(BF16), stack(x).astype(BF16)) for x, p in zip(u, pw)]

    y = [x[c:2 * c] + _dot(rk, jnp.concatenate([vs, stack(uu).astype(BF16)], axis=0))
         for x, rk, vs, uu in zip(x0, r_kb, v_sb, u)]
    upd = [_dot_tn(jnp.concatenate([win(v, ch), x], axis=0).astype(BF16),
                   jnp.concatenate([win(kt, ch), -win(bt, ch)], axis=0).astype(BF16))
           for ch, x in zip(chains, u)]
    for n, (b, j) in enumerate(chains):
        w_end = e_inc[b * c + last:b * c + last + 1, j * LANES:(j + 1) * LANES]
        st_ref[b * n_pairs + j] = jnp.where(diag, (st[n] + upd[n]) * w_end, 0.0)
        y_ref[b, :, j * LANES:(j + 1) * LANES] = y[n].astype(y_ref.dtype)


def _rwkv(p_z, prm, n_ctx, rev):
    b, t, zw = p_z.shape
    w = (zw - DECAY_LORA - ICLR_LORA) // 3
    c = RWKV_CHUNK
    n_tot = t // c
    mu, vecs, wup, aup, e_bf, minc, strict, incl = prm
    tix = lambda i: _scan_chunk(i, n_ctx, n_tot, rev)
    const = lambda i: (0, 0)
    full = lambda a: pl.BlockSpec(a.shape, const)
    return pl.pallas_call(
        functools.partial(_rwkv_kernel, rev=rev, n_ctx=n_ctx, w=w),
        grid=(n_tot,),
        in_specs=[pl.BlockSpec((b, c, zw), lambda i: (0, tix(i), 0)),
                  full(mu), full(vecs), full(wup), full(aup), full(e_bf), full(minc), full(strict), full(incl)],
        out_specs=[pl.BlockSpec((b, c, w), lambda i: (0, tix(i), 0)),
                   pl.BlockSpec((b, c, w), lambda i: (0, tix(i), 0))],
        out_shape=[jax.ShapeDtypeStruct((b, t, w), BF16), jax.ShapeDtypeStruct((b, t, w), BF16)],
        scratch_shapes=[pltpu.VMEM((b * (w // LANES), LANES, LANES), F32), pltpu.VMEM((b, 8, zw), F32)],
        compiler_params=_cparams("arbitrary"),
        name="rwkv7_rev" if rev else "rwkv7_fwd",
    )(p_z, mu, vecs, wup, aup, e_bf, minc, strict, incl)


def _rwkv_params(mu, w0, w_up, a0, a_up, k_k, k_a, r_k, rev, n_batch):
    w = w0.shape[0]
    c = RWKV_CHUNK
    vecs = jnp.concatenate([jnp.stack([w0, a0, k_k, k_a, r_k]), jnp.zeros((3, w), F32)], axis=0)
    wup = jnp.concatenate([w_up, jnp.zeros((ICLR_LORA, w), F32)], axis=0)
    aup = jnp.concatenate([jnp.zeros((DECAY_LORA, w), F32), a_up], axis=0)
    hid = jnp.arange(LANES) // HEAD_DIM
    e_bf = (hid[:, None] == hid[None, :]).astype(BF16)
    t = jnp.arange(c)
    p = (c - 1 - t) if rev else t
    le = p[None, :] <= p[:, None]
    lt = p[None, :] < p[:, None]
    strict = jnp.tile(lt, (1, 2)).astype(F32)
    incl = jnp.tile(le, (1, 2)).astype(F32)
    minc = jnp.tile(le, (1, 3)).astype(BF16)
    return mu.reshape(1, -1), vecs, wup, aup, e_bf, minc, strict, incl


def _lru_kernel(x_ref, cw_ref, vec_ref, wa_ref, wx_ref, h_ref, hcar_ref, ucar_ref, *, rev, n_ctx):
    i = pl.program_id(0)
    nb, c, w = x_ref.shape
    rows = nb * c

    @pl.when(i == 0)
    def _():
        hcar_ref[...] = jnp.zeros_like(hcar_ref)

    @pl.when((i == 0) | (i == n_ctx))
    def _():
        ucar_ref[...] = jnp.zeros_like(ucar_ref)

    u0 = x_ref[...].astype(F32).reshape(rows, w)
    row = lax.broadcasted_iota(jnp.int32, (rows, w), 0) % c

    def per_batch(ref, j):
        return jnp.concatenate([jnp.broadcast_to(ref[b, j:j + 1, :], (c, w)) for b in range(nb)], axis=0)

    def shifted(x, s, carry, fill):
        if rev:
            rolled = pltpu.roll(x, rows - s, axis=0)
            edge = row >= c - s
        else:
            rolled = pltpu.roll(x, s, axis=0)
            edge = row < s
        if carry is None:
            return jnp.where(edge, fill, rolled)
        return jnp.where(edge, carry, rolled)

    conv = vec_ref[0:1, :] + cw_ref[LRU_CONV - 1:LRU_CONV, :] * u0
    for m in range(1, LRU_CONV):
        car = jnp.zeros((rows, w), F32)
        for qpos in range(m):
            r_idx = (c - 1 - qpos) if rev else qpos
            car = jnp.where(row == r_idx, per_batch(ucar_ref, m - qpos - 1), car)
        conv = conv + cw_ref[LRU_CONV - 1 - m:LRU_CONV - m, :] * shifted(u0, m, car, None)
    for m in range(1, LRU_CONV):
        r_idx = (m - 1) if rev else (c - m)
        for b in range(nb):
            ucar_ref[b, m - 1:m, :] = u0[b * c + r_idx:b * c + r_idx + 1, :]

    cb = conv.astype(BF16)
    r = _sigmoid(_dot(cb, wa_ref[...]) + vec_ref[1:2, :])
    ig = _sigmoid(_dot(cb, wx_ref[...]) + vec_ref[2:3, :])
    log_a = -LRU_C * r * vec_ref[3:4, :]
    a = jnp.exp(log_a)
    bb = jnp.sqrt(1.0 - jnp.exp(2.0 * log_a)) * (ig * conv)

    s = 1
    while s < c:
        bb = bb + a * shifted(bb, s, None, 0.0)
        a = a * shifted(a, s, None, 1.0)
        s *= 2
    h = bb + a * per_batch(hcar_ref, 0)
    h_ref[...] = h.reshape(nb, c, w).astype(h_ref.dtype)
    last = 0 if rev else c - 1
    for b in range(nb):
        hcar_ref[b, 0:1, :] = h[b * c + last:b * c + last + 1, :]


def _lru(p_lru, prm, n_ctx, rev):
    b, t, w2 = p_lru.shape
    w = w2 // 2
    c = LRU_CHUNK
    n_tot = t // c
    cw, vecs, wa, wx = prm
    tix = lambda i: _scan_chunk(i, n_ctx, n_tot, rev)
    const = lambda i: (0, 0)
    return pl.pallas_call(
        functools.partial(_lru_kernel, rev=rev, n_ctx=n_ctx),
        grid=(n_tot,),
        in_specs=[pl.BlockSpec((b, c, w), lambda i: (0, tix(i), 0)),
                  pl.BlockSpec(cw.shape, const), pl.BlockSpec(vecs.shape, const),
                  pl.BlockSpec(wa.shape, const), pl.BlockSpec(wx.shape, const)],
        out_specs=pl.BlockSpec((b, c, w), lambda i: (0, tix(i), 0)),
        out_shape=jax.ShapeDtypeStruct((b, t, w), BF16),
        scratch_shapes=[pltpu.VMEM((b, 8, w), F32), pltpu.VMEM((b, 8, w), F32)],
        compiler_params=_cparams("arbitrary"),
        name="rglru_rev" if rev else "rglru_fwd",
    )(p_lru, cw, vecs, wa, wx)


def _lru_params(conv_w, conv_b, wa, ba, wx, bx, lam):
    w = conv_b.shape[0]
    cw = jnp.concatenate([conv_w, jnp.zeros((8 - LRU_CONV, w), F32)], axis=0)
    vecs = jnp.concatenate([jnp.stack([conv_b, ba, bx, jax.nn.softplus(-lam)]), jnp.zeros((4, w), F32)], axis=0)
    return cw, vecs, jax.scipy.linalg.block_diag(*wa).astype(BF16), jax.scipy.linalg.block_diag(*wx).astype(BF16)


def _head_norm(y, e2, gain, bias, eps):
    inv = 1.0 / HEAD_DIM

    def head_sums(x):
        xb = x.astype(BF16)
        return jnp.concatenate([_dot(xb[:, j:j + LANES], e2) for j in range(0, x.shape[1], LANES)], axis=1)

    yc = y - head_sums(y) * inv
    var = head_sums(yc * yc) * inv
    return yc * lax.rsqrt(var + eps) * gain + bias


def _mix_out_kernel(x_ref, mod_ref, of_ref, ob_ref, g_ref, yf_ref, yb_ref, bf_ref, bb_ref, gd_ref,
                    hf_ref, hb_ref, lg_ref, gn_ref, e_ref, gup_ref, wout_ref, n2g_ref, wr_ref, br_ref, tri_ref,
                    xo_ref, h2_ref, idx_ref, gate_ref, rank_ref, cnt_ref, base_ref, *, w_ret, w_rw):
    @pl.when((pl.program_id(0) == 0) & (pl.program_id(1) == 0))
    def _():
        base_ref[...] = jnp.zeros_like(base_ref)

    e_bf = e_ref[...]
    f32 = lambda ref: ref[0].astype(F32)
    g = f32(g_ref)
    ret = _head_norm(f32(of_ref) + f32(ob_ref), e_bf, gn_ref[0:1, :], gn_ref[1:2, :], RET_GN_EPS)
    ret = ret * (g * _sigmoid(g))
    gate = _dot(_sigmoid(f32(gd_ref)).astype(BF16), gup_ref[...])
    rw = _head_norm(f32(yf_ref) + f32(yb_ref), e_bf, gn_ref[2:3, :], gn_ref[3:4, :], RWKV_GN_EPS)
    rw = (rw + f32(bf_ref) + f32(bb_ref)) * gate
    lg = f32(lg_ref)
    gelu = 0.5 * lg * (1.0 + jnp.tanh(0.7978845608028654 * (lg + 0.044715 * (lg * lg * lg))))
    lru = (f32(hf_ref) + f32(hb_ref)) * gelu
    mix = (_dot(ret.astype(BF16), wout_ref[0:w_ret, :])
           + _dot(rw.astype(BF16), wout_ref[w_ret:w_ret + w_rw, :])
           + _dot(lru.astype(BF16), wout_ref[w_ret + w_rw:, :]))
    x = x_ref[0] + mod_ref[0, 0, 0:1, :] * mix
    xo_ref[0] = x
    ms = jnp.mean(x * x, axis=-1, keepdims=True)
    h2 = x * lax.rsqrt(ms + NORM_EPS) * n2g_ref[...]
    h2 = h2 * (1.0 + mod_ref[0, 0, 2:3, :]) + mod_ref[0, 0, 1:2, :]
    h2_ref[0] = h2.astype(BF16)
    logits = _dot_x3(h2, wr_ref[...]) + br_ref[...]
    idx_o, gate_o, rank_o = _route_tile(logits, tri_ref[...], base_ref)
    idx_ref[0] = idx_o
    gate_ref[0] = gate_o
    rank_ref[0] = rank_o
    cnt_ref[...] = base_ref[...].astype(jnp.int32)


def _mix_out(xs, mod, o_f, o_b, p_ret, y_f, y_b, bon_f, bon_b, p_gd, h_f, h_b, p_lru,
             gn, e_bf, g_up_bf, w_out_bf, norm2_g, w_router, b_router, n_ctx_tiles):
    b, t, d = xs.shape
    tm = TOKEN_TILE
    w_ret, w_rw, w_lru = o_f.shape[2], y_f.shape[2], h_f.shape[2]
    ne = w_router.shape[1]
    tok = lambda wd, j=0: pl.BlockSpec((1, tm, wd), lambda bi, i: (bi, i, j))
    const = lambda a: pl.BlockSpec(a.shape, lambda bi, i: (0,) * a.ndim)
    seg = lambda bi, i: (bi, jnp.where(i >= n_ctx_tiles, 1, 0), 0, 0)
    n2g = norm2_g.reshape(1, d)
    br = b_router.reshape(1, ne)
    tt = jnp.arange(tm)
    tri = (tt[None, :] < tt[:, None]).astype(BF16)
    return pl.pallas_call(
        functools.partial(_mix_out_kernel, w_ret=w_ret, w_rw=w_rw),
        grid=(b, t // tm),
        in_specs=[tok(d), pl.BlockSpec((1, 1, 3, d), seg),
                  tok(w_ret), tok(w_ret), tok(w_ret, 3),
                  tok(w_rw), tok(w_rw), tok(w_rw), tok(w_rw), tok(GATE_LORA),
                  tok(w_lru), tok(w_lru), tok(w_lru, 1),
                  const(gn), const(e_bf), const(g_up_bf), const(w_out_bf), const(n2g), const(w_router), const(br),
                  const(tri)],
        out_specs=[tok(d), tok(d), tok(LANES), tok(LANES), tok(LANES),
                   pl.BlockSpec((8, ne), lambda bi, i: (0, 0))],
        out_shape=[jax.ShapeDtypeStruct((b, t, d), F32), jax.ShapeDtypeStruct((b, t, d), BF16),
                   jax.ShapeDtypeStruct((b, t, LANES), jnp.int32), jax.ShapeDtypeStruct((b, t, LANES), F32),
                   jax.ShapeDtypeStruct((b, t, LANES), jnp.int32), jax.ShapeDtypeStruct((8, ne), jnp.int32)],
        scratch_shapes=[pltpu.VMEM((8, ne), F32)],
        compiler_params=_cparams("arbitrary", "arbitrary"),
        name="mix_out",
    )(xs, mod, o_f, o_b, p_ret, y_f, y_b, bon_f, bon_b, p_gd, h_f, h_b, p_lru,
      gn, e_bf, g_up_bf, w_out_bf, n2g, w_router, br, tri)


def _moe_kernel(be_ref, first_ref, nu_ref, *refs, blk0, chained):
    x_ref, w1_ref, b1_ref, w2_ref, b2_ref = refs[1:6] if chained else refs[0:5]
    y_ref, w1b_ref, w2b_ref = refs[-3:]
    i = pl.program_id(0)
    blk = i + blk0
    de = w2_ref.shape[1]

    @pl.when((first_ref[blk] == 1) | (i == 0))
    def _():
        w1b_ref[...] = w1_ref[0].astype(BF16)
        w2b_ref[...] = w2_ref[0].astype(BF16)

    @pl.when(blk < nu_ref[0])
    def _():
        gu = _dot(x_ref[...], w1b_ref[...]) + b1_ref[0]
        glu = jnp.minimum(gu[:, :de], SWIGLU_LIMIT)
        lin = jnp.clip(gu[:, de:], -SWIGLU_LIMIT, SWIGLU_LIMIT)
        act = glu * _sigmoid(SWIGLU_ALPHA * glu) * (lin + 1.0)
        y_ref[...] = (_dot(act.astype(BF16), w2b_ref[...]) + b2_ref[0]).astype(y_ref.dtype)

    @pl.when(blk >= nu_ref[0])
    def _():
        y_ref[...] = jnp.zeros_like(y_ref)


def _moe_ffn(hb, block_e, first, n_used, w1, b1, w2, b2, layer, blk0, n_slots, y_prev=None):
    rows, d = hb.shape
    tm = MOE_TILE
    nl, ne, _, d2 = w1.shape
    de = w2.shape[2]
    chained = y_prev is not None
    wsel = lambda i, be, fi, nu: (layer, be[i + blk0], 0, 0)
    in_specs = [pl.BlockSpec((tm, d), lambda i, be, fi, nu: (i, 0)),
                pl.BlockSpec((None, 1, d, d2), wsel),
                pl.BlockSpec((None, 1, 1, d2), wsel),
                pl.BlockSpec((None, 1, de, d), wsel),
                pl.BlockSpec((None, 1, 1, d), wsel)]
    args = [hb, w1, b1.reshape(nl, ne, 1, d2), w2, b2.reshape(nl, ne, 1, d)]
    if chained:
        in_specs = [pl.BlockSpec(memory_space=pl.ANY)] + in_specs
        args = [y_prev] + args
    return pl.pallas_call(
        functools.partial(_moe_kernel, blk0=blk0, chained=chained),
        grid_spec=pltpu.PrefetchScalarGridSpec(
            num_scalar_prefetch=3,
            grid=(rows // tm,),
            in_specs=in_specs,
            out_specs=pl.BlockSpec((tm, d), lambda i, be, fi, nu: (i + blk0, 0)),
            scratch_shapes=[pltpu.VMEM((d, d2), BF16), pltpu.VMEM((de, d), BF16)],
        ),
        out_shape=jax.ShapeDtypeStruct((n_slots, d), BF16),
        input_output_aliases={3: 0} if chained else {},
        compiler_params=_cparams("arbitrary"),
        name="moe_ffn",
    )(block_e, first, n_used, *args)


def _route_tile(lg, tri, base_ref):
    tr, ne = lg.shape
    lane = lax.broadcasted_iota(jnp.int32, (tr, ne), 1).astype(F32)
    out_lane = lax.broadcasted_iota(jnp.int32, (tr, LANES), 1)
    vals = lg
    sel = jnp.zeros((tr, ne), F32)
    picks, tops = [], []
    for _ in range(TOP_K):
        m = jnp.max(vals, axis=-1, keepdims=True)
        ix = jnp.min(jnp.where(vals == m, lane, float(ne)), axis=-1, keepdims=True)
        hit = lane == ix
        sel = jnp.where(hit, 1.0, sel)
        vals = jnp.where(hit, -jnp.inf, vals)
        picks.append(ix)
        tops.append(m)
    ex = [jnp.exp(t - tops[0]) for t in tops]
    den = ex[0] + ex[1] + ex[2] + ex[3]
    before = _dot(tri, sel.astype(BF16)) + base_ref[0:1, :]
    idx_o = jnp.zeros((tr, LANES), F32)
    gate_o = jnp.zeros((tr, LANES), F32)
    rank_o = jnp.zeros((tr, LANES), F32)
    for k in range(TOP_K):
        rk = jnp.sum(jnp.where(lane == picks[k], before, 0.0), axis=-1, keepdims=True)
        idx_o = jnp.where(out_lane == k, picks[k], idx_o)
        gate_o = jnp.where(out_lane == k, ex[k] / den, gate_o)
        rank_o = jnp.where(out_lane == k, rk, rank_o)
    total = base_ref[0:1, :] + jnp.sum(sel, axis=0, keepdims=True)
    base_ref[...] = jnp.broadcast_to(total, base_ref.shape)
    return idx_o.astype(jnp.int32), gate_o, rank_o.astype(jnp.int32)


def _route_meta(idx, rank, counts):
    n_tok = idx.shape[0]
    ne = counts.shape[0]
    tm = MOE_TILE
    n_assign = n_tok * TOP_K
    padded = (counts + tm - 1) // tm * tm
    pend = jnp.cumsum(padded)
    pstart = pend - padded
    start = jnp.cumsum(counts) - counts
    eid = jnp.arange(ne, dtype=jnp.int32)
    slot = jnp.sum(jnp.where(idx[..., None] == eid, pstart, 0), axis=-1).astype(jnp.int32) + rank
    n_blocks = (n_assign + ne * (tm - 1) + tm - 1) // tm
    blk_start = jnp.arange(n_blocks, dtype=jnp.int32) * tm
    block_e = jnp.minimum(jnp.sum(pend[None, :] <= blk_start[:, None], axis=1), ne - 1).astype(jnp.int32)
    first = jnp.concatenate([jnp.ones((1,), jnp.int32), (block_e[1:] != block_e[:-1]).astype(jnp.int32)])
    n_used = (pend[-1] // tm).astype(jnp.int32).reshape(1)
    assert n_tok <= 1 << 16 and ne <= 1 << 15
    tok_id = jnp.arange(n_tok, dtype=jnp.int32)[:, None]
    order_tok = jnp.sort((idx * (1 << 16) + tok_id).reshape(-1)) & ((1 << 16) - 1)
    off = jnp.arange(n_blocks * tm, dtype=jnp.int32) - jnp.repeat(pstart[block_e], tm)
    valid = off < jnp.repeat(counts[block_e], tm)
    pos = jnp.clip(jnp.repeat(start[block_e], tm) + off, 0, n_assign - 1)
    spread = jnp.arange(n_blocks * tm, dtype=jnp.int32) % n_tok
    slot_tok = jnp.where(valid, order_tok[pos], spread).astype(jnp.int32)
    return slot, slot_tok, block_e, first, n_used


def _combine_kernel(x_ref, mod_ref, y_ref, gate_ref, g_ref, *rest, final):
    o_ref = rest[-1]
    gate = gate_ref[0]
    y = y_ref[0, 0].astype(F32) * gate[:, 0:1]
    for k in range(1, TOP_K):
        y = y + y_ref[k, 0].astype(F32) * gate[:, k:k + 1]
    x = x_ref[0] + mod_ref[0, 0, 0:1, :] * y
    if final:
        ms = jnp.mean(x * x, axis=-1, keepdims=True)
        x = x * lax.rsqrt(ms + NORM_EPS) * g_ref[...]
    o_ref[0] = x


def _combine(xs, mod, yg, gates, final_g, n_ctx_tiles, final, b0, prev=None):
    b, t, d = xs.shape
    nbh = yg.shape[1]
    tm = TOKEN_TILE
    skip = n_ctx_tiles if final else 0
    seg = lambda bi, i: (bi + b0, jnp.where(i + skip >= n_ctx_tiles, 1, 0), 0, 0)
    in_specs = [pl.BlockSpec((1, tm, d), lambda bi, i: (bi + b0, i + skip, 0)),
                pl.BlockSpec((1, 1, 1, d), seg),
                pl.BlockSpec((TOP_K, 1, tm, d), lambda bi, i: (0, bi, i + skip, 0)),
                pl.BlockSpec((1, tm, LANES), lambda bi, i: (bi + b0, i + skip, 0)),
                pl.BlockSpec((1, d), lambda bi, i: (0, 0))]
    args = [xs, mod, yg, gates, final_g.reshape(1, d)]
    aliases = {} if final else {0: 0}
    if prev is not None:
        in_specs.append(pl.BlockSpec(memory_space=pl.ANY))
        args.append(prev)
        aliases = {5: 0}
    return pl.pallas_call(
        functools.partial(_combine_kernel, final=final),
        grid=(nbh, t // tm - skip),
        in_specs=in_specs,
        out_specs=pl.BlockSpec((1, tm, d), lambda bi, i: (bi + b0, i, 0)),
        out_shape=jax.ShapeDtypeStruct((b, t - skip * tm, d), F32),
        input_output_aliases=aliases,
        compiler_params=_cparams("parallel", "parallel"),
        name="combine_final" if final else "combine",
    )(*args)


def kernel(x, c, ctx, c_ctx, w_mod, b_mod, norm1_g, norm2_g, w_in, w_out, ret_decay_logit, ret_gn_g, ret_gn_b, rwkv_mu, rwkv_w0, rwkv_w_up, rwkv_a0, rwkv_a_up, rwkv_k_k, rwkv_k_a, rwkv_g_up, rwkv_r_k, rwkv_gn_g, rwkv_gn_b, lru_conv_w, lru_conv_b, lru_wa, lru_ba, lru_wx, lru_bx, lru_lambda, moe_w_router, moe_b_router, moe_w1, moe_b1, moe_w2, moe_b2, final_norm_g):
    bsz, seq, dm = x.shape
    n_ctx_tok = ctx.shape[1]
    depth = w_in.shape[0]
    n_experts = moe_w_router.shape[2]
    w_ret = 3 * dm // 8
    w_rw = 3 * dm // 8
    w_lru = dm - w_ret - w_rw
    zw = 3 * w_rw + DECAY_LORA + ICLR_LORA
    sizes = (4 * w_ret, zw, GATE_LORA, 2 * w_lru)
    bounds, off = [], 0
    for s in sizes:
        bounds.append((off, off + s))
        off += s
    bounds = tuple(bounds)
    assert off == w_in.shape[2]
    assert n_ctx_tok % TOKEN_TILE == 0 and seq % TOKEN_TILE == 0 and seq % GRID_W == 0
    t_all = n_ctx_tok + seq
    n_ctx_tiles = n_ctx_tok // TOKEN_TILE

    xs = jnp.concatenate([ctx, x], axis=1)
    cos_t, sin_t = _rope_tables(n_ctx_tok, seq, w_ret)
    hid = jnp.arange(LANES) // HEAD_DIM
    e_bf = (hid[:, None] == hid[None, :]).astype(BF16)
    cond =jnp.concatenate([c, c_ctx[None, :], jnp.zeros((8 - (bsz + 1) % 8, dm), F32)], axis=0)

    for l in range(depth):
        last = l == depth - 1
        mod = _modulation(cond, w_mod, b_mod, l)
        mod_l = mod[:bsz].reshape(bsz, 6, dm)
        mod_c = jnp.broadcast_to(mod[bsz].reshape(1, 6, dm), (bsz, 6, dm))
        modsel = jnp.stack([mod_c, mod_l], axis=1)

        p_ret, p_z, p_gd, p_lru = _in_proj(xs, modsel[:, :, 0:2], norm1_g[l], w_in[l].astype(BF16), bounds,
                                           (BF16, F32, BF16, BF16), n_ctx_tiles)

        ret_o, rw_y, rw_bon, lru_h = [], [], [], []
        for d in range(2):
            rev = d == 1
            ret_o.append(_retention(p_ret, cos_t, sin_t, _ret_tables(ret_decay_logit[l, d], w_ret, rev),
                                    n_ctx_tok // RET_CHUNK, rev))
            prm = _rwkv_params(rwkv_mu[l, d], rwkv_w0[l, d], rwkv_w_up[l, d], rwkv_a0[l, d], rwkv_a_up[l, d],
                               rwkv_k_k[l, d], rwkv_k_a[l, d], rwkv_r_k[l], rev, bsz)
            y, bon = _rwkv(p_z, prm, n_ctx_tok // RWKV_CHUNK, rev)
            rw_y.append(y)
            rw_bon.append(bon)
            lru_h.append(_lru(p_lru, _lru_params(lru_conv_w[l, d], lru_conv_b[l, d], lru_wa[l, d], lru_ba[l, d],
                                                 lru_wx[l, d], lru_bx[l, d], lru_lambda[l, d]),
                              n_ctx_tok // LRU_CHUNK, rev))

        gn = jnp.concatenate([jnp.stack([ret_gn_g[l], ret_gn_b[l], rwkv_gn_g[l], rwkv_gn_b[l]]),
                              jnp.zeros((4, w_ret), F32)], axis=0)
        xs, h2, idx, gates, rank, counts = _mix_out(
            xs, modsel[:, :, 2:5], ret_o[0], ret_o[1], p_ret, rw_y[0], rw_y[1],
            rw_bon[0], rw_bon[1], p_gd, lru_h[0], lru_h[1], p_lru,
            gn, e_bf, rwkv_g_up[l].astype(BF16), w_out[l].astype(BF16), norm2_g[l],
            moe_w_router[l], moe_b_router[l], n_ctx_tiles)

        n_tok = bsz * t_all
        slot, slot_tok, block_e, first, n_used = _route_meta(
            idx.reshape(n_tok, LANES)[:, :TOP_K], rank.reshape(n_tok, LANES)[:, :TOP_K], counts[0])
        n_slots = slot_tok.shape[0]
        row_a = max(n_slots // MOE_TILE // 4, 1) * MOE_TILE
        h2f = h2.reshape(n_tok, dm)
        y_sorted = None
        for lo, hi in ((0, row_a), (row_a, n_slots)):
            y_sorted = _moe_ffn(h2f[slot_tok[lo:hi]], block_e, first, n_used, moe_w1, moe_b1, moe_w2, moe_b2, l,
                                lo // MOE_TILE, n_slots, y_prev=y_sorted)
        slot_b = slot.reshape(bsz, t_all, TOP_K)
        out = None
        for lo, hi in [(b, b + 1) for b in range(bsz)]:
            yg = y_sorted[jnp.moveaxis(slot_b[lo:hi], 2, 0)]
            out = _combine(xs, modsel[:, :, 5:6], yg, gates, final_norm_g, n_ctx_tiles, last, lo,
                           prev=out if last else None)
            xs = xs if last else out
        xs = out
    return xs
```

```python
import functools

import jax
import jax.numpy as jnp
from jax import lax
from jax.experimental import pallas as pl
from jax.experimental.pallas import tpu as pltpu

F32 = jnp.float32
BF16 = jnp.bfloat16

HEAD_DIM = 64
NORM_EPS = 1e-6
RET_GN_EPS = 1e-5
RWKV_GN_EPS = 64e-5
ROPE_BASE = 10000.0
GRID_W = 64
LRU_CONV = 4
LRU_C = 8.0
TOP_K = 4
SWIGLU_LIMIT = 7.0
SWIGLU_ALPHA = 1.702
DECAY_LORA = 64
ICLR_LORA = 64
GATE_LORA = 128
DECAY_SCALE = 0.6065306597126334

LANES = 128
TOKEN_TILE = 256
RET_CHUNK = 128
RWKV_CHUNK = 64
LRU_CHUNK = 128
MOE_TILE = 512
VMEM_LIMIT = 56 * 1024 * 1024


def _cparams(*sem):
    return pltpu.CompilerParams(dimension_semantics=sem, vmem_limit_bytes=VMEM_LIMIT)


def _scan_chunk(i, n_ctx, n_tot, rev):
    if not rev:
        return i
    return jnp.where(i < n_ctx, n_ctx - 1 - i, n_tot + n_ctx - 1 - i)


def _split3(a):
    hi = a.astype(BF16)
    r1 = a - hi.astype(F32)
    mid = r1.astype(BF16)
    lo = (r1 - mid.astype(F32)).astype(BF16)
    return hi, mid, lo


def _dot(a, b):
    return jnp.dot(a, b, preferred_element_type=F32)


def _dot_nt(a, b):
    return lax.dot_general(a, b, (((1,), (1,)), ((), ())), preferred_element_type=F32)


def _dot_tn(a, b):
    return lax.dot_general(a, b, (((0,), (0,)), ((), ())), preferred_element_type=F32)


def _dot_exact_rhs(a, b_bf):
    hi, mid, lo = _split3(a)
    return _dot(hi, b_bf) + _dot(mid, b_bf) + _dot(lo, b_bf)


def _dot_exact_lhs(a_bf, b):
    hi, mid, lo = _split3(b)
    return _dot(a_bf, hi) + _dot(a_bf, mid) + _dot(a_bf, lo)


def _dot_x3(a, b):
    a_hi = a.astype(BF16)
    a_lo = (a - a_hi.astype(F32)).astype(BF16)
    b_hi = b.astype(BF16)
    b_lo = (b - b_hi.astype(F32)).astype(BF16)
    return _dot(a_hi, b_hi) + _dot(a_lo, b_hi) + _dot(a_hi, b_lo)


def _dot_x3k(a, b):
    a_hi = a.astype(BF16)
    a_lo = (a - a_hi.astype(F32)).astype(BF16)
    b_hi = b.astype(BF16)
    b_lo = (b - b_hi.astype(F32)).astype(BF16)
    return (_dot(jnp.concatenate([a_hi, a_lo], axis=1), jnp.concatenate([b_hi, b_hi], axis=0))
            + _dot(a_hi, b_lo))


def _sigmoid(x):
    return 1.0 / (1.0 + jnp.exp(-x))


def _mod_kernel(c_ref, w_ref, b_ref, o_ref):
    c = c_ref[...]
    s = c * _sigmoid(c)
    o_ref[...] = _dot_x3(s, w_ref[...]) + b_ref[...]


def _modulation(cond, w_mod, b_mod, layer):
    r, d = cond.shape
    nl, _, n = w_mod.shape
    tn = d
    return pl.pallas_call(
        _mod_kernel,
        grid=(n // tn,),
        in_specs=[pl.BlockSpec((r, d), lambda j: (0, 0)),
                  pl.BlockSpec((None, d, tn), lambda j: (layer, 0, j)),
                  pl.BlockSpec((None, 1, tn), lambda j: (layer, 0, j))],
        out_specs=pl.BlockSpec((r, tn), lambda j: (0, j)),
        out_shape=jax.ShapeDtypeStruct((r, n), F32),
        compiler_params=_cparams("arbitrary"),
        name="modulation",
    )(cond, w_mod, b_mod.reshape(nl, 1, n))


def _in_proj_kernel(x_ref, mod_ref, g_ref, w_ref, *o_refs, bounds):
    x = x_ref[0]
    ms = jnp.mean(x * x, axis=-1, keepdims=True)
    h = x * lax.rsqrt(ms + NORM_EPS) * g_ref[...]
    h = h * (1.0 + mod_ref[0, 0, 1:2, :]) + mod_ref[0, 0, 0:1, :]
    hb = h.astype(BF16)
    for o_ref, (lo, hi) in zip(o_refs, bounds):
        o_ref[0] = _dot(hb, w_ref[:, lo:hi]).astype(o_ref.dtype)


def _in_proj(xs, mod, norm_g, w_in_bf, bounds, dtypes, n_ctx_tiles):
    b, t, d = xs.shape
    tm = TOKEN_TILE
    p = w_in_bf.shape[1]
    seg = lambda bi, i: (bi, jnp.where(i >= n_ctx_tiles, 1, 0), 0, 0)
    return pl.pallas_call(
        functools.partial(_in_proj_kernel, bounds=bounds),
        grid=(b, t // tm),
        in_specs=[pl.BlockSpec((1, tm, d), lambda bi, i: (bi, i, 0)),
                  pl.BlockSpec((1, 1, 2, d), seg),
                  pl.BlockSpec((1, d), lambda bi, i: (0, 0)),
                  pl.BlockSpec((d, p), lambda bi, i: (0, 0))],
        out_specs=[pl.BlockSpec((1, tm, hi - lo), lambda bi, i: (bi, i, 0)) for lo, hi in bounds],
        out_shape=[jax.ShapeDtypeStruct((b, t, hi - lo), dt) for (lo, hi), dt in zip(bounds, dtypes)],
        compiler_params=_cparams("parallel", "parallel"),
        name="in_proj",
    )(xs, mod, norm_g.reshape(1, d), w_in_bf)


def _ret_kernel(q_ref, k_ref, v_ref, cos_ref, sin_ref, dq_ref, dk_ref, dmat_ref, gm_ref, bm_ref,
                o_ref, s_ref):
    i = pl.program_id(0)

    @pl.when(i == 0)
    def _():
        s_ref[...] = jnp.zeros_like(s_ref)

    nb, c, w = q_ref.shape
    cos = cos_ref[...]
    sin = sin_ref[...]
    lane = lax.broadcasted_iota(jnp.int32, (c, LANES), 1)
    first = (lane % 32) < 16

    def rope(u):
        parts = []
        for j in range(w // LANES):
            uj = u[:, j * LANES:(j + 1) * LANES]
            nxt = pltpu.roll(uj, LANES - 16, axis=1)
            prv = pltpu.roll(uj, 16, axis=1)
            parts.append(jnp.where(first, nxt, prv))
        return u * cos + jnp.concatenate(parts, axis=1) * sin

    lane_lo = lane < HEAD_DIM

    def stack(xw):
        return jnp.concatenate([jnp.where(lane_lo, xw, 0.0), jnp.where(lane_lo, 0.0, xw)], axis=0)

    n_pairs = w // LANES
    q = [rope(q_ref[b].astype(F32)) for b in range(nb)]
    k = [rope(k_ref[b].astype(F32)) for b in range(nb)]
    chains = [(b, j) for b in range(nb) for j in range(n_pairs)]
    pair = lambda x, j: x[:, j * LANES:(j + 1) * LANES]
    qw = [pair(q[b], j) for b, j in chains]
    kw = [pair(k[b], j) for b, j in chains]
    vw = [pair(v_ref[b], j) for b, j in chains]
    s = [s_ref[b * n_pairs + j] for b, j in chains]
    inter = [_dot((x * pair(dq_ref[...], j)).astype(BF16), st.astype(BF16)) for x, st, (b, j) in zip(qw, s, chains)]
    sc = [_dot_nt(x.astype(BF16), stack(y).astype(BF16)) * dmat_ref[j]
          for x, y, (b, j) in zip(qw, kw, chains)]
    intra = [_dot(x.astype(BF16), stack(y).astype(BF16)) for x, y in zip(sc, vw)]
    ktv = [_dot_tn((y * pair(dk_ref[...], j)).astype(BF16), z.astype(BF16))
           for y, z, (b, j) in zip(kw, vw, chains)]
    for n, (b, j) in enumerate(chains):
        o_ref[b, :, j * LANES:(j + 1) * LANES] = (inter[n] + intra[n]).astype(o_ref.dtype)
        s_ref[b * n_pairs + j] = gm_ref[j] * s[n] + bm_ref[...] * ktv[n]


def _retention(p_ret, cos_t, sin_t, tabs, n_ctx, rev):
    b, t, w4 = p_ret.shape
    w = w4 // 4
    c = RET_CHUNK
    n_tot = t // c
    n_pairs = w // LANES
    dq, dk, dmat, gm, bm = tabs
    tix = lambda i: _scan_chunk(i, n_ctx, n_tot, rev)
    col = lambda j: (lambda i: (0, tix(i), j))
    const = lambda a: pl.BlockSpec(a.shape, lambda i: (0,) * a.ndim)
    return pl.pallas_call(
        _ret_kernel,
        grid=(n_tot,),
        in_specs=[pl.BlockSpec((b, c, w), col(0)), pl.BlockSpec((b, c, w), col(1)), pl.BlockSpec((b, c, w), col(2)),
                  pl.BlockSpec((c, w), lambda i: (tix(i), 0)),
                  pl.BlockSpec((c, w), lambda i: (tix(i), 0)),
                  const(dq), const(dk), const(dmat), const(gm), const(bm)],
        out_specs=pl.BlockSpec((b, c, w), lambda i: (0, tix(i), 0)),
        out_shape=jax.ShapeDtypeStruct((b, t, w), BF16),
        scratch_shapes=[pltpu.VMEM((b * n_pairs, LANES, LANES), F32)],
        compiler_params=_cparams("arbitrary"),
        name="retention_rev" if rev else "retention_fwd",
    )(p_ret, p_ret, p_ret, cos_t, sin_t, dq, dk, dmat, gm, bm)


def _ret_tables(decay_logit, w, rev):
    n_heads = w // HEAD_DIM
    c = RET_CHUNK
    lg = jax.nn.log_sigmoid(decay_logit.astype(F32))
    t = jnp.arange(c, dtype=F32)
    p = (c - 1.0 - t) if rev else t
    rel = p[:, None] - p[None, :]
    scale = HEAD_DIM ** -0.5
    dmat = jnp.where(rel >= 0, jnp.exp(lg[:, None, None] * jnp.maximum(rel, 0.0)), 0.0) * scale
    dq = jnp.exp(lg[:, None] * (p + 1.0)) * scale
    dk = jnp.exp(lg[:, None] * (c - 1.0 - p))
    lanes = lambda a: jnp.repeat(a.T, HEAD_DIM, axis=1)
    n_pairs = n_heads // 2
    dmat_w = dmat.reshape(n_pairs, 2, c, c).transpose(0, 2, 1, 3).reshape(n_pairs, c, 2 * c)
    hid = jnp.arange(LANES) // HEAD_DIM
    bm = (hid[:, None] == hid[None, :]).astype(F32)
    gm = bm[None] * jnp.exp(lg * c).reshape(n_pairs, 2)[:, hid][:, :, None]
    return lanes(dq), lanes(dk), dmat_w, gm, bm


def _rope_tables(n_ctx_tok, seq, w):
    half = HEAD_DIM // 2
    quarter = half // 2
    inv_freq = ROPE_BASE ** (-jnp.arange(quarter, dtype=F32) / quarter)
    tok = jnp.arange(seq)
    rows = (tok // GRID_W).astype(F32)
    cols = (tok % GRID_W).astype(F32)
    o = jnp.arange(w) % HEAD_DIM
    pos = jnp.where(o[None, :] < half, rows[:, None], cols[:, None])
    ang = pos * inv_freq[o % quarter][None, :]
    sign = jnp.where((o % half) < quarter, -1.0, 1.0)[None, :]
    cos = jnp.concatenate([jnp.ones((n_ctx_tok, w), F32), jnp.cos(ang)], axis=0)
    sin = jnp.concatenate([jnp.zeros((n_ctx_tok, w), F32), jnp.sin(ang) * sign], axis=0)
    return cos, sin


def _rwkv_kernel(z_ref, mu_ref, vec_ref, wup_ref, aup_ref, e_ref, minc_ref, strict_ref, incl_ref,
                 y_ref, bon_ref, st_ref, zprev_ref, *, rev, n_ctx, w):
    i = pl.program_id(0)
    nb, c, zw = z_ref.shape
    n_pairs = w // LANES
    rows = nb * c

    @pl.when(i == 0)
    def _():
        st_ref[...] = jnp.zeros_like(st_ref)

    @pl.when((i == 0) | (i == n_ctx))
    def _():
        zprev_ref[...] = jnp.zeros_like(zprev_ref)

    w0, a0, k_k, k_a, r_k = (vec_ref[j:j + 1, :] for j in range(5))
    e2 = e_ref[...]
    e22 = jnp.concatenate([e2, e2], axis=0)
    strict = strict_ref[...] > 0.0
    incl = incl_ref[...] > 0.0
    lane_lo = lax.broadcasted_iota(jnp.int32, (c, LANES), 1) < HEAD_DIM
    head_r = lax.broadcasted_iota(jnp.int32, (LANES, LANES), 0) // HEAD_DIM
    head_c = lax.broadcasted_iota(jnp.int32, (LANES, LANES), 1) // HEAD_DIM
    diag = head_r == head_c
    last = 0 if rev else c - 1

    def head_sums(x, pieces):
        outs = []
        for j in range(n_pairs):
            xj = x[:, j * LANES:(j + 1) * LANES]
            if pieces == 1:
                outs.append(_dot(xj.astype(BF16), e2))
            else:
                hi = xj.astype(BF16)
                mid = (xj - hi.astype(F32)).astype(BF16)
                outs.append(_dot(jnp.concatenate([hi, mid], axis=1), e22))
        return jnp.concatenate(outs, axis=1)

    def stack(xw):
        return jnp.concatenate([jnp.where(lane_lo, xw, 0.0), jnp.where(lane_lo, 0.0, xw)], axis=0)

    chains = [(b, j) for b in range(nb) for j in range(n_pairs)]

    def win(x, ch):
        b, j = ch
        return x[b * c:(b + 1) * c, j * LANES:(j + 1) * LANES]

    z = z_ref[...].reshape(rows, zw)
    rin = lax.broadcasted_iota(jnp.int32, (rows, zw), 0) % c
    prev = jnp.concatenate([jnp.broadcast_to(zprev_ref[b, 0:1, :], (c, zw)) for b in range(nb)], axis=0)
    if rev:
        zs = jnp.where(rin == c - 1, prev, pltpu.roll(z, rows - 1, axis=0))
        for b in range(nb):
            zprev_ref[b, 0:1, :] = z[b * c:b * c + 1, :]
    else:
        zs = jnp.where(rin == 0, prev, pltpu.roll(z, 1, axis=0))
        for b in range(nb):
            zprev_ref[b, 0:1, :] = z[b * c + c - 1:b * c + c, :]
    zd = z + (zs - z) * mu_ref[...]
    r = zd[:, 0:w]
    k = zd[:, w:2 * w]
    v = zd[:, 2 * w:3 * w]
    lora = zd[:, 3 * w:3 * w + LANES]
    lane = lax.broadcasted_iota(jnp.int32, (rows, LANES), 1)
    lora = jnp.where(lane < DECAY_LORA, jnp.tanh(lora), lora)
    logw = -DECAY_SCALE * _sigmoid(w0 + _dot_x3k(lora, wup_ref[...]))
    a = _sigmoid(a0 + _dot_x3k(lora, aup_ref[...]))
    kk0 = k * k_k
    kk = kk0 / jnp.maximum(jnp.sqrt(head_sums(kk0 * kk0, 2)), 1e-12)
    k2 = k * (1.0 + (a - 1.0) * k_a)
    bon_ref[...] = (head_sums(r * k2 * r_k, 1) * v).reshape(nb, c, w).astype(bon_ref.dtype)

    cinc = jnp.concatenate(
        [_dot(minc_ref[...], jnp.concatenate(_split3(logw[b * c:(b + 1) * c]), axis=0)) for b in range(nb)], axis=0)
    e_inc = jnp.exp(cinc)
    e_neg = jnp.exp(-cinc)
    rt = r * e_inc
    kt = k2 * e_neg
    bt = kk * a * e_neg
    kkt = kk * jnp.exp(cinc - logw)

    st = [st_ref[b * n_pairs + j] for b, j in chains]
    lhs = [jnp.concatenate([win(kkt, ch), win(rt, ch)], axis=0).astype(BF16) for ch in chains]
    g = [_dot_nt(l, jnp.concatenate([stack(win(bt, ch)), stack(win(kt, ch))], axis=0).astype(BF16))
         for l, ch in zip(lhs, chains)]
    a_b = [jnp.where(strict, x[0:c, 0:2 * c], 0.0) for x in g]
    a_k = [jnp.where(strict, x[0:c, 2 * c:4 * c], 0.0).astype(BF16) for x in g]
    r_kb = [jnp.concatenate([jnp.where(incl, x[c:2 * c, 2 * c:4 * c], 0.0),
                             -jnp.where(incl, x[c:2 * c, 0:2 * c], 0.0)], axis=1).astype(BF16) for x in g]
    x0 = [_dot_nt(l, s.astype(BF16)) for l, s in zip(lhs, st)]
    v_sb = [stack(win(v, ch)).astype(BF16) for ch in chains]
    u = [x[0:c] + _dot(ak, vs) for x, ak, vs in zip(x0, a_k, v_sb)]

    pw = a_b
    steps, sign = 1, -1.0
    while 2 * steps < c:
        both = [_dot(p.astype(BF16), jnp.concatenate([stack(p), stack(x)], axis=1).astype(BF16))
                for p, x in zip(pw, u)]
        u = [x + sign * y[:, 2 * c:4 * c] for x, y in zip(u, both)]
        pw = [y[:, 0:2 * c] for y in both]
        steps, sign = 2 * steps, 1.0
    u = [x + sign * _dot(p.astype(BF16), stack(x).astype(BF16)) for x, p in zip(u, pw)]

    y = [x[c:2 * c] + _dot(rk, jnp.concatenate([vs, stack(uu).astype(BF16)], axis=0))
         for x, rk, vs, uu in zip(x0, r_kb, v_sb, u)]
    upd = [_dot_tn(jnp.concatenate([win(v, ch), x], axis=0).astype(BF16),
                   jnp.concatenate([win(kt, ch), -win(bt, ch)], axis=0).astype(BF16))
           for ch, x in zip(chains, u)]
    for n, (b, j) in enumerate(chains):
        w_end = e_inc[b * c + last:b * c + last + 1, j * LANES:(j + 1) * LANES]
        st_ref[b * n_pairs + j] = jnp.where(diag, (st[n] + upd[n]) * w_end, 0.0)
        y_ref[b, :, j * LANES:(j + 1) * LANES] = y[n].astype(y_ref.dtype)


def _rwkv(p_z, prm, n_ctx, rev):
    b, t, zw = p_z.shape
    w = (zw - DECAY_LORA - ICLR_LORA) // 3
    c = RWKV_CHUNK
    n_tot = t // c
    mu, vecs, wup, aup, e_bf, minc, strict, incl = prm
    tix = lambda i: _scan_chunk(i, n_ctx, n_tot, rev)
    const = lambda i: (0, 0)
    full = lambda a: pl.BlockSpec(a.shape, const)
    return pl.pallas_call(
        functools.partial(_rwkv_kernel, rev=rev, n_ctx=n_ctx, w=w),
        grid=(n_tot,),
        in_specs=[pl.BlockSpec((b, c, zw), lambda i: (0, tix(i), 0)),
                  full(mu), full(vecs), full(wup), full(aup), full(e_bf), full(minc), full(strict), full(incl)],
        out_specs=[pl.BlockSpec((b, c, w), lambda i: (0, tix(i), 0)),
                   pl.BlockSpec((b, c, w), lambda i: (0, tix(i), 0))],
        out_shape=[jax.ShapeDtypeStruct((b, t, w), BF16), jax.ShapeDtypeStruct((b, t, w), BF16)],
        scratch_shapes=[pltpu.VMEM((b * (w // LANES), LANES, LANES), F32), pltpu.VMEM((b, 8, zw), F32)],
        compiler_params=_cparams("arbitrary"),
        name="rwkv7_rev" if rev else "rwkv7_fwd",
    )(p_z, mu, vecs, wup, aup, e_bf, minc, strict, incl)


def _rwkv_params(mu, w0, w_up, a0, a_up, k_k, k_a, r_k, rev, n_batch):
    w = w0.shape[0]
    c = RWKV_CHUNK
    vecs = jnp.concatenate([jnp.stack([w0, a0, k_k, k_a, r_k]), jnp.zeros((3, w), F32)], axis=0)
    wup = jnp.concatenate([w_up, jnp.zeros((ICLR_LORA, w), F32)], axis=0)
    aup = jnp.concatenate([jnp.zeros((DECAY_LORA, w), F32), a_up], axis=0)
    hid = jnp.arange(LANES) // HEAD_DIM
    e_bf = (hid[:, None] == hid[None, :]).astype(BF16)
    t = jnp.arange(c)
    p = (c - 1 - t) if rev else t
    le = p[None, :] <= p[:, None]
    lt = p[None, :] < p[:, None]
    strict = jnp.tile(lt, (1, 2)).astype(F32)
    incl = jnp.tile(le, (1, 2)).astype(F32)
    minc = jnp.tile(le, (1, 3)).astype(BF16)
    return mu.reshape(1, -1), vecs, wup, aup, e_bf, minc, strict, incl


def _lru_kernel(x_ref, cw_ref, vec_ref, wa_ref, wx_ref, h_ref, hcar_ref, ucar_ref, *, rev, n_ctx):
    i = pl.program_id(0)
    nb, c, w = x_ref.shape
    rows = nb * c

    @pl.when(i == 0)
    def _():
        hcar_ref[...] = jnp.zeros_like(hcar_ref)

    @pl.when((i == 0) | (i == n_ctx))
    def _():
        ucar_ref[...] = jnp.zeros_like(ucar_ref)

    u0 = x_ref[...].astype(F32).reshape(rows, w)
    row = lax.broadcasted_iota(jnp.int32, (rows, w), 0) % c

    def per_batch(ref, j):
        return jnp.concatenate([jnp.broadcast_to(ref[b, j:j + 1, :], (c, w)) for b in range(nb)], axis=0)

    def shifted(x, s, carry, fill):
        if rev:
            rolled = pltpu.roll(x, rows - s, axis=0)
            edge = row >= c - s
        else:
            rolled = pltpu.roll(x, s, axis=0)
            edge = row < s
        if carry is None:
            return jnp.where(edge, fill, rolled)
        return jnp.where(edge, carry, rolled)

    conv = vec_ref[0:1, :] + cw_ref[LRU_CONV - 1:LRU_CONV, :] * u0
    for m in range(1, LRU_CONV):
        car = jnp.zeros((rows, w), F32)
        for qpos in range(m):
            r_idx = (c - 1 - qpos) if rev else qpos
            car = jnp.where(row == r_idx, per_batch(ucar_ref, m - qpos - 1), car)
        conv = conv + cw_ref[LRU_CONV - 1 - m:LRU_CONV - m, :] * shifted(u0, m, car, None)
    for m in range(1, LRU_CONV):
        r_idx = (m - 1) if rev else (c - m)
        for b in range(nb):
            ucar_ref[b, m - 1:m, :] = u0[b * c + r_idx:b * c + r_idx + 1, :]

    cb = conv.astype(BF16)
    r = _sigmoid(_dot(cb, wa_ref[...]) + vec_ref[1:2, :])
    ig = _sigmoid(_dot(cb, wx_ref[...]) + vec_ref[2:3, :])
    log_a = -LRU_C * r * vec_ref[3:4, :]
    a = jnp.exp(log_a)
    bb = jnp.sqrt(1.0 - jnp.exp(2.0 * log_a)) * (ig * conv)

    s = 1
    while s < c:
        bb = bb + a * shifted(bb, s, None, 0.0)
        a = a * shifted(a, s, None, 1.0)
        s *= 2
    h = bb + a * per_batch(hcar_ref, 0)
    h_ref[...] = h.reshape(nb, c, w).astype(h_ref.dtype)
    last = 0 if rev else c - 1
    for b in range(nb):
        hcar_ref[b, 0:1, :] = h[b * c + last:b * c + last + 1, :]


def _lru(p_lru, prm, n_ctx, rev):
    b, t, w2 = p_lru.shape
    w = w2 // 2
    c = LRU_CHUNK
    n_tot = t // c
    cw, vecs, wa, wx = prm
    tix = lambda i: _scan_chunk(i, n_ctx, n_tot, rev)
    const = lambda i: (0, 0)
    return pl.pallas_call(
        functools.partial(_lru_kernel, rev=rev, n_ctx=n_ctx),
        grid=(n_tot,),
        in_specs=[pl.BlockSpec((b, c, w), lambda i: (0, tix(i), 0)),
                  pl.BlockSpec(cw.shape, const), pl.BlockSpec(vecs.shape, const),
                  pl.BlockSpec(wa.shape, const), pl.BlockSpec(wx.shape, const)],
        out_specs=pl.BlockSpec((b, c, w), lambda i: (0, tix(i), 0)),
        out_shape=jax.ShapeDtypeStruct((b, t, w), BF16),
        scratch_shapes=[pltpu.VMEM((b, 8, w), F32), pltpu.VMEM((b, 8, w), F32)],
        compiler_params=_cparams("arbitrary"),
        name="rglru_rev" if rev else "rglru_fwd",
    )(p_lru, cw, vecs, wa, wx)


def _lru_params(conv_w, conv_b, wa, ba, wx, bx, lam):
    w = conv_b.shape[0]
    cw = jnp.concatenate([conv_w, jnp.zeros((8 - LRU_CONV, w), F32)], axis=0)
    vecs = jnp.concatenate([jnp.stack([conv_b, ba, bx, jax.nn.softplus(-lam)]), jnp.zeros((4, w), F32)], axis=0)
    return cw, vecs, jax.scipy.linalg.block_diag(*wa).astype(BF16), jax.scipy.linalg.block_diag(*wx).astype(BF16)


def _head_norm(y, e2, gain, bias, eps):
    inv = 1.0 / HEAD_DIM

    def head_sums(x):
        xb = x.astype(BF16)
        return jnp.concatenate([_dot(xb[:, j:j + LANES], e2) for j in range(0, x.shape[1], LANES)], axis=1)

    yc = y - head_sums(y) * inv
    var = head_sums(yc * yc) * inv
    return yc * lax.rsqrt(var + eps) * gain + bias


def _mix_out_kernel(x_ref, mod_ref, of_ref, ob_ref, g_ref, yf_ref, yb_ref, bf_ref, bb_ref, gd_ref,
                    hf_ref, hb_ref, lg_ref, gn_ref, e_ref, gup_ref, wout_ref, n2g_ref, wr_ref, br_ref, tri_ref,
                    xo_ref, h2_ref, idx_ref, gate_ref, rank_ref, cnt_ref, base_ref, *, w_ret, w_rw):
    @pl.when((pl.program_id(0) == 0) & (pl.program_id(1) == 0))
    def _():
        base_ref[...] = jnp.zeros_like(base_ref)

    e_bf = e_ref[...]
    f32 = lambda ref: ref[0].astype(F32)
    g = f32(g_ref)
    ret = _head_norm(f32(of_ref) + f32(ob_ref), e_bf, gn_ref[0:1, :], gn_ref[1:2, :], RET_GN_EPS)
    ret = ret * (g * _sigmoid(g))
    gate = _dot(_sigmoid(f32(gd_ref)).astype(BF16), gup_ref[...])
    rw = _head_norm(f32(yf_ref) + f32(yb_ref), e_bf, gn_ref[2:3, :], gn_ref[3:4, :], RWKV_GN_EPS)
    rw = (rw + f32(bf_ref) + f32(bb_ref)) * gate
    lg = f32(lg_ref)
    gelu = 0.5 * lg * (1.0 + jnp.tanh(0.7978845608028654 * (lg + 0.044715 * (lg * lg * lg))))
    lru = (f32(hf_ref) + f32(hb_ref)) * gelu
    mix = (_dot(ret.astype(BF16), wout_ref[0:w_ret, :])
           + _dot(rw.astype(BF16), wout_ref[w_ret:w_ret + w_rw, :])
           + _dot(lru.astype(BF16), wout_ref[w_ret + w_rw:, :]))
    x = x_ref[0] + mod_ref[0, 0, 0:1, :] * mix
    xo_ref[0] = x
    ms = jnp.mean(x * x, axis=-1, keepdims=True)
    h2 = x * lax.rsqrt(ms + NORM_EPS) * n2g_ref[...]
    h2 = h2 * (1.0 + mod_ref[0, 0, 2:3, :]) + mod_ref[0, 0, 1:2, :]
    h2_ref[0] = h2.astype(BF16)
    logits = _dot_x3(h2, wr_ref[...]) + br_ref[...]
    idx_o, gate_o, rank_o = _route_tile(logits, tri_ref[...], base_ref)
    idx_ref[0] = idx_o
    gate_ref[0] = gate_o
    rank_ref[0] = rank_o
    cnt_ref[...] = base_ref[...].astype(jnp.int32)


def _mix_out(xs, mod, o_f, o_b, p_ret, y_f, y_b, bon_f, bon_b, p_gd, h_f, h_b, p_lru,
             gn, e_bf, g_up_bf, w_out_bf, norm2_g, w_router, b_router, n_ctx_tiles):
    b, t, d = xs.shape
    tm = TOKEN_TILE
    w_ret, w_rw, w_lru = o_f.shape[2], y_f.shape[2], h_f.shape[2]
    ne = w_router.shape[1]
    tok = lambda wd, j=0: pl.BlockSpec((1, tm, wd), lambda bi, i: (bi, i, j))
    const = lambda a: pl.BlockSpec(a.shape, lambda bi, i: (0,) * a.ndim)
    seg = lambda bi, i: (bi, jnp.where(i >= n_ctx_tiles, 1, 0), 0, 0)
    n2g = norm2_g.reshape(1, d)
    br = b_router.reshape(1, ne)
    tt = jnp.arange(tm)
    tri = (tt[None, :] < tt[:, None]).astype(BF16)
    return pl.pallas_call(
        functools.partial(_mix_out_kernel, w_ret=w_ret, w_rw=w_rw),
        grid=(b, t // tm),
        in_specs=[tok(d), pl.BlockSpec((1, 1, 3, d), seg),
                  tok(w_ret), tok(w_ret), tok(w_ret, 3),
                  tok(w_rw), tok(w_rw), tok(w_rw), tok(w_rw), tok(GATE_LORA),
                  tok(w_lru), tok(w_lru), tok(w_lru, 1),
                  const(gn), const(e_bf), const(g_up_bf), const(w_out_bf), const(n2g), const(w_router), const(br),
                  const(tri)],
        out_specs=[tok(d), tok(d), tok(LANES), tok(LANES), tok(LANES),
                   pl.BlockSpec((8, ne), lambda bi, i: (0, 0))],
        out_shape=[jax.ShapeDtypeStruct((b, t, d), F32), jax.ShapeDtypeStruct((b, t, d), BF16),
                   jax.ShapeDtypeStruct((b, t, LANES), jnp.int32), jax.ShapeDtypeStruct((b, t, LANES), F32),
                   jax.ShapeDtypeStruct((b, t, LANES), jnp.int32), jax.ShapeDtypeStruct((8, ne), jnp.int32)],
        scratch_shapes=[pltpu.VMEM((8, ne), F32)],
        compiler_params=_cparams("arbitrary", "arbitrary"),
        name="mix_out",
    )(xs, mod, o_f, o_b, p_ret, y_f, y_b, bon_f, bon_b, p_gd, h_f, h_b, p_lru,
      gn, e_bf, g_up_bf, w_out_bf, n2g, w_router, br, tri)


def _moe_kernel(be_ref, first_ref, nu_ref, *refs, blk0, chained):
    x_ref, w1_ref, b1_ref, w2_ref, b2_ref = refs[1:6] if chained else refs[0:5]
    y_ref, w1b_ref, w2b_ref = refs[-3:]
    i = pl.program_id(0)
    blk = i + blk0
    de = w2_ref.shape[1]

    @pl.when((first_ref[blk] == 1) | (i == 0))
    def _():
        w1b_ref[...] = w1_ref[0].astype(BF16)
        w2b_ref[...] = w2_ref[0].astype(BF16)

    @pl.when(blk < nu_ref[0])
    def _():
        gu = _dot(x_ref[...], w1b_ref[...]) + b1_ref[0]
        glu = jnp.minimum(gu[:, :de], SWIGLU_LIMIT)
        lin = jnp.clip(gu[:, de:], -SWIGLU_LIMIT, SWIGLU_LIMIT)
        act = glu * _sigmoid(SWIGLU_ALPHA * glu) * (lin + 1.0)
        y_ref[...] = (_dot(act.astype(BF16), w2b_ref[...]) + b2_ref[0]).astype(y_ref.dtype)

    @pl.when(blk >= nu_ref[0])
    def _():
        y_ref[...] = jnp.zeros_like(y_ref)


def _moe_ffn(hb, block_e, first, n_used, w1, b1, w2, b2, layer, blk0, n_slots, y_prev=None):
    rows, d = hb.shape
    tm = MOE_TILE
    nl, ne, _, d2 = w1.shape
    de = w2.shape[2]
    chained = y_prev is not None
    wsel = lambda i, be, fi, nu: (layer, be[i + blk0], 0, 0)
    in_specs = [pl.BlockSpec((tm, d), lambda i, be, fi, nu: (i, 0)),
                pl.BlockSpec((None, 1, d, d2), wsel),
                pl.BlockSpec((None, 1, 1, d2), wsel),
                pl.BlockSpec((None, 1, de, d), wsel),
                pl.BlockSpec((None, 1, 1, d), wsel)]
    args = [hb, w1, b1.reshape(nl, ne, 1, d2), w2, b2.reshape(nl, ne, 1, d)]
    if chained:
        in_specs = [pl.BlockSpec(memory_space=pl.ANY)] + in_specs
        args = [y_prev] + args
    return pl.pallas_call(
        functools.partial(_moe_kernel, blk0=blk0, chained=chained),
        grid_spec=pltpu.PrefetchScalarGridSpec(
            num_scalar_prefetch=3,
            grid=(rows // tm,),
            in_specs=in_specs,
            out_specs=pl.BlockSpec((tm, d), lambda i, be, fi, nu: (i + blk0, 0)),
            scratch_shapes=[pltpu.VMEM((d, d2), BF16), pltpu.VMEM((de, d), BF16)],
        ),
        out_shape=jax.ShapeDtypeStruct((n_slots, d), BF16),
        input_output_aliases={3: 0} if chained else {},
        compiler_params=_cparams("arbitrary"),
        name="moe_ffn",
    )(block_e, first, n_used, *args)


def _route_tile(lg, tri, base_ref):
    tr, ne = lg.shape
    lane = lax.broadcasted_iota(jnp.int32, (tr, ne), 1).astype(F32)
    out_lane = lax.broadcasted_iota(jnp.int32, (tr, LANES), 1)
    vals = lg
    sel = jnp.zeros((tr, ne), F32)
    picks, tops = [], []
    for _ in range(TOP_K):
        m = jnp.max(vals, axis=-1, keepdims=True)
        ix = jnp.min(jnp.where(vals == m, lane, float(ne)), axis=-1, keepdims=True)
        hit = lane == ix
        sel = jnp.where(hit, 1.0, sel)
        vals = jnp.where(hit, -jnp.inf, vals)
        picks.append(ix)
        tops.append(m)
    ex = [jnp.exp(t - tops[0]) for t in tops]
    den = ex[0] + ex[1] + ex[2] + ex[3]
    before = _dot(tri, sel.astype(BF16)) + base_ref[0:1, :]
    idx_o = jnp.zeros((tr, LANES), F32)
    gate_o = jnp.zeros((tr, LANES), F32)
    rank_o = jnp.zeros((tr, LANES), F32)
    for k in range(TOP_K):
        rk = jnp.sum(jnp.where(lane == picks[k], before, 0.0), axis=-1, keepdims=True)
        idx_o = jnp.where(out_lane == k, picks[k], idx_o)
        gate_o = jnp.where(out_lane == k, ex[k] / den, gate_o)
        rank_o = jnp.where(out_lane == k, rk, rank_o)
    total = base_ref[0:1, :] + jnp.sum(sel, axis=0, keepdims=True)
    base_ref[...] = jnp.broadcast_to(total, base_ref.shape)
    return idx_o.astype(jnp.int32), gate_o, rank_o.astype(jnp.int32)


def _route_meta(idx, rank, counts):
    n_tok = idx.shape[0]
    ne = counts.shape[0]
    tm = MOE_TILE
    n_assign = n_tok * TOP_K
    padded = (counts + tm - 1) // tm * tm
    pend = jnp.cumsum(padded)
    pstart = pend - padded
    start = jnp.cumsum(counts) - counts
    eid = jnp.arange(ne, dtype=jnp.int32)
    slot = jnp.sum(jnp.where(idx[..., None] == eid, pstart, 0), axis=-1).astype(jnp.int32) + rank
    n_blocks = (n_assign + ne * (tm - 1) + tm - 1) // tm
    blk_start = jnp.arange(n_blocks, dtype=jnp.int32) * tm
    block_e = jnp.minimum(jnp.sum(pend[None, :] <= blk_start[:, None], axis=1), ne - 1).astype(jnp.int32)
    first = jnp.concatenate([jnp.ones((1,), jnp.int32), (block_e[1:] != block_e[:-1]).astype(jnp.int32)])
    n_used = (pend[-1] // tm).astype(jnp.int32).reshape(1)
    assert n_tok <= 1 << 16 and ne <= 1 << 15
    tok_id = jnp.arange(n_tok, dtype=jnp.int32)[:, None]
    order_tok = jnp.sort((idx * (1 << 16) + tok_id).reshape(-1)) & ((1 << 16) - 1)
    off = jnp.arange(n_blocks * tm, dtype=jnp.int32) - jnp.repeat(pstart[block_e], tm)
    valid = off < jnp.repeat(counts[block_e], tm)
    pos = jnp.clip(jnp.repeat(start[block_e], tm) + off, 0, n_assign - 1)
    spread = jnp.arange(n_blocks * tm, dtype=jnp.int32) % n_tok
    slot_tok = jnp.where(valid, order_tok[pos], spread).astype(jnp.int32)
    return slot, slot_tok, block_e, first, n_used


def _combine_kernel(x_ref, mod_ref, y_ref, gate_ref, g_ref, *rest, final):
    o_ref = rest[-1]
    gate = gate_ref[0]
    y = y_ref[0, 0].astype(F32) * gate[:, 0:1]
    for k in range(1, TOP_K):
        y = y + y_ref[k, 0].astype(F32) * gate[:, k:k + 1]
    x = x_ref[0] + mod_ref[0, 0, 0:1, :] * y
    if final:
        ms = jnp.mean(x * x, axis=-1, keepdims=True)
        x = x * lax.rsqrt(ms + NORM_EPS) * g_ref[...]
    o_ref[0] = x


def _combine(xs, mod, yg, gates, final_g, n_ctx_tiles, final, b0, prev=None):
    b, t, d = xs.shape
    nbh = yg.shape[1]
    tm = TOKEN_TILE
    skip = n_ctx_tiles if final else 0
    seg = lambda bi, i: (bi + b0, jnp.where(i + skip >= n_ctx_tiles, 1, 0), 0, 0)
    in_specs = [pl.BlockSpec((1, tm, d), lambda bi, i: (bi + b0, i + skip, 0)),
                pl.BlockSpec((1, 1, 1, d), seg),
                pl.BlockSpec((TOP_K, 1, tm, d), lambda bi, i: (0, bi, i + skip, 0)),
                pl.BlockSpec((1, tm, LANES), lambda bi, i: (bi + b0, i + skip, 0)),
                pl.BlockSpec((1, d), lambda bi, i: (0, 0))]
    args = [xs, mod, yg, gates, final_g.reshape(1, d)]
    aliases = {} if final else {0: 0}
    if prev is not None:
        in_specs.append(pl.BlockSpec(memory_space=pl.ANY))
        args.append(prev)
        aliases = {5: 0}
    return pl.pallas_call(
        functools.partial(_combine_kernel, final=final),
        grid=(nbh, t // tm - skip),
        in_specs=in_specs,
        out_specs=pl.BlockSpec((1, tm, d), lambda bi, i: (bi + b0, i, 0)),
        out_shape=jax.ShapeDtypeStruct((b, t - skip * tm, d), F32),
        input_output_aliases=aliases,
        compiler_params=_cparams("parallel", "parallel"),
        name="combine_final" if final else "combine",
    )(*args)


def kernel(x, c, ctx, c_ctx, w_mod, b_mod, norm1_g, norm2_g, w_in, w_out, ret_decay_logit, ret_gn_g, ret_gn_b, rwkv_mu, rwkv_w0, rwkv_w_up, rwkv_a0, rwkv_a_up, rwkv_k_k, rwkv_k_a, rwkv_g_up, rwkv_r_k, rwkv_gn_g, rwkv_gn_b, lru_conv_w, lru_conv_b, lru_wa, lru_ba, lru_wx, lru_bx, lru_lambda, moe_w_router, moe_b_router, moe_w1, moe_b1, moe_w2, moe_b2, final_norm_g):
    bsz, seq, dm = x.shape
    n_ctx_tok = ctx.shape[1]
    depth = w_in.shape[0]
    n_experts = moe_w_router.shape[2]
    w_ret = 3 * dm // 8
    w_rw = 3 * dm // 8
    w_lru = dm - w_ret - w_rw
    zw = 3 * w_rw + DECAY_LORA + ICLR_LORA
    sizes = (4 * w_ret, zw, GATE_LORA, 2 * w_lru)
    bounds, off = [], 0
    for s in sizes:
        bounds.append((off, off + s))
        off += s
    bounds = tuple(bounds)
    assert off == w_in.shape[2]
    assert n_ctx_tok % TOKEN_TILE == 0 and seq % TOKEN_TILE == 0 and seq % GRID_W == 0
    t_all = n_ctx_tok + seq
    n_ctx_tiles = n_ctx_tok // TOKEN_TILE

    xs = jnp.concatenate([ctx, x], axis=1)
    cos_t, sin_t = _rope_tables(n_ctx_tok, seq, w_ret)
    hid = jnp.arange(LANES) // HEAD_DIM
    e_bf = (hid[:, None] == hid[None, :]).astype(BF16)
    cond =jnp.concatenate([c, c_ctx[None, :], jnp.zeros((8 - (bsz + 1) % 8, dm), F32)], axis=0)

    for l in range(depth):
        last = l == depth - 1
        mod = _modulation(cond, w_mod, b_mod, l)
        mod_l = mod[:bsz].reshape(bsz, 6, dm)
        mod_c = jnp.broadcast_to(mod[bsz].reshape(1, 6, dm), (bsz, 6, dm))
        modsel = jnp.stack([mod_c, mod_l], axis=1)

        p_ret, p_z, p_gd, p_lru = _in_proj(xs, modsel[:, :, 0:2], norm1_g[l], w_in[l].astype(BF16), bounds,
                                           (BF16, F32, BF16, BF16), n_ctx_tiles)

        ret_o, rw_y, rw_bon, lru_h = [], [], [], []
        for d in range(2):
            rev = d == 1
            ret_o.append(_retention(p_ret, cos_t, sin_t, _ret_tables(ret_decay_logit[l, d], w_ret, rev),
                                    n_ctx_tok // RET_CHUNK, rev))
            prm = _rwkv_params(rwkv_mu[l, d], rwkv_w0[l, d], rwkv_w_up[l, d], rwkv_a0[l, d], rwkv_a_up[l, d],
                               rwkv_k_k[l, d], rwkv_k_a[l, d], rwkv_r_k[l], rev, bsz)
            y, bon = _rwkv(p_z, prm, n_ctx_tok // RWKV_CHUNK, rev)
            rw_y.append(y)
            rw_bon.append(bon)
            lru_h.append(_lru(p_lru, _lru_params(lru_conv_w[l, d], lru_conv_b[l, d], lru_wa[l, d], lru_ba[l, d],
                                                 lru_wx[l, d], lru_bx[l, d], lru_lambda[l, d]),
                              n_ctx_tok // LRU_CHUNK, rev))

        gn = jnp.concatenate([jnp.stack([ret_gn_g[l], ret_gn_b[l], rwkv_gn_g[l], rwkv_gn_b[l]]),
                              jnp.zeros((4, w_ret), F32)], axis=0)
        xs, h2, idx, gates, rank, counts = _mix_out(
            xs, modsel[:, :, 2:5], ret_o[0], ret_o[1], p_ret, rw_y[0], rw_y[1],
            rw_bon[0], rw_bon[1], p_gd, lru_h[0], lru_h[1], p_lru,
            gn, e_bf, rwkv_g_up[l].astype(BF16), w_out[l].astype(BF16), norm2_g[l],
            moe_w_router[l], moe_b_router[l], n_ctx_tiles)

        n_tok = bsz * t_all
        slot, slot_tok, block_e, first, n_used = _route_meta(
            idx.reshape(n_tok, LANES)[:, :TOP_K], rank.reshape(n_tok, LANES)[:, :TOP_K], counts[0])
        n_slots = slot_tok.shape[0]
        row_a = max(n_slots // MOE_TILE // 4, 1) * MOE_TILE
        h2f = h2.reshape(n_tok, dm)
        y_sorted = None
        for lo, hi in ((0, row_a), (row_a, n_slots)):
            y_sorted = _moe_ffn(h2f[slot_tok[lo:hi]], block_e, first, n_used, moe_w1, moe_b1, moe_w2, moe_b2, l,
                                lo // MOE_TILE, n_slots, y_prev=y_sorted)
        slot_b = slot.reshape(bsz, t_all, TOP_K)
        out = None
        for lo, hi in [(b, b + 1) for b in range(bsz)]:
            yg = y_sorted[jnp.moveaxis(slot_b[lo:hi], 2, 0)]
            out = _combine(xs, modsel[:, :, 5:6], yg, gates, final_norm_g, n_ctx_tiles, last, lo,
                           prev=out if last else None)
            xs = xs if last else out
        xs = out
    return xs
```

```python
import functools

import jax
import jax.numpy as jnp
from jax import lax
from jax.experimental import pallas as pl
from jax.experimental.pallas import tpu as pltpu

F32 = jnp.float32
BF16 = jnp.bfloat16

HEAD_DIM = 64
NORM_EPS = 1e-6
RET_GN_EPS = 1e-5
RWKV_GN_EPS = 64e-5
ROPE_BASE = 10000.0
GRID_W = 64
LRU_CONV = 4
LRU_C = 8.0
TOP_K = 4
SWIGLU_LIMIT = 7.0
SWIGLU_ALPHA = 1.702
DECAY_LORA = 64
ICLR_LORA = 64
GATE_LORA = 128
DECAY_SCALE = 0.6065306597126334

LANES = 128
TOKEN_TILE = 256
RET_CHUNK = 128
RWKV_CHUNK = 64
LRU_CHUNK = 128
MOE_TILE = 512
MIX_ROWS = 2
VMEM_LIMIT = 56 * 1024 * 1024


def _cparams(*sem):
    return pltpu.CompilerParams(dimension_semantics=sem, vmem_limit_bytes=VMEM_LIMIT)


def _scan_chunk(i, n_ctx, n_tot, rev):
    if not rev:
        return i
    return jnp.where(i < n_ctx, n_ctx - 1 - i, n_tot + n_ctx - 1 - i)


def _split3(a):
    hi = a.astype(BF16)
    r1 = a - hi.astype(F32)
    mid = r1.astype(BF16)
    lo = (r1 - mid.astype(F32)).astype(BF16)
    return hi, mid, lo


def _dot(a, b):
    return jnp.dot(a, b, preferred_element_type=F32)


def _dot_nt(a, b):
    return lax.dot_general(a, b, (((1,), (1,)), ((), ())), preferred_element_type=F32)


def _dot_tn(a, b):
    return lax.dot_general(a, b, (((0,), (0,)), ((), ())), preferred_element_type=F32)


def _dot_exact_rhs(a, b_bf):
    hi, mid, lo = _split3(a)
    return _dot(hi, b_bf) + _dot(mid, b_bf) + _dot(lo, b_bf)


def _dot_exact_lhs(a_bf, b):
    hi, mid, lo = _split3(b)
    return _dot(a_bf, hi) + _dot(a_bf, mid) + _dot(a_bf, lo)


def _dot_x3(a, b):
    a_hi = a.astype(BF16)
    a_lo = (a - a_hi.astype(F32)).astype(BF16)
    b_hi = b.astype(BF16)
    b_lo = (b - b_hi.astype(F32)).astype(BF16)
    return _dot(a_hi, b_hi) + _dot(a_lo, b_hi) + _dot(a_hi, b_lo)


def _dot_x3k(a, b):
    a_hi = a.astype(BF16)
    a_lo = (a - a_hi.astype(F32)).astype(BF16)
    b_hi = b.astype(BF16)
    b_lo = (b - b_hi.astype(F32)).astype(BF16)
    return (_dot(jnp.concatenate([a_hi, a_lo], axis=1), jnp.concatenate([b_hi, b_hi], axis=0))
            + _dot(a_hi, b_lo))


def _sigmoid(x):
    return 1.0 / (1.0 + jnp.exp(-x))


def _mod_kernel(c_ref, w_ref, b_ref, o_ref):
    c = c_ref[...]
    s = c * _sigmoid(c)
    o_ref[...] = _dot_x3(s, w_ref[...]) + b_ref[...]


def _modulation(cond, w_mod, b_mod, layer):
    r, d = cond.shape
    nl, _, n = w_mod.shape
    tn = d
    return pl.pallas_call(
        _mod_kernel,
        grid=(n // tn,),
        in_specs=[pl.BlockSpec((r, d), lambda j: (0, 0)),
                  pl.BlockSpec((None, d, tn), lambda j: (layer, 0, j)),
                  pl.BlockSpec((None, 1, tn), lambda j: (layer, 0, j))],
        out_specs=pl.BlockSpec((r, tn), lambda j: (0, j)),
        out_shape=jax.ShapeDtypeStruct((r, n), F32),
        compiler_params=_cparams("arbitrary"),
        name="modulation",
    )(cond, w_mod, b_mod.reshape(nl, 1, n))


def _in_proj_kernel(x_ref, mod_ref, g_ref, w_ref, *o_refs, bounds):
    x = x_ref[0]
    ms = jnp.mean(x * x, axis=-1, keepdims=True)
    h = x * lax.rsqrt(ms + NORM_EPS) * g_ref[...]
    h = h * (1.0 + mod_ref[0, 0, 1:2, :]) + mod_ref[0, 0, 0:1, :]
    hb = h.astype(BF16)
    for o_ref, (lo, hi) in zip(o_refs, bounds):
        o_ref[0] = _dot(hb, w_ref[:, lo:hi]).astype(o_ref.dtype)


def _in_proj(xs, mod, norm_g, w_in_bf, bounds, dtypes, n_ctx_tiles):
    b, t, d = xs.shape
    tm = TOKEN_TILE
    p = w_in_bf.shape[1]
    seg = lambda bi, i: (bi, jnp.where(i >= n_ctx_tiles, 1, 0), 0, 0)
    return pl.pallas_call(
        functools.partial(_in_proj_kernel, bounds=bounds),
        grid=(b, t // tm),
        in_specs=[pl.BlockSpec((1, tm, d), lambda bi, i: (bi, i, 0)),
                  pl.BlockSpec((1, 1, 2, d), seg),
                  pl.BlockSpec((1, d), lambda bi, i: (0, 0)),
                  pl.BlockSpec((d, p), lambda bi, i: (0, 0))],
        out_specs=[pl.BlockSpec((1, tm, hi - lo), lambda bi, i: (bi, i, 0)) for lo, hi in bounds],
        out_shape=[jax.ShapeDtypeStruct((b, t, hi - lo), dt) for (lo, hi), dt in zip(bounds, dtypes)],
        compiler_params=_cparams("parallel", "parallel"),
        name="in_proj",
    )(xs, mod, norm_g.reshape(1, d), w_in_bf)


def _ret_kernel(q_ref, k_ref, v_ref, cos_ref, sin_ref, dq_ref, dk_ref, dmat_ref, gm_ref, bm_ref,
                o_ref, s_ref):
    i = pl.program_id(0)

    @pl.when(i == 0)
    def _():
        s_ref[...] = jnp.zeros_like(s_ref)

    nb, c, w = q_ref.shape
    cos = cos_ref[...]
    sin = sin_ref[...]
    lane = lax.broadcasted_iota(jnp.int32, (c, LANES), 1)
    first = (lane % 32) < 16

    def rope(u):
        parts = []
        for j in range(w // LANES):
            uj = u[:, j * LANES:(j + 1) * LANES]
            nxt = pltpu.roll(uj, LANES - 16, axis=1)
            prv = pltpu.roll(uj, 16, axis=1)
            parts.append(jnp.where(first, nxt, prv))
        return u * cos + jnp.concatenate(parts, axis=1) * sin

    lane_lo = lane < HEAD_DIM

    def stack(xw):
        return jnp.concatenate([jnp.where(lane_lo, xw, 0.0), jnp.where(lane_lo, 0.0, xw)], axis=0)

    n_pairs = w // LANES
    q = [rope(q_ref[b].astype(F32)) for b in range(nb)]
    k = [rope(k_ref[b].astype(F32)) for b in range(nb)]
    chains = [(b, j) for b in range(nb) for j in range(n_pairs)]
    pair = lambda x, j: x[:, j * LANES:(j + 1) * LANES]
    qw = [pair(q[b], j) for b, j in chains]
    kw = [pair(k[b], j) for b, j in chains]
    vw = [pair(v_ref[b], j) for b, j in chains]
    s = [s_ref[b * n_pairs + j] for b, j in chains]
    inter = [_dot((x * pair(dq_ref[...], j)).astype(BF16), st.astype(BF16)) for x, st, (b, j) in zip(qw, s, chains)]
    sc = [_dot_nt(x.astype(BF16), stack(y).astype(BF16)) * dmat_ref[j]
          for x, y, (b, j) in zip(qw, kw, chains)]
    intra = [_dot(x.astype(BF16), stack(y).astype(BF16)) for x, y in zip(sc, vw)]
    ktv = [_dot_tn((y * pair(dk_ref[...], j)).astype(BF16), z.astype(BF16))
           for y, z, (b, j) in zip(kw, vw, chains)]
    for n, (b, j) in enumerate(chains):
        o_ref[b, :, j * LANES:(j + 1) * LANES] = (inter[n] + intra[n]).astype(o_ref.dtype)
        s_ref[b * n_pairs + j] = gm_ref[j] * s[n] + bm_ref[...] * ktv[n]


def _retention(p_ret, cos_t, sin_t, tabs, n_ctx, rev):
    b, t, w4 = p_ret.shape
    w = w4 // 4
    c = RET_CHUNK
    n_tot = t // c
    n_pairs = w // LANES
    dq, dk, dmat, gm, bm = tabs
    tix = lambda i: _scan_chunk(i, n_ctx, n_tot, rev)
    col = lambda j: (lambda i: (0, tix(i), j))
    const = lambda a: pl.BlockSpec(a.shape, lambda i: (0,) * a.ndim)
    return pl.pallas_call(
        _ret_kernel,
        grid=(n_tot,),
        in_specs=[pl.BlockSpec((b, c, w), col(0)), pl.BlockSpec((b, c, w), col(1)), pl.BlockSpec((b, c, w), col(2)),
                  pl.BlockSpec((c, w), lambda i: (tix(i), 0)),
                  pl.BlockSpec((c, w), lambda i: (tix(i), 0)),
                  const(dq), const(dk), const(dmat), const(gm), const(bm)],
        out_specs=pl.BlockSpec((b, c, w), lambda i: (0, tix(i), 0)),
        out_shape=jax.ShapeDtypeStruct((b, t, w), BF16),
        scratch_shapes=[pltpu.VMEM((b * n_pairs, LANES, LANES), F32)],
        compiler_params=_cparams("arbitrary"),
        name="retention_rev" if rev else "retention_fwd",
    )(p_ret, p_ret, p_ret, cos_t, sin_t, dq, dk, dmat, gm, bm)


def _ret_tables(decay_logit, w, rev):
    n_heads = w // HEAD_DIM
    c = RET_CHUNK
    lg = jax.nn.log_sigmoid(decay_logit.astype(F32))
    t = jnp.arange(c, dtype=F32)
    p = (c - 1.0 - t) if rev else t
    rel = p[:, None] - p[None, :]
    scale = HEAD_DIM ** -0.5
    dmat = jnp.where(rel >= 0, jnp.exp(lg[:, None, None] * jnp.maximum(rel, 0.0)), 0.0) * scale
    dq = jnp.exp(lg[:, None] * (p + 1.0)) * scale
    dk = jnp.exp(lg[:, None] * (c - 1.0 - p))
    lanes = lambda a: jnp.repeat(a.T, HEAD_DIM, axis=1)
    n_pairs = n_heads // 2
    dmat_w = dmat.reshape(n_pairs, 2, c, c).transpose(0, 2, 1, 3).reshape(n_pairs, c, 2 * c)
    hid = jnp.arange(LANES) // HEAD_DIM
    bm = (hid[:, None] == hid[None, :]).astype(F32)
    gm = bm[None] * jnp.exp(lg * c).reshape(n_pairs, 2)[:, hid][:, :, None]
    return lanes(dq), lanes(dk), dmat_w, gm, bm


def _rope_tables(n_ctx_tok, seq, w):
    half = HEAD_DIM // 2
    quarter = half // 2
    inv_freq = ROPE_BASE ** (-jnp.arange(quarter, dtype=F32) / quarter)
    tok = jnp.arange(seq)
    rows = (tok // GRID_W).astype(F32)
    cols = (tok % GRID_W).astype(F32)
    o = jnp.arange(w) % HEAD_DIM
    pos = jnp.where(o[None, :] < half, rows[:, None], cols[:, None])
    ang = pos * inv_freq[o % quarter][None, :]
    sign = jnp.where((o % half) < quarter, -1.0, 1.0)[None, :]
    cos = jnp.concatenate([jnp.ones((n_ctx_tok, w), F32), jnp.cos(ang)], axis=0)
    sin = jnp.concatenate([jnp.zeros((n_ctx_tok, w), F32), jnp.sin(ang) * sign], axis=0)
    return cos, sin


def _rwkv_kernel(z_ref, mu_ref, vec_ref, wup_ref, aup_ref, e_ref, minc_ref, strict_ref, incl_ref,
                 y_ref, bon_ref, st_ref, zprev_ref, *, rev, n_ctx, w):
    i = pl.program_id(0)
    nb, c, zw = z_ref.shape
    n_pairs = w // LANES
    rows = nb * c

    @pl.when(i == 0)
    def _():
        st_ref[...] = jnp.zeros_like(st_ref)

    @pl.when((i == 0) | (i == n_ctx))
    def _():
        zprev_ref[...] = jnp.zeros_like(zprev_ref)

    w0, a0, k_k, k_a, r_k = (vec_ref[j:j + 1, :] for j in range(5))
    e2 = e_ref[...]
    e22 = jnp.concatenate([e2, e2], axis=0)
    strict = strict_ref[...] > 0.0
    incl = incl_ref[...] > 0.0
    lane_lo = lax.broadcasted_iota(jnp.int32, (c, LANES), 1) < HEAD_DIM
    head_r = lax.broadcasted_iota(jnp.int32, (LANES, LANES), 0) // HEAD_DIM
    head_c = lax.broadcasted_iota(jnp.int32, (LANES, LANES), 1) // HEAD_DIM
    diag = head_r == head_c
    last = 0 if rev else c - 1

    def head_sums(x, pieces):
        outs = []
        for j in range(n_pairs):
            xj = x[:, j * LANES:(j + 1) * LANES]
            if pieces == 1:
                outs.append(_dot(xj.astype(BF16), e2))
            else:
                hi = xj.astype(BF16)
                mid = (xj - hi.astype(F32)).astype(BF16)
                outs.append(_dot(jnp.concatenate([hi, mid], axis=1), e22))
        return jnp.concatenate(outs, axis=1)

    def stack(xw):
        return jnp.concatenate([jnp.where(lane_lo, xw, 0.0), jnp.where(lane_lo, 0.0, xw)], axis=0)

    chains = [(b, j) for b in range(nb) for j in range(n_pairs)]

    def win(x, ch):
        b, j = ch
        return x[b * c:(b + 1) * c, j * LANES:(j + 1) * LANES]

    z = z_ref[...].reshape(rows, zw)
    rin = lax.broadcasted_iota(jnp.int32, (rows, zw), 0) % c
    prev = jnp.concatenate([jnp.broadcast_to(zprev_ref[b, 0:1, :], (c, zw)) for b in range(nb)], axis=0)
    if rev:
        zs = jnp.where(rin == c - 1, prev, pltpu.roll(z, rows - 1, axis=0))
        for b in range(nb):
            zprev_ref[b, 0:1, :] = z[b * c:b * c + 1, :]
    else:
        zs = jnp.where(rin == 0, prev, pltpu.roll(z, 1, axis=0))
        for b in range(nb):
            zprev_ref[b, 0:1, :] = z[b * c + c - 1:b * c + c, :]
    zd = z + (zs - z) * mu_ref[...]
    r = zd[:, 0:w]
    k = zd[:, w:2 * w]
    v = zd[:, 2 * w:3 * w]
    lora = zd[:, 3 * w:3 * w + LANES]
    lane = lax.broadcasted_iota(jnp.int32, (rows, LANES), 1)
    lora = jnp.where(lane < DECAY_LORA, jnp.tanh(lora), lora)
    logw = -DECAY_SCALE * _sigmoid(w0 + _dot_x3k(lora, wup_ref[...]))
    a = _sigmoid(a0 + _dot_x3k(lora, aup_ref[...]))
    kk0 = k * k_k
    kk = kk0 / jnp.maximum(jnp.sqrt(head_sums(kk0 * kk0, 2)), 1e-12)
    k2 = k * (1.0 + (a - 1.0) * k_a)
    bon_ref[...] = (head_sums(r * k2 * r_k, 1) * v).reshape(nb, c, w).astype(bon_ref.dtype)

    cinc = jnp.concatenate(
        [_dot(minc_ref[...], jnp.concatenate(_split3(logw[b * c:(b + 1) * c]), axis=0)) for b in range(nb)], axis=0)
    e_inc = jnp.exp(cinc)
    e_neg = jnp.exp(-cinc)
    rt = r * e_inc
    kt = k2 * e_neg
    bt = kk * a * e_neg
    kkt = kk * jnp.exp(cinc - logw)

    st = [st_ref[b * n_pairs + j] for b, j in chains]
    lhs = [jnp.concatenate([win(kkt, ch), win(rt, ch)], axis=0).astype(BF16) for ch in chains]
    g = [_dot_nt(l, jnp.concatenate([stack(win(bt, ch)), stack(win(kt, ch))], axis=0).astype(BF16))
         for l, ch in zip(lhs, chains)]
    a_b = [jnp.where(strict, x[0:c, 0:2 * c], 0.0) for x in g]
    a_k = [jnp.where(strict, x[0:c, 2 * c:4 * c], 0.0).astype(BF16) for x in g]
    r_kb = [jnp.concatenate([jnp.where(incl, x[c:2 * c, 2 * c:4 * c], 0.0),
                             -jnp.where(incl, x[c:2 * c, 0:2 * c], 0.0)], axis=1).astype(BF16) for x in g]
    x0 = [_dot_nt(l, s.astype(BF16)) for l, s in zip(lhs, st)]
    v_sb = [stack(win(v, ch)).astype(BF16) for ch in chains]
    u = [x[0:c] + _dot(ak, vs) for x, ak, vs in zip(x0, a_k, v_sb)]

    pw = a_b
    steps, sign = 1, -1.0
    while 2 * steps < c:
        both = [_dot(p.astype(BF16), jnp.concatenate([stack(p), stack(x)], axis=1).astype(BF16))
                for p, x in zip(pw, u)]
        u = [x + sign * y[:, 2 * c:4 * c] for x, y in zip(u, both)]
        pw = [y[:, 0:2 * c] for y in both]
        steps, sign = 2 * steps, 1.0
    u = [x + sign * _dot(p.astype(BF16), stack(x).astype(BF16)) for x, p in zip(u, pw)]

    y = [x[c:2 * c] + _dot(rk, jnp.concatenate([vs, stack(uu).astype(BF16)], axis=0))
         for x, rk, vs, uu in zip(x0, r_kb, v_sb, u)]
    upd = [_dot_tn(jnp.concatenate([win(v, ch), x], axis=0).astype(BF16),
                   jnp.concatenate([win(kt, ch), -win(bt, ch)], axis=0).astype(BF16))
           for ch, x in zip(chains, u)]
    for n, (b, j) in enumerate(chains):
        w_end = e_inc[b * c + last:b * c + last + 1, j * LANES:(j + 1) * LANES]
        st_ref[b * n_pairs + j] = jnp.where(diag, (st[n] + upd[n]) * w_end, 0.0)
        y_ref[b, :, j * LANES:(j + 1) * LANES] = y[n].astype(y_ref.dtype)


def _rwkv(p_z, prm, n_ctx, rev):
    b, t, zw = p_z.shape
    w = (zw - DECAY_LORA - ICLR_LORA) // 3
    c = RWKV_CHUNK
    n_tot = t // c
    mu, vecs, wup, aup, e_bf, minc, strict, incl = prm
    tix = lambda i: _scan_chunk(i, n_ctx, n_tot, rev)
    const = lambda i: (0, 0)
    full = lambda a: pl.BlockSpec(a.shape, const)
    return pl.pallas_call(
        functools.partial(_rwkv_kernel, rev=rev, n_ctx=n_ctx, w=w),
        grid=(n_tot,),
        in_specs=[pl.BlockSpec((b, c, zw), lambda i: (0, tix(i), 0)),
                  full(mu), full(vecs), full(wup), full(aup), full(e_bf), full(minc), full(strict), full(incl)],
        out_specs=[pl.BlockSpec((b, c, w), lambda i: (0, tix(i), 0)),
                   pl.BlockSpec((b, c, w), lambda i: (0, tix(i), 0))],
        out_shape=[jax.ShapeDtypeStruct((b, t, w), BF16), jax.ShapeDtypeStruct((b, t, w), BF16)],
        scratch_shapes=[pltpu.VMEM((b * (w // LANES), LANES, LANES), F32), pltpu.VMEM((b, 8, zw), F32)],
        compiler_params=_cparams("arbitrary"),
        name="rwkv7_rev" if rev else "rwkv7_fwd",
    )(p_z, mu, vecs, wup, aup, e_bf, minc, strict, incl)


def _rwkv_params(mu, w0, w_up, a0, a_up, k_k, k_a, r_k, rev, n_batch):
    w = w0.shape[0]
    c = RWKV_CHUNK
    vecs = jnp.concatenate([jnp.stack([w0, a0, k_k, k_a, r_k]), jnp.zeros((3, w), F32)], axis=0)
    wup = jnp.concatenate([w_up, jnp.zeros((ICLR_LORA, w), F32)], axis=0)
    aup = jnp.concatenate([jnp.zeros((DECAY_LORA, w), F32), a_up], axis=0)
    hid = jnp.arange(LANES) // HEAD_DIM
    e_bf = (hid[:, None] == hid[None, :]).astype(BF16)
    t = jnp.arange(c)
    p = (c - 1 - t) if rev else t
    le = p[None, :] <= p[:, None]
    lt = p[None, :] < p[:, None]
    strict = jnp.tile(lt, (1, 2)).astype(F32)
    incl = jnp.tile(le, (1, 2)).astype(F32)
    minc = jnp.tile(le, (1, 3)).astype(BF16)
    return mu.reshape(1, -1), vecs, wup, aup, e_bf, minc, strict, incl


def _lru_kernel(x_ref, cw_ref, vec_ref, wa_ref, wx_ref, h_ref, hcar_ref, ucar_ref, *, rev, n_ctx):
    i = pl.program_id(0)
    nb, c, w = x_ref.shape
    rows = nb * c

    @pl.when(i == 0)
    def _():
        hcar_ref[...] = jnp.zeros_like(hcar_ref)

    @pl.when((i == 0) | (i == n_ctx))
    def _():
        ucar_ref[...] = jnp.zeros_like(ucar_ref)

    u0 = x_ref[...].astype(F32).reshape(rows, w)
    row = lax.broadcasted_iota(jnp.int32, (rows, w), 0) % c

    def per_batch(ref, j):
        return jnp.concatenate([jnp.broadcast_to(ref[b, j:j + 1, :], (c, w)) for b in range(nb)], axis=0)

    def shifted(x, s, carry, fill):
        if rev:
            rolled = pltpu.roll(x, rows - s, axis=0)
            edge = row >= c - s
        else:
            rolled = pltpu.roll(x, s, axis=0)
            edge = row < s
        if carry is None:
            return jnp.where(edge, fill, rolled)
        return jnp.where(edge, carry, rolled)

    conv = vec_ref[0:1, :] + cw_ref[LRU_CONV - 1:LRU_CONV, :] * u0
    for m in range(1, LRU_CONV):
        car = jnp.zeros((rows, w), F32)
        for qpos in range(m):
            r_idx = (c - 1 - qpos) if rev else qpos
            car = jnp.where(row == r_idx, per_batch(ucar_ref, m - qpos - 1), car)
        conv = conv + cw_ref[LRU_CONV - 1 - m:LRU_CONV - m, :] * shifted(u0, m, car, None)
    for m in range(1, LRU_CONV):
        r_idx = (m - 1) if rev else (c - m)
        for b in range(nb):
            ucar_ref[b, m - 1:m, :] = u0[b * c + r_idx:b * c + r_idx + 1, :]

    cb = conv.astype(BF16)
    r = _sigmoid(_dot(cb, wa_ref[...]) + vec_ref[1:2, :])
    ig = _sigmoid(_dot(cb, wx_ref[...]) + vec_ref[2:3, :])
    log_a = -LRU_C * r * vec_ref[3:4, :]
    a = jnp.exp(log_a)
    bb = jnp.sqrt(1.0 - jnp.exp(2.0 * log_a)) * (ig * conv)

    s = 1
    while s < c:
        bb = bb + a * shifted(bb, s, None, 0.0)
        a = a * shifted(a, s, None, 1.0)
        s *= 2
    h = bb + a * per_batch(hcar_ref, 0)
    h_ref[...] = h.reshape(nb, c, w).astype(h_ref.dtype)
    last = 0 if rev else c - 1
    for b in range(nb):
        hcar_ref[b, 0:1, :] = h[b * c + last:b * c + last + 1, :]


def _lru(p_lru, prm, n_ctx, rev):
    b, t, w2 = p_lru.shape
    w = w2 // 2
    c = LRU_CHUNK
    n_tot = t // c
    cw, vecs, wa, wx = prm
    tix = lambda i: _scan_chunk(i, n_ctx, n_tot, rev)
    const = lambda i: (0, 0)
    return pl.pallas_call(
        functools.partial(_lru_kernel, rev=rev, n_ctx=n_ctx),
        grid=(n_tot,),
        in_specs=[pl.BlockSpec((b, c, w), lambda i: (0, tix(i), 0)),
                  pl.BlockSpec(cw.shape, const), pl.BlockSpec(vecs.shape, const),
                  pl.BlockSpec(wa.shape, const), pl.BlockSpec(wx.shape, const)],
        out_specs=pl.BlockSpec((b, c, w), lambda i: (0, tix(i), 0)),
        out_shape=jax.ShapeDtypeStruct((b, t, w), BF16),
        scratch_shapes=[pltpu.VMEM((b, 8, w), F32), pltpu.VMEM((b, 8, w), F32)],
        compiler_params=_cparams("arbitrary"),
        name="rglru_rev" if rev else "rglru_fwd",
    )(p_lru, cw, vecs, wa, wx)


def _lru_params(conv_w, conv_b, wa, ba, wx, bx, lam):
    w = conv_b.shape[0]
    cw = jnp.concatenate([conv_w, jnp.zeros((8 - LRU_CONV, w), F32)], axis=0)
    vecs = jnp.concatenate([jnp.stack([conv_b, ba, bx, jax.nn.softplus(-lam)]), jnp.zeros((4, w), F32)], axis=0)
    return cw, vecs, jax.scipy.linalg.block_diag(*wa).astype(BF16), jax.scipy.linalg.block_diag(*wx).astype(BF16)


def _head_norm(y, e2, gain, bias, eps):
    inv = 1.0 / HEAD_DIM

    def head_sums(x):
        xb = x.astype(BF16)
        return jnp.concatenate([_dot(xb[:, j:j + LANES], e2) for j in range(0, x.shape[1], LANES)], axis=1)

    yc = y - head_sums(y) * inv
    var = head_sums(yc * yc) * inv
    return yc * lax.rsqrt(var + eps) * gain + bias


def _mix_out_kernel(x_ref, mod_ref, of_ref, ob_ref, g_ref, yf_ref, yb_ref, bf_ref, bb_ref, gd_ref,
                    hf_ref, hb_ref, lg_ref, gn_ref, e_ref, gup_ref, wout_ref, n2g_ref, wr_ref, br_ref, tri_ref,
                    xo_ref, h2_ref, idx_ref, gate_ref, rank_ref, cnt_ref, base_ref, *, w_ret, w_rw):
    @pl.when((pl.program_id(0) == 0) & (pl.program_id(1) == 0))
    def _():
        base_ref[...] = jnp.zeros_like(base_ref)

    rows = range(x_ref.shape[0])
    e_bf = e_ref[...]
    f32 = lambda ref, s: ref[s].astype(F32)
    g = [f32(g_ref, s) for s in rows]
    ret = [_head_norm(f32(of_ref, s) + f32(ob_ref, s), e_bf, gn_ref[0:1, :], gn_ref[1:2, :], RET_GN_EPS) for s in rows]
    ret = [r * (gg * _sigmoid(gg)) for r, gg in zip(ret, g)]
    gate = [_dot(_sigmoid(f32(gd_ref, s)).astype(BF16), gup_ref[...]) for s in rows]
    rw = [_head_norm(f32(yf_ref, s) + f32(yb_ref, s), e_bf, gn_ref[2:3, :], gn_ref[3:4, :], RWKV_GN_EPS) for s in rows]
    rw = [(r + f32(bf_ref, s) + f32(bb_ref, s)) * gt for r, gt, s in zip(rw, gate, rows)]
    lg = [f32(lg_ref, s) for s in rows]
    gelu = [0.5 * u * (1.0 + jnp.tanh(0.7978845608028654 * (u + 0.044715 * (u * u * u)))) for u in lg]
    lru = [(f32(hf_ref, s) + f32(hb_ref, s)) * ge for ge, s in zip(gelu, rows)]
    mix = [_dot(a.astype(BF16), wout_ref[0:w_ret, :])
           + _dot(b.astype(BF16), wout_ref[w_ret:w_ret + w_rw, :])
           + _dot(c.astype(BF16), wout_ref[w_ret + w_rw:, :]) for a, b, c in zip(ret, rw, lru)]
    x = [x_ref[s] + mod_ref[s, 0, 0:1, :] * m for m, s in zip(mix, rows)]
    h2 = []
    for s in rows:
        xo_ref[s] = x[s]
        ms = jnp.mean(x[s] * x[s], axis=-1, keepdims=True)
        hn = x[s] * lax.rsqrt(ms + NORM_EPS) * n2g_ref[...]
        h2.append(hn * (1.0 + mod_ref[s, 0, 2:3, :]) + mod_ref[s, 0, 1:2, :])
        h2_ref[s] = h2[s].astype(BF16)
    logits = [_dot_x3(h, wr_ref[...]) + br_ref[...] for h in h2]
    for s in rows:
        idx_o, gate_o, rank_o = _route_tile(logits[s], tri_ref[...], base_ref)
        idx_ref[s] = idx_o
        gate_ref[s] = gate_o
        rank_ref[s] = rank_o
    cnt_ref[...] = base_ref[...].astype(jnp.int32)


def _mix_out(xs, mod, o_f, o_b, p_ret, y_f, y_b, bon_f, bon_b, p_gd, h_f, h_b, p_lru,
             gn, e_bf, g_up_bf, w_out_bf, norm2_g, w_router, b_router, n_ctx_tiles):
    b, t, d = xs.shape
    tm = TOKEN_TILE
    nbb = MIX_ROWS if b % MIX_ROWS == 0 else 1
    w_ret, w_rw, w_lru = o_f.shape[2], y_f.shape[2], h_f.shape[2]
    ne = w_router.shape[1]
    tok = lambda wd, j=0: pl.BlockSpec((nbb, tm, wd), lambda bi, i: (bi, i, j))
    const = lambda a: pl.BlockSpec(a.shape, lambda bi, i: (0,) * a.ndim)
    seg = lambda bi, i: (bi, jnp.where(i >= n_ctx_tiles, 1, 0), 0, 0)
    n2g = norm2_g.reshape(1, d)
    br = b_router.reshape(1, ne)
    tt = jnp.arange(tm)
    tri = (tt[None, :] < tt[:, None]).astype(BF16)
    return pl.pallas_call(
        functools.partial(_mix_out_kernel, w_ret=w_ret, w_rw=w_rw),
        grid=(b // nbb, t // tm),
        in_specs=[tok(d), pl.BlockSpec((nbb, 1, 3, d), seg),
                  tok(w_ret), tok(w_ret), tok(w_ret, 3),
                  tok(w_rw), tok(w_rw), tok(w_rw), tok(w_rw), tok(GATE_LORA),
                  tok(w_lru), tok(w_lru), tok(w_lru, 1),
                  const(gn), const(e_bf), const(g_up_bf), const(w_out_bf), const(n2g), const(w_router), const(br),
                  const(tri)],
        out_specs=[tok(d), tok(d), tok(LANES), tok(LANES), tok(LANES),
                   pl.BlockSpec((8, ne), lambda bi, i: (0, 0))],
        out_shape=[jax.ShapeDtypeStruct((b, t, d), F32), jax.ShapeDtypeStruct((b, t, d), BF16),
                   jax.ShapeDtypeStruct((b, t, LANES), jnp.int32), jax.ShapeDtypeStruct((b, t, LANES), F32),
                   jax.ShapeDtypeStruct((b, t, LANES), jnp.int32), jax.ShapeDtypeStruct((8, ne), jnp.int32)],
        scratch_shapes=[pltpu.VMEM((8, ne), F32)],
        compiler_params=_cparams("arbitrary", "arbitrary"),
        name="mix_out",
    )(xs, mod, o_f, o_b, p_ret, y_f, y_b, bon_f, bon_b, p_gd, h_f, h_b, p_lru,
      gn, e_bf, g_up_bf, w_out_bf, n2g, w_router, br, tri)


def _moe_kernel(be_ref, first_ref, nu_ref, *refs, blk0, chained):
    x_ref, w1_ref, b1_ref, w2_ref, b2_ref = refs[1:6] if chained else refs[0:5]
    y_ref, w1b_ref, w2b_ref = refs[-3:]
    i = pl.program_id(0)
    blk = i + blk0
    de = w2_ref.shape[1]

    @pl.when((first_ref[blk] == 1) | (i == 0))
    def _():
        w1b_ref[...] = w1_ref[0].astype(BF16)
        w2b_ref[...] = w2_ref[0].astype(BF16)

    @pl.when(blk < nu_ref[0])
    def _():
        gu = _dot(x_ref[...], w1b_ref[...]) + b1_ref[0]
        glu = jnp.minimum(gu[:, :de], SWIGLU_LIMIT)
        lin = jnp.clip(gu[:, de:], -SWIGLU_LIMIT, SWIGLU_LIMIT)
        act = glu * _sigmoid(SWIGLU_ALPHA * glu) * (lin + 1.0)
        y_ref[...] = (_dot(act.astype(BF16), w2b_ref[...]) + b2_ref[0]).astype(y_ref.dtype)

    @pl.when(blk >= nu_ref[0])
    def _():
        y_ref[...] = jnp.zeros_like(y_ref)


def _moe_ffn(hb, block_e, first, n_used, w1, b1, w2, b2, layer, blk0, n_slots, y_prev=None):
    rows, d = hb.shape
    tm = MOE_TILE
    nl, ne, _, d2 = w1.shape
    de = w2.shape[2]
    chained = y_prev is not None
    wsel = lambda i, be, fi, nu: (layer, be[i + blk0], 0, 0)
    in_specs = [pl.BlockSpec((tm, d), lambda i, be, fi, nu: (i, 0)),
                pl.BlockSpec((None, 1, d, d2), wsel),
                pl.BlockSpec((None, 1, 1, d2), wsel),
                pl.BlockSpec((None, 1, de, d), wsel),
                pl.BlockSpec((None, 1, 1, d), wsel)]
    args = [hb, w1, b1.reshape(nl, ne, 1, d2), w2, b2.reshape(nl, ne, 1, d)]
    if chained:
        in_specs = [pl.BlockSpec(memory_space=pl.ANY)] + in_specs
        args = [y_prev] + args
    return pl.pallas_call(
        functools.partial(_moe_kernel, blk0=blk0, chained=chained),
        grid_spec=pltpu.PrefetchScalarGridSpec(
            num_scalar_prefetch=3,
            grid=(rows // tm,),
            in_specs=in_specs,
            out_specs=pl.BlockSpec((tm, d), lambda i, be, fi, nu: (i + blk0, 0)),
            scratch_shapes=[pltpu.VMEM((d, d2), BF16), pltpu.VMEM((de, d), BF16)],
        ),
        out_shape=jax.ShapeDtypeStruct((n_slots, d), BF16),
        input_output_aliases={3: 0} if chained else {},
        compiler_params=_cparams("arbitrary"),
        name="moe_ffn",
    )(block_e, first, n_used, *args)


def _route_tile(lg, tri, base_ref):
    tr, ne = lg.shape
    lane = lax.broadcasted_iota(jnp.int32, (tr, ne), 1).astype(F32)
    out_lane = lax.broadcasted_iota(jnp.int32, (tr, LANES), 1)
    vals = lg
    sel = jnp.zeros((tr, ne), F32)
    picks, tops = [], []
    for _ in range(TOP_K):
        m = jnp.max(vals, axis=-1, keepdims=True)
        ix = jnp.min(jnp.where(vals == m, lane, float(ne)), axis=-1, keepdims=True)
        hit = lane == ix
        sel = jnp.where(hit, 1.0, sel)
        vals = jnp.where(hit, -jnp.inf, vals)
        picks.append(ix)
        tops.append(m)
    ex = [jnp.exp(t - tops[0]) for t in tops]
    den = ex[0] + ex[1] + ex[2] + ex[3]
    before = _dot(tri, sel.astype(BF16)) + base_ref[0:1, :]
    idx_o = jnp.zeros((tr, LANES), F32)
    gate_o = jnp.zeros((tr, LANES), F32)
    rank_o = jnp.zeros((tr, LANES), F32)
    for k in range(TOP_K):
        rk = jnp.sum(jnp.where(lane == picks[k], before, 0.0), axis=-1, keepdims=True)
        idx_o = jnp.where(out_lane == k, picks[k], idx_o)
        gate_o = jnp.where(out_lane == k, ex[k] / den, gate_o)
        rank_o = jnp.where(out_lane == k, rk, rank_o)
    total = base_ref[0:1, :] + jnp.sum(sel, axis=0, keepdims=True)
    base_ref[...] = jnp.broadcast_to(total, base_ref.shape)
    return idx_o.astype(jnp.int32), gate_o, rank_o.astype(jnp.int32)


def _route_meta(idx, rank, counts):
    n_tok = idx.shape[0]
    ne = counts.shape[0]
    tm = MOE_TILE
    n_assign = n_tok * TOP_K
    padded = (counts + tm - 1) // tm * tm
    pend = jnp.cumsum(padded)
    pstart = pend - padded
    start = jnp.cumsum(counts) - counts
    eid = jnp.arange(ne, dtype=jnp.int32)
    slot = jnp.sum(jnp.where(idx[..., None] == eid, pstart, 0), axis=-1).astype(jnp.int32) + rank
    n_blocks = (n_assign + ne * (tm - 1) + tm - 1) // tm
    blk_start = jnp.arange(n_blocks, dtype=jnp.int32) * tm
    block_e = jnp.minimum(jnp.sum(pend[None, :] <= blk_start[:, None], axis=1), ne - 1).astype(jnp.int32)
    first = jnp.concatenate([jnp.ones((1,), jnp.int32), (block_e[1:] != block_e[:-1]).astype(jnp.int32)])
    n_used = (pend[-1] // tm).astype(jnp.int32).reshape(1)
    _, order = lax.sort_key_val(slot.reshape(-1), jnp.arange(n_assign, dtype=jnp.int32))
    order_tok = order // TOP_K
    off = jnp.arange(n_blocks * tm, dtype=jnp.int32) - jnp.repeat(pstart[block_e], tm)
    valid = off < jnp.repeat(counts[block_e], tm)
    pos = jnp.clip(jnp.repeat(start[block_e], tm) + off, 0, n_assign - 1)
    spread = jnp.arange(n_blocks * tm, dtype=jnp.int32) % n_tok
    slot_tok = jnp.where(valid, order_tok[pos], spread).astype(jnp.int32)
    return slot, slot_tok, block_e, first, n_used


def _combine_kernel(x_ref, mod_ref, y_ref, gate_ref, g_ref, *rest, final):
    o_ref = rest[-1]
    gate = gate_ref[0]
    y = y_ref[0, 0].astype(F32) * gate[:, 0:1]
    for k in range(1, TOP_K):
        y = y + y_ref[k, 0].astype(F32) * gate[:, k:k + 1]
    x = x_ref[0] + mod_ref[0, 0, 0:1, :] * y
    if final:
        ms = jnp.mean(x * x, axis=-1, keepdims=True)
        x = x * lax.rsqrt(ms + NORM_EPS) * g_ref[...]
    o_ref[0] = x


def _combine(xs, mod, yg, gates, final_g, n_ctx_tiles, final, b0, prev=None):
    b, t, d = xs.shape
    nbh = yg.shape[1]
    tm = TOKEN_TILE
    skip = n_ctx_tiles if final else 0
    seg = lambda bi, i: (bi + b0, jnp.where(i + skip >= n_ctx_tiles, 1, 0), 0, 0)
    in_specs = [pl.BlockSpec((1, tm, d), lambda bi, i: (bi + b0, i + skip, 0)),
                pl.BlockSpec((1, 1, 1, d), seg),
                pl.BlockSpec((TOP_K, 1, tm, d), lambda bi, i: (0, bi, i + skip, 0)),
                pl.BlockSpec((1, tm, LANES), lambda bi, i: (bi + b0, i + skip, 0)),
                pl.BlockSpec((1, d), lambda bi, i: (0, 0))]
    args = [xs, mod, yg, gates, final_g.reshape(1, d)]
    aliases = {} if final else {0: 0}
    if prev is not None:
        in_specs.append(pl.BlockSpec(memory_space=pl.ANY))
        args.append(prev)
        aliases = {5: 0}
    return pl.pallas_call(
        functools.partial(_combine_kernel, final=final),
        grid=(nbh, t // tm - skip),
        in_specs=in_specs,
        out_specs=pl.BlockSpec((1, tm, d), lambda bi, i: (bi + b0, i, 0)),
        out_shape=jax.ShapeDtypeStruct((b, t - skip * tm, d), F32),
        input_output_aliases=aliases,
        compiler_params=_cparams("parallel", "parallel"),
        name="combine_final" if final else "combine",
    )(*args)


def kernel(x, c, ctx, c_ctx, w_mod, b_mod, norm1_g, norm2_g, w_in, w_out, ret_decay_logit, ret_gn_g, ret_gn_b, rwkv_mu, rwkv_w0, rwkv_w_up, rwkv_a0, rwkv_a_up, rwkv_k_k, rwkv_k_a, rwkv_g_up, rwkv_r_k, rwkv_gn_g, rwkv_gn_b, lru_conv_w, lru_conv_b, lru_wa, lru_ba, lru_wx, lru_bx, lru_lambda, moe_w_router, moe_b_router, moe_w1, moe_b1, moe_w2, moe_b2, final_norm_g):
    bsz, seq, dm = x.shape
    n_ctx_tok = ctx.shape[1]
    depth = w_in.shape[0]
    n_experts = moe_w_router.shape[2]
    w_ret = 3 * dm // 8
    w_rw = 3 * dm // 8
    w_lru = dm - w_ret - w_rw
    zw = 3 * w_rw + DECAY_LORA + ICLR_LORA
    sizes = (4 * w_ret, zw, GATE_LORA, 2 * w_lru)
    bounds, off = [], 0
    for s in sizes:
        bounds.append((off, off + s))
        off += s
    bounds = tuple(bounds)
    assert off == w_in.shape[2]
    assert n_ctx_tok % TOKEN_TILE == 0 and seq % TOKEN_TILE == 0 and seq % GRID_W == 0
    t_all = n_ctx_tok + seq
    n_ctx_tiles = n_ctx_tok // TOKEN_TILE

    xs = jnp.concatenate([ctx, x], axis=1)
    cos_t, sin_t = _rope_tables(n_ctx_tok, seq, w_ret)
    hid = jnp.arange(LANES) // HEAD_DIM
    e_bf = (hid[:, None] == hid[None, :]).astype(BF16)
    cond =jnp.concatenate([c, c_ctx[None, :], jnp.zeros((8 - (bsz + 1) % 8, dm), F32)], axis=0)

    for l in range(depth):
        last = l == depth - 1
        mod = _modulation(cond, w_mod, b_mod, l)
        mod_l = mod[:bsz].reshape(bsz, 6, dm)
        mod_c = jnp.broadcast_to(mod[bsz].reshape(1, 6, dm), (bsz, 6, dm))
        modsel = jnp.stack([mod_c, mod_l], axis=1)

        p_ret, p_z, p_gd, p_lru = _in_proj(xs, modsel[:, :, 0:2], norm1_g[l], w_in[l].astype(BF16), bounds,
                                           (BF16, F32, BF16, BF16), n_ctx_tiles)

        ret_o, rw_y, rw_bon, lru_h = [], [], [], []
        for d in range(2):
            rev = d == 1
            ret_o.append(_retention(p_ret, cos_t, sin_t, _ret_tables(ret_decay_logit[l, d], w_ret, rev),
                                    n_ctx_tok // RET_CHUNK, rev))
            prm = _rwkv_params(rwkv_mu[l, d], rwkv_w0[l, d], rwkv_w_up[l, d], rwkv_a0[l, d], rwkv_a_up[l, d],
                               rwkv_k_k[l, d], rwkv_k_a[l, d], rwkv_r_k[l], rev, bsz)
            y, bon = _rwkv(p_z, prm, n_ctx_tok // RWKV_CHUNK, rev)
            rw_y.append(y)
            rw_bon.append(bon)
            lru_h.append(_lru(p_lru, _lru_params(lru_conv_w[l, d], lru_conv_b[l, d], lru_wa[l, d], lru_ba[l, d],
                                                 lru_wx[l, d], lru_bx[l, d], lru_lambda[l, d]),
                              n_ctx_tok // LRU_CHUNK, rev))

        gn = jnp.concatenate([jnp.stack([ret_gn_g[l], ret_gn_b[l], rwkv_gn_g[l], rwkv_gn_b[l]]),
                              jnp.zeros((4, w_ret), F32)], axis=0)
        xs, h2, idx, gates, rank, counts = _mix_out(
            xs, modsel[:, :, 2:5], ret_o[0], ret_o[1], p_ret, rw_y[0], rw_y[1],
            rw_bon[0], rw_bon[1], p_gd, lru_h[0], lru_h[1], p_lru,
            gn, e_bf, rwkv_g_up[l].astype(BF16), w_out[l].astype(BF16), norm2_g[l],
            moe_w_router[l], moe_b_router[l], n_ctx_tiles)

        n_tok = bsz * t_all
        slot, slot_tok, block_e, first, n_used = _route_meta(
            idx.reshape(n_tok, LANES)[:, :TOP_K], rank.reshape(n_tok, LANES)[:, :TOP_K], counts[0])
        n_slots = slot_tok.shape[0]
        row_a = max(n_slots // MOE_TILE // 4, 1) * MOE_TILE
        h2f = h2.reshape(n_tok, dm)
        y_sorted = None
        for lo, hi in ((0, row_a), (row_a, n_slots)):
            y_sorted = _moe_ffn(h2f[slot_tok[lo:hi]], block_e, first, n_used, moe_w1, moe_b1, moe_w2, moe_b2, l,
                                lo // MOE_TILE, n_slots, y_prev=y_sorted)
        slot_b = slot.reshape(bsz, t_all, TOP_K)
        out = None
        for lo, hi in [(b, b + 1) for b in range(bsz)]:
            yg = y_sorted[jnp.moveaxis(slot_b[lo:hi], 2, 0)]
            out = _combine(xs, modsel[:, :, 5:6], yg, gates, final_norm_g, n_ctx_tiles, last, lo,
                           prev=out if last else None)
            xs = xs if last else out
        xs = out
    return xs
```

```python
import functools

import jax
import jax.numpy as jnp
from jax import lax
from jax.experimental import pallas as pl
from jax.experimental.pallas import tpu as pltpu

F32 = jnp.float32
BF16 = jnp.bfloat16

HEAD_DIM = 64
NORM_EPS = 1e-6
RET_GN_EPS = 1e-5
RWKV_GN_EPS = 64e-5
ROPE_BASE = 10000.0
GRID_W = 64
LRU_CONV = 4
LRU_C = 8.0
TOP_K = 4
SWIGLU_LIMIT = 7.0
SWIGLU_ALPHA = 1.702
DECAY_LORA = 64
ICLR_LORA = 64
GATE_LORA = 128
DECAY_SCALE = 0.6065306597126334

LANES = 128
TOKEN_TILE = 256
RET_CHUNK = 128
RWKV_CHUNK = 64
LRU_CHUNK = 128
MOE_TILE = 512
IN_ROWS = 2
MIX_ROWS = 4
VMEM_LIMIT = 56 * 1024 * 1024


def _cparams(*sem):
    return pltpu.CompilerParams(dimension_semantics=sem, vmem_limit_bytes=VMEM_LIMIT)


def _scan_chunk(i, n_ctx, n_tot, rev):
    if not rev:
        return i
    return jnp.where(i < n_ctx, n_ctx - 1 - i, n_tot + n_ctx - 1 - i)


def _split3(a):
    hi = a.astype(BF16)
    r1 = a - hi.astype(F32)
    mid = r1.astype(BF16)
    lo = (r1 - mid.astype(F32)).astype(BF16)
    return hi, mid, lo


def _dot(a, b):
    return jnp.dot(a, b, preferred_element_type=F32)


def _dot_nt(a, b):
    return lax.dot_general(a, b, (((1,), (1,)), ((), ())), preferred_element_type=F32)


def _dot_tn(a, b):
    return lax.dot_general(a, b, (((0,), (0,)), ((), ())), preferred_element_type=F32)


def _dot_exact_rhs(a, b_bf):
    hi, mid, lo = _split3(a)
    return _dot(hi, b_bf) + _dot(mid, b_bf) + _dot(lo, b_bf)


def _dot_exact_lhs(a_bf, b):
    hi, mid, lo = _split3(b)
    return _dot(a_bf, hi) + _dot(a_bf, mid) + _dot(a_bf, lo)


def _dot_x3(a, b):
    a_hi = a.astype(BF16)
    a_lo = (a - a_hi.astype(F32)).astype(BF16)
    b_hi = b.astype(BF16)
    b_lo = (b - b_hi.astype(F32)).astype(BF16)
    return _dot(a_hi, b_hi) + _dot(a_lo, b_hi) + _dot(a_hi, b_lo)


def _dot_x3k(a, b):
    a_hi = a.astype(BF16)
    a_lo = (a - a_hi.astype(F32)).astype(BF16)
    b_hi = b.astype(BF16)
    b_lo = (b - b_hi.astype(F32)).astype(BF16)
    return (_dot(jnp.concatenate([a_hi, a_lo], axis=1), jnp.concatenate([b_hi, b_hi], axis=0))
            + _dot(a_hi, b_lo))


def _sigmoid(x):
    return 1.0 / (1.0 + jnp.exp(-x))


def _mod_kernel(c_ref, w_ref, b_ref, o_ref):
    c = c_ref[...]
    s = c * _sigmoid(c)
    o_ref[...] = _dot_x3(s, w_ref[...]) + b_ref[...]


def _modulation(cond, w_mod, b_mod, layer):
    r, d = cond.shape
    nl, _, n = w_mod.shape
    tn = d
    return pl.pallas_call(
        _mod_kernel,
        grid=(n // tn,),
        in_specs=[pl.BlockSpec((r, d), lambda j: (0, 0)),
                  pl.BlockSpec((None, d, tn), lambda j: (layer, 0, j)),
                  pl.BlockSpec((None, 1, tn), lambda j: (layer, 0, j))],
        out_specs=pl.BlockSpec((r, tn), lambda j: (0, j)),
        out_shape=jax.ShapeDtypeStruct((r, n), F32),
        compiler_params=_cparams("arbitrary"),
        name="modulation",
    )(cond, w_mod, b_mod.reshape(nl, 1, n))


def _in_proj_kernel(x_ref, mod_ref, g_ref, w_ref, *o_refs, bounds):
    hb = []
    for s in range(x_ref.shape[0]):
        x = x_ref[s]
        ms = jnp.mean(x * x, axis=-1, keepdims=True)
        h = x * lax.rsqrt(ms + NORM_EPS) * g_ref[...]
        hb.append((h * (1.0 + mod_ref[s, 0, 1:2, :]) + mod_ref[s, 0, 0:1, :]).astype(BF16))
    for s, h in enumerate(hb):
        for o_ref, (lo, hi) in zip(o_refs, bounds):
            o_ref[s] = _dot(h, w_ref[:, lo:hi]).astype(o_ref.dtype)


def _in_proj(xs, mod, norm_g, w_in_bf, bounds, dtypes, n_ctx_tiles):
    b, t, d = xs.shape
    tm = TOKEN_TILE
    nbb = IN_ROWS if b % IN_ROWS == 0 else 1
    p = w_in_bf.shape[1]
    seg = lambda bi, i: (bi, jnp.where(i >= n_ctx_tiles, 1, 0), 0, 0)
    return pl.pallas_call(
        functools.partial(_in_proj_kernel, bounds=bounds),
        grid=(b // nbb, t // tm),
        in_specs=[pl.BlockSpec((nbb, tm, d), lambda bi, i: (bi, i, 0)),
                  pl.BlockSpec((nbb, 1, 2, d), seg),
                  pl.BlockSpec((1, d), lambda bi, i: (0, 0)),
                  pl.BlockSpec((d, p), lambda bi, i: (0, 0))],
        out_specs=[pl.BlockSpec((nbb, tm, hi - lo), lambda bi, i: (bi, i, 0)) for lo, hi in bounds],
        out_shape=[jax.ShapeDtypeStruct((b, t, hi - lo), dt) for (lo, hi), dt in zip(bounds, dtypes)],
        compiler_params=_cparams("parallel", "parallel"),
        name="in_proj",
    )(xs, mod, norm_g.reshape(1, d), w_in_bf)


def _ret_kernel(q_ref, k_ref, v_ref, cos_ref, sin_ref, dq_ref, dk_ref, dmat_ref, gm_ref, bm_ref,
                o_ref, s_ref):
    i = pl.program_id(0)

    @pl.when(i == 0)
    def _():
        s_ref[...] = jnp.zeros_like(s_ref)

    nb, c, w = q_ref.shape
    cos = cos_ref[...]
    sin = sin_ref[...]
    lane = lax.broadcasted_iota(jnp.int32, (c, LANES), 1)
    first = (lane % 32) < 16

    def rope(u):
        parts = []
        for j in range(w // LANES):
            uj = u[:, j * LANES:(j + 1) * LANES]
            nxt = pltpu.roll(uj, LANES - 16, axis=1)
            prv = pltpu.roll(uj, 16, axis=1)
            parts.append(jnp.where(first, nxt, prv))
        return u * cos + jnp.concatenate(parts, axis=1) * sin

    lane_lo = lane < HEAD_DIM

    def stack(xw):
        return jnp.concatenate([jnp.where(lane_lo, xw, 0.0), jnp.where(lane_lo, 0.0, xw)], axis=0)

    n_pairs = w // LANES
    q = [rope(q_ref[b].astype(F32)) for b in range(nb)]
    k = [rope(k_ref[b].astype(F32)) for b in range(nb)]
    chains = [(b, j) for b in range(nb) for j in range(n_pairs)]
    pair = lambda x, j: x[:, j * LANES:(j + 1) * LANES]
    qw = [pair(q[b], j) for b, j in chains]
    kw = [pair(k[b], j) for b, j in chains]
    vw = [pair(v_ref[b], j) for b, j in chains]
    s = [s_ref[b * n_pairs + j] for b, j in chains]
    inter = [_dot((x * pair(dq_ref[...], j)).astype(BF16), st.astype(BF16)) for x, st, (b, j) in zip(qw, s, chains)]
    sc = [_dot_nt(x.astype(BF16), stack(y).astype(BF16)) * dmat_ref[j]
          for x, y, (b, j) in zip(qw, kw, chains)]
    intra = [_dot(x.astype(BF16), stack(y).astype(BF16)) for x, y in zip(sc, vw)]
    ktv = [_dot_tn((y * pair(dk_ref[...], j)).astype(BF16), z.astype(BF16))
           for y, z, (b, j) in zip(kw, vw, chains)]
    for n, (b, j) in enumerate(chains):
        o_ref[b, :, j * LANES:(j + 1) * LANES] = (inter[n] + intra[n]).astype(o_ref.dtype)
        s_ref[b * n_pairs + j] = gm_ref[j] * s[n] + bm_ref[...] * ktv[n]


def _retention(p_ret, cos_t, sin_t, tabs, n_ctx, rev):
    b, t, w4 = p_ret.shape
    w = w4 // 4
    c = RET_CHUNK
    n_tot = t // c
    n_pairs = w // LANES
    dq, dk, dmat, gm, bm = tabs
    tix = lambda i: _scan_chunk(i, n_ctx, n_tot, rev)
    col = lambda j: (lambda i: (0, tix(i), j))
    const = lambda a: pl.BlockSpec(a.shape, lambda i: (0,) * a.ndim)
    return pl.pallas_call(
        _ret_kernel,
        grid=(n_tot,),
        in_specs=[pl.BlockSpec((b, c, w), col(0)), pl.BlockSpec((b, c, w), col(1)), pl.BlockSpec((b, c, w), col(2)),
                  pl.BlockSpec((c, w), lambda i: (tix(i), 0)),
                  pl.BlockSpec((c, w), lambda i: (tix(i), 0)),
                  const(dq), const(dk), const(dmat), const(gm), const(bm)],
        out_specs=pl.BlockSpec((b, c, w), lambda i: (0, tix(i), 0)),
        out_shape=jax.ShapeDtypeStruct((b, t, w), BF16),
        scratch_shapes=[pltpu.VMEM((b * n_pairs, LANES, LANES), F32)],
        compiler_params=_cparams("arbitrary"),
        name="retention_rev" if rev else "retention_fwd",
    )(p_ret, p_ret, p_ret, cos_t, sin_t, dq, dk, dmat, gm, bm)


def _ret_tables(decay_logit, w, rev):
    n_heads = w // HEAD_DIM
    c = RET_CHUNK
    lg = jax.nn.log_sigmoid(decay_logit.astype(F32))
    t = jnp.arange(c, dtype=F32)
    p = (c - 1.0 - t) if rev else t
    rel = p[:, None] - p[None, :]
    scale = HEAD_DIM ** -0.5
    dmat = jnp.where(rel >= 0, jnp.exp(lg[:, None, None] * jnp.maximum(rel, 0.0)), 0.0) * scale
    dq = jnp.exp(lg[:, None] * (p + 1.0)) * scale
    dk = jnp.exp(lg[:, None] * (c - 1.0 - p))
    lanes = lambda a: jnp.repeat(a.T, HEAD_DIM, axis=1)
    n_pairs = n_heads // 2
    dmat_w = dmat.reshape(n_pairs, 2, c, c).transpose(0, 2, 1, 3).reshape(n_pairs, c, 2 * c)
    hid = jnp.arange(LANES) // HEAD_DIM
    bm = (hid[:, None] == hid[None, :]).astype(F32)
    gm = bm[None] * jnp.exp(lg * c).reshape(n_pairs, 2)[:, hid][:, :, None]
    return lanes(dq), lanes(dk), dmat_w, gm, bm


def _rope_tables(n_ctx_tok, seq, w):
    half = HEAD_DIM // 2
    quarter = half // 2
    inv_freq = ROPE_BASE ** (-jnp.arange(quarter, dtype=F32) / quarter)
    tok = jnp.arange(seq)
    rows = (tok // GRID_W).astype(F32)
    cols = (tok % GRID_W).astype(F32)
    o = jnp.arange(w) % HEAD_DIM
    pos = jnp.where(o[None, :] < half, rows[:, None], cols[:, None])
    ang = pos * inv_freq[o % quarter][None, :]
    sign = jnp.where((o % half) < quarter, -1.0, 1.0)[None, :]
    cos = jnp.concatenate([jnp.ones((n_ctx_tok, w), F32), jnp.cos(ang)], axis=0)
    sin = jnp.concatenate([jnp.zeros((n_ctx_tok, w), F32), jnp.sin(ang) * sign], axis=0)
    return cos, sin


def _rwkv_kernel(z_ref, mu_ref, vec_ref, wup_ref, aup_ref, e_ref, minc_ref, strict_ref, incl_ref,
                 y_ref, bon_ref, st_ref, zprev_ref, *, rev, n_ctx, w):
    i = pl.program_id(0)
    nb, c, zw = z_ref.shape
    n_pairs = w // LANES
    rows = nb * c

    @pl.when(i == 0)
    def _():
        st_ref[...] = jnp.zeros_like(st_ref)

    @pl.when((i == 0) | (i == n_ctx))
    def _():
        zprev_ref[...] = jnp.zeros_like(zprev_ref)

    w0, a0, k_k, k_a, r_k = (vec_ref[j:j + 1, :] for j in range(5))
    e2 = e_ref[...]
    e22 = jnp.concatenate([e2, e2], axis=0)
    strict = strict_ref[...] > 0.0
    incl = incl_ref[...] > 0.0
    lane_lo = lax.broadcasted_iota(jnp.int32, (c, LANES), 1) < HEAD_DIM
    head_r = lax.broadcasted_iota(jnp.int32, (LANES, LANES), 0) // HEAD_DIM
    head_c = lax.broadcasted_iota(jnp.int32, (LANES, LANES), 1) // HEAD_DIM
    diag = head_r == head_c
    last = 0 if rev else c - 1

    def head_sums(x, pieces):
        outs = []
        for j in range(n_pairs):
            xj = x[:, j * LANES:(j + 1) * LANES]
            if pieces == 1:
                outs.append(_dot(xj.astype(BF16), e2))
            else:
                hi = xj.astype(BF16)
                mid = (xj - hi.astype(F32)).astype(BF16)
                outs.append(_dot(jnp.concatenate([hi, mid], axis=1), e22))
        return jnp.concatenate(outs, axis=1)

    def stack(xw):
        return jnp.concatenate([jnp.where(lane_lo, xw, 0.0), jnp.where(lane_lo, 0.0, xw)], axis=0)

    chains = [(b, j) for b in range(nb) for j in range(n_pairs)]

    def win(x, ch):
        b, j = ch
        return x[b * c:(b + 1) * c, j * LANES:(j + 1) * LANES]

    z = z_ref[...].reshape(rows, zw)
    rin = lax.broadcasted_iota(jnp.int32, (rows, zw), 0) % c
    prev = jnp.concatenate([jnp.broadcast_to(zprev_ref[b, 0:1, :], (c, zw)) for b in range(nb)], axis=0)
    if rev:
        zs = jnp.where(rin == c - 1, prev, pltpu.roll(z, rows - 1, axis=0))
        for b in range(nb):
            zprev_ref[b, 0:1, :] = z[b * c:b * c + 1, :]
    else:
        zs = jnp.where(rin == 0, prev, pltpu.roll(z, 1, axis=0))
        for b in range(nb):
            zprev_ref[b, 0:1, :] = z[b * c + c - 1:b * c + c, :]
    zd = z + (zs - z) * mu_ref[...]
    r = zd[:, 0:w]
    k = zd[:, w:2 * w]
    v = zd[:, 2 * w:3 * w]
    lora = zd[:, 3 * w:3 * w + LANES]
    lane = lax.broadcasted_iota(jnp.int32, (rows, LANES), 1)
    lora = jnp.where(lane < DECAY_LORA, jnp.tanh(lora), lora)
    logw = -DECAY_SCALE * _sigmoid(w0 + _dot_x3k(lora, wup_ref[...]))
    a = _sigmoid(a0 + _dot_x3k(lora, aup_ref[...]))
    kk0 = k * k_k
    kk = kk0 / jnp.maximum(jnp.sqrt(head_sums(kk0 * kk0, 2)), 1e-12)
    k2 = k * (1.0 + (a - 1.0) * k_a)
    bon_ref[...] = (head_sums(r * k2 * r_k, 1) * v).reshape(nb, c, w).astype(bon_ref.dtype)

    cinc = jnp.concatenate(
        [_dot(minc_ref[...], jnp.concatenate(_split3(logw[b * c:(b + 1) * c]), axis=0)) for b in range(nb)], axis=0)
    e_inc = jnp.exp(cinc)
    e_neg = jnp.exp(-cinc)
    rt = r * e_inc
    kt = k2 * e_neg
    bt = kk * a * e_neg
    kkt = kk * jnp.exp(cinc - logw)

    st = [st_ref[b * n_pairs + j] for b, j in chains]
    lhs = [jnp.concatenate([win(kkt, ch), win(rt, ch)], axis=0).astype(BF16) for ch in chains]
    g = [_dot_nt(l, jnp.concatenate([stack(win(bt, ch)), stack(win(kt, ch))], axis=0).astype(BF16))
         for l, ch in zip(lhs, chains)]
    a_b = [jnp.where(strict, x[0:c, 0:2 * c], 0.0) for x in g]
    a_k = [jnp.where(strict, x[0:c, 2 * c:4 * c], 0.0).astype(BF16) for x in g]
    r_kb = [jnp.concatenate([jnp.where(incl, x[c:2 * c, 2 * c:4 * c], 0.0),
                             -jnp.where(incl, x[c:2 * c, 0:2 * c], 0.0)], axis=1).astype(BF16) for x in g]
    x0 = [_dot_nt(l, s.astype(BF16)) for l, s in zip(lhs, st)]
    v_sb = [stack(win(v, ch)).astype(BF16) for ch in chains]
    u = [x[0:c] + _dot(ak, vs) for x, ak, vs in zip(x0, a_k, v_sb)]

    pw = a_b
    steps, sign = 1, -1.0
    while 2 * steps < c:
        both = [_dot(p.astype(BF16), jnp.concatenate([stack(p), stack(x)], axis=1).astype(BF16))
                for p, x in zip(pw, u)]
        u = [x + sign * y[:, 2 * c:4 * c] for x, y in zip(u, both)]
        pw = [y[:, 0:2 * c] for y in both]
        steps, sign = 2 * steps, 1.0
    u = [x + sign * _dot(p.astype(BF16), stack(x).astype(BF16)) for x, p in zip(u, pw)]

    y = [x[c:2 * c] + _dot(rk, jnp.concatenate([vs, stack(uu).astype(BF16)], axis=0))
         for x, rk, vs, uu in zip(x0, r_kb, v_sb, u)]
    upd = [_dot_tn(jnp.concatenate([win(v, ch), x], axis=0).astype(BF16),
                   jnp.concatenate([win(kt, ch), -win(bt, ch)], axis=0).astype(BF16))
           for ch, x in zip(chains, u)]
    for n, (b, j) in enumerate(chains):
        w_end = e_inc[b * c + last:b * c + last + 1, j * LANES:(j + 1) * LANES]
        st_ref[b * n_pairs + j] = jnp.where(diag, (st[n] + upd[n]) * w_end, 0.0)
        y_ref[b, :, j * LANES:(j + 1) * LANES] = y[n].astype(y_ref.dtype)


def _rwkv(p_z, prm, n_ctx, rev):
    b, t, zw = p_z.shape
    w = (zw - DECAY_LORA - ICLR_LORA) // 3
    c = RWKV_CHUNK
    n_tot = t // c
    mu, vecs, wup, aup, e_bf, minc, strict, incl = prm
    tix = lambda i: _scan_chunk(i, n_ctx, n_tot, rev)
    const = lambda i: (0, 0)
    full = lambda a: pl.BlockSpec(a.shape, const)
    return pl.pallas_call(
        functools.partial(_rwkv_kernel, rev=rev, n_ctx=n_ctx, w=w),
        grid=(n_tot,),
        in_specs=[pl.BlockSpec((b, c, zw), lambda i: (0, tix(i), 0)),
                  full(mu), full(vecs), full(wup), full(aup), full(e_bf), full(minc), full(strict), full(incl)],
        out_specs=[pl.BlockSpec((b, c, w), lambda i: (0, tix(i), 0)),
                   pl.BlockSpec((b, c, w), lambda i: (0, tix(i), 0))],
        out_shape=[jax.ShapeDtypeStruct((b, t, w), BF16), jax.ShapeDtypeStruct((b, t, w), BF16)],
        scratch_shapes=[pltpu.VMEM((b * (w // LANES), LANES, LANES), F32), pltpu.VMEM((b, 8, zw), F32)],
        compiler_params=_cparams("arbitrary"),
        name="rwkv7_rev" if rev else "rwkv7_fwd",
    )(p_z, mu, vecs, wup, aup, e_bf, minc, strict, incl)


def _rwkv_params(mu, w0, w_up, a0, a_up, k_k, k_a, r_k, rev, n_batch):
    w = w0.shape[0]
    c = RWKV_CHUNK
    vecs = jnp.concatenate([jnp.stack([w0, a0, k_k, k_a, r_k]), jnp.zeros((3, w), F32)], axis=0)
    wup = jnp.concatenate([w_up, jnp.zeros((ICLR_LORA, w), F32)], axis=0)
    aup = jnp.concatenate([jnp.zeros((DECAY_LORA, w), F32), a_up], axis=0)
    hid = jnp.arange(LANES) // HEAD_DIM
    e_bf = (hid[:, None] == hid[None, :]).astype(BF16)
    t = jnp.arange(c)
    p = (c - 1 - t) if rev else t
    le = p[None, :] <= p[:, None]
    lt = p[None, :] < p[:, None]
    strict = jnp.tile(lt, (1, 2)).astype(F32)
    incl = jnp.tile(le, (1, 2)).astype(F32)
    minc = jnp.tile(le, (1, 3)).astype(BF16)
    return mu.reshape(1, -1), vecs, wup, aup, e_bf, minc, strict, incl


def _lru_kernel(x_ref, cw_ref, vec_ref, wa_ref, wx_ref, h_ref, hcar_ref, ucar_ref, *, rev, n_ctx):
    i = pl.program_id(0)
    nb, c, w = x_ref.shape
    rows = nb * c

    @pl.when(i == 0)
    def _():
        hcar_ref[...] = jnp.zeros_like(hcar_ref)

    @pl.when((i == 0) | (i == n_ctx))
    def _():
        ucar_ref[...] = jnp.zeros_like(ucar_ref)

    u0 = x_ref[...].astype(F32).reshape(rows, w)
    row = lax.broadcasted_iota(jnp.int32, (rows, w), 0) % c

    def per_batch(ref, j):
        return jnp.concatenate([jnp.broadcast_to(ref[b, j:j + 1, :], (c, w)) for b in range(nb)], axis=0)

    def shifted(x, s, carry, fill):
        if rev:
            rolled = pltpu.roll(x, rows - s, axis=0)
            edge = row >= c - s
        else:
            rolled = pltpu.roll(x, s, axis=0)
            edge = row < s
        if carry is None:
            return jnp.where(edge, fill, rolled)
        return jnp.where(edge, carry, rolled)

    conv = vec_ref[0:1, :] + cw_ref[LRU_CONV - 1:LRU_CONV, :] * u0
    for m in range(1, LRU_CONV):
        car = jnp.zeros((rows, w), F32)
        for qpos in range(m):
            r_idx = (c - 1 - qpos) if rev else qpos
            car = jnp.where(row == r_idx, per_batch(ucar_ref, m - qpos - 1), car)
        conv = conv + cw_ref[LRU_CONV - 1 - m:LRU_CONV - m, :] * shifted(u0, m, car, None)
    for m in range(1, LRU_CONV):
        r_idx = (m - 1) if rev else (c - m)
        for b in range(nb):
            ucar_ref[b, m - 1:m, :] = u0[b * c + r_idx:b * c + r_idx + 1, :]

    cb = conv.astype(BF16)
    r = _sigmoid(_dot(cb, wa_ref[...]) + vec_ref[1:2, :])
    ig = _sigmoid(_dot(cb, wx_ref[...]) + vec_ref[2:3, :])
    log_a = -LRU_C * r * vec_ref[3:4, :]
    a = jnp.exp(log_a)
    bb = jnp.sqrt(1.0 - a * a) * (ig * conv)

    s = 1
    while s < c:
        bb = bb + a * shifted(bb, s, None, 0.0)
        a = a * shifted(a, s, None, 1.0)
        s *= 2
    h = bb + a * per_batch(hcar_ref, 0)
    h_ref[...] = h.reshape(nb, c, w).astype(h_ref.dtype)
    last = 0 if rev else c - 1
    for b in range(nb):
        hcar_ref[b, 0:1, :] = h[b * c + last:b * c + last + 1, :]


def _lru(p_lru, prm, n_ctx, rev):
    b, t, w2 = p_lru.shape
    w = w2 // 2
    c = LRU_CHUNK
    n_tot = t // c
    cw, vecs, wa, wx = prm
    tix = lambda i: _scan_chunk(i, n_ctx, n_tot, rev)
    const = lambda i: (0, 0)
    return pl.pallas_call(
        functools.partial(_lru_kernel, rev=rev, n_ctx=n_ctx),
        grid=(n_tot,),
        in_specs=[pl.BlockSpec((b, c, w), lambda i: (0, tix(i), 0)),
                  pl.BlockSpec(cw.shape, const), pl.BlockSpec(vecs.shape, const),
                  pl.BlockSpec(wa.shape, const), pl.BlockSpec(wx.shape, const)],
        out_specs=pl.BlockSpec((b, c, w), lambda i: (0, tix(i), 0)),
        out_shape=jax.ShapeDtypeStruct((b, t, w), BF16),
        scratch_shapes=[pltpu.VMEM((b, 8, w), F32), pltpu.VMEM((b, 8, w), F32)],
        compiler_params=_cparams("arbitrary"),
        name="rglru_rev" if rev else "rglru_fwd",
    )(p_lru, cw, vecs, wa, wx)


def _lru_params(conv_w, conv_b, wa, ba, wx, bx, lam):
    w = conv_b.shape[0]
    cw = jnp.concatenate([conv_w, jnp.zeros((8 - LRU_CONV, w), F32)], axis=0)
    vecs = jnp.concatenate([jnp.stack([conv_b, ba, bx, jax.nn.softplus(-lam)]), jnp.zeros((4, w), F32)], axis=0)
    return cw, vecs, jax.scipy.linalg.block_diag(*wa).astype(BF16), jax.scipy.linalg.block_diag(*wx).astype(BF16)


def _head_norm(y, e2, gain, bias, eps):
    inv = 1.0 / HEAD_DIM

    def head_sums(x):
        xb = x.astype(BF16)
        return jnp.concatenate([_dot(xb[:, j:j + LANES], e2) for j in range(0, x.shape[1], LANES)], axis=1)

    yc = y - head_sums(y) * inv
    var = head_sums(yc * yc) * inv
    return yc * lax.rsqrt(var + eps) * gain + bias


def _mix_out_kernel(x_ref, mod_ref, of_ref, ob_ref, g_ref, yf_ref, yb_ref, bf_ref, bb_ref, gd_ref,
                    hf_ref, hb_ref, lg_ref, gn_ref, e_ref, gup_ref, wout_ref, n2g_ref, wr_ref, br_ref, tri_ref,
                    xo_ref, h2_ref, idx_ref, gate_ref, rank_ref, cnt_ref, base_ref, *, w_ret, w_rw):
    @pl.when((pl.program_id(0) == 0) & (pl.program_id(1) == 0))
    def _():
        base_ref[...] = jnp.zeros_like(base_ref)

    rows = range(x_ref.shape[0])
    e_bf = e_ref[...]
    f32 = lambda ref, s: ref[s].astype(F32)
    g = [f32(g_ref, s) for s in rows]
    ret = [_head_norm(f32(of_ref, s) + f32(ob_ref, s), e_bf, gn_ref[0:1, :], gn_ref[1:2, :], RET_GN_EPS) for s in rows]
    ret = [r * (gg * _sigmoid(gg)) for r, gg in zip(ret, g)]
    gate = [_dot(_sigmoid(f32(gd_ref, s)).astype(BF16), gup_ref[...]) for s in rows]
    rw = [_head_norm(f32(yf_ref, s) + f32(yb_ref, s), e_bf, gn_ref[2:3, :], gn_ref[3:4, :], RWKV_GN_EPS) for s in rows]
    rw = [(r + f32(bf_ref, s) + f32(bb_ref, s)) * gt for r, gt, s in zip(rw, gate, rows)]
    lg = [f32(lg_ref, s) for s in rows]
    gelu = [0.5 * u * (1.0 + jnp.tanh(0.7978845608028654 * (u + 0.044715 * (u * u * u)))) for u in lg]
    lru = [(f32(hf_ref, s) + f32(hb_ref, s)) * ge for ge, s in zip(gelu, rows)]
    mix = [_dot(a.astype(BF16), wout_ref[0:w_ret, :])
           + _dot(b.astype(BF16), wout_ref[w_ret:w_ret + w_rw, :])
           + _dot(c.astype(BF16), wout_ref[w_ret + w_rw:, :]) for a, b, c in zip(ret, rw, lru)]
    x = [x_ref[s] + mod_ref[s, 0, 0:1, :] * m for m, s in zip(mix, rows)]
    h2 = []
    for s in rows:
        xo_ref[s] = x[s]
        ms = jnp.mean(x[s] * x[s], axis=-1, keepdims=True)
        hn = x[s] * lax.rsqrt(ms + NORM_EPS) * n2g_ref[...]
        h2.append(hn * (1.0 + mod_ref[s, 0, 2:3, :]) + mod_ref[s, 0, 1:2, :])
        h2_ref[s] = h2[s].astype(BF16)
    logits = [_dot_x3(h, wr_ref[...]) + br_ref[...] for h in h2]
    for s in rows:
        idx_o, gate_o, rank_o = _route_tile(logits[s], tri_ref[...], base_ref)
        idx_ref[s] = idx_o
        gate_ref[s] = gate_o
        rank_ref[s] = rank_o
    cnt_ref[...] = base_ref[...].astype(jnp.int32)


def _mix_out(xs, mod, o_f, o_b, p_ret, y_f, y_b, bon_f, bon_b, p_gd, h_f, h_b, p_lru,
             gn, e_bf, g_up_bf, w_out_bf, norm2_g, w_router, b_router, n_ctx_tiles):
    b, t, d = xs.shape
    tm = TOKEN_TILE
    nbb = MIX_ROWS if b % MIX_ROWS == 0 else 1
    w_ret, w_rw, w_lru = o_f.shape[2], y_f.shape[2], h_f.shape[2]
    ne = w_router.shape[1]
    tok = lambda wd, j=0: pl.BlockSpec((nbb, tm, wd), lambda bi, i: (bi, i, j))
    const = lambda a: pl.BlockSpec(a.shape, lambda bi, i: (0,) * a.ndim)
    seg = lambda bi, i: (bi, jnp.where(i >= n_ctx_tiles, 1, 0), 0, 0)
    n2g = norm2_g.reshape(1, d)
    br = b_router.reshape(1, ne)
    tt = jnp.arange(tm)
    tri = (tt[None, :] < tt[:, None]).astype(BF16)
    return pl.pallas_call(
        functools.partial(_mix_out_kernel, w_ret=w_ret, w_rw=w_rw),
        grid=(b // nbb, t // tm),
        in_specs=[tok(d), pl.BlockSpec((nbb, 1, 3, d), seg),
                  tok(w_ret), tok(w_ret), tok(w_ret, 3),
                  tok(w_rw), tok(w_rw), tok(w_rw), tok(w_rw), tok(GATE_LORA),
                  tok(w_lru), tok(w_lru), tok(w_lru, 1),
                  const(gn), const(e_bf), const(g_up_bf), const(w_out_bf), const(n2g), const(w_router), const(br),
                  const(tri)],
        out_specs=[tok(d), tok(d), tok(LANES), tok(LANES), tok(LANES),
                   pl.BlockSpec((8, ne), lambda bi, i: (0, 0))],
        out_shape=[jax.ShapeDtypeStruct((b, t, d), F32), jax.ShapeDtypeStruct((b, t, d), BF16),
                   jax.ShapeDtypeStruct((b, t, LANES), jnp.int32), jax.ShapeDtypeStruct((b, t, LANES), F32),
                   jax.ShapeDtypeStruct((b, t, LANES), jnp.int32), jax.ShapeDtypeStruct((8, ne), jnp.int32)],
        scratch_shapes=[pltpu.VMEM((8, ne), F32)],
        compiler_params=_cparams("arbitrary", "arbitrary"),
        name="mix_out",
    )(xs, mod, o_f, o_b, p_ret, y_f, y_b, bon_f, bon_b, p_gd, h_f, h_b, p_lru,
      gn, e_bf, g_up_bf, w_out_bf, n2g, w_router, br, tri)


def _moe_kernel(be_ref, first_ref, nu_ref, *refs, blk0, chained):
    x_ref, w1_ref, b1_ref, w2_ref, b2_ref = refs[1:6] if chained else refs[0:5]
    y_ref, w1b_ref, w2b_ref = refs[-3:]
    i = pl.program_id(0)
    blk = i + blk0
    de = w2_ref.shape[1]

    @pl.when((first_ref[blk] == 1) | (i == 0))
    def _():
        w1b_ref[...] = w1_ref[0].astype(BF16)
        w2b_ref[...] = w2_ref[0].astype(BF16)

    @pl.when(blk < nu_ref[0])
    def _():
        gu = _dot(x_ref[...], w1b_ref[...]) + b1_ref[0]
        glu = jnp.minimum(gu[:, :de], SWIGLU_LIMIT)
        lin = jnp.clip(gu[:, de:], -SWIGLU_LIMIT, SWIGLU_LIMIT)
        act = glu * _sigmoid(SWIGLU_ALPHA * glu) * (lin + 1.0)
        y_ref[...] = (_dot(act.astype(BF16), w2b_ref[...]) + b2_ref[0]).astype(y_ref.dtype)

    @pl.when(blk >= nu_ref[0])
    def _():
        y_ref[...] = jnp.zeros_like(y_ref)


def _moe_ffn(hb, block_e, first, n_used, w1, b1, w2, b2, layer, blk0, n_slots, y_prev=None):
    rows, d = hb.shape
    tm = MOE_TILE
    nl, ne, _, d2 = w1.shape
    de = w2.shape[2]
    chained = y_prev is not None
    wsel = lambda i, be, fi, nu: (layer, be[i + blk0], 0, 0)
    in_specs = [pl.BlockSpec((tm, d), lambda i, be, fi, nu: (i, 0)),
                pl.BlockSpec((None, 1, d, d2), wsel),
                pl.BlockSpec((None, 1, 1, d2), wsel),
                pl.BlockSpec((None, 1, de, d), wsel),
                pl.BlockSpec((None, 1, 1, d), wsel)]
    args = [hb, w1, b1.reshape(nl, ne, 1, d2), w2, b2.reshape(nl, ne, 1, d)]
    if chained:
        in_specs = [pl.BlockSpec(memory_space=pl.ANY)] + in_specs
        args = [y_prev] + args
    return pl.pallas_call(
        functools.partial(_moe_kernel, blk0=blk0, chained=chained),
        grid_spec=pltpu.PrefetchScalarGridSpec(
            num_scalar_prefetch=3,
            grid=(rows // tm,),
            in_specs=in_specs,
            out_specs=pl.BlockSpec((tm, d), lambda i, be, fi, nu: (i + blk0, 0)),
            scratch_shapes=[pltpu.VMEM((d, d2), BF16), pltpu.VMEM((de, d), BF16)],
        ),
        out_shape=jax.ShapeDtypeStruct((n_slots, d), BF16),
        input_output_aliases={3: 0} if chained else {},
        compiler_params=_cparams("arbitrary"),
        name="moe_ffn",
    )(block_e, first, n_used, *args)


def _route_tile(lg, tri, base_ref):
    tr, ne = lg.shape
    lane = lax.broadcasted_iota(jnp.int32, (tr, ne), 1).astype(F32)
    out_lane = lax.broadcasted_iota(jnp.int32, (tr, LANES), 1)
    vals = lg
    sel = jnp.zeros((tr, ne), F32)
    picks, tops = [], []
    for _ in range(TOP_K):
        m = jnp.max(vals, axis=-1, keepdims=True)
        ix = jnp.min(jnp.where(vals == m, lane, float(ne)), axis=-1, keepdims=True)
        hit = lane == ix
        sel = jnp.where(hit, 1.0, sel)
        vals = jnp.where(hit, -jnp.inf, vals)
        picks.append(ix)
        tops.append(m)
    ex = [jnp.exp(t - tops[0]) for t in tops]
    den = ex[0] + ex[1] + ex[2] + ex[3]
    before = _dot(tri, sel.astype(BF16)) + base_ref[0:1, :]
    idx_o = jnp.zeros((tr, LANES), F32)
    gate_o = jnp.zeros((tr, LANES), F32)
    rank_o = jnp.zeros((tr, LANES), F32)
    for k in range(TOP_K):
        rk = jnp.sum(jnp.where(lane == picks[k], before, 0.0), axis=-1, keepdims=True)
        idx_o = jnp.where(out_lane == k, picks[k], idx_o)
        gate_o = jnp.where(out_lane == k, ex[k] / den, gate_o)
        rank_o = jnp.where(out_lane == k, rk, rank_o)
    total = base_ref[0:1, :] + jnp.sum(sel, axis=0, keepdims=True)
    base_ref[...] = jnp.broadcast_to(total, base_ref.shape)
    return idx_o.astype(jnp.int32), gate_o, rank_o.astype(jnp.int32)


def _route_meta(idx, rank, counts):
    n_tok = idx.shape[0]
    ne = counts.shape[0]
    tm = MOE_TILE
    n_assign = n_tok * TOP_K
    padded = (counts + tm - 1) // tm * tm
    pend = jnp.cumsum(padded)
    pstart = pend - padded
    start = jnp.cumsum(counts) - counts
    eid = jnp.arange(ne, dtype=jnp.int32)
    slot = jnp.sum(jnp.where(idx[..., None] == eid, pstart, 0), axis=-1).astype(jnp.int32) + rank
    n_blocks = (n_assign + ne * (tm - 1) + tm - 1) // tm
    blk_start = jnp.arange(n_blocks, dtype=jnp.int32) * tm
    block_e = jnp.minimum(jnp.sum(pend[None, :] <= blk_start[:, None], axis=1), ne - 1).astype(jnp.int32)
    first = jnp.concatenate([jnp.ones((1,), jnp.int32), (block_e[1:] != block_e[:-1]).astype(jnp.int32)])
    n_used = (pend[-1] // tm).astype(jnp.int32).reshape(1)
    _, order = lax.sort_key_val(slot.reshape(-1), jnp.arange(n_assign, dtype=jnp.int32))
    order_tok = order // TOP_K
    off = jnp.arange(n_blocks * tm, dtype=jnp.int32) - jnp.repeat(pstart[block_e], tm)
    valid = off < jnp.repeat(counts[block_e], tm)
    pos = jnp.clip(jnp.repeat(start[block_e], tm) + off, 0, n_assign - 1)
    spread = jnp.arange(n_blocks * tm, dtype=jnp.int32) % n_tok
    slot_tok = jnp.where(valid, order_tok[pos], spread).astype(jnp.int32)
    return slot, slot_tok, block_e, first, n_used


def _combine_kernel(x_ref, mod_ref, y_ref, gate_ref, g_ref, *rest, final):
    o_ref = rest[-1]
    gate = gate_ref[0]
    y = y_ref[0, 0].astype(F32) * gate[:, 0:1]
    for k in range(1, TOP_K):
        y = y + y_ref[k, 0].astype(F32) * gate[:, k:k + 1]
    x = x_ref[0] + mod_ref[0, 0, 0:1, :] * y
    if final:
        ms = jnp.mean(x * x, axis=-1, keepdims=True)
        x = x * lax.rsqrt(ms + NORM_EPS) * g_ref[...]
    o_ref[0] = x


def _combine(xs, mod, yg, gates, final_g, n_ctx_tiles, final, b0, prev=None):
    b, t, d = xs.shape
    nbh = yg.shape[1]
    tm = TOKEN_TILE
    skip = n_ctx_tiles if final else 0
    seg = lambda bi, i: (bi + b0, jnp.where(i + skip >= n_ctx_tiles, 1, 0), 0, 0)
    in_specs = [pl.BlockSpec((1, tm, d), lambda bi, i: (bi + b0, i + skip, 0)),
                pl.BlockSpec((1, 1, 1, d), seg),
                pl.BlockSpec((TOP_K, 1, tm, d), lambda bi, i: (0, bi, i + skip, 0)),
                pl.BlockSpec((1, tm, LANES), lambda bi, i: (bi + b0, i + skip, 0)),
                pl.BlockSpec((1, d), lambda bi, i: (0, 0))]
    args = [xs, mod, yg, gates, final_g.reshape(1, d)]
    aliases = {} if final else {0: 0}
    if prev is not None:
        in_specs.append(pl.BlockSpec(memory_space=pl.ANY))
        args.append(prev)
        aliases = {5: 0}
    return pl.pallas_call(
        functools.partial(_combine_kernel, final=final),
        grid=(nbh, t // tm - skip),
        in_specs=in_specs,
        out_specs=pl.BlockSpec((1, tm, d), lambda bi, i: (bi + b0, i, 0)),
        out_shape=jax.ShapeDtypeStruct((b, t - skip * tm, d), F32),
        input_output_aliases=aliases,
        compiler_params=_cparams("parallel", "parallel"),
        name="combine_final" if final else "combine",
    )(*args)


def kernel(x, c, ctx, c_ctx, w_mod, b_mod, norm1_g, norm2_g, w_in, w_out, ret_decay_logit, ret_gn_g, ret_gn_b, rwkv_mu, rwkv_w0, rwkv_w_up, rwkv_a0, rwkv_a_up, rwkv_k_k, rwkv_k_a, rwkv_g_up, rwkv_r_k, rwkv_gn_g, rwkv_gn_b, lru_conv_w, lru_conv_b, lru_wa, lru_ba, lru_wx, lru_bx, lru_lambda, moe_w_router, moe_b_router, moe_w1, moe_b1, moe_w2, moe_b2, final_norm_g):
    bsz, seq, dm = x.shape
    n_ctx_tok = ctx.shape[1]
    depth = w_in.shape[0]
    n_experts = moe_w_router.shape[2]
    w_ret = 3 * dm // 8
    w_rw = 3 * dm // 8
    w_lru = dm - w_ret - w_rw
    zw = 3 * w_rw + DECAY_LORA + ICLR_LORA
    sizes = (4 * w_ret, zw, GATE_LORA, 2 * w_lru)
    bounds, off = [], 0
    for s in sizes:
        bounds.append((off, off + s))
        off += s
    bounds = tuple(bounds)
    assert off == w_in.shape[2]
    assert n_ctx_tok % TOKEN_TILE == 0 and seq % TOKEN_TILE == 0 and seq % GRID_W == 0
    t_all = n_ctx_tok + seq
    n_ctx_tiles = n_ctx_tok // TOKEN_TILE

    xs = jnp.concatenate([ctx, x], axis=1)
    cos_t, sin_t = _rope_tables(n_ctx_tok, seq, w_ret)
    hid = jnp.arange(LANES) // HEAD_DIM
    e_bf = (hid[:, None] == hid[None, :]).astype(BF16)
    cond =jnp.concatenate([c, c_ctx[None, :], jnp.zeros((8 - (bsz + 1) % 8, dm), F32)], axis=0)

    for l in range(depth):
        last = l == depth - 1
        mod = _modulation(cond, w_mod, b_mod, l)
        mod_l = mod[:bsz].reshape(bsz, 6, dm)
        mod_c = jnp.broadcast_to(mod[bsz].reshape(1, 6, dm), (bsz, 6, dm))
        modsel = jnp.stack([mod_c, mod_l], axis=1)

        p_ret, p_z, p_gd, p_lru = _in_proj(xs, modsel[:, :, 0:2], norm1_g[l], w_in[l].astype(BF16), bounds,
                                           (BF16, F32, BF16, BF16), n_ctx_tiles)

        ret_o, rw_y, rw_bon, lru_h = [], [], [], []
        for d in range(2):
            rev = d == 1
            ret_o.append(_retention(p_ret, cos_t, sin_t, _ret_tables(ret_decay_logit[l, d], w_ret, rev),
                                    n_ctx_tok // RET_CHUNK, rev))
            prm = _rwkv_params(rwkv_mu[l, d], rwkv_w0[l, d], rwkv_w_up[l, d], rwkv_a0[l, d], rwkv_a_up[l, d],
                               rwkv_k_k[l, d], rwkv_k_a[l, d], rwkv_r_k[l], rev, bsz)
            y, bon = _rwkv(p_z, prm, n_ctx_tok // RWKV_CHUNK, rev)
            rw_y.append(y)
            rw_bon.append(bon)
            lru_h.append(_lru(p_lru, _lru_params(lru_conv_w[l, d], lru_conv_b[l, d], lru_wa[l, d], lru_ba[l, d],
                                                 lru_wx[l, d], lru_bx[l, d], lru_lambda[l, d]),
                              n_ctx_tok // LRU_CHUNK, rev))

        gn = jnp.concatenate([jnp.stack([ret_gn_g[l], ret_gn_b[l], rwkv_gn_g[l], rwkv_gn_b[l]]),
                              jnp.zeros((4, w_ret), F32)], axis=0)
        xs, h2, idx, gates, rank, counts = _mix_out(
            xs, modsel[:, :, 2:5], ret_o[0], ret_o[1], p_ret, rw_y[0], rw_y[1],
            rw_bon[0], rw_bon[1], p_gd, lru_h[0], lru_h[1], p_lru,
            gn, e_bf, rwkv_g_up[l].astype(BF16), w_out[l].astype(BF16), norm2_g[l],
            moe_w_router[l], moe_b_router[l], n_ctx_tiles)

        n_tok = bsz * t_all
        slot, slot_tok, block_e, first, n_used = _route_meta(
            idx.reshape(n_tok, LANES)[:, :TOP_K], rank.reshape(n_tok, LANES)[:, :TOP_K], counts[0])
        n_slots = slot_tok.shape[0]
        row_a = max(n_slots // MOE_TILE // 8, 1) * MOE_TILE
        h2f = h2.reshape(n_tok, dm)
        y_sorted = None
        for lo, hi in ((0, row_a), (row_a, n_slots)):
            y_sorted = _moe_ffn(h2f[slot_tok[lo:hi]], block_e, first, n_used, moe_w1, moe_b1, moe_w2, moe_b2, l,
                                lo // MOE_TILE, n_slots, y_prev=y_sorted)
        slot_b = slot.reshape(bsz, t_all, TOP_K)
        out = None
        for lo, hi in [(b, b + 1) for b in range(bsz)]:
            yg = y_sorted[jnp.moveaxis(slot_b[lo:hi], 2, 0)]
            out = _combine(xs, modsel[:, :, 5:6], yg, gates, final_norm_g, n_ctx_tiles, last, lo,
                           prev=out if last else None)
            xs = xs if last else out
        xs = out
    return xs
```

```python
import functools

import jax
import jax.numpy as jnp
from jax import lax
from jax.experimental import pallas as pl
from jax.experimental.pallas import tpu as pltpu

F32 = jnp.float32
BF16 = jnp.bfloat16

HEAD_DIM = 64
NORM_EPS = 1e-6
RET_GN_EPS = 1e-5
RWKV_GN_EPS = 64e-5
ROPE_BASE = 10000.0
GRID_W = 64
LRU_CONV = 4
LRU_C = 8.0
TOP_K = 4
SWIGLU_LIMIT = 7.0
SWIGLU_ALPHA = 1.702
DECAY_LORA = 64
ICLR_LORA = 64
GATE_LORA = 128
DECAY_SCALE = 0.6065306597126334

LANES = 128
SUBLANES = 8
TOKEN_TILE = 256
RET_CHUNK = 128
RWKV_CHUNK = 64
LRU_CHUNK = 128
MOE_TILE = 512
MIX_ROWS = 4
VMEM_LIMIT = 56 * 1024 * 1024
GELU_C0 = 0.7978845608028654
GELU_C1 = 0.044715


def _cparams(*sem):
    return pltpu.CompilerParams(dimension_semantics=sem, vmem_limit_bytes=VMEM_LIMIT)


def _scan_chunk(i, n_ctx, n_tot, rev):
    if not rev:
        return i
    return jnp.where(i < n_ctx, n_ctx - 1 - i, n_tot + n_ctx - 1 - i)


def _split3(a):
    hi = a.astype(BF16)
    r1 = a - hi.astype(F32)
    mid = r1.astype(BF16)
    lo = (r1 - mid.astype(F32)).astype(BF16)
    return hi, mid, lo


def _dot(a, b):
    return jnp.dot(a, b, preferred_element_type=F32)


def _dot_nt(a, b):
    return lax.dot_general(a, b, (((1,), (1,)), ((), ())), preferred_element_type=F32)


def _dot_tn(a, b):
    return lax.dot_general(a, b, (((0,), (0,)), ((), ())), preferred_element_type=F32)


def _dot_x3(a, b):
    a_hi = a.astype(BF16)
    a_lo = (a - a_hi.astype(F32)).astype(BF16)
    b_hi = b.astype(BF16)
    b_lo = (b - b_hi.astype(F32)).astype(BF16)
    return _dot(a_hi, b_hi) + _dot(a_lo, b_hi) + _dot(a_hi, b_lo)


def _dot_x3k(a, b):
    a_hi = a.astype(BF16)
    a_lo = (a - a_hi.astype(F32)).astype(BF16)
    b_hi = b.astype(BF16)
    b_lo = (b - b_hi.astype(F32)).astype(BF16)
    return (_dot(jnp.concatenate([a_hi, a_lo], axis=1), jnp.concatenate([b_hi, b_hi], axis=0))
            + _dot(a_hi, b_lo))


def _pad_rows(a):
    pad = -a.shape[0] % SUBLANES
    return jnp.concatenate([a, jnp.zeros((pad, a.shape[1]), a.dtype)], axis=0) if pad else a


def _sigmoid(x):
    return 1.0 / (1.0 + jnp.exp(-x))


def _mod_kernel(c_ref, w_ref, b_ref, o_ref):
    c = c_ref[...]
    s = c * _sigmoid(c)
    o_ref[...] = _dot_x3(s, w_ref[...]) + b_ref[...]


def _modulation(cond, w_mod, b_mod, layer):
    r, d = cond.shape
    nl, _, n = w_mod.shape
    tn = d
    return pl.pallas_call(
        _mod_kernel,
        grid=(n // tn,),
        in_specs=[pl.BlockSpec((r, d), lambda j: (0, 0)),
                  pl.BlockSpec((None, d, tn), lambda j: (layer, 0, j)),
                  pl.BlockSpec((None, 1, tn), lambda j: (layer, 0, j))],
        out_specs=pl.BlockSpec((r, tn), lambda j: (0, j)),
        out_shape=jax.ShapeDtypeStruct((r, n), F32),
        compiler_params=_cparams("arbitrary"),
        name="modulation",
    )(cond, w_mod, b_mod.reshape(nl, 1, n))


def _in_proj_kernel(x_ref, mod_ref, g_ref, w_ref, *o_refs, bounds):
    x = x_ref[0]
    ms = jnp.mean(x * x, axis=-1, keepdims=True)
    h = x * lax.rsqrt(ms + NORM_EPS) * g_ref[...]
    h = h * (1.0 + mod_ref[0, 0, 1:2, :]) + mod_ref[0, 0, 0:1, :]
    hb = h.astype(BF16)
    for o_ref, (lo, hi) in zip(o_refs, bounds):
        o_ref[0] = _dot(hb, w_ref[:, lo:hi]).astype(o_ref.dtype)


def _in_proj(xs, mod, norm_g, w_in_bf, bounds, dtypes, n_ctx_tiles):
    b, t, d = xs.shape
    tm = TOKEN_TILE
    p = w_in_bf.shape[1]
    seg = lambda bi, i: (bi, jnp.where(i >= n_ctx_tiles, 1, 0), 0, 0)
    return pl.pallas_call(
        functools.partial(_in_proj_kernel, bounds=bounds),
        grid=(b, t // tm),
        in_specs=[pl.BlockSpec((1, tm, d), lambda bi, i: (bi, i, 0)),
                  pl.BlockSpec((1, 1, 2, d), seg),
                  pl.BlockSpec((1, d), lambda bi, i: (0, 0)),
                  pl.BlockSpec((d, p), lambda bi, i: (0, 0))],
        out_specs=[pl.BlockSpec((1, tm, hi - lo), lambda bi, i: (bi, i, 0)) for lo, hi in bounds],
        out_shape=[jax.ShapeDtypeStruct((b, t, hi - lo), dt) for (lo, hi), dt in zip(bounds, dtypes)],
        compiler_params=_cparams("parallel", "parallel"),
        name="in_proj",
    )(xs, mod, norm_g.reshape(1, d), w_in_bf)


def _ret_kernel(q_ref, k_ref, v_ref, cos_ref, sin_ref, dq_ref, dk_ref, dmat_ref, gm_ref, bm_ref,
                o_ref, s_ref):
    i = pl.program_id(0)

    @pl.when(i == 0)
    def _():
        s_ref[...] = jnp.zeros_like(s_ref)

    nb, c, w = q_ref.shape
    cos = cos_ref[...]
    sin = sin_ref[...]
    lane = lax.broadcasted_iota(jnp.int32, (c, LANES), 1)
    half, quarter = HEAD_DIM // 2, HEAD_DIM // 4
    first = (lane % half) < quarter

    def rope(u):
        parts = []
        for j in range(w // LANES):
            uj = u[:, j * LANES:(j + 1) * LANES]
            nxt = pltpu.roll(uj, LANES - quarter, axis=1)
            prv = pltpu.roll(uj, quarter, axis=1)
            parts.append(jnp.where(first, nxt, prv))
        return u * cos + jnp.concatenate(parts, axis=1) * sin

    lane_lo = lane < HEAD_DIM

    def stack(xw):
        return jnp.concatenate([jnp.where(lane_lo, xw, 0.0), jnp.where(lane_lo, 0.0, xw)], axis=0)

    n_pairs = w // LANES
    q = [rope(q_ref[b].astype(F32)) for b in range(nb)]
    k = [rope(k_ref[b].astype(F32)) for b in range(nb)]
    chains = [(b, j) for b in range(nb) for j in range(n_pairs)]
    pair = lambda x, j: x[:, j * LANES:(j + 1) * LANES]
    qw = [pair(q[b], j) for b, j in chains]
    kw = [pair(k[b], j) for b, j in chains]
    vw = [pair(v_ref[b], j) for b, j in chains]
    s = [s_ref[b * n_pairs + j] for b, j in chains]
    inter = [_dot((x * pair(dq_ref[...], j)).astype(BF16), st.astype(BF16)) for x, st, (b, j) in zip(qw, s, chains)]
    sc = [_dot_nt(x.astype(BF16), stack(y).astype(BF16)) * dmat_ref[j]
          for x, y, (b, j) in zip(qw, kw, chains)]
    intra = [_dot(x.astype(BF16), stack(y).astype(BF16)) for x, y in zip(sc, vw)]
    ktv = [_dot_tn((y * pair(dk_ref[...], j)).astype(BF16), z.astype(BF16))
           for y, z, (b, j) in zip(kw, vw, chains)]
    for n, (b, j) in enumerate(chains):
        o_ref[b, :, j * LANES:(j + 1) * LANES] = (inter[n] + intra[n]).astype(o_ref.dtype)
        s_ref[b * n_pairs + j] = gm_ref[j] * s[n] + bm_ref[...] * ktv[n]


def _retention(p_ret, cos_t, sin_t, tabs, n_ctx, rev):
    b, t, w4 = p_ret.shape
    w = w4 // 4
    c = RET_CHUNK
    n_tot = t // c
    n_pairs = w // LANES
    dq, dk, dmat, gm, bm = tabs
    tix = lambda i: _scan_chunk(i, n_ctx, n_tot, rev)
    col = lambda j: (lambda i: (0, tix(i), j))
    const = lambda a: pl.BlockSpec(a.shape, lambda i: (0,) * a.ndim)
    return pl.pallas_call(
        _ret_kernel,
        grid=(n_tot,),
        in_specs=[pl.BlockSpec((b, c, w), col(0)), pl.BlockSpec((b, c, w), col(1)), pl.BlockSpec((b, c, w), col(2)),
                  pl.BlockSpec((c, w), lambda i: (tix(i), 0)),
                  pl.BlockSpec((c, w), lambda i: (tix(i), 0)),
                  const(dq), const(dk), const(dmat), const(gm), const(bm)],
        out_specs=pl.BlockSpec((b, c, w), lambda i: (0, tix(i), 0)),
        out_shape=jax.ShapeDtypeStruct((b, t, w), BF16),
        scratch_shapes=[pltpu.VMEM((b * n_pairs, LANES, LANES), F32)],
        compiler_params=_cparams("arbitrary"),
        name="retention_rev" if rev else "retention_fwd",
    )(p_ret, p_ret, p_ret, cos_t, sin_t, dq, dk, dmat, gm, bm)


def _ret_tables(decay_logit, w, rev):
    n_heads = w // HEAD_DIM
    c = RET_CHUNK
    lg = jax.nn.log_sigmoid(decay_logit.astype(F32))
    t = jnp.arange(c, dtype=F32)
    p = (c - 1.0 - t) if rev else t
    rel = p[:, None] - p[None, :]
    scale = HEAD_DIM ** -0.5
    dmat = jnp.where(rel >= 0, jnp.exp(lg[:, None, None] * jnp.maximum(rel, 0.0)), 0.0) * scale
    dq = jnp.exp(lg[:, None] * (p + 1.0)) * scale
    dk = jnp.exp(lg[:, None] * (c - 1.0 - p))
    lanes = lambda a: jnp.repeat(a.T, HEAD_DIM, axis=1)
    n_pairs = n_heads // 2
    dmat_w = dmat.reshape(n_pairs, 2, c, c).transpose(0, 2, 1, 3).reshape(n_pairs, c, 2 * c)
    hid = jnp.arange(LANES) // HEAD_DIM
    bm = (hid[:, None] == hid[None, :]).astype(F32)
    gm = bm[None] * jnp.exp(lg * c).reshape(n_pairs, 2)[:, hid][:, :, None]
    return lanes(dq), lanes(dk), dmat_w, gm, bm


def _rope_tables(n_ctx_tok, seq, w):
    half = HEAD_DIM // 2
    quarter = half // 2
    inv_freq = ROPE_BASE ** (-jnp.arange(quarter, dtype=F32) / quarter)
    tok = jnp.arange(seq)
    rows = (tok // GRID_W).astype(F32)
    cols = (tok % GRID_W).astype(F32)
    o = jnp.arange(w) % HEAD_DIM
    pos = jnp.where(o[None, :] < half, rows[:, None], cols[:, None])
    ang = pos * inv_freq[o % quarter][None, :]
    sign = jnp.where((o % half) < quarter, -1.0, 1.0)[None, :]
    cos = jnp.concatenate([jnp.ones((n_ctx_tok, w), F32), jnp.cos(ang)], axis=0)
    sin = jnp.concatenate([jnp.zeros((n_ctx_tok, w), F32), jnp.sin(ang) * sign], axis=0)
    return cos, sin


def _rwkv_kernel(z_ref, mu_ref, vec_ref, wup_ref, aup_ref, e_ref, minc_ref, strict_ref, incl_ref,
                 y_ref, bon_ref, st_ref, zprev_ref, *, rev, n_ctx, w):
    i = pl.program_id(0)
    nb, c, zw = z_ref.shape
    n_pairs = w // LANES
    rows = nb * c

    @pl.when(i == 0)
    def _():
        st_ref[...] = jnp.zeros_like(st_ref)

    @pl.when((i == 0) | (i == n_ctx))
    def _():
        zprev_ref[...] = jnp.zeros_like(zprev_ref)

    w0, a0, k_k, k_a, r_k = (vec_ref[j:j + 1, :] for j in range(5))
    e2 = e_ref[...]
    e22 = jnp.concatenate([e2, e2], axis=0)
    strict = strict_ref[...] > 0.0
    incl = incl_ref[...] > 0.0
    lane_lo = lax.broadcasted_iota(jnp.int32, (c, LANES), 1) < HEAD_DIM
    head_r = lax.broadcasted_iota(jnp.int32, (LANES, LANES), 0) // HEAD_DIM
    head_c = lax.broadcasted_iota(jnp.int32, (LANES, LANES), 1) // HEAD_DIM
    diag = head_r == head_c
    last = 0 if rev else c - 1

    def head_sums(x, pieces):
        outs = []
        for j in range(n_pairs):
            xj = x[:, j * LANES:(j + 1) * LANES]
            if pieces == 1:
                outs.append(_dot(xj.astype(BF16), e2))
            else:
                hi = xj.astype(BF16)
                mid = (xj - hi.astype(F32)).astype(BF16)
                outs.append(_dot(jnp.concatenate([hi, mid], axis=1), e22))
        return jnp.concatenate(outs, axis=1)

    def stack(xw):
        return jnp.concatenate([jnp.where(lane_lo, xw, 0.0), jnp.where(lane_lo, 0.0, xw)], axis=0)

    chains = [(b, j) for b in range(nb) for j in range(n_pairs)]

    def win(x, ch):
        b, j = ch
        return x[b * c:(b + 1) * c, j * LANES:(j + 1) * LANES]

    z = z_ref[...].reshape(rows, zw)
    rin = lax.broadcasted_iota(jnp.int32, (rows, zw), 0) % c
    prev = jnp.concatenate([jnp.broadcast_to(zprev_ref[b, 0:1, :], (c, zw)) for b in range(nb)], axis=0)
    if rev:
        zs = jnp.where(rin == c - 1, prev, pltpu.roll(z, rows - 1, axis=0))
        for b in range(nb):
            zprev_ref[b, 0:1, :] = z[b * c:b * c + 1, :]
    else:
        zs = jnp.where(rin == 0, prev, pltpu.roll(z, 1, axis=0))
        for b in range(nb):
            zprev_ref[b, 0:1, :] = z[b * c + c - 1:b * c + c, :]
    zd = z + (zs - z) * mu_ref[...]
    r = zd[:, 0:w]
    k = zd[:, w:2 * w]
    v = zd[:, 2 * w:3 * w]
    lora = zd[:, 3 * w:3 * w + LANES]
    lane = lax.broadcasted_iota(jnp.int32, (rows, LANES), 1)
    lora = jnp.where(lane < DECAY_LORA, jnp.tanh(lora), lora)
    logw = -DECAY_SCALE * _sigmoid(w0 + _dot_x3k(lora, wup_ref[...]))
    a = _sigmoid(a0 + _dot_x3k(lora, aup_ref[...]))
    kk0 = k * k_k
    kk = kk0 / jnp.maximum(jnp.sqrt(head_sums(kk0 * kk0, 2)), 1e-12)
    k2 = k * (1.0 + (a - 1.0) * k_a)
    bon_ref[...] = (head_sums(r * k2 * r_k, 1) * v).reshape(nb, c, w).astype(bon_ref.dtype)

    cinc = jnp.concatenate(
        [_dot(minc_ref[...], jnp.concatenate(_split3(logw[b * c:(b + 1) * c]), axis=0)) for b in range(nb)], axis=0)
    e_inc = jnp.exp(cinc)
    e_neg = jnp.exp(-cinc)
    rt = r * e_inc
    kt = k2 * e_neg
    bt = kk * a * e_neg
    kkt = kk * jnp.exp(cinc - logw)

    st = [st_ref[b * n_pairs + j] for b, j in chains]
    lhs = [jnp.concatenate([win(kkt, ch), win(rt, ch)], axis=0).astype(BF16) for ch in chains]
    g = [_dot_nt(l, jnp.concatenate([stack(win(bt, ch)), stack(win(kt, ch))], axis=0).astype(BF16))
         for l, ch in zip(lhs, chains)]
    a_b = [jnp.where(strict, x[0:c, 0:2 * c], 0.0) for x in g]
    a_k = [jnp.where(strict, x[0:c, 2 * c:4 * c], 0.0).astype(BF16) for x in g]
    r_kb = [jnp.concatenate([jnp.where(incl, x[c:2 * c, 2 * c:4 * c], 0.0),
                             -jnp.where(incl, x[c:2 * c, 0:2 * c], 0.0)], axis=1).astype(BF16) for x in g]
    x0 = [_dot_nt(l, s.astype(BF16)) for l, s in zip(lhs, st)]
    v_sb = [stack(win(v, ch)).astype(BF16) for ch in chains]
    u = [x[0:c] + _dot(ak, vs) for x, ak, vs in zip(x0, a_k, v_sb)]

    pw = a_b
    steps, sign = 1, -1.0
    while 2 * steps < c:
        both = [_dot(p.astype(BF16), jnp.concatenate([stack(p), stack(x)], axis=1).astype(BF16))
                for p, x in zip(pw, u)]
        u = [x + sign * y[:, 2 * c:4 * c] for x, y in zip(u, both)]
        pw = [y[:, 0:2 * c] for y in both]
        steps, sign = 2 * steps, 1.0
    u = [x + sign * _dot(p.astype(BF16), stack(x).astype(BF16)) for x, p in zip(u, pw)]

    y = [x[c:2 * c] + _dot(rk, jnp.concatenate([vs, stack(uu).astype(BF16)], axis=0))
         for x, rk, vs, uu in zip(x0, r_kb, v_sb, u)]
    upd = [_dot_tn(jnp.concatenate([win(v, ch), x], axis=0).astype(BF16),
                   jnp.concatenate([win(kt, ch), -win(bt, ch)], axis=0).astype(BF16))
           for ch, x in zip(chains, u)]
    for n, (b, j) in enumerate(chains):
        w_end = e_inc[b * c + last:b * c + last + 1, j * LANES:(j + 1) * LANES]
        st_ref[b * n_pairs + j] = jnp.where(diag, (st[n] + upd[n]) * w_end, 0.0)
        y_ref[b, :, j * LANES:(j + 1) * LANES] = y[n].astype(y_ref.dtype)


def _rwkv(p_z, prm, n_ctx, rev):
    b, t, zw = p_z.shape
    w = (zw - DECAY_LORA - ICLR_LORA) // 3
    c = RWKV_CHUNK
    n_tot = t // c
    mu, vecs, wup, aup, e_bf, minc, strict, incl = prm
    tix = lambda i: _scan_chunk(i, n_ctx, n_tot, rev)
    const = lambda i: (0, 0)
    full = lambda a: pl.BlockSpec(a.shape, const)
    return pl.pallas_call(
        functools.partial(_rwkv_kernel, rev=rev, n_ctx=n_ctx, w=w),
        grid=(n_tot,),
        in_specs=[pl.BlockSpec((b, c, zw), lambda i: (0, tix(i), 0)),
                  full(mu), full(vecs), full(wup), full(aup), full(e_bf), full(minc), full(strict), full(incl)],
        out_specs=[pl.BlockSpec((b, c, w), lambda i: (0, tix(i), 0)),
                   pl.BlockSpec((b, c, w), lambda i: (0, tix(i), 0))],
        out_shape=[jax.ShapeDtypeStruct((b, t, w), BF16), jax.ShapeDtypeStruct((b, t, w), BF16)],
        scratch_shapes=[pltpu.VMEM((b * (w // LANES), LANES, LANES), F32), pltpu.VMEM((b, SUBLANES, zw), F32)],
        compiler_params=_cparams("arbitrary"),
        name="rwkv7_rev" if rev else "rwkv7_fwd",
    )(p_z, mu, vecs, wup, aup, e_bf, minc, strict, incl)


def _rwkv_params(mu, w0, w_up, a0, a_up, k_k, k_a, r_k, rev, n_batch):
    w = w0.shape[0]
    c = RWKV_CHUNK
    vecs = _pad_rows(jnp.stack([w0, a0, k_k, k_a, r_k]))
    wup = jnp.concatenate([w_up, jnp.zeros((ICLR_LORA, w), F32)], axis=0)
    aup = jnp.concatenate([jnp.zeros((DECAY_LORA, w), F32), a_up], axis=0)
    hid = jnp.arange(LANES) // HEAD_DIM
    e_bf = (hid[:, None] == hid[None, :]).astype(BF16)
    t = jnp.arange(c)
    p = (c - 1 - t) if rev else t
    le = p[None, :] <= p[:, None]
    lt = p[None, :] < p[:, None]
    strict = jnp.tile(lt, (1, 2)).astype(F32)
    incl = jnp.tile(le, (1, 2)).astype(F32)
    minc = jnp.tile(le, (1, 3)).astype(BF16)
    return mu.reshape(1, -1), vecs, wup, aup, e_bf, minc, strict, incl


def _lru_kernel(x_ref, cw_ref, vec_ref, wa_ref, wx_ref, h_ref, hcar_ref, ucar_ref, *, rev, n_ctx):
    i = pl.program_id(0)
    nb, c, w = x_ref.shape
    rows = nb * c

    @pl.when(i == 0)
    def _():
        hcar_ref[...] = jnp.zeros_like(hcar_ref)

    @pl.when((i == 0) | (i == n_ctx))
    def _():
        ucar_ref[...] = jnp.zeros_like(ucar_ref)

    u0 = x_ref[...].astype(F32).reshape(rows, w)
    row = lax.broadcasted_iota(jnp.int32, (rows, w), 0) % c

    def per_batch(ref, j):
        return jnp.concatenate([jnp.broadcast_to(ref[b, j:j + 1, :], (c, w)) for b in range(nb)], axis=0)

    def shifted(x, s, carry, fill):
        if rev:
            rolled = pltpu.roll(x, rows - s, axis=0)
            edge = row >= c - s
        else:
            rolled = pltpu.roll(x, s, axis=0)
            edge = row < s
        if carry is None:
            return jnp.where(edge, fill, rolled)
        return jnp.where(edge, carry, rolled)

    conv = vec_ref[0:1, :] + cw_ref[LRU_CONV - 1:LRU_CONV, :] * u0
    for m in range(1, LRU_CONV):
        car = jnp.zeros((rows, w), F32)
        for qpos in range(m):
            r_idx = (c - 1 - qpos) if rev else qpos
            car = jnp.where(row == r_idx, per_batch(ucar_ref, m - qpos - 1), car)
        conv = conv + cw_ref[LRU_CONV - 1 - m:LRU_CONV - m, :] * shifted(u0, m, car, None)
    for m in range(1, LRU_CONV):
        r_idx = (m - 1) if rev else (c - m)
        for b in range(nb):
            ucar_ref[b, m - 1:m, :] = u0[b * c + r_idx:b * c + r_idx + 1, :]

    cb = conv.astype(BF16)
    r = _sigmoid(_dot(cb, wa_ref[...]) + vec_ref[1:2, :])
    ig = _sigmoid(_dot(cb, wx_ref[...]) + vec_ref[2:3, :])
    log_a = -LRU_C * r * vec_ref[3:4, :]
    a = jnp.exp(log_a)
    bb = jnp.sqrt(1.0 - a * a) * (ig * conv)

    s = 1
    while s < c:
        bb = bb + a * shifted(bb, s, None, 0.0)
        a = a * shifted(a, s, None, 1.0)
        s *= 2
    h = bb + a * per_batch(hcar_ref, 0)
    h_ref[...] = h.reshape(nb, c, w).astype(h_ref.dtype)
    last = 0 if rev else c - 1
    for b in range(nb):
        hcar_ref[b, 0:1, :] = h[b * c + last:b * c + last + 1, :]


def _lru(p_lru, prm, n_ctx, rev):
    b, t, w2 = p_lru.shape
    w = w2 // 2
    c = LRU_CHUNK
    n_tot = t // c
    cw, vecs, wa, wx = prm
    tix = lambda i: _scan_chunk(i, n_ctx, n_tot, rev)
    const = lambda i: (0, 0)
    return pl.pallas_call(
        functools.partial(_lru_kernel, rev=rev, n_ctx=n_ctx),
        grid=(n_tot,),
        in_specs=[pl.BlockSpec((b, c, w), lambda i: (0, tix(i), 0)),
                  pl.BlockSpec(cw.shape, const), pl.BlockSpec(vecs.shape, const),
                  pl.BlockSpec(wa.shape, const), pl.BlockSpec(wx.shape, const)],
        out_specs=pl.BlockSpec((b, c, w), lambda i: (0, tix(i), 0)),
        out_shape=jax.ShapeDtypeStruct((b, t, w), BF16),
        scratch_shapes=[pltpu.VMEM((b, SUBLANES, w), F32), pltpu.VMEM((b, SUBLANES, w), F32)],
        compiler_params=_cparams("arbitrary"),
        name="rglru_rev" if rev else "rglru_fwd",
    )(p_lru, cw, vecs, wa, wx)


def _lru_params(conv_w, conv_b, wa, ba, wx, bx, lam):
    w = conv_b.shape[0]
    cw = _pad_rows(conv_w)
    vecs = _pad_rows(jnp.stack([conv_b, ba, bx, jax.nn.softplus(-lam)]))
    return cw, vecs, jax.scipy.linalg.block_diag(*wa).astype(BF16), jax.scipy.linalg.block_diag(*wx).astype(BF16)


def _head_norm(y, e2, gain, bias, eps):
    inv = 1.0 / HEAD_DIM

    def head_sums(x):
        xb = x.astype(BF16)
        return jnp.concatenate([_dot(xb[:, j:j + LANES], e2) for j in range(0, x.shape[1], LANES)], axis=1)

    yc = y - head_sums(y) * inv
    var = head_sums(yc * yc) * inv
    return yc * lax.rsqrt(var + eps) * gain + bias


def _mix_out_kernel(x_ref, mod_ref, of_ref, ob_ref, g_ref, yf_ref, yb_ref, bf_ref, bb_ref, gd_ref,
                    hf_ref, hb_ref, lg_ref, gn_ref, e_ref, gup_ref, wout_ref, n2g_ref, wr_ref, br_ref, tri_ref,
                    xo_ref, h2_ref, idx_ref, gate_ref, rank_ref, cnt_ref, base_ref, *, w_ret, w_rw):
    @pl.when((pl.program_id(0) == 0) & (pl.program_id(1) == 0))
    def _():
        base_ref[...] = jnp.zeros_like(base_ref)

    rows = range(x_ref.shape[0])
    e_bf = e_ref[...]
    f32 = lambda ref, s: ref[s].astype(F32)
    g = [f32(g_ref, s) for s in rows]
    ret = [_head_norm(f32(of_ref, s) + f32(ob_ref, s), e_bf, gn_ref[0:1, :], gn_ref[1:2, :], RET_GN_EPS) for s in rows]
    ret = [r * (gg * _sigmoid(gg)) for r, gg in zip(ret, g)]
    gate = [_dot(_sigmoid(f32(gd_ref, s)).astype(BF16), gup_ref[...]) for s in rows]
    rw = [_head_norm(f32(yf_ref, s) + f32(yb_ref, s), e_bf, gn_ref[2:3, :], gn_ref[3:4, :], RWKV_GN_EPS) for s in rows]
    rw = [(r + f32(bf_ref, s) + f32(bb_ref, s)) * gt for r, gt, s in zip(rw, gate, rows)]
    lg = [f32(lg_ref, s) for s in rows]
    gelu = [0.5 * u * (1.0 + jnp.tanh(GELU_C0 * (u + GELU_C1 * (u * u * u)))) for u in lg]
    lru = [(f32(hf_ref, s) + f32(hb_ref, s)) * ge for ge, s in zip(gelu, rows)]
    mix = [_dot(a.astype(BF16), wout_ref[0:w_ret, :])
           + _dot(b.astype(BF16), wout_ref[w_ret:w_ret + w_rw, :])
           + _dot(c.astype(BF16), wout_ref[w_ret + w_rw:, :]) for a, b, c in zip(ret, rw, lru)]
    x = [x_ref[s] + mod_ref[s, 0, 0:1, :] * m for m, s in zip(mix, rows)]
    h2 = []
    for s in rows:
        xo_ref[s] = x[s]
        ms = jnp.mean(x[s] * x[s], axis=-1, keepdims=True)
        hn = x[s] * lax.rsqrt(ms + NORM_EPS) * n2g_ref[...]
        h2.append(hn * (1.0 + mod_ref[s, 0, 2:3, :]) + mod_ref[s, 0, 1:2, :])
        h2_ref[s] = h2[s].astype(BF16)
    logits = [_dot_x3(h, wr_ref[...]) + br_ref[...] for h in h2]
    for s in rows:
        idx_o, gate_o, rank_o = _route_tile(logits[s], tri_ref[...], base_ref)
        idx_ref[s] = idx_o
        gate_ref[s] = gate_o
        rank_ref[s] = rank_o
    cnt_ref[...] = base_ref[...].astype(jnp.int32)


def _mix_out(xs, mod, o_f, o_b, p_ret, y_f, y_b, bon_f, bon_b, p_gd, h_f, h_b, p_lru,
             gn, e_bf, g_up_bf, w_out_bf, norm2_g, w_router, b_router, n_ctx_tiles):
    b, t, d = xs.shape
    tm = TOKEN_TILE
    nbb = MIX_ROWS if b % MIX_ROWS == 0 else 1
    w_ret, w_rw, w_lru = o_f.shape[2], y_f.shape[2], h_f.shape[2]
    ne = w_router.shape[1]
    tok = lambda wd, j=0: pl.BlockSpec((nbb, tm, wd), lambda bi, i: (bi, i, j))
    const = lambda a: pl.BlockSpec(a.shape, lambda bi, i: (0,) * a.ndim)
    seg = lambda bi, i: (bi, jnp.where(i >= n_ctx_tiles, 1, 0), 0, 0)
    n2g = norm2_g.reshape(1, d)
    br = b_router.reshape(1, ne)
    tt = jnp.arange(tm)
    tri = (tt[None, :] < tt[:, None]).astype(BF16)
    return pl.pallas_call(
        functools.partial(_mix_out_kernel, w_ret=w_ret, w_rw=w_rw),
        grid=(b // nbb, t // tm),
        in_specs=[tok(d), pl.BlockSpec((nbb, 1, 3, d), seg),
                  tok(w_ret), tok(w_ret), tok(w_ret, 3),
                  tok(w_rw), tok(w_rw), tok(w_rw), tok(w_rw), tok(GATE_LORA),
                  tok(w_lru), tok(w_lru), tok(w_lru, 1),
                  const(gn), const(e_bf), const(g_up_bf), const(w_out_bf), const(n2g), const(w_router), const(br),
                  const(tri)],
        out_specs=[tok(d), tok(d), tok(LANES), tok(LANES), tok(LANES),
                   pl.BlockSpec((SUBLANES, ne), lambda bi, i: (0, 0))],
        out_shape=[jax.ShapeDtypeStruct((b, t, d), F32), jax.ShapeDtypeStruct((b, t, d), BF16),
                   jax.ShapeDtypeStruct((b, t, LANES), jnp.int32), jax.ShapeDtypeStruct((b, t, LANES), F32),
                   jax.ShapeDtypeStruct((b, t, LANES), jnp.int32), jax.ShapeDtypeStruct((SUBLANES, ne), jnp.int32)],
        scratch_shapes=[pltpu.VMEM((SUBLANES, ne), F32)],
        compiler_params=_cparams("arbitrary", "arbitrary"),
        name="mix_out",
    )(xs, mod, o_f, o_b, p_ret, y_f, y_b, bon_f, bon_b, p_gd, h_f, h_b, p_lru,
      gn, e_bf, g_up_bf, w_out_bf, n2g, w_router, br, tri)


def _moe_kernel(be_ref, first_ref, nu_ref, *refs, blk0, chained):
    x_ref, w1_ref, b1_ref, w2_ref, b2_ref = refs[1:6] if chained else refs[0:5]
    y_ref, w1b_ref, w2b_ref = refs[-3:]
    i = pl.program_id(0)
    blk = i + blk0
    de = w2_ref.shape[1]

    @pl.when((first_ref[blk] == 1) | (i == 0))
    def _():
        w1b_ref[...] = w1_ref[0].astype(BF16)
        w2b_ref[...] = w2_ref[0].astype(BF16)

    @pl.when(blk < nu_ref[0])
    def _():
        gu = _dot(x_ref[...], w1b_ref[...]) + b1_ref[0]
        glu = jnp.minimum(gu[:, :de], SWIGLU_LIMIT)
        lin = jnp.clip(gu[:, de:], -SWIGLU_LIMIT, SWIGLU_LIMIT)
        act = glu * _sigmoid(SWIGLU_ALPHA * glu) * (lin + 1.0)
        y_ref[...] = (_dot(act.astype(BF16), w2b_ref[...]) + b2_ref[0]).astype(y_ref.dtype)

    @pl.when(blk >= nu_ref[0])
    def _():
        y_ref[...] = jnp.zeros_like(y_ref)


def _moe_ffn(hb, block_e, first, n_used, w1, b1, w2, b2, layer, blk0, n_slots, y_prev=None):
    rows, d = hb.shape
    tm = MOE_TILE
    nl, ne, _, d2 = w1.shape
    de = w2.shape[2]
    chained = y_prev is not None
    wsel = lambda i, be, fi, nu: (layer, be[i + blk0], 0, 0)
    in_specs = [pl.BlockSpec((tm, d), lambda i, be, fi, nu: (i, 0)),
                pl.BlockSpec((None, 1, d, d2), wsel),
                pl.BlockSpec((None, 1, 1, d2), wsel),
                pl.BlockSpec((None, 1, de, d), wsel),
                pl.BlockSpec((None, 1, 1, d), wsel)]
    args = [hb, w1, b1.reshape(nl, ne, 1, d2), w2, b2.reshape(nl, ne, 1, d)]
    if chained:
        in_specs = [pl.BlockSpec(memory_space=pl.ANY)] + in_specs
        args = [y_prev] + args
    return pl.pallas_call(
        functools.partial(_moe_kernel, blk0=blk0, chained=chained),
        grid_spec=pltpu.PrefetchScalarGridSpec(
            num_scalar_prefetch=3,
            grid=(rows // tm,),
            in_specs=in_specs,
            out_specs=pl.BlockSpec((tm, d), lambda i, be, fi, nu: (i + blk0, 0)),
            scratch_shapes=[pltpu.VMEM((d, d2), BF16), pltpu.VMEM((de, d), BF16)],
        ),
        out_shape=jax.ShapeDtypeStruct((n_slots, d), BF16),
        input_output_aliases={3: 0} if chained else {},
        compiler_params=_cparams("arbitrary"),
        name="moe_ffn",
    )(block_e, first, n_used, *args)


def _route_tile(lg, tri, base_ref):
    tr, ne = lg.shape
    lane = lax.broadcasted_iota(jnp.int32, (tr, ne), 1).astype(F32)
    out_lane = lax.broadcasted_iota(jnp.int32, (tr, LANES), 1)
    vals = lg
    sel = jnp.zeros((tr, ne), F32)
    picks, tops = [], []
    for _ in range(TOP_K):
        m = jnp.max(vals, axis=-1, keepdims=True)
        ix = jnp.min(jnp.where(vals == m, lane, float(ne)), axis=-1, keepdims=True)
        hit = lane == ix
        sel = jnp.where(hit, 1.0, sel)
        vals = jnp.where(hit, -jnp.inf, vals)
        picks.append(ix)
        tops.append(m)
    ex = [jnp.exp(t - tops[0]) for t in tops]
    den = ex[0] + ex[1] + ex[2] + ex[3]
    before = _dot(tri, sel.astype(BF16)) + base_ref[0:1, :]
    idx_o = jnp.zeros((tr, LANES), F32)
    gate_o = jnp.zeros((tr, LANES), F32)
    rank_o = jnp.zeros((tr, LANES), F32)
    for k in range(TOP_K):
        rk = jnp.sum(jnp.where(lane == picks[k], before, 0.0), axis=-1, keepdims=True)
        idx_o = jnp.where(out_lane == k, picks[k], idx_o)
        gate_o = jnp.where(out_lane == k, ex[k] / den, gate_o)
        rank_o = jnp.where(out_lane == k, rk, rank_o)
    total = base_ref[0:1, :] + jnp.sum(sel, axis=0, keepdims=True)
    base_ref[...] = jnp.broadcast_to(total, base_ref.shape)
    return idx_o.astype(jnp.int32), gate_o, rank_o.astype(jnp.int32)


def _route_meta(idx, rank, counts):
    n_tok = idx.shape[0]
    ne = counts.shape[0]
    tm = MOE_TILE
    n_assign = n_tok * TOP_K
    padded = (counts + tm - 1) // tm * tm
    pend = jnp.cumsum(padded)
    pstart = pend - padded
    start = jnp.cumsum(counts) - counts
    eid = jnp.arange(ne, dtype=jnp.int32)
    slot = jnp.sum(jnp.where(idx[..., None] == eid, pstart, 0), axis=-1).astype(jnp.int32) + rank
    n_blocks = (n_assign + ne * (tm - 1) + tm - 1) // tm
    blk_start = jnp.arange(n_blocks, dtype=jnp.int32) * tm
    block_e = jnp.minimum(jnp.sum(pend[None, :] <= blk_start[:, None], axis=1), ne - 1).astype(jnp.int32)
    first = jnp.concatenate([jnp.ones((1,), jnp.int32), (block_e[1:] != block_e[:-1]).astype(jnp.int32)])
    n_used = (pend[-1] // tm).astype(jnp.int32).reshape(1)
    _, order = lax.sort_key_val(slot.reshape(-1), jnp.arange(n_assign, dtype=jnp.int32))
    order_tok = order // TOP_K
    off = jnp.arange(n_blocks * tm, dtype=jnp.int32) - jnp.repeat(pstart[block_e], tm)
    valid = off < jnp.repeat(counts[block_e], tm)
    pos = jnp.clip(jnp.repeat(start[block_e], tm) + off, 0, n_assign - 1)
    spread = jnp.arange(n_blocks * tm, dtype=jnp.int32) % n_tok
    slot_tok = jnp.where(valid, order_tok[pos], spread).astype(jnp.int32)
    return slot, slot_tok, block_e, first, n_used


def _combine_kernel(x_ref, mod_ref, y_ref, gate_ref, g_ref, *rest, final):
    o_ref = rest[-1]
    gate = gate_ref[0]
    y = y_ref[0, 0].astype(F32) * gate[:, 0:1]
    for k in range(1, TOP_K):
        y = y + y_ref[k, 0].astype(F32) * gate[:, k:k + 1]
    x = x_ref[0] + mod_ref[0, 0, 0:1, :] * y
    if final:
        ms = jnp.mean(x * x, axis=-1, keepdims=True)
        x = x * lax.rsqrt(ms + NORM_EPS) * g_ref[...]
    o_ref[0] = x


def _combine(xs, mod, yg, gates, final_g, n_ctx_tiles, final, b0, prev=None):
    b, t, d = xs.shape
    nbh = yg.shape[1]
    tm = TOKEN_TILE
    skip = n_ctx_tiles if final else 0
    seg = lambda bi, i: (bi + b0, jnp.where(i + skip >= n_ctx_tiles, 1, 0), 0, 0)
    in_specs = [pl.BlockSpec((1, tm, d), lambda bi, i: (bi + b0, i + skip, 0)),
                pl.BlockSpec((1, 1, 1, d), seg),
                pl.BlockSpec((TOP_K, 1, tm, d), lambda bi, i: (0, bi, i + skip, 0)),
                pl.BlockSpec((1, tm, LANES), lambda bi, i: (bi + b0, i + skip, 0)),
                pl.BlockSpec((1, d), lambda bi, i: (0, 0))]
    args = [xs, mod, yg, gates, final_g.reshape(1, d)]
    aliases = {} if final else {0: 0}
    if prev is not None:
        in_specs.append(pl.BlockSpec(memory_space=pl.ANY))
        args.append(prev)
        aliases = {5: 0}
    return pl.pallas_call(
        functools.partial(_combine_kernel, final=final),
        grid=(nbh, t // tm - skip),
        in_specs=in_specs,
        out_specs=pl.BlockSpec((1, tm, d), lambda bi, i: (bi + b0, i, 0)),
        out_shape=jax.ShapeDtypeStruct((b, t - skip * tm, d), F32),
        input_output_aliases=aliases,
        compiler_params=_cparams("parallel", "parallel"),
        name="combine_final" if final else "combine",
    )(*args)


def kernel(x, c, ctx, c_ctx, w_mod, b_mod, norm1_g, norm2_g, w_in, w_out, ret_decay_logit, ret_gn_g, ret_gn_b, rwkv_mu, rwkv_w0, rwkv_w_up, rwkv_a0, rwkv_a_up, rwkv_k_k, rwkv_k_a, rwkv_g_up, rwkv_r_k, rwkv_gn_g, rwkv_gn_b, lru_conv_w, lru_conv_b, lru_wa, lru_ba, lru_wx, lru_bx, lru_lambda, moe_w_router, moe_b_router, moe_w1, moe_b1, moe_w2, moe_b2, final_norm_g):
    bsz, seq, dm = x.shape
    n_ctx_tok = ctx.shape[1]
    depth = w_in.shape[0]
    n_experts = moe_w_router.shape[2]
    w_ret = 3 * dm // 8
    w_rw = 3 * dm // 8
    w_lru = dm - w_ret - w_rw
    zw = 3 * w_rw + DECAY_LORA + ICLR_LORA
    sizes = (4 * w_ret, zw, GATE_LORA, 2 * w_lru)
    bounds, off = [], 0
    for s in sizes:
        bounds.append((off, off + s))
        off += s
    bounds = tuple(bounds)
    assert off == w_in.shape[2]
    assert n_ctx_tok % TOKEN_TILE == 0 and seq % TOKEN_TILE == 0 and seq % GRID_W == 0
    t_all = n_ctx_tok + seq
    n_ctx_tiles = n_ctx_tok // TOKEN_TILE

    xs = jnp.concatenate([ctx, x], axis=1)
    cos_t, sin_t = _rope_tables(n_ctx_tok, seq, w_ret)
    hid = jnp.arange(LANES) // HEAD_DIM
    e_bf = (hid[:, None] == hid[None, :]).astype(BF16)
    cond = _pad_rows(jnp.concatenate([c, c_ctx[None, :]], axis=0))

    for l in range(depth):
        last = l == depth - 1
        mod = _modulation(cond, w_mod, b_mod, l)
        mod_l = mod[:bsz].reshape(bsz, 6, dm)
        mod_c = jnp.broadcast_to(mod[bsz].reshape(1, 6, dm), (bsz, 6, dm))
        modsel = jnp.stack([mod_c, mod_l], axis=1)

        p_ret, p_z, p_gd, p_lru = _in_proj(xs, modsel[:, :, 0:2], norm1_g[l], w_in[l].astype(BF16), bounds,
                                           (BF16, F32, BF16, BF16), n_ctx_tiles)

        ret_o, rw_y, rw_bon, lru_h = [], [], [], []
        for d in range(2):
            rev = d == 1
            ret_o.append(_retention(p_ret, cos_t, sin_t, _ret_tables(ret_decay_logit[l, d], w_ret, rev),
                                    n_ctx_tok // RET_CHUNK, rev))
            prm = _rwkv_params(rwkv_mu[l, d], rwkv_w0[l, d], rwkv_w_up[l, d], rwkv_a0[l, d], rwkv_a_up[l, d],
                               rwkv_k_k[l, d], rwkv_k_a[l, d], rwkv_r_k[l], rev, bsz)
            y, bon = _rwkv(p_z, prm, n_ctx_tok // RWKV_CHUNK, rev)
            rw_y.append(y)
            rw_bon.append(bon)
            lru_h.append(_lru(p_lru, _lru_params(lru_conv_w[l, d], lru_conv_b[l, d], lru_wa[l, d], lru_ba[l, d],
                                                 lru_wx[l, d], lru_bx[l, d], lru_lambda[l, d]),
                              n_ctx_tok // LRU_CHUNK, rev))

        gn = _pad_rows(jnp.stack([ret_gn_g[l], ret_gn_b[l], rwkv_gn_g[l], rwkv_gn_b[l]]))
        xs, h2, idx, gates, rank, counts = _mix_out(
            xs, modsel[:, :, 2:5], ret_o[0], ret_o[1], p_ret, rw_y[0], rw_y[1],
            rw_bon[0], rw_bon[1], p_gd, lru_h[0], lru_h[1], p_lru,
            gn, e_bf, rwkv_g_up[l].astype(BF16), w_out[l].astype(BF16), norm2_g[l],
            moe_w_router[l], moe_b_router[l], n_ctx_tiles)

        n_tok = bsz * t_all
        slot, slot_tok, block_e, first, n_used = _route_meta(
            idx.reshape(n_tok, LANES)[:, :TOP_K], rank.reshape(n_tok, LANES)[:, :TOP_K], counts[0])
        n_slots = slot_tok.shape[0]
        row_a = max(n_slots // MOE_TILE // 4, 1) * MOE_TILE
        h2f = h2.reshape(n_tok, dm)
        y_sorted = None
        for lo, hi in ((0, row_a), (row_a, n_slots)):
            y_sorted = _moe_ffn(h2f[slot_tok[lo:hi]], block_e, first, n_used, moe_w1, moe_b1, moe_w2, moe_b2, l,
                                lo // MOE_TILE, n_slots, y_prev=y_sorted)
        slot_b = slot.reshape(bsz, t_all, TOP_K)
        out = None
        for lo, hi in [(b, b + 1) for b in range(bsz)]:
            yg = y_sorted[jnp.moveaxis(slot_b[lo:hi], 2, 0)]
            out = _combine(xs, modsel[:, :, 5:6], yg, gates, final_norm_g, n_ctx_tiles, last, lo,
                           prev=out if last else None)
            xs = xs if last else out
        xs = out
    return xs
```

```python
import functools

import jax
import jax.numpy as jnp
from jax import lax
from jax.experimental import pallas as pl
from jax.experimental.pallas import tpu as pltpu

F32 = jnp.float32
BF16 = jnp.bfloat16

HEAD_DIM = 64
NORM_EPS = 1e-6
RET_GN_EPS = 1e-5
RWKV_GN_EPS = 64e-5
ROPE_BASE = 10000.0
GRID_W = 64
LRU_CONV = 4
LRU_C = 8.0
TOP_K = 4
SWIGLU_LIMIT = 7.0
SWIGLU_ALPHA = 1.702
DECAY_LORA = 64
ICLR_LORA = 64
GATE_LORA = 128
DECAY_SCALE = 0.6065306597126334

LANES = 128
SUBLANES = 8
TOKEN_TILE = 256
RET_CHUNK = 128
RWKV_CHUNK = 64
LRU_CHUNK = 128
MOE_TILE = 512
MIX_ROWS = 4
VMEM_LIMIT = 56 * 1024 * 1024
GELU_C0 = 0.7978845608028654
GELU_C1 = 0.044715


def _cparams(*sem):
    return pltpu.CompilerParams(dimension_semantics=sem, vmem_limit_bytes=VMEM_LIMIT)


def _scan_chunk(i, n_ctx, n_tot, rev):
    if not rev:
        return i
    return jnp.where(i < n_ctx, n_ctx - 1 - i, n_tot + n_ctx - 1 - i)


def _split3(a):
    hi = a.astype(BF16)
    r1 = a - hi.astype(F32)
    mid = r1.astype(BF16)
    lo = (r1 - mid.astype(F32)).astype(BF16)
    return hi, mid, lo


def _dot(a, b):
    return jnp.dot(a, b, preferred_element_type=F32)


def _dot_nt(a, b):
    return lax.dot_general(a, b, (((1,), (1,)), ((), ())), preferred_element_type=F32)


def _dot_tn(a, b):
    return lax.dot_general(a, b, (((0,), (0,)), ((), ())), preferred_element_type=F32)


def _dot_x3(a, b):
    a_hi = a.astype(BF16)
    a_lo = (a - a_hi.astype(F32)).astype(BF16)
    b_hi = b.astype(BF16)
    b_lo = (b - b_hi.astype(F32)).astype(BF16)
    return _dot(a_hi, b_hi) + _dot(a_lo, b_hi) + _dot(a_hi, b_lo)


def _dot_x3k(a, b):
    a_hi = a.astype(BF16)
    a_lo = (a - a_hi.astype(F32)).astype(BF16)
    b_hi = b.astype(BF16)
    b_lo = (b - b_hi.astype(F32)).astype(BF16)
    return (_dot(jnp.concatenate([a_hi, a_lo], axis=1), jnp.concatenate([b_hi, b_hi], axis=0))
            + _dot(a_hi, b_lo))


def _pad_rows(a):
    pad = -a.shape[0] % SUBLANES
    return jnp.concatenate([a, jnp.zeros((pad, a.shape[1]), a.dtype)], axis=0) if pad else a


def _sigmoid(x):
    return 1.0 / (1.0 + jnp.exp(-x))


def _mod_kernel(c_ref, w_ref, b_ref, o_ref):
    c = c_ref[...]
    s = c * _sigmoid(c)
    o_ref[...] = _dot_x3(s, w_ref[...]) + b_ref[...]


def _modulation(cond, w_mod, b_mod, layer):
    r, d = cond.shape
    nl, _, n = w_mod.shape
    tn = d
    return pl.pallas_call(
        _mod_kernel,
        grid=(n // tn,),
        in_specs=[pl.BlockSpec((r, d), lambda j: (0, 0)),
                  pl.BlockSpec((None, d, tn), lambda j: (layer, 0, j)),
                  pl.BlockSpec((None, 1, tn), lambda j: (layer, 0, j))],
        out_specs=pl.BlockSpec((r, tn), lambda j: (0, j)),
        out_shape=jax.ShapeDtypeStruct((r, n), F32),
        compiler_params=_cparams("arbitrary"),
        name="modulation",
    )(cond, w_mod, b_mod.reshape(nl, 1, n))


def _in_proj_kernel(x_ref, mod_ref, g_ref, w_ref, *o_refs, bounds):
    x = x_ref[0]
    ms = jnp.mean(x * x, axis=-1, keepdims=True)
    h = x * lax.rsqrt(ms + NORM_EPS) * g_ref[...]
    h = h * (1.0 + mod_ref[0, 0, 1:2, :]) + mod_ref[0, 0, 0:1, :]
    hb = h.astype(BF16)
    for o_ref, (lo, hi) in zip(o_refs, bounds):
        o_ref[0] = _dot(hb, w_ref[:, lo:hi]).astype(o_ref.dtype)


def _in_proj(xs, mod, norm_g, w_in_bf, bounds, dtypes, n_ctx_tiles):
    b, t, d = xs.shape
    tm = TOKEN_TILE
    p = w_in_bf.shape[1]
    seg = lambda bi, i: (bi, jnp.where(i >= n_ctx_tiles, 1, 0), 0, 0)
    return pl.pallas_call(
        functools.partial(_in_proj_kernel, bounds=bounds),
        grid=(b, t // tm),
        in_specs=[pl.BlockSpec((1, tm, d), lambda bi, i: (bi, i, 0)),
                  pl.BlockSpec((1, 1, 2, d), seg),
                  pl.BlockSpec((1, d), lambda bi, i: (0, 0)),
                  pl.BlockSpec((d, p), lambda bi, i: (0, 0))],
        out_specs=[pl.BlockSpec((1, tm, hi - lo), lambda bi, i: (bi, i, 0)) for lo, hi in bounds],
        out_shape=[jax.ShapeDtypeStruct((b, t, hi - lo), dt) for (lo, hi), dt in zip(bounds, dtypes)],
        compiler_params=_cparams("parallel", "parallel"),
        name="in_proj",
    )(xs, mod, norm_g.reshape(1, d), w_in_bf)


def _ret_kernel(q_ref, k_ref, v_ref, cos_ref, sin_ref, dq_ref, dk_ref, dmat_ref, gm_ref, bm_ref,
                o_ref, s_ref):
    i = pl.program_id(0)

    @pl.when(i == 0)
    def _():
        s_ref[...] = jnp.zeros_like(s_ref)

    nb, c, w = q_ref.shape
    cos = cos_ref[...]
    sin = sin_ref[...]
    lane = lax.broadcasted_iota(jnp.int32, (c, LANES), 1)
    half, quarter = HEAD_DIM // 2, HEAD_DIM // 4
    first = (lane % half) < quarter

    def rope(u):
        parts = []
        for j in range(w // LANES):
            uj = u[:, j * LANES:(j + 1) * LANES]
            nxt = pltpu.roll(uj, LANES - quarter, axis=1)
            prv = pltpu.roll(uj, quarter, axis=1)
            parts.append(jnp.where(first, nxt, prv))
        return u * cos + jnp.concatenate(parts, axis=1) * sin

    lane_lo = lane < HEAD_DIM

    def stack(xw):
        return jnp.concatenate([jnp.where(lane_lo, xw, 0.0), jnp.where(lane_lo, 0.0, xw)], axis=0)

    n_pairs = w // LANES
    q = [rope(q_ref[b].astype(F32)) for b in range(nb)]
    k = [rope(k_ref[b].astype(F32)) for b in range(nb)]
    chains = [(b, j) for b in range(nb) for j in range(n_pairs)]
    pair = lambda x, j: x[:, j * LANES:(j + 1) * LANES]
    qw = [pair(q[b], j) for b, j in chains]
    kw = [pair(k[b], j) for b, j in chains]
    vw = [pair(v_ref[b], j) for b, j in chains]
    s = [s_ref[b * n_pairs + j] for b, j in chains]
    inter = [_dot((x * pair(dq_ref[...], j)).astype(BF16), st.astype(BF16)) for x, st, (b, j) in zip(qw, s, chains)]
    sc = [_dot_nt(x.astype(BF16), stack(y).astype(BF16)) * dmat_ref[j]
          for x, y, (b, j) in zip(qw, kw, chains)]
    intra = [_dot(x.astype(BF16), stack(y).astype(BF16)) for x, y in zip(sc, vw)]
    ktv = [_dot_tn((y * pair(dk_ref[...], j)).astype(BF16), z.astype(BF16))
           for y, z, (b, j) in zip(kw, vw, chains)]
    for n, (b, j) in enumerate(chains):
        o_ref[b, :, j * LANES:(j + 1) * LANES] = (inter[n] + intra[n]).astype(o_ref.dtype)
        s_ref[b * n_pairs + j] = gm_ref[j] * s[n] + bm_ref[...] * ktv[n]


def _retention(p_ret, cos_t, sin_t, tabs, n_ctx, rev):
    b, t, w4 = p_ret.shape
    w = w4 // 4
    c = RET_CHUNK
    n_tot = t // c
    n_pairs = w // LANES
    dq, dk, dmat, gm, bm = tabs
    tix = lambda i: _scan_chunk(i, n_ctx, n_tot, rev)
    col = lambda j: (lambda i: (0, tix(i), j))
    const = lambda a: pl.BlockSpec(a.shape, lambda i: (0,) * a.ndim)
    return pl.pallas_call(
        _ret_kernel,
        grid=(n_tot,),
        in_specs=[pl.BlockSpec((b, c, w), col(0)), pl.BlockSpec((b, c, w), col(1)), pl.BlockSpec((b, c, w), col(2)),
                  pl.BlockSpec((c, w), lambda i: (tix(i), 0)),
                  pl.BlockSpec((c, w), lambda i: (tix(i), 0)),
                  const(dq), const(dk), const(dmat), const(gm), const(bm)],
        out_specs=pl.BlockSpec((b, c, w), lambda i: (0, tix(i), 0)),
        out_shape=jax.ShapeDtypeStruct((b, t, w), BF16),
        scratch_shapes=[pltpu.VMEM((b * n_pairs, LANES, LANES), F32)],
        compiler_params=_cparams("arbitrary"),
        name="retention_rev" if rev else "retention_fwd",
    )(p_ret, p_ret, p_ret, cos_t, sin_t, dq, dk, dmat, gm, bm)


def _ret_tables(decay_logit, w, rev):
    n_heads = w // HEAD_DIM
    c = RET_CHUNK
    lg = jax.nn.log_sigmoid(decay_logit.astype(F32))
    t = jnp.arange(c, dtype=F32)
    p = (c - 1.0 - t) if rev else t
    rel = p[:, None] - p[None, :]
    scale = HEAD_DIM ** -0.5
    dmat = jnp.where(rel >= 0, jnp.exp(lg[:, None, None] * jnp.maximum(rel, 0.0)), 0.0) * scale
    dq = jnp.exp(lg[:, None] * (p + 1.0)) * scale
    dk = jnp.exp(lg[:, None] * (c - 1.0 - p))
    lanes = lambda a: jnp.repeat(a.T, HEAD_DIM, axis=1)
    n_pairs = n_heads // 2
    dmat_w = dmat.reshape(n_pairs, 2, c, c).transpose(0, 2, 1, 3).reshape(n_pairs, c, 2 * c)
    hid = jnp.arange(LANES) // HEAD_DIM
    bm = (hid[:, None] == hid[None, :]).astype(F32)
    gm = bm[None] * jnp.exp(lg * c).reshape(n_pairs, 2)[:, hid][:, :, None]
    return lanes(dq), lanes(dk), dmat_w, gm, bm


def _rope_tables(n_ctx_tok, seq, w):
    half = HEAD_DIM // 2
    quarter = half // 2
    inv_freq = ROPE_BASE ** (-jnp.arange(quarter, dtype=F32) / quarter)
    tok = jnp.arange(seq)
    rows = (tok // GRID_W).astype(F32)
    cols = (tok % GRID_W).astype(F32)
    o = jnp.arange(w) % HEAD_DIM
    pos = jnp.where(o[None, :] < half, rows[:, None], cols[:, None])
    ang = pos * inv_freq[o % quarter][None, :]
    sign = jnp.where((o % half) < quarter, -1.0, 1.0)[None, :]
    cos = jnp.concatenate([jnp.ones((n_ctx_tok, w), F32), jnp.cos(ang)], axis=0)
    sin = jnp.concatenate([jnp.zeros((n_ctx_tok, w), F32), jnp.sin(ang) * sign], axis=0)
    return cos, sin


def _rwkv_kernel(z_ref, mu_ref, vec_ref, wup_ref, aup_ref, e_ref, minc_ref, strict_ref, incl_ref,
                 y_ref, bon_ref, st_ref, zprev_ref, *, rev, n_ctx, w):
    i = pl.program_id(0)
    nb, c, zw = z_ref.shape
    n_pairs = w // LANES
    rows = nb * c

    @pl.when(i == 0)
    def _():
        st_ref[...] = jnp.zeros_like(st_ref)

    @pl.when((i == 0) | (i == n_ctx))
    def _():
        zprev_ref[...] = jnp.zeros_like(zprev_ref)

    w0, a0, k_k, k_a, r_k = (vec_ref[j:j + 1, :] for j in range(5))
    e2 = e_ref[...]
    e22 = jnp.concatenate([e2, e2], axis=0)
    strict = strict_ref[...] > 0.0
    incl = incl_ref[...] > 0.0
    lane_lo = lax.broadcasted_iota(jnp.int32, (c, LANES), 1) < HEAD_DIM
    head_r = lax.broadcasted_iota(jnp.int32, (LANES, LANES), 0) // HEAD_DIM
    head_c = lax.broadcasted_iota(jnp.int32, (LANES, LANES), 1) // HEAD_DIM
    diag = head_r == head_c
    last = 0 if rev else c - 1

    def head_sums(x, pieces):
        outs = []
        for j in range(n_pairs):
            xj = x[:, j * LANES:(j + 1) * LANES]
            if pieces == 1:
                outs.append(_dot(xj.astype(BF16), e2))
            else:
                hi = xj.astype(BF16)
                mid = (xj - hi.astype(F32)).astype(BF16)
                outs.append(_dot(jnp.concatenate([hi, mid], axis=1), e22))
        return jnp.concatenate(outs, axis=1)

    def stack(xw):
        return jnp.concatenate([jnp.where(lane_lo, xw, 0.0), jnp.where(lane_lo, 0.0, xw)], axis=0)

    chains = [(b, j) for b in range(nb) for j in range(n_pairs)]

    def win(x, ch):
        b, j = ch
        return x[b * c:(b + 1) * c, j * LANES:(j + 1) * LANES]

    z = z_ref[...].reshape(rows, zw)
    rin = lax.broadcasted_iota(jnp.int32, (rows, zw), 0) % c
    prev = jnp.concatenate([jnp.broadcast_to(zprev_ref[b, 0:1, :], (c, zw)) for b in range(nb)], axis=0)
    if rev:
        zs = jnp.where(rin == c - 1, prev, pltpu.roll(z, rows - 1, axis=0))
        for b in range(nb):
            zprev_ref[b, 0:1, :] = z[b * c:b * c + 1, :]
    else:
        zs = jnp.where(rin == 0, prev, pltpu.roll(z, 1, axis=0))
        for b in range(nb):
            zprev_ref[b, 0:1, :] = z[b * c + c - 1:b * c + c, :]
    zd = z + (zs - z) * mu_ref[...]
    r = zd[:, 0:w]
    k = zd[:, w:2 * w]
    v = zd[:, 2 * w:3 * w]
    lora = zd[:, 3 * w:3 * w + LANES]
    lane = lax.broadcasted_iota(jnp.int32, (rows, LANES), 1)
    lora = jnp.where(lane < DECAY_LORA, jnp.tanh(lora), lora)
    logw = -DECAY_SCALE * _sigmoid(w0 + _dot_x3k(lora, wup_ref[...]))
    a = _sigmoid(a0 + _dot_x3k(lora, aup_ref[...]))
    kk0 = k * k_k
    kk = kk0 / jnp.maximum(jnp.sqrt(head_sums(kk0 * kk0, 2)), 1e-12)
    k2 = k * (1.0 + (a - 1.0) * k_a)
    bon_ref[...] = (head_sums(r * k2 * r_k, 1) * v).reshape(nb, c, w).astype(bon_ref.dtype)

    cinc = jnp.concatenate(
        [_dot(minc_ref[...], jnp.concatenate(_split3(logw[b * c:(b + 1) * c]), axis=0)) for b in range(nb)], axis=0)
    e_inc = jnp.exp(cinc)
    e_neg = jnp.exp(-cinc)
    rt = r * e_inc
    kt = k2 * e_neg
    bt = kk * a * e_neg
    kkt = kk * jnp.exp(cinc - logw)

    st = [st_ref[b * n_pairs + j] for b, j in chains]
    kkt, rt, kt, bt, vb = (x.astype(BF16) for x in (kkt, rt, kt, bt, v))
    lhs = [jnp.concatenate([win(kkt, ch), win(rt, ch)], axis=0) for ch in chains]
    g = [_dot_nt(l, jnp.concatenate([stack(win(bt, ch)), stack(win(kt, ch))], axis=0))
         for l, ch in zip(lhs, chains)]
    gb = [x.astype(BF16) for x in g]
    a_b = [jnp.where(strict, x[0:c, 0:2 * c], 0.0) for x in gb]
    a_k = [jnp.where(strict, x[0:c, 2 * c:4 * c], 0.0) for x in gb]
    r_kb = [jnp.concatenate([jnp.where(incl, x[c:2 * c, 2 * c:4 * c], 0.0),
                             -jnp.where(incl, x[c:2 * c, 0:2 * c], 0.0)], axis=1) for x in gb]
    x0 = [_dot_nt(l, s.astype(BF16)) for l, s in zip(lhs, st)]
    v_sb = [stack(win(vb, ch)) for ch in chains]
    u = [x[0:c] + _dot(ak, vs) for x, ak, vs in zip(x0, a_k, v_sb)]

    pw = a_b
    steps, sign = 1, -1.0
    while 2 * steps < c:
        both = [_dot(p, jnp.concatenate([stack(p), stack(x.astype(BF16))], axis=1))
                for p, x in zip(pw, u)]
        u = [x + sign * y[:, 2 * c:4 * c] for x, y in zip(u, both)]
        pw = [y[:, 0:2 * c].astype(BF16) for y in both]
        steps, sign = 2 * steps, 1.0
    u = [x + sign * _dot(p, stack(x.astype(BF16))) for x, p in zip(u, pw)]

    y = [x[c:2 * c] + _dot(rk, jnp.concatenate([vs, stack(uu.astype(BF16))], axis=0))
         for x, rk, vs, uu in zip(x0, r_kb, v_sb, u)]
    upd = [_dot_tn(jnp.concatenate([win(vb, ch), (-x).astype(BF16)], axis=0),
                   jnp.concatenate([win(kt, ch), win(bt, ch)], axis=0))
           for ch, x in zip(chains, u)]
    for n, (b, j) in enumerate(chains):
        w_end = e_inc[b * c + last:b * c + last + 1, j * LANES:(j + 1) * LANES]
        st_ref[b * n_pairs + j] = jnp.where(diag, (st[n] + upd[n]) * w_end, 0.0)
        y_ref[b, :, j * LANES:(j + 1) * LANES] = y[n].astype(y_ref.dtype)


def _rwkv(p_z, prm, n_ctx, rev):
    b, t, zw = p_z.shape
    w = (zw - DECAY_LORA - ICLR_LORA) // 3
    c = RWKV_CHUNK
    n_tot = t // c
    mu, vecs, wup, aup, e_bf, minc, strict, incl = prm
    tix = lambda i: _scan_chunk(i, n_ctx, n_tot, rev)
    const = lambda i: (0, 0)
    full = lambda a: pl.BlockSpec(a.shape, const)
    return pl.pallas_call(
        functools.partial(_rwkv_kernel, rev=rev, n_ctx=n_ctx, w=w),
        grid=(n_tot,),
        in_specs=[pl.BlockSpec((b, c, zw), lambda i: (0, tix(i), 0)),
                  full(mu), full(vecs), full(wup), full(aup), full(e_bf), full(minc), full(strict), full(incl)],
        out_specs=[pl.BlockSpec((b, c, w), lambda i: (0, tix(i), 0)),
                   pl.BlockSpec((b, c, w), lambda i: (0, tix(i), 0))],
        out_shape=[jax.ShapeDtypeStruct((b, t, w), BF16), jax.ShapeDtypeStruct((b, t, w), BF16)],
        scratch_shapes=[pltpu.VMEM((b * (w // LANES), LANES, LANES), F32), pltpu.VMEM((b, SUBLANES, zw), F32)],
        compiler_params=_cparams("arbitrary"),
        name="rwkv7_rev" if rev else "rwkv7_fwd",
    )(p_z, mu, vecs, wup, aup, e_bf, minc, strict, incl)


def _rwkv_params(mu, w0, w_up, a0, a_up, k_k, k_a, r_k, rev, n_batch):
    w = w0.shape[0]
    c = RWKV_CHUNK
    vecs = _pad_rows(jnp.stack([w0, a0, k_k, k_a, r_k]))
    wup = jnp.concatenate([w_up, jnp.zeros((ICLR_LORA, w), F32)], axis=0)
    aup = jnp.concatenate([jnp.zeros((DECAY_LORA, w), F32), a_up], axis=0)
    hid = jnp.arange(LANES) // HEAD_DIM
    e_bf = (hid[:, None] == hid[None, :]).astype(BF16)
    t = jnp.arange(c)
    p = (c - 1 - t) if rev else t
    le = p[None, :] <= p[:, None]
    lt = p[None, :] < p[:, None]
    strict = jnp.tile(lt, (1, 2)).astype(F32)
    incl = jnp.tile(le, (1, 2)).astype(F32)
    minc = jnp.tile(le, (1, 3)).astype(BF16)
    return mu.reshape(1, -1), vecs, wup, aup, e_bf, minc, strict, incl


def _lru_kernel(x_ref, cw_ref, vec_ref, wa_ref, wx_ref, h_ref, hcar_ref, ucar_ref, *, rev, n_ctx):
    i = pl.program_id(0)
    nb, c, w = x_ref.shape
    rows = nb * c

    @pl.when(i == 0)
    def _():
        hcar_ref[...] = jnp.zeros_like(hcar_ref)

    @pl.when((i == 0) | (i == n_ctx))
    def _():
        ucar_ref[...] = jnp.zeros_like(ucar_ref)

    u0 = x_ref[...].astype(F32).reshape(rows, w)
    row = lax.broadcasted_iota(jnp.int32, (rows, w), 0) % c

    def per_batch(ref, j):
        return jnp.concatenate([jnp.broadcast_to(ref[b, j:j + 1, :], (c, w)) for b in range(nb)], axis=0)

    def shifted(x, s, carry, fill):
        if rev:
            rolled = pltpu.roll(x, rows - s, axis=0)
            edge = row >= c - s
        else:
            rolled = pltpu.roll(x, s, axis=0)
            edge = row < s
        if carry is None:
            return jnp.where(edge, fill, rolled)
        return jnp.where(edge, carry, rolled)

    conv = vec_ref[0:1, :] + cw_ref[LRU_CONV - 1:LRU_CONV, :] * u0
    for m in range(1, LRU_CONV):
        car = jnp.zeros((rows, w), F32)
        for qpos in range(m):
            r_idx = (c - 1 - qpos) if rev else qpos
            car = jnp.where(row == r_idx, per_batch(ucar_ref, m - qpos - 1), car)
        conv = conv + cw_ref[LRU_CONV - 1 - m:LRU_CONV - m, :] * shifted(u0, m, car, None)
    for m in range(1, LRU_CONV):
        r_idx = (m - 1) if rev else (c - m)
        for b in range(nb):
            ucar_ref[b, m - 1:m, :] = u0[b * c + r_idx:b * c + r_idx + 1, :]

    cb = conv.astype(BF16)
    r = _sigmoid(_dot(cb, wa_ref[...]) + vec_ref[1:2, :])
    ig = _sigmoid(_dot(cb, wx_ref[...]) + vec_ref[2:3, :])
    log_a = -LRU_C * r * vec_ref[3:4, :]
    a = jnp.exp(log_a)
    bb = jnp.sqrt(1.0 - a * a) * (ig * conv)

    s = 1
    while s < c:
        bb = bb + a * shifted(bb, s, None, 0.0)
        a = a * shifted(a, s, None, 1.0)
        s *= 2
    h = bb + a * per_batch(hcar_ref, 0)
    h_ref[...] = h.reshape(nb, c, w).astype(h_ref.dtype)
    last = 0 if rev else c - 1
    for b in range(nb):
        hcar_ref[b, 0:1, :] = h[b * c + last:b * c + last + 1, :]


def _lru(p_lru, prm, n_ctx, rev):
    b, t, w2 = p_lru.shape
    w = w2 // 2
    c = LRU_CHUNK
    n_tot = t // c
    cw, vecs, wa, wx = prm
    tix = lambda i: _scan_chunk(i, n_ctx, n_tot, rev)
    const = lambda i: (0, 0)
    return pl.pallas_call(
        functools.partial(_lru_kernel, rev=rev, n_ctx=n_ctx),
        grid=(n_tot,),
        in_specs=[pl.BlockSpec((b, c, w), lambda i: (0, tix(i), 0)),
                  pl.BlockSpec(cw.shape, const), pl.BlockSpec(vecs.shape, const),
                  pl.BlockSpec(wa.shape, const), pl.BlockSpec(wx.shape, const)],
        out_specs=pl.BlockSpec((b, c, w), lambda i: (0, tix(i), 0)),
        out_shape=jax.ShapeDtypeStruct((b, t, w), BF16),
        scratch_shapes=[pltpu.VMEM((b, SUBLANES, w), F32), pltpu.VMEM((b, SUBLANES, w), F32)],
        compiler_params=_cparams("arbitrary"),
        name="rglru_rev" if rev else "rglru_fwd",
    )(p_lru, cw, vecs, wa, wx)


def _lru_params(conv_w, conv_b, wa, ba, wx, bx, lam):
    w = conv_b.shape[0]
    cw = _pad_rows(conv_w)
    vecs = _pad_rows(jnp.stack([conv_b, ba, bx, jax.nn.softplus(-lam)]))
    return cw, vecs, jax.scipy.linalg.block_diag(*wa).astype(BF16), jax.scipy.linalg.block_diag(*wx).astype(BF16)


def _head_norm(y, e2, gain, bias, eps):
    inv = 1.0 / HEAD_DIM

    def head_sums(x):
        xb = x.astype(BF16)
        return jnp.concatenate([_dot(xb[:, j:j + LANES], e2) for j in range(0, x.shape[1], LANES)], axis=1)

    yc = y - head_sums(y) * inv
    var = head_sums(yc * yc) * inv
    return yc * lax.rsqrt(var + eps) * gain + bias


def _mix_out_kernel(x_ref, mod_ref, of_ref, ob_ref, g_ref, yf_ref, yb_ref, bf_ref, bb_ref, gd_ref,
                    hf_ref, hb_ref, lg_ref, gn_ref, e_ref, gup_ref, wout_ref, n2g_ref, wr_ref, br_ref, tri_ref,
                    xo_ref, h2_ref, idx_ref, gate_ref, rank_ref, cnt_ref, base_ref, *, w_ret, w_rw):
    @pl.when((pl.program_id(0) == 0) & (pl.program_id(1) == 0))
    def _():
        base_ref[...] = jnp.zeros_like(base_ref)

    rows = range(x_ref.shape[0])
    e_bf = e_ref[...]
    f32 = lambda ref, s: ref[s].astype(F32)
    g = [f32(g_ref, s) for s in rows]
    ret = [_head_norm(f32(of_ref, s) + f32(ob_ref, s), e_bf, gn_ref[0:1, :], gn_ref[1:2, :], RET_GN_EPS) for s in rows]
    ret = [r * (gg * _sigmoid(gg)) for r, gg in zip(ret, g)]
    gate = [_dot(_sigmoid(f32(gd_ref, s)).astype(BF16), gup_ref[...]) for s in rows]
    rw = [_head_norm(f32(yf_ref, s) + f32(yb_ref, s), e_bf, gn_ref[2:3, :], gn_ref[3:4, :], RWKV_GN_EPS) for s in rows]
    rw = [(r + f32(bf_ref, s) + f32(bb_ref, s)) * gt for r, gt, s in zip(rw, gate, rows)]
    lg = [f32(lg_ref, s) for s in rows]
    gelu = [0.5 * u * (1.0 + jnp.tanh(GELU_C0 * (u + GELU_C1 * (u * u * u)))) for u in lg]
    lru = [(f32(hf_ref, s) + f32(hb_ref, s)) * ge for ge, s in zip(gelu, rows)]
    mix = [_dot(a.astype(BF16), wout_ref[0:w_ret, :])
           + _dot(b.astype(BF16), wout_ref[w_ret:w_ret + w_rw, :])
           + _dot(c.astype(BF16), wout_ref[w_ret + w_rw:, :]) for a, b, c in zip(ret, rw, lru)]
    x = [x_ref[s] + mod_ref[s, 0, 0:1, :] * m for m, s in zip(mix, rows)]
    h2 = []
    for s in rows:
        xo_ref[s] = x[s]
        ms = jnp.mean(x[s] * x[s], axis=-1, keepdims=True)
        hn = x[s] * lax.rsqrt(ms + NORM_EPS) * n2g_ref[...]
        h2.append(hn * (1.0 + mod_ref[s, 0, 2:3, :]) + mod_ref[s, 0, 1:2, :])
        h2_ref[s] = h2[s].astype(BF16)
    logits = [_dot_x3(h, wr_ref[...]) + br_ref[...] for h in h2]
    for s in rows:
        idx_o, gate_o, rank_o = _route_tile(logits[s], tri_ref[...], base_ref)
        idx_ref[s] = idx_o
        gate_ref[s] = gate_o
        rank_ref[s] = rank_o
    cnt_ref[...] = base_ref[...].astype(jnp.int32)


def _mix_out(xs, mod, o_f, o_b, p_ret, y_f, y_b, bon_f, bon_b, p_gd, h_f, h_b, p_lru,
             gn, e_bf, g_up_bf, w_out_bf, norm2_g, w_router, b_router, n_ctx_tiles):
    b, t, d = xs.shape
    tm = TOKEN_TILE
    nbb = MIX_ROWS if b % MIX_ROWS == 0 else 1
    w_ret, w_rw, w_lru = o_f.shape[2], y_f.shape[2], h_f.shape[2]
    ne = w_router.shape[1]
    tok = lambda wd, j=0: pl.BlockSpec((nbb, tm, wd), lambda bi, i: (bi, i, j))
    const = lambda a: pl.BlockSpec(a.shape, lambda bi, i: (0,) * a.ndim)
    seg = lambda bi, i: (bi, jnp.where(i >= n_ctx_tiles, 1, 0), 0, 0)
    n2g = norm2_g.reshape(1, d)
    br = b_router.reshape(1, ne)
    tt = jnp.arange(tm)
    tri = (tt[None, :] < tt[:, None]).astype(BF16)
    return pl.pallas_call(
        functools.partial(_mix_out_kernel, w_ret=w_ret, w_rw=w_rw),
        grid=(b // nbb, t // tm),
        in_specs=[tok(d), pl.BlockSpec((nbb, 1, 3, d), seg),
                  tok(w_ret), tok(w_ret), tok(w_ret, 3),
                  tok(w_rw), tok(w_rw), tok(w_rw), tok(w_rw), tok(GATE_LORA),
                  tok(w_lru), tok(w_lru), tok(w_lru, 1),
                  const(gn), const(e_bf), const(g_up_bf), const(w_out_bf), const(n2g), const(w_router), const(br),
                  const(tri)],
        out_specs=[tok(d), tok(d), tok(LANES), tok(LANES), tok(LANES),
                   pl.BlockSpec((SUBLANES, ne), lambda bi, i: (0, 0))],
        out_shape=[jax.ShapeDtypeStruct((b, t, d), F32), jax.ShapeDtypeStruct((b, t, d), BF16),
                   jax.ShapeDtypeStruct((b, t, LANES), jnp.int32), jax.ShapeDtypeStruct((b, t, LANES), F32),
                   jax.ShapeDtypeStruct((b, t, LANES), jnp.int32), jax.ShapeDtypeStruct((SUBLANES, ne), jnp.int32)],
        scratch_shapes=[pltpu.VMEM((SUBLANES, ne), F32)],
        compiler_params=_cparams("arbitrary", "arbitrary"),
        name="mix_out",
    )(xs, mod, o_f, o_b, p_ret, y_f, y_b, bon_f, bon_b, p_gd, h_f, h_b, p_lru,
      gn, e_bf, g_up_bf, w_out_bf, n2g, w_router, br, tri)


def _moe_kernel(be_ref, first_ref, nu_ref, *refs, blk0, chained):
    x_ref, w1_ref, b1_ref, w2_ref, b2_ref = refs[1:6] if chained else refs[0:5]
    y_ref, w1b_ref, w2b_ref = refs[-3:]
    i = pl.program_id(0)
    blk = i + blk0
    de = w2_ref.shape[1]

    @pl.when((first_ref[blk] == 1) | (i == 0))
    def _():
        w1b_ref[...] = w1_ref[0].astype(BF16)
        w2b_ref[...] = w2_ref[0].astype(BF16)

    @pl.when(blk < nu_ref[0])
    def _():
        gu = _dot(x_ref[...], w1b_ref[...]) + b1_ref[0]
        glu = jnp.minimum(gu[:, :de], SWIGLU_LIMIT)
        lin = jnp.clip(gu[:, de:], -SWIGLU_LIMIT, SWIGLU_LIMIT)
        act = glu * _sigmoid(SWIGLU_ALPHA * glu) * (lin + 1.0)
        y_ref[...] = (_dot(act.astype(BF16), w2b_ref[...]) + b2_ref[0]).astype(y_ref.dtype)

    @pl.when(blk >= nu_ref[0])
    def _():
        y_ref[...] = jnp.zeros_like(y_ref)


def _moe_ffn(hb, block_e, first, n_used, w1, b1, w2, b2, layer, blk0, n_slots, y_prev=None):
    rows, d = hb.shape
    tm = MOE_TILE
    nl, ne, _, d2 = w1.shape
    de = w2.shape[2]
    chained = y_prev is not None
    wsel = lambda i, be, fi, nu: (layer, be[i + blk0], 0, 0)
    in_specs = [pl.BlockSpec((tm, d), lambda i, be, fi, nu: (i, 0)),
                pl.BlockSpec((None, 1, d, d2), wsel),
                pl.BlockSpec((None, 1, 1, d2), wsel),
                pl.BlockSpec((None, 1, de, d), wsel),
                pl.BlockSpec((None, 1, 1, d), wsel)]
    args = [hb, w1, b1.reshape(nl, ne, 1, d2), w2, b2.reshape(nl, ne, 1, d)]
    if chained:
        in_specs = [pl.BlockSpec(memory_space=pl.ANY)] + in_specs
        args = [y_prev] + args
    return pl.pallas_call(
        functools.partial(_moe_kernel, blk0=blk0, chained=chained),
        grid_spec=pltpu.PrefetchScalarGridSpec(
            num_scalar_prefetch=3,
            grid=(rows // tm,),
            in_specs=in_specs,
            out_specs=pl.BlockSpec((tm, d), lambda i, be, fi, nu: (i + blk0, 0)),
            scratch_shapes=[pltpu.VMEM((d, d2), BF16), pltpu.VMEM((de, d), BF16)],
        ),
        out_shape=jax.ShapeDtypeStruct((n_slots, d), BF16),
        input_output_aliases={3: 0} if chained else {},
        compiler_params=_cparams("arbitrary"),
        name="moe_ffn",
    )(block_e, first, n_used, *args)


def _route_tile(lg, tri, base_ref):
    tr, ne = lg.shape
    lane = lax.broadcasted_iota(jnp.int32, (tr, ne), 1).astype(F32)
    out_lane = lax.broadcasted_iota(jnp.int32, (tr, LANES), 1)
    vals = lg
    sel = jnp.zeros((tr, ne), F32)
    picks, tops = [], []
    for _ in range(TOP_K):
        m = jnp.max(vals, axis=-1, keepdims=True)
        ix = jnp.min(jnp.where(vals == m, lane, float(ne)), axis=-1, keepdims=True)
        hit = lane == ix
        sel = jnp.where(hit, 1.0, sel)
        vals = jnp.where(hit, -jnp.inf, vals)
        picks.append(ix)
        tops.append(m)
    ex = [jnp.exp(t - tops[0]) for t in tops]
    den = ex[0] + ex[1] + ex[2] + ex[3]
    before = _dot(tri, sel.astype(BF16)) + base_ref[0:1, :]
    idx_o = jnp.zeros((tr, LANES), F32)
    gate_o = jnp.zeros((tr, LANES), F32)
    rank_o = jnp.zeros((tr, LANES), F32)
    for k in range(TOP_K):
        rk = jnp.sum(jnp.where(lane == picks[k], before, 0.0), axis=-1, keepdims=True)
        idx_o = jnp.where(out_lane == k, picks[k], idx_o)
        gate_o = jnp.where(out_lane == k, ex[k] / den, gate_o)
        rank_o = jnp.where(out_lane == k, rk, rank_o)
    total = base_ref[0:1, :] + jnp.sum(sel, axis=0, keepdims=True)
    base_ref[...] = jnp.broadcast_to(total, base_ref.shape)
    return idx_o.astype(jnp.int32), gate_o, rank_o.astype(jnp.int32)


def _route_meta(idx, rank, counts):
    n_tok = idx.shape[0]
    ne = counts.shape[0]
    tm = MOE_TILE
    n_assign = n_tok * TOP_K
    padded = (counts + tm - 1) // tm * tm
    pend = jnp.cumsum(padded)
    pstart = pend - padded
    start = jnp.cumsum(counts) - counts
    eid = jnp.arange(ne, dtype=jnp.int32)
    slot = jnp.sum(jnp.where(idx[..., None] == eid, pstart, 0), axis=-1).astype(jnp.int32) + rank
    n_blocks = (n_assign + ne * (tm - 1) + tm - 1) // tm
    blk_start = jnp.arange(n_blocks, dtype=jnp.int32) * tm
    block_e = jnp.minimum(jnp.sum(pend[None, :] <= blk_start[:, None], axis=1), ne - 1).astype(jnp.int32)
    first = jnp.concatenate([jnp.ones((1,), jnp.int32), (block_e[1:] != block_e[:-1]).astype(jnp.int32)])
    n_used = (pend[-1] // tm).astype(jnp.int32).reshape(1)
    _, order = lax.sort_key_val(slot.reshape(-1), jnp.arange(n_assign, dtype=jnp.int32))
    order_tok = order // TOP_K
    off = jnp.arange(n_blocks * tm, dtype=jnp.int32) - jnp.repeat(pstart[block_e], tm)
    valid = off < jnp.repeat(counts[block_e], tm)
    pos = jnp.clip(jnp.repeat(start[block_e], tm) + off, 0, n_assign - 1)
    spread = jnp.arange(n_blocks * tm, dtype=jnp.int32) % n_tok
    slot_tok = jnp.where(valid, order_tok[pos], spread).astype(jnp.int32)
    return slot, slot_tok, block_e, first, n_used


def _combine_kernel(x_ref, mod_ref, y_ref, gate_ref, g_ref, *rest, final):
    o_ref = rest[-1]
    gate = gate_ref[0]
    y = y_ref[0, 0].astype(F32) * gate[:, 0:1]
    for k in range(1, TOP_K):
        y = y + y_ref[k, 0].astype(F32) * gate[:, k:k + 1]
    x = x_ref[0] + mod_ref[0, 0, 0:1, :] * y
    if final:
        ms = jnp.mean(x * x, axis=-1, keepdims=True)
        x = x * lax.rsqrt(ms + NORM_EPS) * g_ref[...]
    o_ref[0] = x


def _combine(xs, mod, yg, gates, final_g, n_ctx_tiles, final, b0, prev=None):
    b, t, d = xs.shape
    nbh = yg.shape[1]
    tm = TOKEN_TILE
    skip = n_ctx_tiles if final else 0
    seg = lambda bi, i: (bi + b0, jnp.where(i + skip >= n_ctx_tiles, 1, 0), 0, 0)
    in_specs = [pl.BlockSpec((1, tm, d), lambda bi, i: (bi + b0, i + skip, 0)),
                pl.BlockSpec((1, 1, 1, d), seg),
                pl.BlockSpec((TOP_K, 1, tm, d), lambda bi, i: (0, bi, i + skip, 0)),
                pl.BlockSpec((1, tm, LANES), lambda bi, i: (bi + b0, i + skip, 0)),
                pl.BlockSpec((1, d), lambda bi, i: (0, 0))]
    args = [xs, mod, yg, gates, final_g.reshape(1, d)]
    aliases = {} if final else {0: 0}
    if prev is not None:
        in_specs.append(pl.BlockSpec(memory_space=pl.ANY))
        args.append(prev)
        aliases = {5: 0}
    return pl.pallas_call(
        functools.partial(_combine_kernel, final=final),
        grid=(nbh, t // tm - skip),
        in_specs=in_specs,
        out_specs=pl.BlockSpec((1, tm, d), lambda bi, i: (bi + b0, i, 0)),
        out_shape=jax.ShapeDtypeStruct((b, t - skip * tm, d), F32),
        input_output_aliases=aliases,
        compiler_params=_cparams("parallel", "parallel"),
        name="combine_final" if final else "combine",
    )(*args)


def kernel(x, c, ctx, c_ctx, w_mod, b_mod, norm1_g, norm2_g, w_in, w_out, ret_decay_logit, ret_gn_g, ret_gn_b, rwkv_mu, rwkv_w0, rwkv_w_up, rwkv_a0, rwkv_a_up, rwkv_k_k, rwkv_k_a, rwkv_g_up, rwkv_r_k, rwkv_gn_g, rwkv_gn_b, lru_conv_w, lru_conv_b, lru_wa, lru_ba, lru_wx, lru_bx, lru_lambda, moe_w_router, moe_b_router, moe_w1, moe_b1, moe_w2, moe_b2, final_norm_g):
    bsz, seq, dm = x.shape
    n_ctx_tok = ctx.shape[1]
    depth = w_in.shape[0]
    n_experts = moe_w_router.shape[2]
    w_ret = 3 * dm // 8
    w_rw = 3 * dm // 8
    w_lru = dm - w_ret - w_rw
    zw = 3 * w_rw + DECAY_LORA + ICLR_LORA
    sizes = (4 * w_ret, zw, GATE_LORA, 2 * w_lru)
    bounds, off = [], 0
    for s in sizes:
        bounds.append((off, off + s))
        off += s
    bounds = tuple(bounds)
    assert off == w_in.shape[2]
    assert n_ctx_tok % TOKEN_TILE == 0 and seq % TOKEN_TILE == 0 and seq % GRID_W == 0
    t_all = n_ctx_tok + seq
    n_ctx_tiles = n_ctx_tok // TOKEN_TILE

    xs = jnp.concatenate([ctx, x], axis=1)
    cos_t, sin_t = _rope_tables(n_ctx_tok, seq, w_ret)
    hid = jnp.arange(LANES) // HEAD_DIM
    e_bf = (hid[:, None] == hid[None, :]).astype(BF16)
    cond = _pad_rows(jnp.concatenate([c, c_ctx[None, :]], axis=0))

    for l in range(depth):
        last = l == depth - 1
        mod = _modulation(cond, w_mod, b_mod, l)
        mod_l = mod[:bsz].reshape(bsz, 6, dm)
        mod_c = jnp.broadcast_to(mod[bsz].reshape(1, 6, dm), (bsz, 6, dm))
        modsel = jnp.stack([mod_c, mod_l], axis=1)

        p_ret, p_z, p_gd, p_lru = _in_proj(xs, modsel[:, :, 0:2], norm1_g[l], w_in[l].astype(BF16), bounds,
                                           (BF16, F32, BF16, BF16), n_ctx_tiles)

        ret_o, rw_y, rw_bon, lru_h = [], [], [], []
        for d in range(2):
            rev = d == 1
            ret_o.append(_retention(p_ret, cos_t, sin_t, _ret_tables(ret_decay_logit[l, d], w_ret, rev),
                                    n_ctx_tok // RET_CHUNK, rev))
            prm = _rwkv_params(rwkv_mu[l, d], rwkv_w0[l, d], rwkv_w_up[l, d], rwkv_a0[l, d], rwkv_a_up[l, d],
                               rwkv_k_k[l, d], rwkv_k_a[l, d], rwkv_r_k[l], rev, bsz)
            y, bon = _rwkv(p_z, prm, n_ctx_tok // RWKV_CHUNK, rev)
            rw_y.append(y)
            rw_bon.append(bon)
            lru_h.append(_lru(p_lru, _lru_params(lru_conv_w[l, d], lru_conv_b[l, d], lru_wa[l, d], lru_ba[l, d],
                                                 lru_wx[l, d], lru_bx[l, d], lru_lambda[l, d]),
                              n_ctx_tok // LRU_CHUNK, rev))

        gn = _pad_rows(jnp.stack([ret_gn_g[l], ret_gn_b[l], rwkv_gn_g[l], rwkv_gn_b[l]]))
        xs, h2, idx, gates, rank, counts = _mix_out(
            xs, modsel[:, :, 2:5], ret_o[0], ret_o[1], p_ret, rw_y[0], rw_y[1],
            rw_bon[0], rw_bon[1], p_gd, lru_h[0], lru_h[1], p_lru,
            gn, e_bf, rwkv_g_up[l].astype(BF16), w_out[l].astype(BF16), norm2_g[l],
            moe_w_router[l], moe_b_router[l], n_ctx_tiles)

        n_tok = bsz * t_all
        slot, slot_tok, block_e, first, n_used = _route_meta(
            idx.reshape(n_tok, LANES)[:, :TOP_K], rank.reshape(n_tok, LANES)[:, :TOP_K], counts[0])
        n_slots = slot_tok.shape[0]
        row_a = max(n_slots // MOE_TILE // 4, 1) * MOE_TILE
        h2f = h2.reshape(n_tok, dm)
        y_sorted = None
        for lo, hi in ((0, row_a), (row_a, n_slots)):
            y_sorted = _moe_ffn(h2f[slot_tok[lo:hi]], block_e, first, n_used, moe_w1, moe_b1, moe_w2, moe_b2, l,
                                lo // MOE_TILE, n_slots, y_prev=y_sorted)
        slot_b = slot.reshape(bsz, t_all, TOP_K)
        out = None
        for lo, hi in [(b, b + 1) for b in range(bsz)]:
            yg = y_sorted[jnp.moveaxis(slot_b[lo:hi], 2, 0)]
            out = _combine(xs, modsel[:, :, 5:6], yg, gates, final_norm_g, n_ctx_tiles, last, lo,
                           prev=out if last else None)
            xs = xs if last else out
        xs = out
    return xs
```

```python
import functools

import jax
import jax.numpy as jnp
from jax import lax
from jax.experimental import pallas as pl
from jax.experimental.pallas import tpu as pltpu

F32 = jnp.float32
BF16 = jnp.bfloat16

HEAD_DIM = 64
NORM_EPS = 1e-6
RET_GN_EPS = 1e-5
RWKV_GN_EPS = 64e-5
ROPE_BASE = 10000.0
GRID_W = 64
LRU_CONV = 4
LRU_C = 8.0
TOP_K = 4
SWIGLU_LIMIT = 7.0
SWIGLU_ALPHA = 1.702
DECAY_LORA = 64
ICLR_LORA = 64
GATE_LORA = 128
DECAY_SCALE = 0.6065306597126334

LANES = 128
SUBLANES = 8
TOKEN_TILE = 256
RET_CHUNK = 128
RWKV_CHUNK = 64
LRU_CHUNK = 256
MOE_TILE = 512
MIX_ROWS = 4
VMEM_LIMIT = 56 * 1024 * 1024
GELU_C0 = 0.7978845608028654
GELU_C1 = 0.044715


def _cparams(*sem):
    return pltpu.CompilerParams(dimension_semantics=sem, vmem_limit_bytes=VMEM_LIMIT)


def _scan_chunk(i, n_ctx, n_tot, rev):
    if not rev:
        return i
    return jnp.where(i < n_ctx, n_ctx - 1 - i, n_tot + n_ctx - 1 - i)


def _split3(a):
    hi = a.astype(BF16)
    r1 = a - hi.astype(F32)
    mid = r1.astype(BF16)
    lo = (r1 - mid.astype(F32)).astype(BF16)
    return hi, mid, lo


def _dot(a, b):
    return jnp.dot(a, b, preferred_element_type=F32)


def _dot_nt(a, b):
    return lax.dot_general(a, b, (((1,), (1,)), ((), ())), preferred_element_type=F32)


def _dot_tn(a, b):
    return lax.dot_general(a, b, (((0,), (0,)), ((), ())), preferred_element_type=F32)


def _dot_x3(a, b):
    a_hi = a.astype(BF16)
    a_lo = (a - a_hi.astype(F32)).astype(BF16)
    b_hi = b.astype(BF16)
    b_lo = (b - b_hi.astype(F32)).astype(BF16)
    return _dot(a_hi, b_hi) + _dot(a_lo, b_hi) + _dot(a_hi, b_lo)


def _dot_x3k(a, b):
    a_hi = a.astype(BF16)
    a_lo = (a - a_hi.astype(F32)).astype(BF16)
    b_hi = b.astype(BF16)
    b_lo = (b - b_hi.astype(F32)).astype(BF16)
    return (_dot(jnp.concatenate([a_hi, a_lo], axis=1), jnp.concatenate([b_hi, b_hi], axis=0))
            + _dot(a_hi, b_lo))


def _pad_rows(a):
    pad = -a.shape[0] % SUBLANES
    return jnp.concatenate([a, jnp.zeros((pad, a.shape[1]), a.dtype)], axis=0) if pad else a


def _sigmoid(x):
    return 1.0 / (1.0 + jnp.exp(-x))


def _mod_kernel(c_ref, w_ref, b_ref, o_ref):
    c = c_ref[...]
    s = c * _sigmoid(c)
    o_ref[...] = _dot_x3(s, w_ref[...]) + b_ref[...]


def _modulation(cond, w_mod, b_mod, layer):
    r, d = cond.shape
    nl, _, n = w_mod.shape
    tn = d
    return pl.pallas_call(
        _mod_kernel,
        grid=(n // tn,),
        in_specs=[pl.BlockSpec((r, d), lambda j: (0, 0)),
                  pl.BlockSpec((None, d, tn), lambda j: (layer, 0, j)),
                  pl.BlockSpec((None, 1, tn), lambda j: (layer, 0, j))],
        out_specs=pl.BlockSpec((r, tn), lambda j: (0, j)),
        out_shape=jax.ShapeDtypeStruct((r, n), F32),
        compiler_params=_cparams("arbitrary"),
        name="modulation",
    )(cond, w_mod, b_mod.reshape(nl, 1, n))


def _in_proj_kernel(x_ref, mod_ref, g_ref, w_ref, *o_refs, bounds):
    x = x_ref[0]
    ms = jnp.mean(x * x, axis=-1, keepdims=True)
    h = x * lax.rsqrt(ms + NORM_EPS) * g_ref[...]
    h = h * (1.0 + mod_ref[0, 0, 1:2, :]) + mod_ref[0, 0, 0:1, :]
    hb = h.astype(BF16)
    for o_ref, (lo, hi) in zip(o_refs, bounds):
        o_ref[0] = _dot(hb, w_ref[:, lo:hi]).astype(o_ref.dtype)


def _in_proj(xs, mod, norm_g, w_in_bf, bounds, dtypes, n_ctx_tiles):
    b, t, d = xs.shape
    tm = TOKEN_TILE
    p = w_in_bf.shape[1]
    seg = lambda bi, i: (bi, jnp.where(i >= n_ctx_tiles, 1, 0), 0, 0)
    return pl.pallas_call(
        functools.partial(_in_proj_kernel, bounds=bounds),
        grid=(b, t // tm),
        in_specs=[pl.BlockSpec((1, tm, d), lambda bi, i: (bi, i, 0)),
                  pl.BlockSpec((1, 1, 2, d), seg),
                  pl.BlockSpec((1, d), lambda bi, i: (0, 0)),
                  pl.BlockSpec((d, p), lambda bi, i: (0, 0))],
        out_specs=[pl.BlockSpec((1, tm, hi - lo), lambda bi, i: (bi, i, 0)) for lo, hi in bounds],
        out_shape=[jax.ShapeDtypeStruct((b, t, hi - lo), dt) for (lo, hi), dt in zip(bounds, dtypes)],
        compiler_params=_cparams("parallel", "parallel"),
        name="in_proj",
    )(xs, mod, norm_g.reshape(1, d), w_in_bf)


def _ret_kernel(q_ref, k_ref, v_ref, cos_ref, sin_ref, dq_ref, dk_ref, dmat_ref, gm_ref, bm_ref,
                o_ref, s_ref):
    i = pl.program_id(0)

    @pl.when(i == 0)
    def _():
        s_ref[...] = jnp.zeros_like(s_ref)

    nb, c, w = q_ref.shape
    cos = cos_ref[...]
    sin = sin_ref[...]
    lane = lax.broadcasted_iota(jnp.int32, (c, LANES), 1)
    half, quarter = HEAD_DIM // 2, HEAD_DIM // 4
    first = (lane % half) < quarter

    def rope(u):
        parts = []
        for j in range(w // LANES):
            uj = u[:, j * LANES:(j + 1) * LANES]
            nxt = pltpu.roll(uj, LANES - quarter, axis=1)
            prv = pltpu.roll(uj, quarter, axis=1)
            parts.append(jnp.where(first, nxt, prv))
        return u * cos + jnp.concatenate(parts, axis=1) * sin

    lane_lo = lane < HEAD_DIM

    def stack(xw):
        return jnp.concatenate([jnp.where(lane_lo, xw, 0.0), jnp.where(lane_lo, 0.0, xw)], axis=0)

    n_pairs = w // LANES
    q = [rope(q_ref[b].astype(F32)) for b in range(nb)]
    k = [rope(k_ref[b].astype(F32)) for b in range(nb)]
    chains = [(b, j) for b in range(nb) for j in range(n_pairs)]
    pair = lambda x, j: x[:, j * LANES:(j + 1) * LANES]
    qw = [pair(q[b], j) for b, j in chains]
    kw = [pair(k[b], j) for b, j in chains]
    vw = [pair(v_ref[b], j) for b, j in chains]
    s = [s_ref[b * n_pairs + j] for b, j in chains]
    inter = [_dot((x * pair(dq_ref[...], j)).astype(BF16), st.astype(BF16)) for x, st, (b, j) in zip(qw, s, chains)]
    sc = [_dot_nt(x.astype(BF16), stack(y).astype(BF16)) * dmat_ref[j]
          for x, y, (b, j) in zip(qw, kw, chains)]
    intra = [_dot(x.astype(BF16), stack(y).astype(BF16)) for x, y in zip(sc, vw)]
    ktv = [_dot_tn((y * pair(dk_ref[...], j)).astype(BF16), z.astype(BF16))
           for y, z, (b, j) in zip(kw, vw, chains)]
    for n, (b, j) in enumerate(chains):
        o_ref[b, :, j * LANES:(j + 1) * LANES] = (inter[n] + intra[n]).astype(o_ref.dtype)
        s_ref[b * n_pairs + j] = gm_ref[j] * s[n] + bm_ref[...] * ktv[n]


def _retention(p_ret, cos_t, sin_t, tabs, n_ctx, rev):
    b, t, w4 = p_ret.shape
    w = w4 // 4
    c = RET_CHUNK
    n_tot = t // c
    n_pairs = w // LANES
    dq, dk, dmat, gm, bm = tabs
    tix = lambda i: _scan_chunk(i, n_ctx, n_tot, rev)
    col = lambda j: (lambda i: (0, tix(i), j))
    const = lambda a: pl.BlockSpec(a.shape, lambda i: (0,) * a.ndim)
    return pl.pallas_call(
        _ret_kernel,
        grid=(n_tot,),
        in_specs=[pl.BlockSpec((b, c, w), col(0)), pl.BlockSpec((b, c, w), col(1)), pl.BlockSpec((b, c, w), col(2)),
                  pl.BlockSpec((c, w), lambda i: (tix(i), 0)),
                  pl.BlockSpec((c, w), lambda i: (tix(i), 0)),
                  const(dq), const(dk), const(dmat), const(gm), const(bm)],
        out_specs=pl.BlockSpec((b, c, w), lambda i: (0, tix(i), 0)),
        out_shape=jax.ShapeDtypeStruct((b, t, w), BF16),
        scratch_shapes=[pltpu.VMEM((b * n_pairs, LANES, LANES), F32)],
        compiler_params=_cparams("arbitrary"),
        name="retention_rev" if rev else "retention_fwd",
    )(p_ret, p_ret, p_ret, cos_t, sin_t, dq, dk, dmat, gm, bm)


def _ret_tables(decay_logit, w, rev):
    n_heads = w // HEAD_DIM
    c = RET_CHUNK
    lg = jax.nn.log_sigmoid(decay_logit.astype(F32))
    t = jnp.arange(c, dtype=F32)
    p = (c - 1.0 - t) if rev else t
    rel = p[:, None] - p[None, :]
    scale = HEAD_DIM ** -0.5
    dmat = jnp.where(rel >= 0, jnp.exp(lg[:, None, None] * jnp.maximum(rel, 0.0)), 0.0) * scale
    dq = jnp.exp(lg[:, None] * (p + 1.0)) * scale
    dk = jnp.exp(lg[:, None] * (c - 1.0 - p))
    lanes = lambda a: jnp.repeat(a.T, HEAD_DIM, axis=1)
    n_pairs = n_heads // 2
    dmat_w = dmat.reshape(n_pairs, 2, c, c).transpose(0, 2, 1, 3).reshape(n_pairs, c, 2 * c)
    hid = jnp.arange(LANES) // HEAD_DIM
    bm = (hid[:, None] == hid[None, :]).astype(F32)
    gm = bm[None] * jnp.exp(lg * c).reshape(n_pairs, 2)[:, hid][:, :, None]
    return lanes(dq), lanes(dk), dmat_w, gm, bm


def _rope_tables(n_ctx_tok, seq, w):
    half = HEAD_DIM // 2
    quarter = half // 2
    inv_freq = ROPE_BASE ** (-jnp.arange(quarter, dtype=F32) / quarter)
    tok = jnp.arange(seq)
    rows = (tok // GRID_W).astype(F32)
    cols = (tok % GRID_W).astype(F32)
    o = jnp.arange(w) % HEAD_DIM
    pos = jnp.where(o[None, :] < half, rows[:, None], cols[:, None])
    ang = pos * inv_freq[o % quarter][None, :]
    sign = jnp.where((o % half) < quarter, -1.0, 1.0)[None, :]
    cos = jnp.concatenate([jnp.ones((n_ctx_tok, w), F32), jnp.cos(ang)], axis=0)
    sin = jnp.concatenate([jnp.zeros((n_ctx_tok, w), F32), jnp.sin(ang) * sign], axis=0)
    return cos, sin


def _rwkv_kernel(z_ref, mu_ref, vec_ref, wup_ref, aup_ref, e_ref, minc_ref, strict_ref, incl_ref,
                 y_ref, bon_ref, st_ref, zprev_ref, *, rev, n_ctx, w):
    i = pl.program_id(0)
    nb, c, zw = z_ref.shape
    n_pairs = w // LANES
    rows = nb * c

    @pl.when(i == 0)
    def _():
        st_ref[...] = jnp.zeros_like(st_ref)

    @pl.when((i == 0) | (i == n_ctx))
    def _():
        zprev_ref[...] = jnp.zeros_like(zprev_ref)

    w0, a0, k_k, k_a, r_k = (vec_ref[j:j + 1, :] for j in range(5))
    e2 = e_ref[...]
    e22 = jnp.concatenate([e2, e2], axis=0)
    strict = strict_ref[...] > 0.0
    incl = incl_ref[...] > 0.0
    lane_lo = lax.broadcasted_iota(jnp.int32, (c, LANES), 1) < HEAD_DIM
    head_r = lax.broadcasted_iota(jnp.int32, (LANES, LANES), 0) // HEAD_DIM
    head_c = lax.broadcasted_iota(jnp.int32, (LANES, LANES), 1) // HEAD_DIM
    diag = head_r == head_c
    last = 0 if rev else c - 1

    def head_sums(x, pieces):
        outs = []
        for j in range(n_pairs):
            xj = x[:, j * LANES:(j + 1) * LANES]
            if pieces == 1:
                outs.append(_dot(xj.astype(BF16), e2))
            else:
                hi = xj.astype(BF16)
                mid = (xj - hi.astype(F32)).astype(BF16)
                outs.append(_dot(jnp.concatenate([hi, mid], axis=1), e22))
        return jnp.concatenate(outs, axis=1)

    def stack(xw):
        return jnp.concatenate([jnp.where(lane_lo, xw, 0.0), jnp.where(lane_lo, 0.0, xw)], axis=0)

    chains = [(b, j) for b in range(nb) for j in range(n_pairs)]

    def win(x, ch):
        b, j = ch
        return x[b * c:(b + 1) * c, j * LANES:(j + 1) * LANES]

    z = z_ref[...].reshape(rows, zw)
    rin = lax.broadcasted_iota(jnp.int32, (rows, zw), 0) % c
    prev = jnp.concatenate([jnp.broadcast_to(zprev_ref[b, 0:1, :], (c, zw)) for b in range(nb)], axis=0)
    if rev:
        zs = jnp.where(rin == c - 1, prev, pltpu.roll(z, rows - 1, axis=0))
        for b in range(nb):
            zprev_ref[b, 0:1, :] = z[b * c:b * c + 1, :]
    else:
        zs = jnp.where(rin == 0, prev, pltpu.roll(z, 1, axis=0))
        for b in range(nb):
            zprev_ref[b, 0:1, :] = z[b * c + c - 1:b * c + c, :]
    zd = z + (zs - z) * mu_ref[...]
    r = zd[:, 0:w]
    k = zd[:, w:2 * w]
    v = zd[:, 2 * w:3 * w]
    lora = zd[:, 3 * w:3 * w + LANES]
    lane = lax.broadcasted_iota(jnp.int32, (rows, LANES), 1)
    lora = jnp.where(lane < DECAY_LORA, jnp.tanh(lora), lora)
    logw = -DECAY_SCALE * _sigmoid(w0 + _dot_x3k(lora, wup_ref[...]))
    a = _sigmoid(a0 + _dot_x3k(lora, aup_ref[...]))
    kk0 = k * k_k
    kk = kk0 / jnp.maximum(jnp.sqrt(head_sums(kk0 * kk0, 2)), 1e-12)
    k2 = k * (1.0 + (a - 1.0) * k_a)
    bon_ref[...] = (head_sums(r * k2 * r_k, 1) * v).reshape(nb, c, w).astype(bon_ref.dtype)

    cinc = jnp.concatenate(
        [_dot(minc_ref[...], jnp.concatenate(_split3(logw[b * c:(b + 1) * c]), axis=0)) for b in range(nb)], axis=0)
    e_inc = jnp.exp(cinc)
    e_neg = jnp.exp(-cinc)
    rt = r * e_inc
    kt = k2 * e_neg
    bt = kk * a * e_neg
    kkt = kk * jnp.exp(cinc - logw)

    st = [st_ref[b * n_pairs + j] for b, j in chains]
    lhs = [jnp.concatenate([win(kkt, ch), win(rt, ch)], axis=0).astype(BF16) for ch in chains]
    g = [_dot_nt(l, jnp.concatenate([stack(win(bt, ch)), stack(win(kt, ch))], axis=0).astype(BF16))
         for l, ch in zip(lhs, chains)]
    a_b = [jnp.where(strict, x[0:c, 0:2 * c], 0.0) for x in g]
    a_k = [jnp.where(strict, x[0:c, 2 * c:4 * c], 0.0).astype(BF16) for x in g]
    r_kb = [jnp.concatenate([jnp.where(incl, x[c:2 * c, 2 * c:4 * c], 0.0),
                             -jnp.where(incl, x[c:2 * c, 0:2 * c], 0.0)], axis=1).astype(BF16) for x in g]
    x0 = [_dot_nt(l, s.astype(BF16)) for l, s in zip(lhs, st)]
    v_sb = [stack(win(v, ch)).astype(BF16) for ch in chains]
    u = [x[0:c] + _dot(ak, vs) for x, ak, vs in zip(x0, a_k, v_sb)]

    pw = a_b
    steps, sign = 1, -1.0
    while 2 * steps < c:
        both = [_dot(p.astype(BF16), jnp.concatenate([stack(p), stack(x)], axis=1).astype(BF16))
                for p, x in zip(pw, u)]
        u = [x + sign * y[:, 2 * c:4 * c] for x, y in zip(u, both)]
        pw = [y[:, 0:2 * c] for y in both]
        steps, sign = 2 * steps, 1.0
    u = [x + sign * _dot(p.astype(BF16), stack(x).astype(BF16)) for x, p in zip(u, pw)]

    y = [x[c:2 * c] + _dot(rk, jnp.concatenate([vs, stack(uu).astype(BF16)], axis=0))
         for x, rk, vs, uu in zip(x0, r_kb, v_sb, u)]
    upd = [_dot_tn(jnp.concatenate([win(v, ch), x], axis=0).astype(BF16),
                   jnp.concatenate([win(kt, ch), -win(bt, ch)], axis=0).astype(BF16))
           for ch, x in zip(chains, u)]
    for n, (b, j) in enumerate(chains):
        w_end = e_inc[b * c + last:b * c + last + 1, j * LANES:(j + 1) * LANES]
        st_ref[b * n_pairs + j] = jnp.where(diag, (st[n] + upd[n]) * w_end, 0.0)
        y_ref[b, :, j * LANES:(j + 1) * LANES] = y[n].astype(y_ref.dtype)


def _rwkv(p_z, prm, n_ctx, rev):
    b, t, zw = p_z.shape
    w = (zw - DECAY_LORA - ICLR_LORA) // 3
    c = RWKV_CHUNK
    n_tot = t // c
    mu, vecs, wup, aup, e_bf, minc, strict, incl = prm
    tix = lambda i: _scan_chunk(i, n_ctx, n_tot, rev)
    const = lambda i: (0, 0)
    full = lambda a: pl.BlockSpec(a.shape, const)
    return pl.pallas_call(
        functools.partial(_rwkv_kernel, rev=rev, n_ctx=n_ctx, w=w),
        grid=(n_tot,),
        in_specs=[pl.BlockSpec((b, c, zw), lambda i: (0, tix(i), 0)),
                  full(mu), full(vecs), full(wup), full(aup), full(e_bf), full(minc), full(strict), full(incl)],
        out_specs=[pl.BlockSpec((b, c, w), lambda i: (0, tix(i), 0)),
                   pl.BlockSpec((b, c, w), lambda i: (0, tix(i), 0))],
        out_shape=[jax.ShapeDtypeStruct((b, t, w), BF16), jax.ShapeDtypeStruct((b, t, w), BF16)],
        scratch_shapes=[pltpu.VMEM((b * (w // LANES), LANES, LANES), F32), pltpu.VMEM((b, SUBLANES, zw), F32)],
        compiler_params=_cparams("arbitrary"),
        name="rwkv7_rev" if rev else "rwkv7_fwd",
    )(p_z, mu, vecs, wup, aup, e_bf, minc, strict, incl)


def _rwkv_params(mu, w0, w_up, a0, a_up, k_k, k_a, r_k, rev, n_batch):
    w = w0.shape[0]
    c = RWKV_CHUNK
    vecs = _pad_rows(jnp.stack([w0, a0, k_k, k_a, r_k]))
    wup = jnp.concatenate([w_up, jnp.zeros((ICLR_LORA, w), F32)], axis=0)
    aup = jnp.concatenate([jnp.zeros((DECAY_LORA, w), F32), a_up], axis=0)
    hid = jnp.arange(LANES) // HEAD_DIM
    e_bf = (hid[:, None] == hid[None, :]).astype(BF16)
    t = jnp.arange(c)
    p = (c - 1 - t) if rev else t
    le = p[None, :] <= p[:, None]
    lt = p[None, :] < p[:, None]
    strict = jnp.tile(lt, (1, 2)).astype(F32)
    incl = jnp.tile(le, (1, 2)).astype(F32)
    minc = jnp.tile(le, (1, 3)).astype(BF16)
    return mu.reshape(1, -1), vecs, wup, aup, e_bf, minc, strict, incl


def _lru_kernel(x_ref, cw_ref, vec_ref, wa_ref, wx_ref, h_ref, hcar_ref, ucar_ref, *, rev, n_ctx):
    i = pl.program_id(0)
    nb, c, w = x_ref.shape
    rows = nb * c

    @pl.when(i == 0)
    def _():
        hcar_ref[...] = jnp.zeros_like(hcar_ref)

    @pl.when((i == 0) | (i == n_ctx))
    def _():
        ucar_ref[...] = jnp.zeros_like(ucar_ref)

    u0 = x_ref[...].astype(F32).reshape(rows, w)
    row = lax.broadcasted_iota(jnp.int32, (rows, w), 0) % c

    def per_batch(ref, j):
        return jnp.concatenate([jnp.broadcast_to(ref[b, j:j + 1, :], (c, w)) for b in range(nb)], axis=0)

    def shifted(x, s, carry, fill):
        if rev:
            rolled = pltpu.roll(x, rows - s, axis=0)
            edge = row >= c - s
        else:
            rolled = pltpu.roll(x, s, axis=0)
            edge = row < s
        if carry is None:
            return jnp.where(edge, fill, rolled)
        return jnp.where(edge, carry, rolled)

    conv = vec_ref[0:1, :] + cw_ref[LRU_CONV - 1:LRU_CONV, :] * u0
    for m in range(1, LRU_CONV):
        car = jnp.zeros((rows, w), F32)
        for qpos in range(m):
            r_idx = (c - 1 - qpos) if rev else qpos
            car = jnp.where(row == r_idx, per_batch(ucar_ref, m - qpos - 1), car)
        conv = conv + cw_ref[LRU_CONV - 1 - m:LRU_CONV - m, :] * shifted(u0, m, car, None)
    for m in range(1, LRU_CONV):
        r_idx = (m - 1) if rev else (c - m)
        for b in range(nb):
            ucar_ref[b, m - 1:m, :] = u0[b * c + r_idx:b * c + r_idx + 1, :]

    cb = conv.astype(BF16)
    r = _sigmoid(_dot(cb, wa_ref[...]) + vec_ref[1:2, :])
    ig = _sigmoid(_dot(cb, wx_ref[...]) + vec_ref[2:3, :])
    log_a = -LRU_C * r * vec_ref[3:4, :]
    a = jnp.exp(log_a)
    bb = jnp.sqrt(1.0 - a * a) * (ig * conv)

    s = 1
    while s < c:
        bb = bb + a * shifted(bb, s, None, 0.0)
        a = a * shifted(a, s, None, 1.0)
        s *= 2
    h = bb + a * per_batch(hcar_ref, 0)
    h_ref[...] = h.reshape(nb, c, w).astype(h_ref.dtype)
    last = 0 if rev else c - 1
    for b in range(nb):
        hcar_ref[b, 0:1, :] = h[b * c + last:b * c + last + 1, :]


def _lru(p_lru, prm, n_ctx, rev):
    b, t, w2 = p_lru.shape
    w = w2 // 2
    c = LRU_CHUNK
    n_tot = t // c
    cw, vecs, wa, wx = prm
    tix = lambda i: _scan_chunk(i, n_ctx, n_tot, rev)
    const = lambda i: (0, 0)
    return pl.pallas_call(
        functools.partial(_lru_kernel, rev=rev, n_ctx=n_ctx),
        grid=(n_tot,),
        in_specs=[pl.BlockSpec((b, c, w), lambda i: (0, tix(i), 0)),
                  pl.BlockSpec(cw.shape, const), pl.BlockSpec(vecs.shape, const),
                  pl.BlockSpec(wa.shape, const), pl.BlockSpec(wx.shape, const)],
        out_specs=pl.BlockSpec((b, c, w), lambda i: (0, tix(i), 0)),
        out_shape=jax.ShapeDtypeStruct((b, t, w), BF16),
        scratch_shapes=[pltpu.VMEM((b, SUBLANES, w), F32), pltpu.VMEM((b, SUBLANES, w), F32)],
        compiler_params=_cparams("arbitrary"),
        name="rglru_rev" if rev else "rglru_fwd",
    )(p_lru, cw, vecs, wa, wx)


def _lru_params(conv_w, conv_b, wa, ba, wx, bx, lam):
    w = conv_b.shape[0]
    cw = _pad_rows(conv_w)
    vecs = _pad_rows(jnp.stack([conv_b, ba, bx, jax.nn.softplus(-lam)]))
    return cw, vecs, jax.scipy.linalg.block_diag(*wa).astype(BF16), jax.scipy.linalg.block_diag(*wx).astype(BF16)


def _head_norm(y, e2, gain, bias, eps):
    inv = 1.0 / HEAD_DIM

    def head_sums(x):
        xb = x.astype(BF16)
        return jnp.concatenate([_dot(xb[:, j:j + LANES], e2) for j in range(0, x.shape[1], LANES)], axis=1)

    yc = y - head_sums(y) * inv
    var = head_sums(yc * yc) * inv
    return yc * lax.rsqrt(var + eps) * gain + bias


def _mix_out_kernel(x_ref, mod_ref, of_ref, ob_ref, g_ref, yf_ref, yb_ref, bf_ref, bb_ref, gd_ref,
                    hf_ref, hb_ref, lg_ref, gn_ref, e_ref, gup_ref, wout_ref, n2g_ref, wr_ref, br_ref, tri_ref,
                    xo_ref, h2_ref, idx_ref, gate_ref, rank_ref, cnt_ref, base_ref, *, w_ret, w_rw):
    @pl.when((pl.program_id(0) == 0) & (pl.program_id(1) == 0))
    def _():
        base_ref[...] = jnp.zeros_like(base_ref)

    rows = range(x_ref.shape[0])
    e_bf = e_ref[...]
    f32 = lambda ref, s: ref[s].astype(F32)
    g = [f32(g_ref, s) for s in rows]
    ret = [_head_norm(f32(of_ref, s) + f32(ob_ref, s), e_bf, gn_ref[0:1, :], gn_ref[1:2, :], RET_GN_EPS) for s in rows]
    ret = [r * (gg * _sigmoid(gg)) for r, gg in zip(ret, g)]
    gate = [_dot(_sigmoid(f32(gd_ref, s)).astype(BF16), gup_ref[...]) for s in rows]
    rw = [_head_norm(f32(yf_ref, s) + f32(yb_ref, s), e_bf, gn_ref[2:3, :], gn_ref[3:4, :], RWKV_GN_EPS) for s in rows]
    rw = [(r + f32(bf_ref, s) + f32(bb_ref, s)) * gt for r, gt, s in zip(rw, gate, rows)]
    lg = [f32(lg_ref, s) for s in rows]
    gelu = [0.5 * u * (1.0 + jnp.tanh(GELU_C0 * (u + GELU_C1 * (u * u * u)))) for u in lg]
    lru = [(f32(hf_ref, s) + f32(hb_ref, s)) * ge for ge, s in zip(gelu, rows)]
    mix = [_dot(a.astype(BF16), wout_ref[0:w_ret, :])
           + _dot(b.astype(BF16), wout_ref[w_ret:w_ret + w_rw, :])
           + _dot(c.astype(BF16), wout_ref[w_ret + w_rw:, :]) for a, b, c in zip(ret, rw, lru)]
    x = [x_ref[s] + mod_ref[s, 0, 0:1, :] * m for m, s in zip(mix, rows)]
    h2 = []
    for s in rows:
        xo_ref[s] = x[s]
        ms = jnp.mean(x[s] * x[s], axis=-1, keepdims=True)
        hn = x[s] * lax.rsqrt(ms + NORM_EPS) * n2g_ref[...]
        h2.append(hn * (1.0 + mod_ref[s, 0, 2:3, :]) + mod_ref[s, 0, 1:2, :])
        h2_ref[s] = h2[s].astype(BF16)
    logits = [_dot_x3(h, wr_ref[...]) + br_ref[...] for h in h2]
    for s in rows:
        idx_o, gate_o, rank_o = _route_tile(logits[s], tri_ref[...], base_ref)
        idx_ref[s] = idx_o
        gate_ref[s] = gate_o
        rank_ref[s] = rank_o
    cnt_ref[...] = base_ref[...].astype(jnp.int32)


def _mix_out(xs, mod, o_f, o_b, p_ret, y_f, y_b, bon_f, bon_b, p_gd, h_f, h_b, p_lru,
             gn, e_bf, g_up_bf, w_out_bf, norm2_g, w_router, b_router, n_ctx_tiles):
    b, t, d = xs.shape
    tm = TOKEN_TILE
    nbb = MIX_ROWS if b % MIX_ROWS == 0 else 1
    w_ret, w_rw, w_lru = o_f.shape[2], y_f.shape[2], h_f.shape[2]
    ne = w_router.shape[1]
    tok = lambda wd, j=0: pl.BlockSpec((nbb, tm, wd), lambda bi, i: (bi, i, j))
    const = lambda a: pl.BlockSpec(a.shape, lambda bi, i: (0,) * a.ndim)
    seg = lambda bi, i: (bi, jnp.where(i >= n_ctx_tiles, 1, 0), 0, 0)
    n2g = norm2_g.reshape(1, d)
    br = b_router.reshape(1, ne)
    tt = jnp.arange(tm)
    tri = (tt[None, :] < tt[:, None]).astype(BF16)
    return pl.pallas_call(
        functools.partial(_mix_out_kernel, w_ret=w_ret, w_rw=w_rw),
        grid=(b // nbb, t // tm),
        in_specs=[tok(d), pl.BlockSpec((nbb, 1, 3, d), seg),
                  tok(w_ret), tok(w_ret), tok(w_ret, 3),
                  tok(w_rw), tok(w_rw), tok(w_rw), tok(w_rw), tok(GATE_LORA),
                  tok(w_lru), tok(w_lru), tok(w_lru, 1),
                  const(gn), const(e_bf), const(g_up_bf), const(w_out_bf), const(n2g), const(w_router), const(br),
                  const(tri)],
        out_specs=[tok(d), tok(d), tok(LANES), tok(LANES), tok(LANES),
                   pl.BlockSpec((SUBLANES, ne), lambda bi, i: (0, 0))],
        out_shape=[jax.ShapeDtypeStruct((b, t, d), F32), jax.ShapeDtypeStruct((b, t, d), BF16),
                   jax.ShapeDtypeStruct((b, t, LANES), jnp.int32), jax.ShapeDtypeStruct((b, t, LANES), F32),
                   jax.ShapeDtypeStruct((b, t, LANES), jnp.int32), jax.ShapeDtypeStruct((SUBLANES, ne), jnp.int32)],
        scratch_shapes=[pltpu.VMEM((SUBLANES, ne), F32)],
        compiler_params=_cparams("arbitrary", "arbitrary"),
        name="mix_out",
    )(xs, mod, o_f, o_b, p_ret, y_f, y_b, bon_f, bon_b, p_gd, h_f, h_b, p_lru,
      gn, e_bf, g_up_bf, w_out_bf, n2g, w_router, br, tri)


def _moe_kernel(be_ref, first_ref, nu_ref, *refs, blk0, chained):
    x_ref, w1_ref, b1_ref, w2_ref, b2_ref = refs[1:6] if chained else refs[0:5]
    y_ref, w1b_ref, w2b_ref = refs[-3:]
    i = pl.program_id(0)
    blk = i + blk0
    de = w2_ref.shape[1]

    @pl.when((first_ref[blk] == 1) | (i == 0))
    def _():
        w1b_ref[...] = w1_ref[0].astype(BF16)
        w2b_ref[...] = w2_ref[0].astype(BF16)

    @pl.when(blk < nu_ref[0])
    def _():
        gu = _dot(x_ref[...], w1b_ref[...]) + b1_ref[0]
        glu = jnp.minimum(gu[:, :de], SWIGLU_LIMIT)
        lin = jnp.clip(gu[:, de:], -SWIGLU_LIMIT, SWIGLU_LIMIT)
        act = glu * _sigmoid(SWIGLU_ALPHA * glu) * (lin + 1.0)
        y_ref[...] = (_dot(act.astype(BF16), w2b_ref[...]) + b2_ref[0]).astype(y_ref.dtype)

    @pl.when(blk >= nu_ref[0])
    def _():
        y_ref[...] = jnp.zeros_like(y_ref)


def _moe_ffn(hb, block_e, first, n_used, w1, b1, w2, b2, layer, blk0, n_slots, y_prev=None):
    rows, d = hb.shape
    tm = MOE_TILE
    nl, ne, _, d2 = w1.shape
    de = w2.shape[2]
    chained = y_prev is not None
    wsel = lambda i, be, fi, nu: (layer, be[i + blk0], 0, 0)
    in_specs = [pl.BlockSpec((tm, d), lambda i, be, fi, nu: (i, 0)),
                pl.BlockSpec((None, 1, d, d2), wsel),
                pl.BlockSpec((None, 1, 1, d2), wsel),
                pl.BlockSpec((None, 1, de, d), wsel),
                pl.BlockSpec((None, 1, 1, d), wsel)]
    args = [hb, w1, b1.reshape(nl, ne, 1, d2), w2, b2.reshape(nl, ne, 1, d)]
    if chained:
        in_specs = [pl.BlockSpec(memory_space=pl.ANY)] + in_specs
        args = [y_prev] + args
    return pl.pallas_call(
        functools.partial(_moe_kernel, blk0=blk0, chained=chained),
        grid_spec=pltpu.PrefetchScalarGridSpec(
            num_scalar_prefetch=3,
            grid=(rows // tm,),
            in_specs=in_specs,
            out_specs=pl.BlockSpec((tm, d), lambda i, be, fi, nu: (i + blk0, 0)),
            scratch_shapes=[pltpu.VMEM((d, d2), BF16), pltpu.VMEM((de, d), BF16)],
        ),
        out_shape=jax.ShapeDtypeStruct((n_slots, d), BF16),
        input_output_aliases={3: 0} if chained else {},
        compiler_params=_cparams("arbitrary"),
        name="moe_ffn",
    )(block_e, first, n_used, *args)


def _route_tile(lg, tri, base_ref):
    tr, ne = lg.shape
    lane = lax.broadcasted_iota(jnp.int32, (tr, ne), 1).astype(F32)
    out_lane = lax.broadcasted_iota(jnp.int32, (tr, LANES), 1)
    vals = lg
    sel = jnp.zeros((tr, ne), F32)
    picks, tops = [], []
    for _ in range(TOP_K):
        m = jnp.max(vals, axis=-1, keepdims=True)
        ix = jnp.min(jnp.where(vals == m, lane, float(ne)), axis=-1, keepdims=True)
        hit = lane == ix
        sel = jnp.where(hit, 1.0, sel)
        vals = jnp.where(hit, -jnp.inf, vals)
        picks.append(ix)
        tops.append(m)
    ex = [jnp.exp(t - tops[0]) for t in tops]
    den = ex[0] + ex[1] + ex[2] + ex[3]
    before = _dot(tri, sel.astype(BF16)) + base_ref[0:1, :]
    idx_o = jnp.zeros((tr, LANES), F32)
    gate_o = jnp.zeros((tr, LANES), F32)
    rank_o = jnp.zeros((tr, LANES), F32)
    for k in range(TOP_K):
        rk = jnp.sum(jnp.where(lane == picks[k], before, 0.0), axis=-1, keepdims=True)
        idx_o = jnp.where(out_lane == k, picks[k], idx_o)
        gate_o = jnp.where(out_lane == k, ex[k] / den, gate_o)
        rank_o = jnp.where(out_lane == k, rk, rank_o)
    total = base_ref[0:1, :] + jnp.sum(sel, axis=0, keepdims=True)
    base_ref[...] = jnp.broadcast_to(total, base_ref.shape)
    return idx_o.astype(jnp.int32), gate_o, rank_o.astype(jnp.int32)


def _route_meta(idx, rank, counts):
    n_tok = idx.shape[0]
    ne = counts.shape[0]
    tm = MOE_TILE
    n_assign = n_tok * TOP_K
    padded = (counts + tm - 1) // tm * tm
    pend = jnp.cumsum(padded)
    pstart = pend - padded
    start = jnp.cumsum(counts) - counts
    eid = jnp.arange(ne, dtype=jnp.int32)
    slot = jnp.sum(jnp.where(idx[..., None] == eid, pstart, 0), axis=-1).astype(jnp.int32) + rank
    n_blocks = (n_assign + ne * (tm - 1) + tm - 1) // tm
    blk_start = jnp.arange(n_blocks, dtype=jnp.int32) * tm
    block_e = jnp.minimum(jnp.sum(pend[None, :] <= blk_start[:, None], axis=1), ne - 1).astype(jnp.int32)
    first = jnp.concatenate([jnp.ones((1,), jnp.int32), (block_e[1:] != block_e[:-1]).astype(jnp.int32)])
    n_used = (pend[-1] // tm).astype(jnp.int32).reshape(1)
    _, order = lax.sort_key_val(slot.reshape(-1), jnp.arange(n_assign, dtype=jnp.int32))
    order_tok = order // TOP_K
    off = jnp.arange(n_blocks * tm, dtype=jnp.int32) - jnp.repeat(pstart[block_e], tm)
    valid = off < jnp.repeat(counts[block_e], tm)
    pos = jnp.clip(jnp.repeat(start[block_e], tm) + off, 0, n_assign - 1)
    spread = jnp.arange(n_blocks * tm, dtype=jnp.int32) % n_tok
    slot_tok = jnp.where(valid, order_tok[pos], spread).astype(jnp.int32)
    return slot, slot_tok, block_e, first, n_used


def _combine_kernel(x_ref, mod_ref, y_ref, gate_ref, g_ref, *rest, final):
    o_ref = rest[-1]
    gate = gate_ref[0]
    y = y_ref[0, 0].astype(F32) * gate[:, 0:1]
    for k in range(1, TOP_K):
        y = y + y_ref[k, 0].astype(F32) * gate[:, k:k + 1]
    x = x_ref[0] + mod_ref[0, 0, 0:1, :] * y
    if final:
        ms = jnp.mean(x * x, axis=-1, keepdims=True)
        x = x * lax.rsqrt(ms + NORM_EPS) * g_ref[...]
    o_ref[0] = x


def _combine(xs, mod, yg, gates, final_g, n_ctx_tiles, final, b0, prev=None):
    b, t, d = xs.shape
    nbh = yg.shape[1]
    tm = TOKEN_TILE
    skip = n_ctx_tiles if final else 0
    seg = lambda bi, i: (bi + b0, jnp.where(i + skip >= n_ctx_tiles, 1, 0), 0, 0)
    in_specs = [pl.BlockSpec((1, tm, d), lambda bi, i: (bi + b0, i + skip, 0)),
                pl.BlockSpec((1, 1, 1, d), seg),
                pl.BlockSpec((TOP_K, 1, tm, d), lambda bi, i: (0, bi, i + skip, 0)),
                pl.BlockSpec((1, tm, LANES), lambda bi, i: (bi + b0, i + skip, 0)),
                pl.BlockSpec((1, d), lambda bi, i: (0, 0))]
    args = [xs, mod, yg, gates, final_g.reshape(1, d)]
    aliases = {} if final else {0: 0}
    if prev is not None:
        in_specs.append(pl.BlockSpec(memory_space=pl.ANY))
        args.append(prev)
        aliases = {5: 0}
    return pl.pallas_call(
        functools.partial(_combine_kernel, final=final),
        grid=(nbh, t // tm - skip),
        in_specs=in_specs,
        out_specs=pl.BlockSpec((1, tm, d), lambda bi, i: (bi + b0, i, 0)),
        out_shape=jax.ShapeDtypeStruct((b, t - skip * tm, d), F32),
        input_output_aliases=aliases,
        compiler_params=_cparams("parallel", "parallel"),
        name="combine_final" if final else "combine",
    )(*args)


def kernel(x, c, ctx, c_ctx, w_mod, b_mod, norm1_g, norm2_g, w_in, w_out, ret_decay_logit, ret_gn_g, ret_gn_b, rwkv_mu, rwkv_w0, rwkv_w_up, rwkv_a0, rwkv_a_up, rwkv_k_k, rwkv_k_a, rwkv_g_up, rwkv_r_k, rwkv_gn_g, rwkv_gn_b, lru_conv_w, lru_conv_b, lru_wa, lru_ba, lru_wx, lru_bx, lru_lambda, moe_w_router, moe_b_router, moe_w1, moe_b1, moe_w2, moe_b2, final_norm_g):
    bsz, seq, dm = x.shape
    n_ctx_tok = ctx.shape[1]
    depth = w_in.shape[0]
    n_experts = moe_w_router.shape[2]
    w_ret = 3 * dm // 8
    w_rw = 3 * dm // 8
    w_lru = dm - w_ret - w_rw
    zw = 3 * w_rw + DECAY_LORA + ICLR_LORA
    sizes = (4 * w_ret, zw, GATE_LORA, 2 * w_lru)
    bounds, off = [], 0
    for s in sizes:
        bounds.append((off, off + s))
        off += s
    bounds = tuple(bounds)
    assert off == w_in.shape[2]
    assert n_ctx_tok % TOKEN_TILE == 0 and seq % TOKEN_TILE == 0 and seq % GRID_W == 0
    t_all = n_ctx_tok + seq
    n_ctx_tiles = n_ctx_tok // TOKEN_TILE

    xs = jnp.concatenate([ctx, x], axis=1)
    cos_t, sin_t = _rope_tables(n_ctx_tok, seq, w_ret)
    hid = jnp.arange(LANES) // HEAD_DIM
    e_bf = (hid[:, None] == hid[None, :]).astype(BF16)
    cond = _pad_rows(jnp.concatenate([c, c_ctx[None, :]], axis=0))

    for l in range(depth):
        last = l == depth - 1
        mod = _modulation(cond, w_mod, b_mod, l)
        mod_l = mod[:bsz].reshape(bsz, 6, dm)
        mod_c = jnp.broadcast_to(mod[bsz].reshape(1, 6, dm), (bsz, 6, dm))
        modsel = jnp.stack([mod_c, mod_l], axis=1)

        p_ret, p_z, p_gd, p_lru = _in_proj(xs, modsel[:, :, 0:2], norm1_g[l], w_in[l].astype(BF16), bounds,
                                           (BF16, F32, BF16, BF16), n_ctx_tiles)

        ret_o, rw_y, rw_bon, lru_h = [], [], [], []
        for d in range(2):
            rev = d == 1
            ret_o.append(_retention(p_ret, cos_t, sin_t, _ret_tables(ret_decay_logit[l, d], w_ret, rev),
                                    n_ctx_tok // RET_CHUNK, rev))
            prm = _rwkv_params(rwkv_mu[l, d], rwkv_w0[l, d], rwkv_w_up[l, d], rwkv_a0[l, d], rwkv_a_up[l, d],
                               rwkv_k_k[l, d], rwkv_k_a[l, d], rwkv_r_k[l], rev, bsz)
            y, bon = _rwkv(p_z, prm, n_ctx_tok // RWKV_CHUNK, rev)
            rw_y.append(y)
            rw_bon.append(bon)
            lru_h.append(_lru(p_lru, _lru_params(lru_conv_w[l, d], lru_conv_b[l, d], lru_wa[l, d], lru_ba[l, d],
                                                 lru_wx[l, d], lru_bx[l, d], lru_lambda[l, d]),
                              n_ctx_tok // LRU_CHUNK, rev))

        gn = _pad_rows(jnp.stack([ret_gn_g[l], ret_gn_b[l], rwkv_gn_g[l], rwkv_gn_b[l]]))
        xs, h2, idx, gates, rank, counts = _mix_out(
            xs, modsel[:, :, 2:5], ret_o[0], ret_o[1], p_ret, rw_y[0], rw_y[1],
            rw_bon[0], rw_bon[1], p_gd, lru_h[0], lru_h[1], p_lru,
            gn, e_bf, rwkv_g_up[l].astype(BF16), w_out[l].astype(BF16), norm2_g[l],
            moe_w_router[l], moe_b_router[l], n_ctx_tiles)

        n_tok = bsz * t_all
        slot, slot_tok, block_e, first, n_used = _route_meta(
            idx.reshape(n_tok, LANES)[:, :TOP_K], rank.reshape(n_tok, LANES)[:, :TOP_K], counts[0])
        n_slots = slot_tok.shape[0]
        row_a = max(n_slots // MOE_TILE // 4, 1) * MOE_TILE
        h2f = h2.reshape(n_tok, dm)
        y_sorted = None
        for lo, hi in ((0, row_a), (row_a, n_slots)):
            y_sorted = _moe_ffn(h2f[slot_tok[lo:hi]], block_e, first, n_used, moe_w1, moe_b1, moe_w2, moe_b2, l,
                                lo // MOE_TILE, n_slots, y_prev=y_sorted)
        slot_b = slot.reshape(bsz, t_all, TOP_K)
        out = None
        for lo, hi in [(b, b + 1) for b in range(bsz)]:
            yg = y_sorted[jnp.moveaxis(slot_b[lo:hi], 2, 0)]
            out = _combine(xs, modsel[:, :, 5:6], yg, gates, final_norm_g, n_ctx_tiles, last, lo,
                           prev=out if last else None)
            xs = xs if last else out
        xs = out
    return xs
```
